```python
import math
import jax, jax.numpy as jnp
from jax import lax
import numpy as np


D_MODEL = 4096
BATCH = 1
SEQ = 8192
DEPTH = 2

RWKV_HEAD_DIM = 64
RWKV_WIDTH = 3 * D_MODEL // 8
RWKV_HEADS = RWKV_WIDTH // RWKV_HEAD_DIM
DECAY_LORA = max(32, int(round(1.8 * D_MODEL ** 0.5 / 32)) * 32)
AAA_LORA = max(32, int(round(1.8 * D_MODEL ** 0.5 / 32)) * 32)
MV_LORA = max(32, int(round(1.3 * D_MODEL ** 0.5 / 32)) * 32)
GATE_LORA = max(32, int(round(0.6 * D_MODEL ** 0.8 / 32)) * 32)
GN_EPS = 64e-5

GMLP_WIDTH = D_MODEL // 4
GMLP_GROUP_DIM = 128
GMLP_GROUPS = GMLP_WIDTH // GMLP_GROUP_DIM
GMLP_CHUNK = 128

MLA_V_DIM = 128
MLA_WIDTH = D_MODEL - RWKV_WIDTH - GMLP_WIDTH
MLA_HEADS = MLA_WIDTH // MLA_V_DIM
MLA_NOPE_DIM = 128
MLA_ROPE_DIM = 64
MLA_QK_DIM = MLA_NOPE_DIM + MLA_ROPE_DIM
MLA_Q_RANK = 3 * D_MODEL // 16
MLA_KV_RANK = 512
MLA_Q_BLOCK = 128
ROPE_THETA = 10000.0

RWKV_BASE_SIZES = (RWKV_WIDTH, RWKV_WIDTH, RWKV_WIDTH, DECAY_LORA, AAA_LORA, GATE_LORA)
RWKV_BASE_COLS = sum(RWKV_BASE_SIZES)
GMLP_COLS = 2 * GMLP_WIDTH
IN_SIZES = (RWKV_BASE_COLS, GMLP_COLS, MLA_Q_RANK, MLA_KV_RANK, MLA_ROPE_DIM)
IN_COLS = sum(IN_SIZES)

N_EXPERTS = 64
TOP_K = 8
N_EXPERT_GROUPS = 8
TOPK_GROUPS = 4
D_EXPERT = 384
D_SHARED = 384
ROUTED_SCALE = 2.5
MOE_TOKEN_BLOCK = 1024

ALPHA = (2 * DEPTH) ** 0.25
BETA = (8 * DEPTH) ** -0.25
LN_EPS = 1e-5
RMS_EPS = 1e-6

kernel_name = 'hybrid_rwkv7_gmlp_mla_moe_deepnorm'


def _split(x, sizes):
    offsets = [int(o) for o in np.cumsum(sizes)[:-1]]
    return jnp.split(x, offsets, axis=-1)


def layer_norm(x, g, b):
    xf = x.astype(jnp.float32)
    mu = jnp.mean(xf, axis=-1, keepdims=True)
    var = jnp.mean(jnp.square(xf - mu), axis=-1, keepdims=True)
    return ((xf - mu) * lax.rsqrt(var + LN_EPS)).astype(x.dtype) * g + b


def rms_norm(x, g):
    xf = x.astype(jnp.float32)
    ms = jnp.mean(jnp.square(xf), axis=-1, keepdims=True)
    return (xf * lax.rsqrt(ms + RMS_EPS)).astype(x.dtype) * g


def token_shift(p):
    return jnp.pad(p, ((0, 0), (1, 0), (0, 0)))[:, :-1]


def apply_rope(x, cos, sin):
    half = x.shape[-1] // 2
    x1, x2 = x[..., :half], x[..., half:]
    return jnp.concatenate([x1 * cos - x2 * sin, x2 * cos + x1 * sin], axis=-1)


def wkv7_scan(r, w, k, v, a_vec, b_vec):
    def step(state, inp):
        r_t, w_t, k_t, v_t, a_t, b_t = inp
        sa = jnp.einsum('bhvk,bhk->bhv', state, a_t)
        state = (state * w_t[:, :, None, :] + sa[..., None] * b_t[:, :, None, :]
                 + v_t[..., None] * k_t[:, :, None, :])
        return state, jnp.einsum('bhvk,bhk->bhv', state, r_t)
    B, S, H, N = r.shape
    xs = tuple(jnp.moveaxis(t, 1, 0) for t in (r, w, k, v, a_vec, b_vec))
    s0 = jnp.zeros((B, H, N, N), jnp.float32)
    _, ys = lax.scan(step, s0, xs)
    return jnp.moveaxis(ys, 0, 1)


def rwkv7_group(p, mu, w0, w2, a0, a2, g2, k_k, k_a, r_k, ln_g, ln_b,
                v_first, v_lo, v0, v2):
    B, S, _ = p.shape
    H, N = RWKV_HEADS, RWKV_HEAD_DIM
    dt = p.dtype
    p = p + (token_shift(p) - p) * mu
    r, k, v, w_lo, a_lo, g_lo = _split(p, RWKV_BASE_SIZES)
    log_w = -jax.nn.softplus(-(w0 + jnp.tanh(w_lo) @ w2)) - 0.5
    decay = jnp.exp(-jnp.exp(log_w.astype(jnp.float32)))
    a = jax.nn.sigmoid(a0 + a_lo @ a2)
    g = jax.nn.sigmoid(g_lo) @ g2
    if v_first is None:
        v_first = v
    else:
        v = v + (v_first - v) * jax.nn.sigmoid(v0 + v_lo @ v2)
    hv = lambda t: t.reshape(B, S, H, N)
    r, k, v, a, decay = hv(r), hv(k), hv(v), hv(a), hv(decay)
    kk = (k * k_k.reshape(H, N)).astype(jnp.float32)
    kk = kk * lax.rsqrt(jnp.maximum(jnp.sum(jnp.square(kk), axis=-1, keepdims=True), 1e-24))
    k = k * (1 + (a - 1) * k_a.reshape(H, N))
    y = wkv7_scan(r.astype(jnp.float32), decay, k.astype(jnp.float32), v.astype(jnp.float32),
                  -kk, kk * a.astype(jnp.float32))
    m = jnp.mean(y, axis=-1, keepdims=True)
    var = jnp.mean(jnp.square(y - m), axis=-1, keepdims=True)
    y = ((y - m) * lax.rsqrt(var + GN_EPS)).astype(dt).reshape(B, S, H * N) * ln_g + ln_b
    bonus = jnp.sum(r * k * r_k, axis=-1, keepdims=True) * v
    y = (y + bonus.reshape(B, S, H * N)) * g
    return y, v_first


def gmlp_group(p, ln_g, ln_b, w_s, b_s):
    B, S, _ = p.shape
    z = jax.nn.gelu(p)
    u, v = z[..., :GMLP_WIDTH], z[..., GMLP_WIDTH:]
    v = layer_norm(v, ln_g, ln_b)
    nc = S // GMLP_CHUNK
    v = v.reshape(B, nc, GMLP_CHUNK, GMLP_GROUPS, GMLP_GROUP_DIM)
    causal = jnp.tril(jnp.ones((GMLP_CHUNK, GMLP_CHUNK), dtype=bool))
    w = jnp.where(causal[None], w_s, 0)
    f = jnp.einsum('gts,bnsgc->bntgc', w, v) + b_s.T[:, :, None]
    return u * f.reshape(B, S, GMLP_WIDTH)


def mla_group(c_q, c_kv, k_rope, positions, q_norm, kv_norm, w_uq, w_ukv):
    B, S, _ = c_q.shape
    H = MLA_HEADS
    dt = c_q.dtype
    q = (rms_norm(c_q, q_norm) @ w_uq).reshape(B, S, H, MLA_QK_DIM)
    q_nope, q_rope = q[..., :MLA_NOPE_DIM], q[..., MLA_NOPE_DIM:]
    kv = (rms_norm(c_kv, kv_norm) @ w_ukv).reshape(B, S, H, MLA_NOPE_DIM + MLA_V_DIM)
    k_nope, v = kv[..., :MLA_NOPE_DIM], kv[..., MLA_NOPE_DIM:]
    inv_freq = jnp.power(ROPE_THETA, -jnp.arange(0, MLA_ROPE_DIM, 2, dtype=jnp.float32) / MLA_ROPE_DIM)
    ang = positions.astype(jnp.float32)[..., None] * inv_freq
    cos, sin = jnp.cos(ang).astype(dt), jnp.sin(ang).astype(dt)
    q_rope = apply_rope(q_rope, cos[:, :, None], sin[:, :, None])
    k_rope = apply_rope(k_rope, cos, sin)
    scale = MLA_QK_DIM ** -0.5
    nb = S // MLA_Q_BLOCK
    qn_b = q_nope.reshape(B, nb, MLA_Q_BLOCK, H, MLA_NOPE_DIM).swapaxes(0, 1)
    qr_b = q_rope.reshape(B, nb, MLA_Q_BLOCK, H, MLA_ROPE_DIM).swapaxes(0, 1)
    key_idx = jnp.arange(S)

    def attend(args):
        qn, qr, i = args
        s = (jnp.einsum('bqhd,bkhd->bhqk', qn, k_nope)
             + jnp.einsum('bqhd,bkd->bhqk', qr, k_rope)).astype(jnp.float32) * scale
        q_idx = i * MLA_Q_BLOCK + jnp.arange(MLA_Q_BLOCK)
        s = jnp.where(key_idx[None, :] <= q_idx[:, None], s, -jnp.inf)
        pr = jax.nn.softmax(s, axis=-1).astype(dt)
        return jnp.einsum('bhqk,bkhd->bqhd', pr, v)

    out = lax.map(attend, (qn_b, qr_b, jnp.arange(nb)))
    return out.swapaxes(0, 1).reshape(B, S, H * MLA_V_DIM)


def moe_ffn(h, router_w, router_bias, w1, w3, w2, sw1, sw3, sw2):
    B, S, D = h.shape
    T = B * S
    t = h.reshape(T, D)
    scores = jax.nn.sigmoid((t @ router_w).astype(jnp.float32))
    biased = scores + router_bias.astype(jnp.float32)
    per_group = N_EXPERTS // N_EXPERT_GROUPS
    grp_score = lax.top_k(biased.reshape(T, N_EXPERT_GROUPS, per_group), 2)[0].sum(-1)
    _, top_groups = lax.top_k(grp_score, TOPK_GROUPS)
    group_mask = jax.nn.one_hot(top_groups, N_EXPERT_GROUPS, dtype=jnp.float32).sum(-2) > 0
    expert_mask = jnp.repeat(group_mask, per_group, axis=-1)
    _, idx = lax.top_k(jnp.where(expert_mask, biased, -jnp.inf), TOP_K)
    sel = jnp.take_along_axis(scores, idx, axis=-1)
    gates = sel / jnp.sum(sel, axis=-1, keepdims=True) * ROUTED_SCALE
    gate_dense = jnp.einsum('tk,tke->te', gates, jax.nn.one_hot(idx, N_EXPERTS, dtype=jnp.float32)).astype(h.dtype)
    blk = math.gcd(T, MOE_TOKEN_BLOCK)

    def expert_block(args):
        tb, gb = args
        hg = jnp.einsum('td,edf->tef', tb, w1)
        hu = jnp.einsum('td,edf->tef', tb, w3)
        act = jax.nn.silu(hg) * hu * gb[..., None]
        return jnp.einsum('tef,efd->td', act, w2)

    routed = lax.map(expert_block, (t.reshape(T // blk, blk, D),
                                    gate_dense.reshape(T // blk, blk, N_EXPERTS))).reshape(T, D)
    shared = (jax.nn.silu(t @ sw1) * (t @ sw3)) @ sw2
    return (routed + shared).reshape(B, S, D)


def setup_inputs(seed: int = 0) -> dict:
    key = jax.random.key(seed)
    ks = iter(jax.random.split(key, 48))
    f32 = jnp.float32

    def nrm(shape, scale):
        return jax.random.normal(next(ks), shape, f32) * scale

    def unif(shape, lo, hi):
        return jax.random.uniform(next(ks), shape, f32, lo, hi)

    L, D, HN = DEPTH, D_MODEL, RWKV_WIDTH
    x = nrm((BATCH, SEQ, D), 1.0)
    offset = jax.random.randint(next(ks), (BATCH, 1), 0, 4096, dtype=jnp.int32)
    positions = offset + jnp.arange(SEQ, dtype=jnp.int32)[None, :]
    return dict(
        x=x,
        positions=positions,
        w_in_first=nrm((D, IN_COLS), D ** -0.5),
        w_in_rest=nrm((L - 1, D, IN_COLS + MV_LORA), D ** -0.5),
        rwkv_mu=unif((L, RWKV_BASE_COLS), 0.0, 1.0),
        rwkv_mu_vres=unif((L - 1, MV_LORA), 0.0, 1.0),
        rwkv_w0=unif((L, HN), -6.0, 0.0),
        rwkv_w2=nrm((L, DECAY_LORA, HN), 0.5 * DECAY_LORA ** -0.5),
        rwkv_a0=nrm((L, HN), 0.5),
        rwkv_a2=nrm((L, AAA_LORA, HN), 0.5 * AAA_LORA ** -0.5),
        rwkv_v0=nrm((L - 1, HN), 0.5),
        rwkv_v2=nrm((L - 1, MV_LORA, HN), 0.5 * MV_LORA ** -0.5),
        rwkv_g2=nrm((L, GATE_LORA, HN), GATE_LORA ** -0.5),
        rwkv_k_k=0.85 + nrm((L, HN), 0.1),
        rwkv_k_a=1.0 + nrm((L, HN), 0.1),
        rwkv_r_k=nrm((L, RWKV_HEADS, RWKV_HEAD_DIM), 0.1),
        rwkv_ln_g=1.0 + nrm((L, HN), 0.02),
        rwkv_ln_b=nrm((L, HN), 0.02),
        gmlp_ln_g=1.0 + nrm((L, GMLP_WIDTH), 0.02),
        gmlp_ln_b=nrm((L, GMLP_WIDTH), 0.02),
        gmlp_w_s=nrm((L, GMLP_GROUPS, GMLP_CHUNK, GMLP_CHUNK), GMLP_CHUNK ** -0.5),
        gmlp_b_s=1.0 + nrm((L, GMLP_GROUPS, GMLP_CHUNK), 0.1),
        mla_q_norm=1.0 + nrm((L, MLA_Q_RANK), 0.02),
        mla_kv_norm=1.0 + nrm((L, MLA_KV_RANK), 0.02),
        mla_w_uq=nrm((L, MLA_Q_RANK, MLA_HEADS * MLA_QK_DIM), MLA_Q_RANK ** -0.5),
        mla_w_ukv=nrm((L, MLA_KV_RANK, MLA_HEADS * (MLA_NOPE_DIM + MLA_V_DIM)), MLA_KV_RANK ** -0.5),
        w_out=nrm((L, D, D), BETA * D ** -0.5),
        ln1_g=1.0 + nrm((L, D), 0.02),
        ln1_b=nrm((L, D), 0.02),
        router_w=nrm((L, D, N_EXPERTS), D ** -0.5),
        router_bias=nrm((L, N_EXPERTS), 0.01),
        exp_w1=nrm((L, N_EXPERTS, D, D_EXPERT), D ** -0.5),
        exp_w3=nrm((L, N_EXPERTS, D, D_EXPERT), D ** -0.5),
        exp_w2=nrm((L, N_EXPERTS, D_EXPERT, D), BETA * D_EXPERT ** -0.5),
        shared_w1=nrm((L, D, D_SHARED), D ** -0.5),
        shared_w3=nrm((L, D, D_SHARED), D ** -0.5),
        shared_w2=nrm((L, D_SHARED, D), BETA * D_SHARED ** -0.5),
        ln2_g=1.0 + nrm((L, D), 0.02),
        ln2_b=nrm((L, D), 0.02),
    )


def reference(x, positions, w_in_first, w_in_rest, rwkv_mu, rwkv_mu_vres, rwkv_w0, rwkv_w2,
              rwkv_a0, rwkv_a2, rwkv_v0, rwkv_v2, rwkv_g2, rwkv_k_k, rwkv_k_a, rwkv_r_k,
              rwkv_ln_g, rwkv_ln_b, gmlp_ln_g, gmlp_ln_b, gmlp_w_s, gmlp_b_s,
              mla_q_norm, mla_kv_norm, mla_w_uq, mla_w_ukv, w_out, ln1_g, ln1_b,
              router_w, router_bias, exp_w1, exp_w3, exp_w2, shared_w1, shared_w3, shared_w2,
              ln2_g, ln2_b):
    v_first = None
    for l in range(DEPTH):
        w_in = w_in_first if l == 0 else w_in_rest[l - 1]
        proj = x @ w_in
        p_rwkv, p_gmlp, c_q, c_kv, k_rope = _split(proj[..., :IN_COLS], IN_SIZES)
        if l == 0:
            v_lo, v0, v2 = None, None, None
        else:
            v_lo = proj[..., IN_COLS:]
            v_lo = v_lo + (token_shift(v_lo) - v_lo) * rwkv_mu_vres[l - 1]
            v0, v2 = rwkv_v0[l - 1], rwkv_v2[l - 1]
        y_a, v_first = rwkv7_group(p_rwkv, rwkv_mu[l], rwkv_w0[l], rwkv_w2[l], rwkv_a0[l], rwkv_a2[l],
                                   rwkv_g2[l], rwkv_k_k[l], rwkv_k_a[l], rwkv_r_k[l],
                                   rwkv_ln_g[l], rwkv_ln_b[l], v_first, v_lo, v0, v2)
        y_b = gmlp_group(p_gmlp, gmlp_ln_g[l], gmlp_ln_b[l], gmlp_w_s[l], gmlp_b_s[l])
        y_c = mla_group(c_q, c_kv, k_rope, positions, mla_q_norm[l], mla_kv_norm[l],
                        mla_w_uq[l], mla_w_ukv[l])
        mix = jnp.concatenate([y_a, y_b, y_c], axis=-1) @ w_out[l]
        h = layer_norm(ALPHA * x + mix, ln1_g[l], ln1_b[l])
        ffn = moe_ffn(h, router_w[l], router_bias[l], exp_w1[l], exp_w3[l], exp_w2[l],
                      shared_w1[l], shared_w3[l], shared_w2[l])
        x = layer_norm(ALPHA * h + ffn, ln2_g[l], ln2_b[l])
    return x
```

```python
import functools

import jax
import jax.numpy as jnp
import numpy as np
from jax import lax
from jax.experimental import pallas as pl
from jax.experimental.pallas import tpu as pltpu

F32 = jnp.float32
BF16 = jnp.bfloat16

D_MODEL = 4096
DEPTH = 2
RWKV_HEAD_DIM = 64
RWKV_WIDTH = 3 * D_MODEL // 8
DECAY_LORA = 128
AAA_LORA = 128
MV_LORA = 96
GATE_LORA = 480
GN_EPS = 64e-5
GMLP_WIDTH = D_MODEL // 4
GMLP_GROUP_DIM = 128
GMLP_GROUPS = GMLP_WIDTH // GMLP_GROUP_DIM
GMLP_CHUNK = 128
MLA_V_DIM = 128
MLA_WIDTH = D_MODEL - RWKV_WIDTH - GMLP_WIDTH
MLA_HEADS = MLA_WIDTH // MLA_V_DIM
MLA_NOPE_DIM = 128
MLA_ROPE_DIM = 64
MLA_QK_DIM = MLA_NOPE_DIM + MLA_ROPE_DIM
MLA_Q_RANK = 768
MLA_KV_RANK = 512
ROPE_THETA = 10000.0
N_EXPERTS = 64
TOP_K = 8
N_EXPERT_GROUPS = 8
TOPK_GROUPS = 4
D_EXPERT = 384
ROUTED_SCALE = 2.5
ALPHA = (2 * DEPTH) ** 0.25
LN_EPS = 1e-5
RMS_EPS = 1e-6

LANES = 128
SUBLANES = 8
VMEM_LIMIT = 56 * 1024 * 1024

P_GU, P_GV = 0, 1024
P_R, P_K, P_V = 2048, 3584, 5120
P_WLO, P_ALO = 6656, 6784
P_CQ, P_CKV = 6912, 7680
P_GLO, P_KROPE, P_VLO = 8192, 8704, 8832
P_COLS = 8960
GLO_PAD = 512
RW_CH = 512
RW_CHUNK = 64


def _cparams(sem):
    return pltpu.CompilerParams(dimension_semantics=sem, vmem_limit_bytes=VMEM_LIMIT)


def _sigmoid(x):
    return 1.0 / (1.0 + jnp.exp(-x))


def _dot(a, b):
    return jnp.dot(a, b, preferred_element_type=F32)


def _dot_nt(a, b):
    return lax.dot_general(a, b, (((1,), (1,)), ((), ())), preferred_element_type=F32)


def _dot_tn(a, b):
    return lax.dot_general(a, b, (((0,), (0,)), ((), ())), preferred_element_type=F32)


def _split_bf16(x):
    hi = x.astype(BF16)
    lo = (x - hi.astype(F32)).astype(BF16)
    return hi, lo


def _mm_kernel(x_ref, w_ref, o_ref):
    o_ref[...] = _dot(x_ref[...], w_ref[...]).astype(o_ref.dtype)


def _matmul(x, w, out_dtype, tm, tn):
    m, k = x.shape
    n = w.shape[1]
    assert m % tm == 0 and n % tn == 0
    return pl.pallas_call(
        _mm_kernel,
        grid=(n // tn, m // tm),
        in_specs=[pl.BlockSpec((tm, k), lambda j, i: (i, 0)),
                  pl.BlockSpec((k, tn), lambda j, i: (0, j))],
        out_specs=pl.BlockSpec((tm, tn), lambda j, i: (i, j)),
        out_shape=jax.ShapeDtypeStruct((m, n), out_dtype),
        compiler_params=_cparams(("arbitrary", "arbitrary")),
        name="matmul",
    )(x, w)


def _res_ln_kernel(res_ref, y_ref, g_ref, b_ref, of_ref, ob_ref):
    t = ALPHA * res_ref[...] + y_ref[...]
    mu = jnp.mean(t, axis=-1, keepdims=True)
    c = t - mu
    var = jnp.mean(c * c, axis=-1, keepdims=True)
    out = c * lax.rsqrt(var + LN_EPS) * g_ref[...] + b_ref[...]
    of_ref[...] = out
    ob_ref[...] = out.astype(BF16)


def _res_ln(res, y, g, b, tm=256):
    m, d = res.shape
    row = pl.BlockSpec((tm, d), lambda i: (i, 0))
    vec = pl.BlockSpec((1, d), lambda i: (0, 0))
    return pl.pallas_call(
        _res_ln_kernel,
        grid=(m // tm,),
        in_specs=[row, row, vec, vec],
        out_specs=[row, row],
        out_shape=[jax.ShapeDtypeStruct((m, d), F32), jax.ShapeDtypeStruct((m, d), BF16)],
        compiler_params=_cparams(("arbitrary",)),
        name="res_ln",
    )(res, y, g.reshape(1, d), b.reshape(1, d))


def _shift_mix(cur, prev8, mu, is_first):
    prev_row = jnp.where(is_first, 0.0, prev8[SUBLANES - 1:SUBLANES, :])
    rolled = pltpu.roll(cur, 1, 0)
    row = lax.broadcasted_iota(jnp.int32, cur.shape, 0)
    shifted = jnp.where(row == 0, prev_row, rolled)
    return cur + (shifted - cur) * mu


def _head_block_ones(width):
    r = lax.broadcasted_iota(jnp.int32, (width, width), 0) // RWKV_HEAD_DIM
    c = lax.broadcasted_iota(jnp.int32, (width, width), 1) // RWKV_HEAD_DIM
    return jnp.where(r == c, 1.0, 0.0).astype(BF16)


def _head_sum(x, ones_bd):
    w = ones_bd.shape[0]
    outs = []
    for c in range(x.shape[1] // w):
        hi, lo = _split_bf16(x[:, c * w:(c + 1) * w])
        outs.append(_dot(hi, ones_bd) + _dot(lo, ones_bd))
    return jnp.concatenate(outs, axis=1)


def _rwkv_prep_body(first, r_ref, k_ref, v_ref, rp_ref, kp_ref, vp_ref,
                    wlo_ref, alo_ref, glo_ref, wlop_ref, alop_ref, glop_ref,
                    mur_ref, muk_ref, muv_ref, muw_ref, mua_ref, mug_ref,
                    w2_ref, a2_ref, g2_ref, w0_ref, a0_ref, kk_ref, ka_ref, rk_ref, vres):
    r = _shift_mix(r_ref[...], rp_ref[...], mur_ref[...], first)
    k = _shift_mix(k_ref[...], kp_ref[...], muk_ref[...], first)
    v = _shift_mix(v_ref[...], vp_ref[...], muv_ref[...], first)
    w_lo = _shift_mix(wlo_ref[...], wlop_ref[...], muw_ref[...], first)
    a_lo = _shift_mix(alo_ref[...], alop_ref[...], mua_ref[...], first)
    g_lo = _shift_mix(glo_ref[...], glop_ref[...], mug_ref[...], first)

    z = w0_ref[...] + _dot(jnp.tanh(w_lo).astype(BF16), w2_ref[...])
    nz = -z
    softplus = jnp.maximum(nz, 0.0) + jnp.log(1.0 + jnp.exp(-jnp.abs(nz)))
    log_w = -softplus - 0.5
    ld = -jnp.exp(log_w)
    a = _sigmoid(a0_ref[...] + _dot(a_lo.astype(BF16), a2_ref[...]))
    g = _dot(_sigmoid(g_lo).astype(BF16), g2_ref[...])
    if vres is not None:
        vlo_ref, vlop_ref, muvl_ref, v2_ref, v0_ref, vf_ref = vres
        v_lo = _shift_mix(vlo_ref[...], vlop_ref[...], muvl_ref[...], first)
        mix = _sigmoid(v0_ref[...] + _dot(v_lo.astype(BF16), v2_ref[...]))
        v = v + (vf_ref[...] - v) * mix

    ones_bd = _head_block_ones(2 * LANES)
    kk = k * kk_ref[...]
    ss = _head_sum(kk * kk, ones_bd)
    kn = kk * lax.rsqrt(jnp.maximum(ss, 1e-24))
    k_mod = k * (1.0 + (a - 1.0) * ka_ref[...])
    bonus = _head_sum(r * k_mod * rk_ref[...], ones_bd) * v
    return r, ld, k_mod, v, kn, kn * a, g, bonus


def _make_rwkv_prep_kernel(has_vres):
    n_common = 26

    def kern(*refs):
        common = refs[:n_common]
        if has_vres:
            vres = refs[n_common:n_common + 6]
            outs = refs[n_common + 6:]
        else:
            vres = None
            outs = refs[n_common:]
        first = pl.program_id(0) == 0
        vals = _rwkv_prep_body(first, *common, vres)
        for o_ref, val in zip(outs, vals):
            o_ref[...] = val

    return kern


def _rwkv_prep(proj, mu_p, p, l, v_first, tb):
    t = proj.shape[0]
    has_vres = v_first is not None
    nj = RWKV_WIDTH // RW_CH
    pb = tb // SUBLANES

    def cur(width, col0):
        return pl.BlockSpec((tb, width), lambda i, j: (i, col0 // width))

    def cur_j(col0):
        return pl.BlockSpec((tb, RW_CH), lambda i, j: (i, col0 // RW_CH + j))

    def prev(width, col0):
        return pl.BlockSpec((SUBLANES, width), lambda i, j: (jnp.maximum(i * pb - 1, 0), col0 // width))

    def prev_j(col0):
        return pl.BlockSpec((SUBLANES, RW_CH), lambda i, j: (jnp.maximum(i * pb - 1, 0), col0 // RW_CH + j))

    def vec(width, col0):
        return pl.BlockSpec((1, width), lambda i, j: (0, col0 // width))

    def vec_j(col0=0):
        return pl.BlockSpec((1, RW_CH), lambda i, j: (0, col0 // RW_CH + j))

    def lora(rank):
        return pl.BlockSpec((rank, RW_CH), lambda i, j: (0, j))

    row = lambda a: a.reshape(1, -1)
    g2 = jnp.pad(p["rwkv_g2"][l], ((0, GLO_PAD - GATE_LORA), (0, 0))).astype(BF16)
    args = [proj, proj, proj, proj, proj, proj,
            proj, proj, proj, proj, proj, proj,
            mu_p, mu_p, mu_p, mu_p, mu_p, mu_p,
            p["rwkv_w2"][l].astype(BF16), p["rwkv_a2"][l].astype(BF16), g2,
            row(p["rwkv_w0"][l]), row(p["rwkv_a0"][l]), row(p["rwkv_k_k"][l]), row(p["rwkv_k_a"][l]),
            row(p["rwkv_r_k"][l])]
    in_specs = [cur_j(P_R), cur_j(P_K), cur_j(P_V), prev_j(P_R), prev_j(P_K), prev_j(P_V),
                cur(LANES, P_WLO), cur(LANES, P_ALO), cur(GLO_PAD, P_GLO),
                prev(LANES, P_WLO), prev(LANES, P_ALO), prev(GLO_PAD, P_GLO),
                vec_j(P_R), vec_j(P_K), vec_j(P_V), vec(LANES, P_WLO), vec(LANES, P_ALO), vec(GLO_PAD, P_GLO),
                lora(DECAY_LORA), lora(AAA_LORA), lora(GLO_PAD),
                vec_j(), vec_j(), vec_j(), vec_j(), vec_j()]
    if has_vres:
        v2 = jnp.pad(p["rwkv_v2"][l - 1], ((0, LANES - MV_LORA), (0, 0))).astype(BF16)
        args += [proj, proj, mu_p, v2, row(p["rwkv_v0"][l - 1]), v_first]
        in_specs += [cur(LANES, P_VLO), prev(LANES, P_VLO), vec(LANES, P_VLO), lora(LANES), vec_j(),
                     pl.BlockSpec((tb, RW_CH), lambda i, j: (i, j))]
    out_spec = pl.BlockSpec((tb, RW_CH), lambda i, j: (i, j))
    out_sds = jax.ShapeDtypeStruct((t, RWKV_WIDTH), F32)
    return pl.pallas_call(
        _make_rwkv_prep_kernel(has_vres),
        grid=(t // tb, nj),
        in_specs=in_specs,
        out_specs=[out_spec] * 8,
        out_shape=[out_sds] * 8,
        compiler_params=_cparams(("arbitrary", "arbitrary")),
        name="rwkv_prep",
    )(*args)


def _unit_lower_inverse(a_strict, c):
    row = lax.broadcasted_iota(jnp.int32, (c, c), 0)
    col = lax.broadcasted_iota(jnp.int32, (c, c), 1)
    eye = jnp.where(row == col, 1.0, 0.0)
    bd = lambda x: x.astype(BF16)
    base = SUBLANES
    d1 = jnp.where(row // base == col // base, a_strict, 0.0)
    d2 = _dot(bd(d1), bd(d1))
    d4 = _dot(bd(d2), bd(d2))
    inv = eye + d1
    inv = inv + _dot(bd(d2), bd(inv))
    inv = inv + _dot(bd(d4), bd(inv))
    blk = base
    while blk < c:
        off = jnp.where((row // (2 * blk) == col // (2 * blk)) & (row // blk != col // blk), a_strict, 0.0)
        inv = inv + _dot(bd(inv), bd(_dot(bd(off), bd(inv))))
        blk *= 2
    return inv


def _rwkv_scan_kernel(r_ref, ld_ref, k_ref, v_ref, kn_ref, b_ref, g_ref, bonus_ref, lng_ref, lnb_ref,
                      y_ref, state_ref):
    c = RW_CHUNK
    n = RWKV_HEAD_DIM
    heads = RW_CH // n

    @pl.when(pl.program_id(1) == 0)
    def _():
        state_ref[...] = jnp.zeros_like(state_ref)

    row = lax.broadcasted_iota(jnp.int32, (c, c), 0)
    col = lax.broadcasted_iota(jnp.int32, (c, c), 1)
    tri_incl = jnp.where(row >= col, 1.0, 0.0).astype(BF16)
    strict = row > col
    incl = row >= col

    ld = ld_ref[...]
    ld_hi, ld_lo = _split_bf16(ld)
    cum = _dot(tri_incl, ld_hi) + _dot(tri_incl, ld_lo)
    g_inc = jnp.exp(cum)
    g_inv = jnp.exp(-cum)
    a_t = -kn_ref[...] * jnp.exp(cum - ld)
    b_t = b_ref[...] * g_inv
    k_t = k_ref[...] * g_inv
    r_t = r_ref[...] * g_inc
    g_last = g_inc[c - 1:c, :]
    v_all = v_ref[...]

    ys = []
    for h in range(heads):
        sl = slice(h * n, (h + 1) * n)
        v_h = v_all[:, sl]
        ar = jnp.concatenate([a_t[:, sl], r_t[:, sl]], axis=0).astype(BF16)
        bk = jnp.concatenate([b_t[:, sl], k_t[:, sl]], axis=0).astype(BF16)
        p1 = _dot_nt(ar, bk)
        a_ab = jnp.where(strict, p1[:c, :c], 0.0)
        a_ak = jnp.where(strict, p1[:c, c:], 0.0)
        a_rb = jnp.where(incl, p1[c:, :c], 0.0)
        a_rk = jnp.where(incl, p1[c:, c:], 0.0)
        inv = _unit_lower_inverse(a_ab, c)
        rhs = jnp.concatenate([a_t[:, sl], _dot(a_ak.astype(BF16), v_h.astype(BF16))], axis=1)
        sol = _dot(inv.astype(BF16), rhs.astype(BF16))
        s0 = state_ref[h].astype(BF16)
        u = _dot_nt(sol[:, :n].astype(BF16), s0) + sol[:, n:]
        uv = jnp.concatenate([u, v_h], axis=0).astype(BF16)
        y = (_dot_nt(r_t[:, sl].astype(BF16), s0)
             + _dot(jnp.concatenate([a_rb, a_rk], axis=1).astype(BF16), uv))
        state_ref[h] = (state_ref[h] + _dot_tn(uv, bk)) * g_last[:, sl]
        ys.append(y)
    y = jnp.concatenate(ys, axis=1)

    ones_bd = _head_block_ones(2 * LANES)
    mean = _head_sum(y, ones_bd) * (1.0 / n)
    yc = y - mean
    var = _head_sum(yc * yc, ones_bd) * (1.0 / n)
    yn = yc * lax.rsqrt(var + GN_EPS) * lng_ref[...] + lnb_ref[...]
    y_ref[...] = ((yn + bonus_ref[...]) * g_ref[...]).astype(y_ref.dtype)


def _rwkv_scan(prep, ln_g, ln_b):
    t = prep[0].shape[0]
    c = RW_CHUNK
    blk = pl.BlockSpec((c, RW_CH), lambda j, i: (i, j))
    vec = pl.BlockSpec((1, RW_CH), lambda j, i: (0, j))
    return pl.pallas_call(
        _rwkv_scan_kernel,
        grid=(RWKV_WIDTH // RW_CH, t // c),
        in_specs=[blk] * 8 + [vec, vec],
        out_specs=blk,
        out_shape=jax.ShapeDtypeStruct((t, RWKV_WIDTH), BF16),
        scratch_shapes=[pltpu.VMEM((RW_CH // RWKV_HEAD_DIM, RWKV_HEAD_DIM, RWKV_HEAD_DIM), F32)],
        compiler_params=_cparams(("arbitrary", "arbitrary")),
        name="rwkv_scan",
    )(*prep, ln_g.reshape(1, -1), ln_b.reshape(1, -1))


def _gelu_tanh(x):
    return 0.5 * x * (1.0 + jnp.tanh(np.sqrt(2.0 / np.pi).astype(np.float32) * (x + 0.044715 * (x * x * x))))


def _gmlp_kernel(u_ref, v_ref, lng_ref, lnb_ref, ws_ref, bs_ref, o_ref, *, chunks):
    ch = GMLP_CHUNK
    gd = GMLP_GROUP_DIM
    u = _gelu_tanh(u_ref[...])
    v = _gelu_tanh(v_ref[...])
    mu = jnp.mean(v, axis=-1, keepdims=True)
    vc = v - mu
    var = jnp.mean(vc * vc, axis=-1, keepdims=True)
    vn = (vc * lax.rsqrt(var + LN_EPS) * lng_ref[...] + lnb_ref[...]).astype(BF16)
    row = lax.broadcasted_iota(jnp.int32, (ch, ch), 0)
    col = lax.broadcasted_iota(jnp.int32, (ch, ch), 1)
    causal = row >= col
    bs = bs_ref[...]
    for g in range(GMLP_GROUPS):
        w = jnp.where(causal, ws_ref[g], 0.0).astype(BF16)
        bias = bs[:, g:g + 1]
        for n in range(chunks):
            f = _dot(w, vn[n * ch:(n + 1) * ch, g * gd:(g + 1) * gd]) + bias
            o_ref[n * ch:(n + 1) * ch, g * gd:(g + 1) * gd] = (
                u[n * ch:(n + 1) * ch, g * gd:(g + 1) * gd] * f).astype(o_ref.dtype)


def _gmlp(proj, ln_g, ln_b, w_s, b_s, tb=256):
    t = proj.shape[0]
    bs_t = jnp.pad(b_s.T, ((0, 0), (0, LANES - GMLP_GROUPS)))
    return pl.pallas_call(
        functools.partial(_gmlp_kernel, chunks=tb // GMLP_CHUNK),
        grid=(t // tb,),
        in_specs=[pl.BlockSpec((tb, GMLP_WIDTH), lambda i: (i, P_GU // GMLP_WIDTH)),
                  pl.BlockSpec((tb, GMLP_WIDTH), lambda i: (i, P_GV // GMLP_WIDTH)),
                  pl.BlockSpec((1, GMLP_WIDTH), lambda i: (0, 0)),
                  pl.BlockSpec((1, GMLP_WIDTH), lambda i: (0, 0)),
                  pl.BlockSpec((GMLP_GROUPS, GMLP_CHUNK, GMLP_CHUNK), lambda i: (0, 0, 0)),
                  pl.BlockSpec((GMLP_CHUNK, LANES), lambda i: (0, 0))],
        out_specs=pl.BlockSpec((tb, GMLP_WIDTH), lambda i: (i, 0)),
        out_shape=jax.ShapeDtypeStruct((t, GMLP_WIDTH), BF16),
        compiler_params=_cparams(("arbitrary",)),
        name="gmlp",
    )(proj, proj, ln_g.reshape(1, -1), ln_b.reshape(1, -1), w_s, bs_t)


def _rope_lanes(t, cc, s1, s2):
    return t * cc + pltpu.roll(t, LANES - MLA_ROPE_DIM // 2, 1) * s1 + pltpu.roll(t, MLA_ROPE_DIM // 2, 1) * s2


def _mla_proj_kernel(cq_ref, ckv_ref, kr_ref, qn_ref, kvn_ref, wq_ref, wk_ref, wvt_ref, cc_ref, s1_ref, s2_ref,
                     q_o, kn_o, vt_o, kr_o):
    cq = cq_ref[...]
    qn = (cq * lax.rsqrt(jnp.mean(cq * cq, axis=-1, keepdims=True) + RMS_EPS) * qn_ref[...]).astype(BF16)
    ckv = ckv_ref[...]
    kvn = (ckv * lax.rsqrt(jnp.mean(ckv * ckv, axis=-1, keepdims=True) + RMS_EPS) * kvn_ref[...]).astype(BF16)
    cc, s1, s2 = cc_ref[...], s1_ref[...], s2_ref[...]
    scale = MLA_QK_DIM ** -0.5
    for h in range(MLA_HEADS):
        q = _dot(qn, wq_ref[:, 2 * LANES * h:2 * LANES * (h + 1)]) * scale
        q_o[:, 2 * LANES * h:2 * LANES * h + LANES] = q[:, :LANES].astype(BF16)
        q_o[:, 2 * LANES * h + LANES:2 * LANES * (h + 1)] = _rope_lanes(q[:, LANES:], cc, s1, s2).astype(BF16)
    kn_o[...] = _dot(kvn, wk_ref[...]).astype(BF16)
    vt_o[...] = _dot_nt(wvt_ref[...], kvn).astype(BF16)
    kr_o[...] = _rope_lanes(kr_ref[...], cc, s1, s2).astype(BF16)


def _mla_proj(proj, positions, q_norm, kv_norm, w_uq, w_ukv, tm=512):
    t = proj.shape[0]
    h = MLA_HEADS
    half = MLA_ROPE_DIM // 2
    inv_freq = jnp.power(ROPE_THETA, -jnp.arange(0, MLA_ROPE_DIM, 2, dtype=F32) / MLA_ROPE_DIM)
    ang = positions.reshape(t).astype(F32)[:, None] * inv_freq
    cos, sin = jnp.cos(ang), jnp.sin(ang)
    z = jnp.zeros((t, half), F32)
    cc = jnp.concatenate([cos, cos, z, z], axis=1)
    s1 = jnp.concatenate([-sin, z, z, z], axis=1)
    s2 = jnp.concatenate([z, sin, z, z], axis=1)
    wq = w_uq.reshape(MLA_Q_RANK, h, MLA_QK_DIM)
    wq = jnp.pad(wq, ((0, 0), (0, 0), (0, 2 * LANES - MLA_QK_DIM))).reshape(MLA_Q_RANK, h * 2 * LANES).astype(BF16)
    wkv = w_ukv.reshape(MLA_KV_RANK, h, MLA_NOPE_DIM + MLA_V_DIM)
    wk = wkv[:, :, :MLA_NOPE_DIM].reshape(MLA_KV_RANK, h * MLA_NOPE_DIM).astype(BF16)
    wvt = wkv[:, :, MLA_NOPE_DIM:].reshape(MLA_KV_RANK, h * MLA_V_DIM).T.astype(BF16)
    full = lambda a: pl.BlockSpec(a.shape, lambda i: (0,) * a.ndim)
    tab = pl.BlockSpec((tm, LANES), lambda i: (i, 0))
    qn2, kvn2 = q_norm.reshape(1, -1), kv_norm.reshape(1, -1)
    return pl.pallas_call(
        _mla_proj_kernel,
        grid=(t // tm,),
        in_specs=[pl.BlockSpec((tm, MLA_Q_RANK), lambda i: (i, P_CQ // MLA_Q_RANK)),
                  pl.BlockSpec((tm, MLA_KV_RANK), lambda i: (i, P_CKV // MLA_KV_RANK)),
                  pl.BlockSpec((tm, LANES), lambda i: (i, P_KROPE // LANES)),
                  full(qn2), full(kvn2), full(wq), full(wk), full(wvt), tab, tab, tab],
        out_specs=[pl.BlockSpec((tm, h * 2 * LANES), lambda i: (i, 0)),
                   pl.BlockSpec((tm, h * MLA_NOPE_DIM), lambda i: (i, 0)),
                   pl.BlockSpec((h * MLA_V_DIM, tm), lambda i: (0, i)),
                   pl.BlockSpec((tm, LANES), lambda i: (i, 0))],
        out_shape=[jax.ShapeDtypeStruct((t, h * 2 * LANES), BF16),
                   jax.ShapeDtypeStruct((t, h * MLA_NOPE_DIM), BF16),
                   jax.ShapeDtypeStruct((h * MLA_V_DIM, t), BF16),
                   jax.ShapeDtypeStruct((t, LANES), BF16)],
        compiler_params=_cparams(("arbitrary",)),
        name="mla_proj",
    )(proj, proj, proj, qn2, kvn2, wq, wk, wvt, cc, s1, s2)


def _flash_kernel(qi_ref, kj_ref, q_ref, kn_ref, kr_ref, vt_ref, o_ref, m_sc, l_sc, acc_sc, *, tq, tk):
    s = pl.program_id(1)
    qi, kj = qi_ref[s], kj_ref[s]

    @pl.when(kj == 0)
    def _():
        m_sc[...] = jnp.full_like(m_sc, -jnp.inf)
        l_sc[...] = jnp.zeros_like(l_sc)
        acc_sc[...] = jnp.zeros_like(acc_sc)

    def step(masked):
        k = jnp.concatenate([kn_ref[...], kr_ref[...]], axis=1)
        st = _dot_nt(k, q_ref[...])
        if masked:
            key = kj * tk + lax.broadcasted_iota(jnp.int32, (tk, tq), 0)
            qry = qi * tq + lax.broadcasted_iota(jnp.int32, (tk, tq), 1)
            st = jnp.where(key <= qry, st, -jnp.inf)
        m_prev = m_sc[...]
        m_new = jnp.maximum(m_prev, jnp.max(st, axis=0, keepdims=True))
        alpha = jnp.exp(m_prev - m_new)
        p = jnp.exp(st - m_new)
        l_sc[...] = alpha * l_sc[...] + jnp.sum(p, axis=0, keepdims=True)
        acc_sc[...] = alpha * acc_sc[...] + _dot(vt_ref[...], p.astype(BF16))
        m_sc[...] = m_new

    last_key_of_block = kj * tk + tk - 1
    on_diag = last_key_of_block > qi * tq

    @pl.when(jnp.logical_not(on_diag))
    def _():
        step(False)

    @pl.when(on_diag)
    def _():
        step(True)

    @pl.when(last_key_of_block >= qi * tq + tq - 1)
    def _():
        o_ref[...] = (acc_sc[...] / l_sc[...]).T.astype(o_ref.dtype)


def _flash(q, kn, vt, kr, tq=512, tk=512):
    t = q.shape[0]
    assert tq % tk == 0
    pairs = [(i, j) for i in range(t // tq) for j in range((i + 1) * tq // tk)]
    qi = jnp.asarray([pr[0] for pr in pairs], jnp.int32)
    kj = jnp.asarray([pr[1] for pr in pairs], jnp.int32)
    grid_spec = pltpu.PrefetchScalarGridSpec(
        num_scalar_prefetch=2,
        grid=(MLA_HEADS, len(pairs)),
        in_specs=[pl.BlockSpec((tq, 2 * LANES), lambda h, s, qi, kj: (qi[s], h)),
                  pl.BlockSpec((tk, MLA_NOPE_DIM), lambda h, s, qi, kj: (kj[s], h)),
                  pl.BlockSpec((tk, LANES), lambda h, s, qi, kj: (kj[s], 0)),
                  pl.BlockSpec((MLA_V_DIM, tk), lambda h, s, qi, kj: (h, kj[s]))],
        out_specs=pl.BlockSpec((tq, MLA_V_DIM), lambda h, s, qi, kj: (qi[s], h)),
        scratch_shapes=[pltpu.VMEM((1, tq), F32), pltpu.VMEM((1, tq), F32), pltpu.VMEM((MLA_V_DIM, tq), F32)],
    )
    return pl.pallas_call(
        functools.partial(_flash_kernel, tq=tq, tk=tk),
        grid_spec=grid_spec,
        out_shape=jax.ShapeDtypeStruct((t, MLA_HEADS * MLA_V_DIM), BF16),
        compiler_params=_cparams(("arbitrary", "arbitrary")),
        name="flash",
    )(qi, kj, q, kn, kr, vt)


def _router_kernel(h_ref, rwt_ref, bias_ref, gate_ref):
    e = N_EXPERTS
    per = e // N_EXPERT_GROUPS
    h_hi, h_lo = _split_bf16(h_ref[...])
    w_hi, w_lo = _split_bf16(rwt_ref[...])
    logits = _dot_nt(w_hi, h_hi) + _dot_nt(w_hi, h_lo) + _dot_nt(w_lo, h_hi)
    scores = _sigmoid(logits)
    biased = scores + bias_ref[...][:, 0:1]
    tb = biased.shape[1]
    neg = -jnp.inf
    sub = lax.broadcasted_iota(jnp.int32, (per, tb), 0)
    grp_rows = []
    for g in range(N_EXPERT_GROUPS):
        blk = biased[g * per:(g + 1) * per, :]
        m1 = jnp.max(blk, axis=0, keepdims=True)
        first = jnp.min(jnp.where(blk == m1, sub, per), axis=0, keepdims=True)
        m2 = jnp.max(jnp.where(sub == first, neg, blk), axis=0, keepdims=True)
        grp_rows.append(m1 + m2)
    grp = jnp.concatenate(grp_rows, axis=0)
    gidx = lax.broadcasted_iota(jnp.int32, grp.shape, 0)
    grank = jnp.zeros(grp.shape, jnp.int32)
    for g in range(N_EXPERT_GROUPS):
        other = grp[g:g + 1, :]
        ahead = (other > grp) | ((other == grp) & (g < gidx))
        grank = grank + jnp.where(ahead, 1, 0)
    gsel = grank < TOPK_GROUPS
    masked = jnp.concatenate(
        [jnp.where(gsel[g:g + 1, :], biased[g * per:(g + 1) * per, :], neg) for g in range(N_EXPERT_GROUPS)], axis=0)
    eidx = lax.broadcasted_iota(jnp.int32, masked.shape, 0)
    rank = jnp.zeros(masked.shape, jnp.int32)
    for j in range(e):
        other = masked[j:j + 1, :]
        ahead = (other > masked) | ((other == masked) & (j < eidx))
        rank = rank + jnp.where(ahead, 1, 0)
    sel = jnp.where(rank < TOP_K, scores, 0.0)
    gate_ref[...] = sel / jnp.sum(sel, axis=0, keepdims=True) * ROUTED_SCALE


def _router(h, router_w, router_bias, tb=512):
    t, d = h.shape
    bias = jnp.broadcast_to(router_bias.astype(F32)[:, None], (N_EXPERTS, LANES))
    return pl.pallas_call(
        _router_kernel,
        grid=(t // tb,),
        in_specs=[pl.BlockSpec((tb, d), lambda i: (i, 0)),
                  pl.BlockSpec((N_EXPERTS, d), lambda i: (0, 0)),
                  pl.BlockSpec((N_EXPERTS, LANES), lambda i: (0, 0))],
        out_specs=pl.BlockSpec((N_EXPERTS, tb), lambda i: (0, i)),
        out_shape=jax.ShapeDtypeStruct((N_EXPERTS, t), F32),
        compiler_params=_cparams(("arbitrary",)),
        name="router",
    )(h, router_w.T, bias)


def _moe_kernel(x_ref, gate_ref, w1_ref, w3_ref, w2_ref, sw1_ref, sw3_ref, sw2_ref, o_ref):
    e = pl.program_id(1)

    @pl.when(e == 0)
    def _():
        o_ref[...] = jnp.zeros_like(o_ref)

    def ffn(w1, w3, w2, scale):
        x = x_ref[...]
        hg = _dot(x, w1)
        hu = _dot(x, w3)
        act = hg * _sigmoid(hg) * hu
        if scale is not None:
            act = act * scale
        o_ref[...] += _dot(act.astype(BF16), w2)

    @pl.when(e < N_EXPERTS)
    def _():
        gate = gate_ref[...]
        lane = lax.broadcasted_iota(jnp.int32, gate.shape, 1)
        gcol = jnp.sum(jnp.where(lane == e, gate, 0.0), axis=1, keepdims=True)
        ffn(w1_ref[0], w3_ref[0], w2_ref[0], gcol)

    @pl.when(e == N_EXPERTS)
    def _():
        ffn(sw1_ref[...], sw3_ref[...], sw2_ref[...], None)


def _moe(x_bf, gate, w1, w3, w2, sw1, sw3, sw2, tb=512):
    t, d = x_bf.shape
    ne = N_EXPERTS
    wspec_in = pl.BlockSpec((1, d, D_EXPERT), lambda i, e: (jnp.minimum(e, ne - 1), 0, 0))
    wspec_out = pl.BlockSpec((1, D_EXPERT, d), lambda i, e: (jnp.minimum(e, ne - 1), 0, 0))
    return pl.pallas_call(
        _moe_kernel,
        grid=(t // tb, ne + 1),
        in_specs=[pl.BlockSpec((tb, d), lambda i, e: (i, 0)),
                  pl.BlockSpec((tb, ne), lambda i, e: (i, 0)),
                  wspec_in, wspec_in, wspec_out,
                  pl.BlockSpec((d, D_EXPERT), lambda i, e: (0, 0)),
                  pl.BlockSpec((d, D_EXPERT), lambda i, e: (0, 0)),
                  pl.BlockSpec((D_EXPERT, d), lambda i, e: (0, 0))],
        out_specs=pl.BlockSpec((tb, d), lambda i, e: (i, 0)),
        out_shape=jax.ShapeDtypeStruct((t, d), F32),
        compiler_params=_cparams(("arbitrary", "arbitrary")),
        name="moe",
    )(x_bf, gate, w1, w3, w2, sw1, sw3, sw2)


def _prep_w_in(w_in, has_vres):
    d = w_in.shape[0]
    sizes = [RWKV_WIDTH, RWKV_WIDTH, RWKV_WIDTH, DECAY_LORA, AAA_LORA, GATE_LORA,
             GMLP_WIDTH, GMLP_WIDTH, MLA_Q_RANK, MLA_KV_RANK, MLA_ROPE_DIM]
    if has_vres:
        sizes.append(MV_LORA)
    offs = np.concatenate([[0], np.cumsum(sizes)])
    seg = [w_in[:, offs[i]:offs[i + 1]] for i in range(len(sizes))]
    r, k, v, w_lo, a_lo, g_lo, gu, gv, cq, ckv, kr = seg[:11]
    z = lambda n: jnp.zeros((d, n), w_in.dtype)
    v_lo = jnp.concatenate([seg[11], z(LANES - MV_LORA)], axis=1) if has_vres else z(LANES)
    cols = [gu, gv, r, k, v, w_lo, a_lo, cq, ckv, g_lo, z(GLO_PAD - GATE_LORA), kr, z(LANES - MLA_ROPE_DIM), v_lo]
    out = jnp.concatenate(cols, axis=1).astype(BF16)
    assert out.shape[1] == P_COLS
    return out


def _prep_mu(mu, mu_vres):
    out = jnp.zeros((1, P_COLS), F32)
    offs = np.concatenate([[0], np.cumsum([RWKV_WIDTH] * 3 + [DECAY_LORA, AAA_LORA, GATE_LORA])])
    for dst, i in zip([P_R, P_K, P_V, P_WLO, P_ALO, P_GLO], range(6)):
        out = lax.dynamic_update_slice(out, mu[offs[i]:offs[i + 1]].reshape(1, -1), (0, dst))
    if mu_vres is not None:
        out = lax.dynamic_update_slice(out, mu_vres.reshape(1, -1), (0, P_VLO))
    return out


def kernel(x, positions, w_in_first, w_in_rest, rwkv_mu, rwkv_mu_vres, rwkv_w0, rwkv_w2, rwkv_a0, rwkv_a2, rwkv_v0, rwkv_v2, rwkv_g2, rwkv_k_k, rwkv_k_a, rwkv_r_k, rwkv_ln_g, rwkv_ln_b, gmlp_ln_g, gmlp_ln_b, gmlp_w_s, gmlp_b_s, mla_q_norm, mla_kv_norm, mla_w_uq, mla_w_ukv, w_out, ln1_g, ln1_b, router_w, router_bias, exp_w1, exp_w3, exp_w2, shared_w1, shared_w3, shared_w2, ln2_g, ln2_b):
    b, s, d = x.shape
    t = b * s
    p = dict(rwkv_w0=rwkv_w0, rwkv_w2=rwkv_w2, rwkv_a0=rwkv_a0, rwkv_a2=rwkv_a2, rwkv_v0=rwkv_v0, rwkv_v2=rwkv_v2,
             rwkv_g2=rwkv_g2, rwkv_k_k=rwkv_k_k, rwkv_k_a=rwkv_k_a, rwkv_r_k=rwkv_r_k)
    xf = x.reshape(t, d)
    xb = xf.astype(BF16)
    v_first = None
    for l in range(DEPTH):
        has_vres = l > 0
        w_in = _prep_w_in(w_in_first if l == 0 else w_in_rest[l - 1], has_vres)
        mu_p = _prep_mu(rwkv_mu[l], rwkv_mu_vres[l - 1] if has_vres else None)
        proj = _matmul(xb, w_in, F32, tm=min(512, t), tn=1280)
        prep = _rwkv_prep(proj, mu_p, p, l, v_first, tb=min(256, t))
        if l == 0:
            v_first = prep[3]
        y_a = _rwkv_scan(prep, rwkv_ln_g[l], rwkv_ln_b[l])
        y_b = _gmlp(proj, gmlp_ln_g[l], gmlp_ln_b[l], gmlp_w_s[l], gmlp_b_s[l], tb=min(256, t))
        q, kn, vt, kr = _mla_proj(proj, positions, mla_q_norm[l], mla_kv_norm[l], mla_w_uq[l], mla_w_ukv[l],
                                  tm=min(512, t))
        y_c = _flash(q, kn, vt, kr, tq=min(512, t), tk=min(512, t))
        ycat = jnp.concatenate([y_a, y_b, y_c], axis=1)
        mix = _matmul(ycat, w_out[l].astype(BF16), F32, tm=min(512, t), tn=1024)
        hf, hb = _res_ln(xf, mix, ln1_g[l], ln1_b[l], tm=min(256, t))
        gate_t = _router(hf, router_w[l], router_bias[l], tb=min(512, t))
        ffn = _moe(hb, gate_t.T, exp_w1[l].astype(BF16), exp_w3[l].astype(BF16), exp_w2[l].astype(BF16),
                   shared_w1[l].astype(BF16), shared_w3[l].astype(BF16), shared_w2[l].astype(BF16), tb=min(256, t))
        xf, xb = _res_ln(hf, ffn, ln2_g[l], ln2_b[l], tm=min(256, t))
    return xf.reshape(b, s, d)
```

```python
import functools

import jax
import jax.numpy as jnp
import numpy as np
from jax import lax
from jax.experimental import pallas as pl
from jax.experimental.pallas import tpu as pltpu

F32 = jnp.float32
BF16 = jnp.bfloat16

D_MODEL = 4096
DEPTH = 2
RWKV_HEAD_DIM = 64
RWKV_WIDTH = 3 * D_MODEL // 8
DECAY_LORA = 128
AAA_LORA = 128
MV_LORA = 96
GATE_LORA = 480
GN_EPS = 64e-5
GMLP_WIDTH = D_MODEL // 4
GMLP_GROUP_DIM = 128
GMLP_GROUPS = GMLP_WIDTH // GMLP_GROUP_DIM
GMLP_CHUNK = 128
MLA_V_DIM = 128
MLA_WIDTH = D_MODEL - RWKV_WIDTH - GMLP_WIDTH
MLA_HEADS = MLA_WIDTH // MLA_V_DIM
MLA_NOPE_DIM = 128
MLA_ROPE_DIM = 64
MLA_QK_DIM = MLA_NOPE_DIM + MLA_ROPE_DIM
MLA_Q_RANK = 768
MLA_KV_RANK = 512
ROPE_THETA = 10000.0
N_EXPERTS = 64
TOP_K = 8
N_EXPERT_GROUPS = 8
TOPK_GROUPS = 4
D_EXPERT = 384
ROUTED_SCALE = 2.5
ALPHA = (2 * DEPTH) ** 0.25
LN_EPS = 1e-5
RMS_EPS = 1e-6

LANES = 128
SUBLANES = 8
VMEM_LIMIT = 56 * 1024 * 1024

P_GU, P_GV = 0, 1024
P_R, P_K, P_V = 2048, 3584, 5120
P_WLO, P_ALO = 6656, 6784
P_CQ, P_CKV = 6912, 7680
P_GLO, P_KROPE, P_VLO = 8192, 8704, 8832
P_COLS = 8960
GLO_PAD = 512
RW_CH = 512
RW_CHUNK = 64
MOE_ROW_BLOCK = 256


def _cparams(sem):
    return pltpu.CompilerParams(dimension_semantics=sem, vmem_limit_bytes=VMEM_LIMIT)


def _sigmoid(x):
    return 1.0 / (1.0 + jnp.exp(-x))


def _dot(a, b):
    return jnp.dot(a, b, preferred_element_type=F32)


def _dot_nt(a, b):
    return lax.dot_general(a, b, (((1,), (1,)), ((), ())), preferred_element_type=F32)


def _dot_tn(a, b):
    return lax.dot_general(a, b, (((0,), (0,)), ((), ())), preferred_element_type=F32)


def _split_bf16(x):
    hi = x.astype(BF16)
    lo = (x - hi.astype(F32)).astype(BF16)
    return hi, lo


def _mm_kernel(x_ref, w_ref, o_ref):
    o_ref[...] = _dot(x_ref[...], w_ref[...]).astype(o_ref.dtype)


def _matmul(x, w, out_dtype, tm, tn):
    m, k = x.shape
    n = w.shape[1]
    assert m % tm == 0 and n % tn == 0
    return pl.pallas_call(
        _mm_kernel,
        grid=(n // tn, m // tm),
        in_specs=[pl.BlockSpec((tm, k), lambda j, i: (i, 0)),
                  pl.BlockSpec((k, tn), lambda j, i: (0, j))],
        out_specs=pl.BlockSpec((tm, tn), lambda j, i: (i, j)),
        out_shape=jax.ShapeDtypeStruct((m, n), out_dtype),
        compiler_params=_cparams(("arbitrary", "arbitrary")),
        name="matmul",
    )(x, w)


def _res_ln_kernel(res_ref, y_ref, g_ref, b_ref, of_ref, ob_ref, opk_ref):
    t = ALPHA * res_ref[...] + y_ref[...]
    mu = jnp.mean(t, axis=-1, keepdims=True)
    c = t - mu
    var = jnp.mean(c * c, axis=-1, keepdims=True)
    out = c * lax.rsqrt(var + LN_EPS) * g_ref[...] + b_ref[...]
    of_ref[...] = out
    ob_ref[...] = out.astype(BF16)
    opk_ref[...] = _pack_bf16_pairs(out)


def _res_ln(res, y, g, b, tm=256):
    m, d = res.shape
    row = pl.BlockSpec((tm, d), lambda i: (i, 0))
    half = pl.BlockSpec((tm, d // 2), lambda i: (i, 0))
    vec = pl.BlockSpec((1, d), lambda i: (0, 0))
    return pl.pallas_call(
        _res_ln_kernel,
        grid=(m // tm,),
        in_specs=[row, row, vec, vec],
        out_specs=[row, row, half],
        out_shape=[jax.ShapeDtypeStruct((m, d), F32), jax.ShapeDtypeStruct((m, d), BF16),
                   jax.ShapeDtypeStruct((m, d // 2), jnp.uint32)],
        compiler_params=_cparams(("arbitrary",)),
        name="res_ln",
    )(res, y, g.reshape(1, d), b.reshape(1, d))


def _shift_mix(cur, prev8, mu, is_first):
    prev_row = jnp.where(is_first, 0.0, prev8[SUBLANES - 1:SUBLANES, :])
    rolled = pltpu.roll(cur, 1, 0)
    row = lax.broadcasted_iota(jnp.int32, cur.shape, 0)
    shifted = jnp.where(row == 0, prev_row, rolled)
    return cur + (shifted - cur) * mu


def _head_block_ones(width):
    r = lax.broadcasted_iota(jnp.int32, (width, width), 0) // RWKV_HEAD_DIM
    c = lax.broadcasted_iota(jnp.int32, (width, width), 1) // RWKV_HEAD_DIM
    return jnp.where(r == c, 1.0, 0.0).astype(BF16)


def _head_sum(x, ones_bd):
    w = ones_bd.shape[0]
    outs = []
    for c in range(x.shape[1] // w):
        hi, lo = _split_bf16(x[:, c * w:(c + 1) * w])
        outs.append(_dot(hi, ones_bd) + _dot(lo, ones_bd))
    return jnp.concatenate(outs, axis=1)


def _rwkv_prep_body(first, r_ref, k_ref, v_ref, rp_ref, kp_ref, vp_ref,
                    wlo_ref, alo_ref, glo_ref, wlop_ref, alop_ref, glop_ref,
                    mur_ref, muk_ref, muv_ref, muw_ref, mua_ref, mug_ref,
                    w2_ref, a2_ref, g2_ref, w0_ref, a0_ref, kk_ref, ka_ref, rk_ref, vres):
    r = _shift_mix(r_ref[...], rp_ref[...], mur_ref[...], first)
    k = _shift_mix(k_ref[...], kp_ref[...], muk_ref[...], first)
    v = _shift_mix(v_ref[...], vp_ref[...], muv_ref[...], first)
    w_lo = _shift_mix(wlo_ref[...], wlop_ref[...], muw_ref[...], first)
    a_lo = _shift_mix(alo_ref[...], alop_ref[...], mua_ref[...], first)
    g_lo = _shift_mix(glo_ref[...], glop_ref[...], mug_ref[...], first)

    z = w0_ref[...] + _dot(jnp.tanh(w_lo).astype(BF16), w2_ref[...])
    nz = -z
    softplus = jnp.maximum(nz, 0.0) + jnp.log(1.0 + jnp.exp(-jnp.abs(nz)))
    log_w = -softplus - 0.5
    ld = -jnp.exp(log_w)
    a = _sigmoid(a0_ref[...] + _dot(a_lo.astype(BF16), a2_ref[...]))
    g = _dot(_sigmoid(g_lo).astype(BF16), g2_ref[...])
    if vres is not None:
        vlo_ref, vlop_ref, muvl_ref, v2_ref, v0_ref, vf_ref = vres
        v_lo = _shift_mix(vlo_ref[...], vlop_ref[...], muvl_ref[...], first)
        mix = _sigmoid(v0_ref[...] + _dot(v_lo.astype(BF16), v2_ref[...]))
        v = v + (vf_ref[...] - v) * mix

    ones_bd = _head_block_ones(2 * LANES)
    kk = k * kk_ref[...]
    ss = _head_sum(kk * kk, ones_bd)
    kn = kk * lax.rsqrt(jnp.maximum(ss, 1e-24))
    k_mod = k * (1.0 + (a - 1.0) * ka_ref[...])
    bonus = _head_sum(r * k_mod * rk_ref[...], ones_bd) * v
    return r, ld, k_mod, v, kn, kn * a, g, bonus


def _make_rwkv_prep_kernel(has_vres):
    n_common = 26

    def kern(*refs):
        common = refs[:n_common]
        if has_vres:
            vres = refs[n_common:n_common + 6]
            outs = refs[n_common + 6:]
        else:
            vres = None
            outs = refs[n_common:]
        first = pl.program_id(0) == 0
        vals = _rwkv_prep_body(first, *common, vres)
        for o_ref, val in zip(outs, vals):
            o_ref[...] = val

    return kern


def _rwkv_prep(proj, mu_p, p, l, v_first, tb):
    t = proj.shape[0]
    has_vres = v_first is not None
    nj = RWKV_WIDTH // RW_CH
    pb = tb // SUBLANES

    def cur(width, col0):
        return pl.BlockSpec((tb, width), lambda i, j: (i, col0 // width))

    def cur_j(col0):
        return pl.BlockSpec((tb, RW_CH), lambda i, j: (i, col0 // RW_CH + j))

    def prev(width, col0):
        return pl.BlockSpec((SUBLANES, width), lambda i, j: (jnp.maximum(i * pb - 1, 0), col0 // width))

    def prev_j(col0):
        return pl.BlockSpec((SUBLANES, RW_CH), lambda i, j: (jnp.maximum(i * pb - 1, 0), col0 // RW_CH + j))

    def vec(width, col0):
        return pl.BlockSpec((1, width), lambda i, j: (0, col0 // width))

    def vec_j(col0=0):
        return pl.BlockSpec((1, RW_CH), lambda i, j: (0, col0 // RW_CH + j))

    def lora(rank):
        return pl.BlockSpec((rank, RW_CH), lambda i, j: (0, j))

    row = lambda a: a.reshape(1, -1)
    g2 = jnp.pad(p["rwkv_g2"][l], ((0, GLO_PAD - GATE_LORA), (0, 0))).astype(BF16)
    args = [proj, proj, proj, proj, proj, proj,
            proj, proj, proj, proj, proj, proj,
            mu_p, mu_p, mu_p, mu_p, mu_p, mu_p,
            p["rwkv_w2"][l].astype(BF16), p["rwkv_a2"][l].astype(BF16), g2,
            row(p["rwkv_w0"][l]), row(p["rwkv_a0"][l]), row(p["rwkv_k_k"][l]), row(p["rwkv_k_a"][l]),
            row(p["rwkv_r_k"][l])]
    in_specs = [cur_j(P_R), cur_j(P_K), cur_j(P_V), prev_j(P_R), prev_j(P_K), prev_j(P_V),
                cur(LANES, P_WLO), cur(LANES, P_ALO), cur(GLO_PAD, P_GLO),
                prev(LANES, P_WLO), prev(LANES, P_ALO), prev(GLO_PAD, P_GLO),
                vec_j(P_R), vec_j(P_K), vec_j(P_V), vec(LANES, P_WLO), vec(LANES, P_ALO), vec(GLO_PAD, P_GLO),
                lora(DECAY_LORA), lora(AAA_LORA), lora(GLO_PAD),
                vec_j(), vec_j(), vec_j(), vec_j(), vec_j()]
    if has_vres:
        v2 = jnp.pad(p["rwkv_v2"][l - 1], ((0, LANES - MV_LORA), (0, 0))).astype(BF16)
        args += [proj, proj, mu_p, v2, row(p["rwkv_v0"][l - 1]), v_first]
        in_specs += [cur(LANES, P_VLO), prev(LANES, P_VLO), vec(LANES, P_VLO), lora(LANES), vec_j(),
                     pl.BlockSpec((tb, RW_CH), lambda i, j: (i, j))]
    out_spec = pl.BlockSpec((tb, RW_CH), lambda i, j: (i, j))
    out_sds = jax.ShapeDtypeStruct((t, RWKV_WIDTH), F32)
    return pl.pallas_call(
        _make_rwkv_prep_kernel(has_vres),
        grid=(t // tb, nj),
        in_specs=in_specs,
        out_specs=[out_spec] * 8,
        out_shape=[out_sds] * 8,
        compiler_params=_cparams(("arbitrary", "arbitrary")),
        name="rwkv_prep",
    )(*args)


def _unit_lower_inverse(a_strict, c):
    row = lax.broadcasted_iota(jnp.int32, (c, c), 0)
    col = lax.broadcasted_iota(jnp.int32, (c, c), 1)
    eye = jnp.where(row == col, 1.0, 0.0)
    bd = lambda x: x.astype(BF16)
    base = SUBLANES
    same_base = row // base == col // base
    d1 = [bd(jnp.where(same_base, a, 0.0)) for a in a_strict]
    d2 = [bd(_dot(d, d)) for d in d1]
    inv = [eye + d.astype(F32) for d in d1]
    inv = [i + _dot(d, bd(i)) for i, d in zip(inv, d2)]
    d4 = [bd(_dot(d, d)) for d in d2]
    inv = [i + _dot(d, bd(i)) for i, d in zip(inv, d4)]
    blk = base
    while blk < c:
        band = (row // (2 * blk) == col // (2 * blk)) & (row // blk != col // blk)
        off = [bd(jnp.where(band, a, 0.0)) for a in a_strict]
        inv_b = [bd(i) for i in inv]
        tmp = [bd(_dot(o, i)) for o, i in zip(off, inv_b)]
        inv = [i + _dot(ib, t) for i, ib, t in zip(inv, inv_b, tmp)]
        blk *= 2
    return inv


def _rwkv_scan_kernel(r_ref, ld_ref, k_ref, v_ref, kn_ref, b_ref, g_ref, bonus_ref, lng_ref, lnb_ref,
                      y_ref, state_ref):
    c = RW_CHUNK
    n = RWKV_HEAD_DIM
    heads = RW_CH // n

    @pl.when(pl.program_id(1) == 0)
    def _():
        state_ref[...] = jnp.zeros_like(state_ref)

    row = lax.broadcasted_iota(jnp.int32, (c, c), 0)
    col = lax.broadcasted_iota(jnp.int32, (c, c), 1)
    tri_incl = jnp.where(row >= col, 1.0, 0.0).astype(BF16)
    strict = row > col
    incl = row >= col

    ld = ld_ref[...]
    ld_hi, ld_lo = _split_bf16(ld)
    cum = _dot(tri_incl, ld_hi) + _dot(tri_incl, ld_lo)
    g_inc = jnp.exp(cum)
    g_inv = jnp.exp(-cum)
    a_t = -kn_ref[...] * jnp.exp(cum - ld)
    b_t = b_ref[...] * g_inv
    k_t = k_ref[...] * g_inv
    r_t = r_ref[...] * g_inc
    g_last = g_inc[c - 1:c, :]
    v_all = v_ref[...]

    hs = range(heads)
    sls = [slice(h * n, (h + 1) * n) for h in hs]
    bd = lambda x: x.astype(BF16)
    incl2 = (lax.broadcasted_iota(jnp.int32, (c, 2 * c), 0)
             >= lax.broadcasted_iota(jnp.int32, (c, 2 * c), 1) % c)
    v_h = [v_all[:, s] for s in sls]
    a_h = [a_t[:, s] for s in sls]
    ar = [bd(jnp.concatenate([a_h[h], r_t[:, sls[h]]], axis=0)) for h in hs]
    bk = [bd(jnp.concatenate([b_t[:, s], k_t[:, s]], axis=0)) for s in sls]
    p1 = [_dot_nt(ar[h], bk[h]) for h in hs]
    a_ab = [jnp.where(strict, p[:c, :c], 0.0) for p in p1]
    a_ak = [bd(jnp.where(strict, p[:c, c:], 0.0)) for p in p1]
    a_r = [bd(jnp.where(incl2, p[c:, :], 0.0)) for p in p1]
    akv = [_dot(a_ak[h], bd(v_h[h])) for h in hs]
    inv = _unit_lower_inverse(a_ab, c)
    sol = [_dot(bd(inv[h]), bd(jnp.concatenate([a_h[h], akv[h]], axis=1))) for h in hs]
    s0 = [state_ref[h] for h in hs]
    s0b = [bd(s) for s in s0]
    u = [_dot_nt(bd(sol[h][:, :n]), s0b[h]) + sol[h][:, n:] for h in hs]
    uv = [bd(jnp.concatenate([u[h], v_h[h]], axis=0)) for h in hs]
    ys = [_dot_nt(bd(r_t[:, sls[h]]), s0b[h]) + _dot(a_r[h], uv[h]) for h in hs]
    for h in hs:
        state_ref[h] = (s0[h] + _dot_tn(uv[h], bk[h])) * g_last[:, sls[h]]
    y = jnp.concatenate(ys, axis=1)

    ones_bd = _head_block_ones(2 * LANES)
    mean = _head_sum(y, ones_bd) * (1.0 / n)
    yc = y - mean
    var = _head_sum(yc * yc, ones_bd) * (1.0 / n)
    yn = yc * lax.rsqrt(var + GN_EPS) * lng_ref[...] + lnb_ref[...]
    y_ref[...] = ((yn + bonus_ref[...]) * g_ref[...]).astype(y_ref.dtype)


def _rwkv_scan(prep, ln_g, ln_b):
    t = prep[0].shape[0]
    c = RW_CHUNK
    blk = pl.BlockSpec((c, RW_CH), lambda j, i: (i, j))
    vec = pl.BlockSpec((1, RW_CH), lambda j, i: (0, j))
    return pl.pallas_call(
        _rwkv_scan_kernel,
        grid=(RWKV_WIDTH // RW_CH, t // c),
        in_specs=[blk] * 8 + [vec, vec],
        out_specs=blk,
        out_shape=jax.ShapeDtypeStruct((t, RWKV_WIDTH), BF16),
        scratch_shapes=[pltpu.VMEM((RW_CH // RWKV_HEAD_DIM, RWKV_HEAD_DIM, RWKV_HEAD_DIM), F32)],
        compiler_params=_cparams(("arbitrary", "arbitrary")),
        name="rwkv_scan",
    )(*prep, ln_g.reshape(1, -1), ln_b.reshape(1, -1))


def _gelu_tanh(x):
    return 0.5 * x * (1.0 + jnp.tanh(np.sqrt(2.0 / np.pi).astype(np.float32) * (x + 0.044715 * (x * x * x))))


def _gmlp_kernel(u_ref, v_ref, lng_ref, lnb_ref, ws_ref, bs_ref, o_ref, *, chunks):
    ch = GMLP_CHUNK
    gd = GMLP_GROUP_DIM
    u = _gelu_tanh(u_ref[...])
    v = _gelu_tanh(v_ref[...])
    mu = jnp.mean(v, axis=-1, keepdims=True)
    vc = v - mu
    var = jnp.mean(vc * vc, axis=-1, keepdims=True)
    vn = (vc * lax.rsqrt(var + LN_EPS) * lng_ref[...] + lnb_ref[...]).astype(BF16)
    row = lax.broadcasted_iota(jnp.int32, (ch, ch), 0)
    col = lax.broadcasted_iota(jnp.int32, (ch, ch), 1)
    causal = row >= col
    bs = bs_ref[...]
    for g in range(GMLP_GROUPS):
        w = jnp.where(causal, ws_ref[g], 0.0).astype(BF16)
        bias = bs[:, g:g + 1]
        for n in range(chunks):
            f = _dot(w, vn[n * ch:(n + 1) * ch, g * gd:(g + 1) * gd]) + bias
            o_ref[n * ch:(n + 1) * ch, g * gd:(g + 1) * gd] = (
                u[n * ch:(n + 1) * ch, g * gd:(g + 1) * gd] * f).astype(o_ref.dtype)


def _gmlp(proj, ln_g, ln_b, w_s, b_s, tb=256):
    t = proj.shape[0]
    bs_t = jnp.pad(b_s.T, ((0, 0), (0, LANES - GMLP_GROUPS)))
    return pl.pallas_call(
        functools.partial(_gmlp_kernel, chunks=tb // GMLP_CHUNK),
        grid=(t // tb,),
        in_specs=[pl.BlockSpec((tb, GMLP_WIDTH), lambda i: (i, P_GU // GMLP_WIDTH)),
                  pl.BlockSpec((tb, GMLP_WIDTH), lambda i: (i, P_GV // GMLP_WIDTH)),
                  pl.BlockSpec((1, GMLP_WIDTH), lambda i: (0, 0)),
                  pl.BlockSpec((1, GMLP_WIDTH), lambda i: (0, 0)),
                  pl.BlockSpec((GMLP_GROUPS, GMLP_CHUNK, GMLP_CHUNK), lambda i: (0, 0, 0)),
                  pl.BlockSpec((GMLP_CHUNK, LANES), lambda i: (0, 0))],
        out_specs=pl.BlockSpec((tb, GMLP_WIDTH), lambda i: (i, 0)),
        out_shape=jax.ShapeDtypeStruct((t, GMLP_WIDTH), BF16),
        compiler_params=_cparams(("arbitrary",)),
        name="gmlp",
    )(proj, proj, ln_g.reshape(1, -1), ln_b.reshape(1, -1), w_s, bs_t)


def _rope_lanes(t, cc, s1, s2):
    return t * cc + pltpu.roll(t, LANES - MLA_ROPE_DIM // 2, 1) * s1 + pltpu.roll(t, MLA_ROPE_DIM // 2, 1) * s2


def _mla_proj_kernel(cq_ref, ckv_ref, kr_ref, qn_ref, kvn_ref, wq_ref, wk_ref, wvt_ref, cc_ref, s1_ref, s2_ref,
                     q_o, kn_o, vt_o, kr_o):
    cq = cq_ref[...]
    qn = (cq * lax.rsqrt(jnp.mean(cq * cq, axis=-1, keepdims=True) + RMS_EPS) * qn_ref[...]).astype(BF16)
    ckv = ckv_ref[...]
    kvn = (ckv * lax.rsqrt(jnp.mean(ckv * ckv, axis=-1, keepdims=True) + RMS_EPS) * kvn_ref[...]).astype(BF16)
    cc, s1, s2 = cc_ref[...], s1_ref[...], s2_ref[...]
    scale = MLA_QK_DIM ** -0.5 * np.log2(np.e)
    for h in range(MLA_HEADS):
        q = _dot(qn, wq_ref[:, 2 * LANES * h:2 * LANES * (h + 1)]) * scale
        q_o[:, 2 * LANES * h:2 * LANES * h + LANES] = q[:, :LANES].astype(BF16)
        q_o[:, 2 * LANES * h + LANES:2 * LANES * (h + 1)] = _rope_lanes(q[:, LANES:], cc, s1, s2).astype(BF16)
    kn_o[...] = _dot(kvn, wk_ref[...]).astype(BF16)
    vt_o[...] = _dot_nt(wvt_ref[...], kvn).astype(BF16)
    kr_o[...] = _rope_lanes(kr_ref[...], cc, s1, s2).astype(BF16)


def _mla_proj(proj, positions, q_norm, kv_norm, w_uq, w_ukv, tm=512):
    t = proj.shape[0]
    h = MLA_HEADS
    half = MLA_ROPE_DIM // 2
    inv_freq = jnp.power(ROPE_THETA, -jnp.arange(0, MLA_ROPE_DIM, 2, dtype=F32) / MLA_ROPE_DIM)
    ang = positions.reshape(t).astype(F32)[:, None] * inv_freq
    cos, sin = jnp.cos(ang), jnp.sin(ang)
    z = jnp.zeros((t, half), F32)
    cc = jnp.concatenate([cos, cos, z, z], axis=1)
    s1 = jnp.concatenate([-sin, z, z, z], axis=1)
    s2 = jnp.concatenate([z, sin, z, z], axis=1)
    wq = w_uq.reshape(MLA_Q_RANK, h, MLA_QK_DIM)
    wq = jnp.pad(wq, ((0, 0), (0, 0), (0, 2 * LANES - MLA_QK_DIM))).reshape(MLA_Q_RANK, h * 2 * LANES).astype(BF16)
    wkv = w_ukv.reshape(MLA_KV_RANK, h, MLA_NOPE_DIM + MLA_V_DIM)
    wk = wkv[:, :, :MLA_NOPE_DIM].reshape(MLA_KV_RANK, h * MLA_NOPE_DIM).astype(BF16)
    wvt = wkv[:, :, MLA_NOPE_DIM:].reshape(MLA_KV_RANK, h * MLA_V_DIM).T.astype(BF16)
    full = lambda a: pl.BlockSpec(a.shape, lambda i: (0,) * a.ndim)
    tab = pl.BlockSpec((tm, LANES), lambda i: (i, 0))
    qn2, kvn2 = q_norm.reshape(1, -1), kv_norm.reshape(1, -1)
    return pl.pallas_call(
        _mla_proj_kernel,
        grid=(t // tm,),
        in_specs=[pl.BlockSpec((tm, MLA_Q_RANK), lambda i: (i, P_CQ // MLA_Q_RANK)),
                  pl.BlockSpec((tm, MLA_KV_RANK), lambda i: (i, P_CKV // MLA_KV_RANK)),
                  pl.BlockSpec((tm, LANES), lambda i: (i, P_KROPE // LANES)),
                  full(qn2), full(kvn2), full(wq), full(wk), full(wvt), tab, tab, tab],
        out_specs=[pl.BlockSpec((tm, h * 2 * LANES), lambda i: (i, 0)),
                   pl.BlockSpec((tm, h * MLA_NOPE_DIM), lambda i: (i, 0)),
                   pl.BlockSpec((h * MLA_V_DIM, tm), lambda i: (0, i)),
                   pl.BlockSpec((tm, LANES), lambda i: (i, 0))],
        out_shape=[jax.ShapeDtypeStruct((t, h * 2 * LANES), BF16),
                   jax.ShapeDtypeStruct((t, h * MLA_NOPE_DIM), BF16),
                   jax.ShapeDtypeStruct((h * MLA_V_DIM, t), BF16),
                   jax.ShapeDtypeStruct((t, LANES), BF16)],
        compiler_params=_cparams(("arbitrary",)),
        name="mla_proj",
    )(proj, proj, proj, qn2, kvn2, wq, wk, wvt, cc, s1, s2)


def _flash_kernel(qi_ref, kj_ref, q_ref, kn_ref, kr_ref, vt_ref, o_ref, m_sc, l_sc, acc_sc, *, tq, tk):
    s = pl.program_id(1)
    qi, kj = qi_ref[s], kj_ref[s]

    @pl.when(kj == 0)
    def _():
        m_sc[...] = jnp.full_like(m_sc, -jnp.inf)
        l_sc[...] = jnp.zeros_like(l_sc)
        acc_sc[...] = jnp.zeros_like(acc_sc)

    def step(masked):
        k = jnp.concatenate([kn_ref[...], kr_ref[...]], axis=1)
        st = _dot_nt(k, q_ref[...])
        if masked:
            key = kj * tk + lax.broadcasted_iota(jnp.int32, (tk, tq), 0)
            qry = qi * tq + lax.broadcasted_iota(jnp.int32, (tk, tq), 1)
            st = jnp.where(key <= qry, st, -jnp.inf)
        m_prev = m_sc[...]
        m_new = jnp.maximum(m_prev, jnp.max(st, axis=0, keepdims=True))
        alpha = jnp.exp2(m_prev - m_new)
        p = jnp.exp2(st - m_new)
        l_sc[...] = alpha * l_sc[...] + jnp.sum(p, axis=0, keepdims=True)
        acc_sc[...] = alpha * acc_sc[...] + _dot(vt_ref[...], p.astype(BF16))
        m_sc[...] = m_new

    last_key_of_block = kj * tk + tk - 1
    on_diag = last_key_of_block > qi * tq

    @pl.when(jnp.logical_not(on_diag))
    def _():
        step(False)

    @pl.when(on_diag)
    def _():
        step(True)

    @pl.when(last_key_of_block >= qi * tq + tq - 1)
    def _():
        o_ref[...] = (acc_sc[...] / l_sc[...]).T.astype(o_ref.dtype)


def _flash(q, kn, vt, kr, tq=512, tk=512):
    t = q.shape[0]
    assert tq % tk == 0
    pairs = [(i, j) for i in range(t // tq) for j in range((i + 1) * tq // tk)]
    qi = jnp.asarray([pr[0] for pr in pairs], jnp.int32)
    kj = jnp.asarray([pr[1] for pr in pairs], jnp.int32)
    grid_spec = pltpu.PrefetchScalarGridSpec(
        num_scalar_prefetch=2,
        grid=(MLA_HEADS, len(pairs)),
        in_specs=[pl.BlockSpec((tq, 2 * LANES), lambda h, s, qi, kj: (qi[s], h)),
                  pl.BlockSpec((tk, MLA_NOPE_DIM), lambda h, s, qi, kj: (kj[s], h)),
                  pl.BlockSpec((tk, LANES), lambda h, s, qi, kj: (kj[s], 0)),
                  pl.BlockSpec((MLA_V_DIM, tk), lambda h, s, qi, kj: (h, kj[s]))],
        out_specs=pl.BlockSpec((tq, MLA_V_DIM), lambda h, s, qi, kj: (qi[s], h)),
        scratch_shapes=[pltpu.VMEM((1, tq), F32), pltpu.VMEM((1, tq), F32), pltpu.VMEM((MLA_V_DIM, tq), F32)],
    )
    return pl.pallas_call(
        functools.partial(_flash_kernel, tq=tq, tk=tk),
        grid_spec=grid_spec,
        out_shape=jax.ShapeDtypeStruct((t, MLA_HEADS * MLA_V_DIM), BF16),
        compiler_params=_cparams(("arbitrary", "arbitrary")),
        name="flash",
    )(qi, kj, q, kn, kr, vt)


def _router_kernel(h_ref, rwt_ref, bias_ref, exp_o, pos_o, gate_o, cnt_o, carry_sc):
    e = N_EXPERTS
    per = e // N_EXPERT_GROUPS
    h_hi, h_lo = _split_bf16(h_ref[...])
    w_hi, w_lo = _split_bf16(rwt_ref[...])
    logits = _dot_nt(w_hi, h_hi) + _dot_nt(w_hi, h_lo) + _dot_nt(w_lo, h_hi)
    scores = _sigmoid(logits)
    biased = scores + bias_ref[...][:, 0:1]
    tb = biased.shape[1]
    neg = -jnp.inf
    sub = lax.broadcasted_iota(jnp.int32, (per, tb), 0)
    grp_rows = []
    for g in range(N_EXPERT_GROUPS):
        blk = biased[g * per:(g + 1) * per, :]
        m1 = jnp.max(blk, axis=0, keepdims=True)
        first = jnp.min(jnp.where(blk == m1, sub, per), axis=0, keepdims=True)
        m2 = jnp.max(jnp.where(sub == first, neg, blk), axis=0, keepdims=True)
        grp_rows.append(m1 + m2)
    grp = jnp.concatenate(grp_rows, axis=0)
    gidx = lax.broadcasted_iota(jnp.int32, grp.shape, 0)
    grank = jnp.zeros(grp.shape, jnp.int32)
    for g in range(N_EXPERT_GROUPS):
        other = grp[g:g + 1, :]
        ahead = (other > grp) | ((other == grp) & (g < gidx))
        grank = grank + jnp.where(ahead, 1, 0)
    gsel = grank < TOPK_GROUPS
    masked = jnp.concatenate(
        [jnp.where(gsel[g:g + 1, :], biased[g * per:(g + 1) * per, :], neg) for g in range(N_EXPERT_GROUPS)], axis=0)
    eidx = lax.broadcasted_iota(jnp.int32, masked.shape, 0)
    rank = jnp.zeros(masked.shape, jnp.int32)
    for j in range(e):
        other = masked[j:j + 1, :]
        ahead = (other > masked) | ((other == masked) & (j < eidx))
        rank = rank + jnp.where(ahead, 1, 0)
    chosen = rank < TOP_K
    sel = jnp.where(chosen, scores, 0.0)
    gate = sel / jnp.sum(sel, axis=0, keepdims=True) * ROUTED_SCALE

    @pl.when(pl.program_id(0) == 0)
    def _():
        carry_sc[...] = jnp.zeros_like(carry_sc)

    chosen_f = jnp.where(chosen, 1.0, 0.0)
    earlier = (lax.broadcasted_iota(jnp.int32, (tb, tb), 0) < lax.broadcasted_iota(jnp.int32, (tb, tb), 1))
    carry = carry_sc[...]
    pos = _dot(chosen_f.astype(BF16), jnp.where(earlier, 1.0, 0.0).astype(BF16)) + carry[:, 0:1]
    carry_sc[...] = carry + jnp.sum(chosen_f, axis=1, keepdims=True)
    cnt_o[...] = carry_sc[...].astype(jnp.int32)

    eidx_f = eidx.astype(F32)
    pick = lambda hit, val: jnp.sum(jnp.where(hit, val, 0.0), axis=0, keepdims=True)
    hits = [rank == k for k in range(TOP_K)]
    exp_o[...] = jnp.concatenate([pick(hit, eidx_f) for hit in hits], axis=0).astype(jnp.int32)
    pos_o[...] = jnp.concatenate([pick(hit, pos) for hit in hits], axis=0).astype(jnp.int32)
    gate_o[...] = jnp.concatenate([pick(hit, gate) for hit in hits], axis=0)


def _router(h, router_w, router_bias, tb=512):
    t, d = h.shape
    bias = jnp.broadcast_to(router_bias.astype(F32)[:, None], (N_EXPERTS, LANES))
    per_tok = pl.BlockSpec((TOP_K, tb), lambda i: (0, i))
    return pl.pallas_call(
        _router_kernel,
        grid=(t // tb,),
        in_specs=[pl.BlockSpec((tb, d), lambda i: (i, 0)),
                  pl.BlockSpec((N_EXPERTS, d), lambda i: (0, 0)),
                  pl.BlockSpec((N_EXPERTS, LANES), lambda i: (0, 0))],
        out_specs=[per_tok, per_tok, per_tok, pl.BlockSpec((N_EXPERTS, LANES), lambda i: (0, 0))],
        out_shape=[jax.ShapeDtypeStruct((TOP_K, t), jnp.int32), jax.ShapeDtypeStruct((TOP_K, t), jnp.int32),
                   jax.ShapeDtypeStruct((TOP_K, t), F32), jax.ShapeDtypeStruct((N_EXPERTS, LANES), jnp.int32)],
        scratch_shapes=[pltpu.VMEM((N_EXPERTS, LANES), F32)],
        compiler_params=_cparams(("arbitrary",)),
        name="router",
    )(h, router_w.T, bias)


HALF_MASK = 0xFFFF0000


def _pack_bf16_pairs(x):
    n = x.shape[1] // 2
    lo = pltpu.bitcast(x[:, :n].astype(BF16).astype(F32), jnp.uint32) >> 16
    hi = pltpu.bitcast(x[:, n:].astype(BF16).astype(F32), jnp.uint32) & jnp.uint32(HALF_MASK)
    return lo | hi


def _unpack_bf16_pairs(w):
    return pltpu.bitcast(w << 16, F32), pltpu.bitcast(w & jnp.uint32(HALF_MASK), F32)


def _dispatch_plan(exp_r, pos_r, cnt, bm, nb_max):
    nb = (cnt + bm - 1) // bm
    bend = jnp.cumsum(nb)
    total = bend[-1]
    slot_start = (bend - nb) * bm
    slot = (slot_start[exp_r] + pos_r).T.reshape(-1)
    blocks = jnp.minimum(jnp.arange(nb_max, dtype=jnp.int32), total - 1)
    blk_exp = jnp.searchsorted(bend, blocks, side="right")
    return slot.astype(jnp.int32), blk_exp.astype(jnp.int32), total.reshape(1).astype(jnp.int32)


def _dispatch_kernel(slot_ref, h_hbm, xs_init_hbm, xs_hbm, sem, *, tb):
    del xs_init_hbm
    base = pl.program_id(0) * tb

    def body(t, carry):
        for k in range(TOP_K):
            s = slot_ref[t * TOP_K + k]
            pltpu.make_async_copy(h_hbm.at[pl.ds(base + t, 1), :], xs_hbm.at[pl.ds(s, 1), :], sem).start()
        return carry

    lax.fori_loop(0, tb, body, 0)
    rows = tb * TOP_K
    pltpu.make_async_copy(xs_hbm.at[pl.ds(0, rows), :], xs_hbm.at[pl.ds(0, rows), :], sem).wait()


def _dispatch(h_pk, slot, n_slots, tb):
    t, w = h_pk.shape
    xs0 = jnp.zeros((n_slots, w), jnp.uint32)
    return pl.pallas_call(
        functools.partial(_dispatch_kernel, tb=tb),
        grid=(t // tb,),
        in_specs=[pl.BlockSpec((tb * TOP_K,), lambda i: (i,), memory_space=pltpu.SMEM),
                  pl.BlockSpec(memory_space=pl.ANY),
                  pl.BlockSpec(memory_space=pl.ANY)],
        out_specs=pl.BlockSpec(memory_space=pl.ANY),
        out_shape=jax.ShapeDtypeStruct((n_slots, w), jnp.uint32),
        scratch_shapes=[pltpu.SemaphoreType.DMA(())],
        input_output_aliases={2: 0},
        compiler_params=pltpu.CompilerParams(dimension_semantics=("arbitrary",), vmem_limit_bytes=VMEM_LIMIT,
                                             has_side_effects=True, disable_bounds_checks=True),
        name="moe_dispatch",
    )(slot, h_pk, xs0)


def _expert_kernel(be_ref, tot_ref, xs_ref, w1_ref, w3_ref, w2_ref, ys_ref):
    del be_ref

    @pl.when(pl.program_id(0) < tot_ref[0])
    def _():
        lo, hi = _unpack_bf16_pairs(xs_ref[...])
        x = jnp.concatenate([lo.astype(BF16), hi.astype(BF16)], axis=1)
        hg = _dot(x, w1_ref[0, 0])
        hu = _dot(x, w3_ref[0, 0])
        act = (hg * _sigmoid(hg) * hu).astype(BF16)
        ys_ref[...] = _pack_bf16_pairs(_dot(act, w2_ref[0, 0]))

    @pl.when(pl.program_id(0) >= tot_ref[0])
    def _():
        ys_ref[...] = jnp.zeros_like(ys_ref)


def _experts(xs, blk_exp, total, w1, w3, w2, layer, bm):
    n_slots, w = xs.shape
    d = 2 * w
    row = lambda b, be, tot: (jnp.minimum(b, tot[0] - 1), 0)
    wsel = lambda b, be, tot: (layer, be[b], 0, 0)
    grid_spec = pltpu.PrefetchScalarGridSpec(
        num_scalar_prefetch=2,
        grid=(n_slots // bm,),
        in_specs=[pl.BlockSpec((bm, w), row),
                  pl.BlockSpec((1, 1, d, D_EXPERT), wsel),
                  pl.BlockSpec((1, 1, d, D_EXPERT), wsel),
                  pl.BlockSpec((1, 1, D_EXPERT, d), wsel)],
        out_specs=pl.BlockSpec((bm, w), lambda b, be, tot: (b, 0)),
    )
    return pl.pallas_call(
        _expert_kernel,
        grid_spec=grid_spec,
        out_shape=jax.ShapeDtypeStruct((n_slots, w), jnp.uint32),
        compiler_params=_cparams(("arbitrary",)),
        name="moe_experts",
    )(blk_exp, total, xs, w1, w3, w2)


def _combine_kernel(slot_ref, slot_next_ref, gate_ref, h_ref, hb_ref, sw13_ref, sw2_ref, g_ref, b_ref, ys_hbm,
                    of_ref, ob_ref, rows_buf, sem, *, tb, steps):
    i = pl.program_id(0)
    cur = lax.rem(i, 2)

    def issue(table_ref, buf):
        def body(t, carry):
            for k in range(TOP_K):
                s = table_ref[t * TOP_K + k]
                pltpu.make_async_copy(ys_hbm.at[pl.ds(s, 1), :], rows_buf.at[buf, k, pl.ds(t, 1), :],
                                      sem.at[buf]).start()
            return carry

        lax.fori_loop(0, tb, body, 0)

    @pl.when(i == 0)
    def _():
        issue(slot_ref, 0)

    @pl.when(i + 1 < steps)
    def _():
        issue(slot_next_ref, 1 - cur)

    hgu = _dot(hb_ref[...], sw13_ref[...])
    hg, hu = hgu[:, :D_EXPERT], hgu[:, D_EXPERT:]
    shared = _dot((hg * _sigmoid(hg) * hu).astype(BF16), sw2_ref[...])

    pltpu.make_async_copy(rows_buf.at[cur], rows_buf.at[cur], sem.at[cur]).wait()
    gate = gate_ref[...]
    acc_lo = jnp.zeros((tb, rows_buf.shape[-1]), F32)
    acc_hi = jnp.zeros((tb, rows_buf.shape[-1]), F32)
    for k in range(TOP_K):
        lo, hi = _unpack_bf16_pairs(rows_buf[cur, k])
        gk = gate[:, k:k + 1]
        acc_lo = acc_lo + gk * lo
        acc_hi = acc_hi + gk * hi
    t = ALPHA * h_ref[...] + shared + jnp.concatenate([acc_lo, acc_hi], axis=1)
    mu = jnp.mean(t, axis=-1, keepdims=True)
    c = t - mu
    var = jnp.mean(c * c, axis=-1, keepdims=True)
    out = c * lax.rsqrt(var + LN_EPS) * g_ref[...] + b_ref[...]
    of_ref[...] = out
    ob_ref[...] = out.astype(BF16)


def _combine(ys, slot, gate_tk, hf, hb, sw1, sw3, sw2, g, b, tb):
    t, d = hf.shape
    steps = t // tb
    sw13 = jnp.concatenate([sw1, sw3], axis=1).astype(BF16)
    row = pl.BlockSpec((tb, d), lambda i: (i, 0))
    full = lambda a: pl.BlockSpec(a.shape, lambda i: (0,) * a.ndim)
    g2, b2, sw2b = g.reshape(1, d), b.reshape(1, d), sw2.astype(BF16)
    return pl.pallas_call(
        functools.partial(_combine_kernel, tb=tb, steps=steps),
        grid=(steps,),
        in_specs=[pl.BlockSpec((tb * TOP_K,), lambda i: (i,), memory_space=pltpu.SMEM),
                  pl.BlockSpec((tb * TOP_K,), lambda i: (jnp.minimum(i + 1, steps - 1),), memory_space=pltpu.SMEM),
                  pl.BlockSpec((tb, TOP_K), lambda i: (i, 0)),
                  row, row, full(sw13), full(sw2b), full(g2), full(b2),
                  pl.BlockSpec(memory_space=pl.ANY)],
        out_specs=[row, row],
        out_shape=[jax.ShapeDtypeStruct((t, d), F32), jax.ShapeDtypeStruct((t, d), BF16)],
        scratch_shapes=[pltpu.VMEM((2, TOP_K, tb, ys.shape[1]), jnp.uint32), pltpu.SemaphoreType.DMA((2,))],
        compiler_params=pltpu.CompilerParams(dimension_semantics=("arbitrary",), vmem_limit_bytes=VMEM_LIMIT,
                                             disable_bounds_checks=True),
        name="moe_combine",
    )(slot, slot, gate_tk, hf, hb, sw13, sw2b, g2, b2, ys)


def _moe_ffn(hf, hb, h_pk, router_w, router_bias, w1, w3, w2, layer, sw1, sw3, sw2, ln_g, ln_b, bm):
    t = hf.shape[0]
    exp_r, pos_r, gate_r, cnt = _router(hf, router_w, router_bias, tb=min(512, t))
    nb_max = t * TOP_K // bm + N_EXPERTS
    slot, blk_exp, total = _dispatch_plan(exp_r, pos_r, cnt[:, 0], bm, nb_max)
    xs = _dispatch(h_pk, slot, nb_max * bm, tb=min(256, t // TOP_K))
    ys = _experts(xs, blk_exp, total, w1, w3, w2, layer, bm)
    return _combine(ys, slot, gate_r.T, hf, hb, sw1, sw3, sw2, ln_g, ln_b, tb=min(128, t))


def _prep_w_in(w_in, has_vres):
    d = w_in.shape[0]
    sizes = [RWKV_WIDTH, RWKV_WIDTH, RWKV_WIDTH, DECAY_LORA, AAA_LORA, GATE_LORA,
             GMLP_WIDTH, GMLP_WIDTH, MLA_Q_RANK, MLA_KV_RANK, MLA_ROPE_DIM]
    if has_vres:
        sizes.append(MV_LORA)
    offs = np.concatenate([[0], np.cumsum(sizes)])
    seg = [w_in[:, offs[i]:offs[i + 1]] for i in range(len(sizes))]
    r, k, v, w_lo, a_lo, g_lo, gu, gv, cq, ckv, kr = seg[:11]
    z = lambda n: jnp.zeros((d, n), w_in.dtype)
    v_lo = jnp.concatenate([seg[11], z(LANES - MV_LORA)], axis=1) if has_vres else z(LANES)
    cols = [gu, gv, r, k, v, w_lo, a_lo, cq, ckv, g_lo, z(GLO_PAD - GATE_LORA), kr, z(LANES - MLA_ROPE_DIM), v_lo]
    out = jnp.concatenate(cols, axis=1).astype(BF16)
    assert out.shape[1] == P_COLS
    return out


def _prep_mu(mu, mu_vres):
    out = jnp.zeros((1, P_COLS), F32)
    offs = np.concatenate([[0], np.cumsum([RWKV_WIDTH] * 3 + [DECAY_LORA, AAA_LORA, GATE_LORA])])
    for dst, i in zip([P_R, P_K, P_V, P_WLO, P_ALO, P_GLO], range(6)):
        out = lax.dynamic_update_slice(out, mu[offs[i]:offs[i + 1]].reshape(1, -1), (0, dst))
    if mu_vres is not None:
        out = lax.dynamic_update_slice(out, mu_vres.reshape(1, -1), (0, P_VLO))
    return out


def kernel(x, positions, w_in_first, w_in_rest, rwkv_mu, rwkv_mu_vres, rwkv_w0, rwkv_w2, rwkv_a0, rwkv_a2, rwkv_v0, rwkv_v2, rwkv_g2, rwkv_k_k, rwkv_k_a, rwkv_r_k, rwkv_ln_g, rwkv_ln_b, gmlp_ln_g, gmlp_ln_b, gmlp_w_s, gmlp_b_s, mla_q_norm, mla_kv_norm, mla_w_uq, mla_w_ukv, w_out, ln1_g, ln1_b, router_w, router_bias, exp_w1, exp_w3, exp_w2, shared_w1, shared_w3, shared_w2, ln2_g, ln2_b):
    b, s, d = x.shape
    t = b * s
    p = dict(rwkv_w0=rwkv_w0, rwkv_w2=rwkv_w2, rwkv_a0=rwkv_a0, rwkv_a2=rwkv_a2, rwkv_v0=rwkv_v0, rwkv_v2=rwkv_v2,
             rwkv_g2=rwkv_g2, rwkv_k_k=rwkv_k_k, rwkv_k_a=rwkv_k_a, rwkv_r_k=rwkv_r_k)
    xf = x.reshape(t, d)
    xb = xf.astype(BF16)
    v_first = None
    ew1, ew3, ew2 = exp_w1.astype(BF16), exp_w3.astype(BF16), exp_w2.astype(BF16)
    for l in range(DEPTH):
        has_vres = l > 0
        w_in = _prep_w_in(w_in_first if l == 0 else w_in_rest[l - 1], has_vres)
        mu_p = _prep_mu(rwkv_mu[l], rwkv_mu_vres[l - 1] if has_vres else None)
        proj = _matmul(xb, w_in, F32, tm=min(512, t), tn=1280)
        prep = _rwkv_prep(proj, mu_p, p, l, v_first, tb=min(256, t))
        if l == 0:
            v_first = prep[3]
        y_a = _rwkv_scan(prep, rwkv_ln_g[l], rwkv_ln_b[l])
        y_b = _gmlp(proj, gmlp_ln_g[l], gmlp_ln_b[l], gmlp_w_s[l], gmlp_b_s[l], tb=min(256, t))
        q, kn, vt, kr = _mla_proj(proj, positions, mla_q_norm[l], mla_kv_norm[l], mla_w_uq[l], mla_w_ukv[l],
                                  tm=min(512, t))
        y_c = _flash(q, kn, vt, kr, tq=min(1024, t), tk=min(512, t))
        ycat = jnp.concatenate([y_a, y_b, y_c], axis=1)
        mix = _matmul(ycat, w_out[l].astype(BF16), F32, tm=min(512, t), tn=1024)
        hf, hb, h_pk = _res_ln(xf, mix, ln1_g[l], ln1_b[l], tm=min(256, t))
        xf, xb = _moe_ffn(hf, hb, h_pk, router_w[l], router_bias[l],
                          ew1, ew3, ew2, l,
                          shared_w1[l], shared_w3[l], shared_w2[l], ln2_g[l], ln2_b[l], bm=MOE_ROW_BLOCK)
    return xf.reshape(b, s, d)
```

```python
import functools

import jax
import jax.numpy as jnp
import numpy as np
from jax import lax
from jax.experimental import pallas as pl
from jax.experimental.pallas import tpu as pltpu

F32 = jnp.float32
BF16 = jnp.bfloat16

D_MODEL = 4096
DEPTH = 2
RWKV_HEAD_DIM = 64
RWKV_WIDTH = 3 * D_MODEL // 8
DECAY_LORA = 128
AAA_LORA = 128
MV_LORA = 96
GATE_LORA = 480
GN_EPS = 64e-5
GMLP_WIDTH = D_MODEL // 4
GMLP_GROUP_DIM = 128
GMLP_GROUPS = GMLP_WIDTH // GMLP_GROUP_DIM
GMLP_CHUNK = 128
MLA_V_DIM = 128
MLA_WIDTH = D_MODEL - RWKV_WIDTH - GMLP_WIDTH
MLA_HEADS = MLA_WIDTH // MLA_V_DIM
MLA_NOPE_DIM = 128
MLA_ROPE_DIM = 64
MLA_QK_DIM = MLA_NOPE_DIM + MLA_ROPE_DIM
MLA_Q_RANK = 768
MLA_KV_RANK = 512
ROPE_THETA = 10000.0
N_EXPERTS = 64
TOP_K = 8
N_EXPERT_GROUPS = 8
TOPK_GROUPS = 4
D_EXPERT = 384
ROUTED_SCALE = 2.5
ALPHA = (2 * DEPTH) ** 0.25
LN_EPS = 1e-5
RMS_EPS = 1e-6

LANES = 128
SUBLANES = 8
VMEM_LIMIT = 56 * 1024 * 1024

P_GU, P_GV = 0, 1024
P_R, P_K, P_V = 2048, 3584, 5120
P_WLO, P_ALO = 6656, 6784
P_CQ, P_CKV = 6912, 7680
P_GLO, P_KROPE, P_VLO = 8192, 8704, 8832
P_COLS = 8960
GLO_PAD = 512
RW_CH = 512
RW_CHUNK = 64
RW_SUB = 4
MOE_ROW_BLOCK = 256


def _cparams(sem):
    return pltpu.CompilerParams(dimension_semantics=sem, vmem_limit_bytes=VMEM_LIMIT)


def _sigmoid(x):
    return 1.0 / (1.0 + jnp.exp(-x))


def _dot(a, b):
    return jnp.dot(a, b, preferred_element_type=F32)


def _dot_nt(a, b):
    return lax.dot_general(a, b, (((1,), (1,)), ((), ())), preferred_element_type=F32)


def _dot_tn(a, b):
    return lax.dot_general(a, b, (((0,), (0,)), ((), ())), preferred_element_type=F32)


def _split_bf16(x):
    hi = x.astype(BF16)
    lo = (x - hi.astype(F32)).astype(BF16)
    return hi, lo


def _mm_kernel(x_ref, w_ref, o_ref):
    o_ref[...] = _dot(x_ref[...], w_ref[...]).astype(o_ref.dtype)


def _matmul(x, w, out_dtype, tm, tn):
    m, k = x.shape
    n = w.shape[1]
    assert m % tm == 0 and n % tn == 0
    return pl.pallas_call(
        _mm_kernel,
        grid=(n // tn, m // tm),
        in_specs=[pl.BlockSpec((tm, k), lambda j, i: (i, 0)),
                  pl.BlockSpec((k, tn), lambda j, i: (0, j))],
        out_specs=pl.BlockSpec((tm, tn), lambda j, i: (i, j)),
        out_shape=jax.ShapeDtypeStruct((m, n), out_dtype),
        compiler_params=_cparams(("arbitrary", "arbitrary")),
        name="matmul",
    )(x, w)


def _res_ln_kernel(res_ref, y_ref, g_ref, b_ref, of_ref, ob_ref, opk_ref):
    t = ALPHA * res_ref[...] + y_ref[...]
    mu = jnp.mean(t, axis=-1, keepdims=True)
    c = t - mu
    var = jnp.mean(c * c, axis=-1, keepdims=True)
    out = c * lax.rsqrt(var + LN_EPS) * g_ref[...] + b_ref[...]
    of_ref[...] = out
    ob_ref[...] = out.astype(BF16)
    opk_ref[...] = _pack_bf16_pairs(out)


def _res_ln(res, y, g, b, tm=256):
    m, d = res.shape
    row = pl.BlockSpec((tm, d), lambda i: (i, 0))
    half = pl.BlockSpec((tm, d // 2), lambda i: (i, 0))
    vec = pl.BlockSpec((1, d), lambda i: (0, 0))
    return pl.pallas_call(
        _res_ln_kernel,
        grid=(m // tm,),
        in_specs=[row, row, vec, vec],
        out_specs=[row, row, half],
        out_shape=[jax.ShapeDtypeStruct((m, d), F32), jax.ShapeDtypeStruct((m, d), BF16),
                   jax.ShapeDtypeStruct((m, d // 2), jnp.uint32)],
        compiler_params=_cparams(("arbitrary",)),
        name="res_ln",
    )(res, y, g.reshape(1, d), b.reshape(1, d))


def _shift_mix(cur, prev8, mu, is_first):
    prev_row = jnp.where(is_first, 0.0, prev8[SUBLANES - 1:SUBLANES, :])
    rolled = pltpu.roll(cur, 1, 0)
    row = lax.broadcasted_iota(jnp.int32, cur.shape, 0)
    shifted = jnp.where(row == 0, prev_row, rolled)
    return cur + (shifted - cur) * mu


def _head_block_ones(width):
    r = lax.broadcasted_iota(jnp.int32, (width, width), 0) // RWKV_HEAD_DIM
    c = lax.broadcasted_iota(jnp.int32, (width, width), 1) // RWKV_HEAD_DIM
    return jnp.where(r == c, 1.0, 0.0).astype(BF16)


def _head_sum(x, ones_bd):
    w = ones_bd.shape[0]
    outs = []
    for c in range(x.shape[1] // w):
        hi, lo = _split_bf16(x[:, c * w:(c + 1) * w])
        outs.append(_dot(hi, ones_bd) + _dot(lo, ones_bd))
    return jnp.concatenate(outs, axis=1)


def _rwkv_prep_body(first, r_ref, k_ref, v_ref, rp_ref, kp_ref, vp_ref,
                    wlo_ref, alo_ref, glo_ref, wlop_ref, alop_ref, glop_ref,
                    mur_ref, muk_ref, muv_ref, muw_ref, mua_ref, mug_ref,
                    w2_ref, a2_ref, g2_ref, w0_ref, a0_ref, kk_ref, ka_ref, rk_ref, vres):
    r = _shift_mix(r_ref[...], rp_ref[...], mur_ref[...], first)
    k = _shift_mix(k_ref[...], kp_ref[...], muk_ref[...], first)
    v = _shift_mix(v_ref[...], vp_ref[...], muv_ref[...], first)
    w_lo = _shift_mix(wlo_ref[...], wlop_ref[...], muw_ref[...], first)
    a_lo = _shift_mix(alo_ref[...], alop_ref[...], mua_ref[...], first)
    g_lo = _shift_mix(glo_ref[...], glop_ref[...], mug_ref[...], first)

    z = w0_ref[...] + _dot(jnp.tanh(w_lo).astype(BF16), w2_ref[...])
    nz = -z
    softplus = jnp.maximum(nz, 0.0) + jnp.log(1.0 + jnp.exp(-jnp.abs(nz)))
    log_w = -softplus - 0.5
    ld = -jnp.exp(log_w)
    a = _sigmoid(a0_ref[...] + _dot(a_lo.astype(BF16), a2_ref[...]))
    g = _dot(_sigmoid(g_lo).astype(BF16), g2_ref[...])
    if vres is not None:
        vlo_ref, vlop_ref, muvl_ref, v2_ref, v0_ref, vf_ref = vres
        v_lo = _shift_mix(vlo_ref[...], vlop_ref[...], muvl_ref[...], first)
        mix = _sigmoid(v0_ref[...] + _dot(v_lo.astype(BF16), v2_ref[...]))
        v = v + (vf_ref[...] - v) * mix

    ones_bd = _head_block_ones(2 * LANES)
    kk = k * kk_ref[...]
    ss = _head_sum(kk * kk, ones_bd)
    kn = kk * lax.rsqrt(jnp.maximum(ss, 1e-24))
    k_mod = k * (1.0 + (a - 1.0) * ka_ref[...])
    bonus = _head_sum(r * k_mod * rk_ref[...], ones_bd) * v
    return r, ld, k_mod, v, kn, kn * a, g, bonus


def _make_rwkv_prep_kernel(has_vres):
    n_common = 26

    def kern(*refs):
        common = refs[:n_common]
        if has_vres:
            vres = refs[n_common:n_common + 6]
            outs = refs[n_common + 6:]
        else:
            vres = None
            outs = refs[n_common:]
        first = pl.program_id(0) == 0
        vals = _rwkv_prep_body(first, *common, vres)
        for o_ref, val in zip(outs, vals):
            o_ref[...] = val

    return kern


def _rwkv_prep(proj, mu_p, p, l, v_first, tb):
    t = proj.shape[0]
    has_vres = v_first is not None
    nj = RWKV_WIDTH // RW_CH
    pb = tb // SUBLANES

    def cur(width, col0):
        return pl.BlockSpec((tb, width), lambda i, j: (i, col0 // width))

    def cur_j(col0):
        return pl.BlockSpec((tb, RW_CH), lambda i, j: (i, col0 // RW_CH + j))

    def prev(width, col0):
        return pl.BlockSpec((SUBLANES, width), lambda i, j: (jnp.maximum(i * pb - 1, 0), col0 // width))

    def prev_j(col0):
        return pl.BlockSpec((SUBLANES, RW_CH), lambda i, j: (jnp.maximum(i * pb - 1, 0), col0 // RW_CH + j))

    def vec(width, col0):
        return pl.BlockSpec((1, width), lambda i, j: (0, col0 // width))

    def vec_j(col0=0):
        return pl.BlockSpec((1, RW_CH), lambda i, j: (0, col0 // RW_CH + j))

    def lora(rank):
        return pl.BlockSpec((rank, RW_CH), lambda i, j: (0, j))

    row = lambda a: a.reshape(1, -1)
    g2 = jnp.pad(p["rwkv_g2"][l], ((0, GLO_PAD - GATE_LORA), (0, 0))).astype(BF16)
    args = [proj, proj, proj, proj, proj, proj,
            proj, proj, proj, proj, proj, proj,
            mu_p, mu_p, mu_p, mu_p, mu_p, mu_p,
            p["rwkv_w2"][l].astype(BF16), p["rwkv_a2"][l].astype(BF16), g2,
            row(p["rwkv_w0"][l]), row(p["rwkv_a0"][l]), row(p["rwkv_k_k"][l]), row(p["rwkv_k_a"][l]),
            row(p["rwkv_r_k"][l])]
    in_specs = [cur_j(P_R), cur_j(P_K), cur_j(P_V), prev_j(P_R), prev_j(P_K), prev_j(P_V),
                cur(LANES, P_WLO), cur(LANES, P_ALO), cur(GLO_PAD, P_GLO),
                prev(LANES, P_WLO), prev(LANES, P_ALO), prev(GLO_PAD, P_GLO),
                vec_j(P_R), vec_j(P_K), vec_j(P_V), vec(LANES, P_WLO), vec(LANES, P_ALO), vec(GLO_PAD, P_GLO),
                lora(DECAY_LORA), lora(AAA_LORA), lora(GLO_PAD),
                vec_j(), vec_j(), vec_j(), vec_j(), vec_j()]
    if has_vres:
        v2 = jnp.pad(p["rwkv_v2"][l - 1], ((0, LANES - MV_LORA), (0, 0))).astype(BF16)
        args += [proj, proj, mu_p, v2, row(p["rwkv_v0"][l - 1]), v_first]
        in_specs += [cur(LANES, P_VLO), prev(LANES, P_VLO), vec(LANES, P_VLO), lora(LANES), vec_j(),
                     pl.BlockSpec((tb, RW_CH), lambda i, j: (i, j))]
    out_spec = pl.BlockSpec((tb, RW_CH), lambda i, j: (i, j))
    out_sds = jax.ShapeDtypeStruct((t, RWKV_WIDTH), F32)
    return pl.pallas_call(
        _make_rwkv_prep_kernel(has_vres),
        grid=(t // tb, nj),
        in_specs=in_specs,
        out_specs=[out_spec] * 8,
        out_shape=[out_sds] * 8,
        compiler_params=_cparams(("arbitrary", "arbitrary")),
        name="rwkv_prep",
    )(*args)


def _unit_lower_inverse(a_strict, c):
    row = lax.broadcasted_iota(jnp.int32, (c, c), 0)
    col = lax.broadcasted_iota(jnp.int32, (c, c), 1)
    eye = jnp.where(row == col, 1.0, 0.0)
    bd = lambda x: x.astype(BF16)
    base = SUBLANES
    same_base = row // base == col // base
    d1 = [bd(jnp.where(same_base, a, 0.0)) for a in a_strict]
    d2 = [bd(_dot(d, d)) for d in d1]
    inv = [eye + d.astype(F32) for d in d1]
    inv = [i + _dot(d, bd(i)) for i, d in zip(inv, d2)]
    d4 = [bd(_dot(d, d)) for d in d2]
    inv = [i + _dot(d, bd(i)) for i, d in zip(inv, d4)]
    blk = base
    while blk < c:
        band = (row // (2 * blk) == col // (2 * blk)) & (row // blk != col // blk)
        off = [bd(jnp.where(band, a, 0.0)) for a in a_strict]
        inv_b = [bd(i) for i in inv]
        tmp = [bd(_dot(o, i)) for o, i in zip(off, inv_b)]
        inv = [i + _dot(ib, t) for i, ib, t in zip(inv, inv_b, tmp)]
        blk *= 2
    return inv


def _rwkv_scan_kernel(r_ref, ld_ref, k_ref, v_ref, kn_ref, b_ref, g_ref, bonus_ref, lng_ref, lnb_ref,
                      y_ref, state_ref):
    c = RW_CHUNK
    n = RWKV_HEAD_DIM
    heads = RW_CH // n
    rows = ld_ref.shape[0]
    subs = rows // c

    @pl.when(pl.program_id(1) == 0)
    def _():
        state_ref[...] = jnp.zeros_like(state_ref)

    row = lax.broadcasted_iota(jnp.int32, (c, c), 0)
    col = lax.broadcasted_iota(jnp.int32, (c, c), 1)
    strict = row > col
    incl2 = (lax.broadcasted_iota(jnp.int32, (c, 2 * c), 0)
             >= lax.broadcasted_iota(jnp.int32, (c, 2 * c), 1) % c)
    brow = lax.broadcasted_iota(jnp.int32, (rows, rows), 0)
    bcol = lax.broadcasted_iota(jnp.int32, (rows, rows), 1)
    tri_incl = jnp.where((brow >= bcol) & (brow // c == bcol // c), 1.0, 0.0).astype(BF16)

    ld = ld_ref[...]
    ld_hi, ld_lo = _split_bf16(ld)
    cum = _dot(tri_incl, ld_hi) + _dot(tri_incl, ld_lo)
    g_inc = jnp.exp(cum)
    g_inv = jnp.exp(-cum)
    a_t = -kn_ref[...] * jnp.exp(cum - ld)
    b_t = b_ref[...] * g_inv
    k_t = k_ref[...] * g_inv
    r_t = r_ref[...] * g_inc
    v_all = v_ref[...]

    bd = lambda x: x.astype(BF16)
    idx = [(s, h) for s in range(subs) for h in range(heads)]
    rs = lambda s: slice(s * c, (s + 1) * c)
    ls = lambda h: slice(h * n, (h + 1) * n)
    v_h = [v_all[rs(s), ls(h)] for s, h in idx]
    a_h = [a_t[rs(s), ls(h)] for s, h in idx]
    r_h = [bd(r_t[rs(s), ls(h)]) for s, h in idx]
    ar = [bd(jnp.concatenate([a_h[i], r_t[rs(s), ls(h)]], axis=0)) for i, (s, h) in enumerate(idx)]
    bk = [bd(jnp.concatenate([b_t[rs(s), ls(h)], k_t[rs(s), ls(h)]], axis=0)) for s, h in idx]
    p1 = [_dot_nt(x, y) for x, y in zip(ar, bk)]
    a_ab = [jnp.where(strict, p[:c, :c], 0.0) for p in p1]
    a_ak = [bd(jnp.where(strict, p[:c, c:], 0.0)) for p in p1]
    a_r = [bd(jnp.where(incl2, p[c:, :], 0.0)) for p in p1]
    akv = [_dot(x, bd(y)) for x, y in zip(a_ak, v_h)]
    inv = _unit_lower_inverse(a_ab, c)
    sol = [_dot(bd(inv[i]), bd(jnp.concatenate([a_h[i], akv[i]], axis=1))) for i in range(len(idx))]

    state = [state_ref[h] for h in range(heads)]
    y_rows = []
    for s in range(subs):
        at = lambda lst, h: lst[s * heads + h]
        sb = [bd(x) for x in state]
        u = [_dot_nt(bd(at(sol, h)[:, :n]), sb[h]) + at(sol, h)[:, n:] for h in range(heads)]
        uv = [bd(jnp.concatenate([u[h], at(v_h, h)], axis=0)) for h in range(heads)]
        ys = [_dot_nt(at(r_h, h), sb[h]) + _dot(at(a_r, h), uv[h]) for h in range(heads)]
        g_last = g_inc[(s + 1) * c - 1:(s + 1) * c, :]
        state = [(state[h] + _dot_tn(uv[h], at(bk, h))) * g_last[:, ls(h)] for h in range(heads)]
        y_rows.append(jnp.concatenate(ys, axis=1))
    for h in range(heads):
        state_ref[h] = state[h]
    y = jnp.concatenate(y_rows, axis=0)

    ones_bd = _head_block_ones(2 * LANES)
    mean = _head_sum(y, ones_bd) * (1.0 / n)
    yc = y - mean
    var = _head_sum(yc * yc, ones_bd) * (1.0 / n)
    yn = yc * lax.rsqrt(var + GN_EPS) * lng_ref[...] + lnb_ref[...]
    y_ref[...] = ((yn + bonus_ref[...]) * g_ref[...]).astype(y_ref.dtype)


def _rwkv_scan(prep, ln_g, ln_b):
    t = prep[0].shape[0]
    c = RW_CHUNK * RW_SUB
    blk = pl.BlockSpec((c, RW_CH), lambda j, i: (i, j))
    vec = pl.BlockSpec((1, RW_CH), lambda j, i: (0, j))
    return pl.pallas_call(
        _rwkv_scan_kernel,
        grid=(RWKV_WIDTH // RW_CH, t // c),
        in_specs=[blk] * 8 + [vec, vec],
        out_specs=blk,
        out_shape=jax.ShapeDtypeStruct((t, RWKV_WIDTH), BF16),
        scratch_shapes=[pltpu.VMEM((RW_CH // RWKV_HEAD_DIM, RWKV_HEAD_DIM, RWKV_HEAD_DIM), F32)],
        compiler_params=_cparams(("arbitrary", "arbitrary")),
        name="rwkv_scan",
    )(*prep, ln_g.reshape(1, -1), ln_b.reshape(1, -1))


def _gelu_tanh(x):
    return 0.5 * x * (1.0 + jnp.tanh(np.sqrt(2.0 / np.pi).astype(np.float32) * (x + 0.044715 * (x * x * x))))


def _gmlp_kernel(u_ref, v_ref, lng_ref, lnb_ref, ws_ref, bs_ref, o_ref, *, chunks):
    ch = GMLP_CHUNK
    gd = GMLP_GROUP_DIM
    u = _gelu_tanh(u_ref[...])
    v = _gelu_tanh(v_ref[...])
    mu = jnp.mean(v, axis=-1, keepdims=True)
    vc = v - mu
    var = jnp.mean(vc * vc, axis=-1, keepdims=True)
    vn = (vc * lax.rsqrt(var + LN_EPS) * lng_ref[...] + lnb_ref[...]).astype(BF16)
    row = lax.broadcasted_iota(jnp.int32, (ch, ch), 0)
    col = lax.broadcasted_iota(jnp.int32, (ch, ch), 1)
    causal = row >= col
    bs = bs_ref[...]
    for g in range(GMLP_GROUPS):
        w = jnp.where(causal, ws_ref[g], 0.0).astype(BF16)
        bias = bs[:, g:g + 1]
        for n in range(chunks):
            f = _dot(w, vn[n * ch:(n + 1) * ch, g * gd:(g + 1) * gd]) + bias
            o_ref[n * ch:(n + 1) * ch, g * gd:(g + 1) * gd] = (
                u[n * ch:(n + 1) * ch, g * gd:(g + 1) * gd] * f).astype(o_ref.dtype)


def _gmlp(proj, ln_g, ln_b, w_s, b_s, tb=256):
    t = proj.shape[0]
    bs_t = jnp.pad(b_s.T, ((0, 0), (0, LANES - GMLP_GROUPS)))
    return pl.pallas_call(
        functools.partial(_gmlp_kernel, chunks=tb // GMLP_CHUNK),
        grid=(t // tb,),
        in_specs=[pl.BlockSpec((tb, GMLP_WIDTH), lambda i: (i, P_GU // GMLP_WIDTH)),
                  pl.BlockSpec((tb, GMLP_WIDTH), lambda i: (i, P_GV // GMLP_WIDTH)),
                  pl.BlockSpec((1, GMLP_WIDTH), lambda i: (0, 0)),
                  pl.BlockSpec((1, GMLP_WIDTH), lambda i: (0, 0)),
                  pl.BlockSpec((GMLP_GROUPS, GMLP_CHUNK, GMLP_CHUNK), lambda i: (0, 0, 0)),
                  pl.BlockSpec((GMLP_CHUNK, LANES), lambda i: (0, 0))],
        out_specs=pl.BlockSpec((tb, GMLP_WIDTH), lambda i: (i, 0)),
        out_shape=jax.ShapeDtypeStruct((t, GMLP_WIDTH), BF16),
        compiler_params=_cparams(("arbitrary",)),
        name="gmlp",
    )(proj, proj, ln_g.reshape(1, -1), ln_b.reshape(1, -1), w_s, bs_t)


def _rope_lanes(t, cc, s1, s2):
    return t * cc + pltpu.roll(t, LANES - MLA_ROPE_DIM // 2, 1) * s1 + pltpu.roll(t, MLA_ROPE_DIM // 2, 1) * s2


def _mla_proj_kernel(cq_ref, ckv_ref, kr_ref, qn_ref, kvn_ref, wq_ref, wk_ref, wvt_ref, cc_ref, s1_ref, s2_ref,
                     q_o, kn_o, vt_o, kr_o):
    cq = cq_ref[...]
    qn = (cq * lax.rsqrt(jnp.mean(cq * cq, axis=-1, keepdims=True) + RMS_EPS) * qn_ref[...]).astype(BF16)
    ckv = ckv_ref[...]
    kvn = (ckv * lax.rsqrt(jnp.mean(ckv * ckv, axis=-1, keepdims=True) + RMS_EPS) * kvn_ref[...]).astype(BF16)
    cc, s1, s2 = cc_ref[...], s1_ref[...], s2_ref[...]
    scale = MLA_QK_DIM ** -0.5 * np.log2(np.e)
    for h in range(MLA_HEADS):
        q = _dot(qn, wq_ref[:, 2 * LANES * h:2 * LANES * (h + 1)]) * scale
        q_o[:, 2 * LANES * h:2 * LANES * h + LANES] = q[:, :LANES].astype(BF16)
        q_o[:, 2 * LANES * h + LANES:2 * LANES * (h + 1)] = _rope_lanes(q[:, LANES:], cc, s1, s2).astype(BF16)
    kn_o[...] = _dot(kvn, wk_ref[...]).astype(BF16)
    vt_o[...] = _dot_nt(wvt_ref[...], kvn).astype(BF16)
    kr_o[...] = _rope_lanes(kr_ref[...], cc, s1, s2).astype(BF16)


def _mla_proj(proj, positions, q_norm, kv_norm, w_uq, w_ukv, tm=512):
    t = proj.shape[0]
    h = MLA_HEADS
    half = MLA_ROPE_DIM // 2
    inv_freq = jnp.power(ROPE_THETA, -jnp.arange(0, MLA_ROPE_DIM, 2, dtype=F32) / MLA_ROPE_DIM)
    ang = positions.reshape(t).astype(F32)[:, None] * inv_freq
    cos, sin = jnp.cos(ang), jnp.sin(ang)
    z = jnp.zeros((t, half), F32)
    cc = jnp.concatenate([cos, cos, z, z], axis=1)
    s1 = jnp.concatenate([-sin, z, z, z], axis=1)
    s2 = jnp.concatenate([z, sin, z, z], axis=1)
    wq = w_uq.reshape(MLA_Q_RANK, h, MLA_QK_DIM)
    wq = jnp.pad(wq, ((0, 0), (0, 0), (0, 2 * LANES - MLA_QK_DIM))).reshape(MLA_Q_RANK, h * 2 * LANES).astype(BF16)
    wkv = w_ukv.reshape(MLA_KV_RANK, h, MLA_NOPE_DIM + MLA_V_DIM)
    wk = wkv[:, :, :MLA_NOPE_DIM].reshape(MLA_KV_RANK, h * MLA_NOPE_DIM).astype(BF16)
    wvt = wkv[:, :, MLA_NOPE_DIM:].reshape(MLA_KV_RANK, h * MLA_V_DIM).T.astype(BF16)
    full = lambda a: pl.BlockSpec(a.shape, lambda i: (0,) * a.ndim)
    tab = pl.BlockSpec((tm, LANES), lambda i: (i, 0))
    qn2, kvn2 = q_norm.reshape(1, -1), kv_norm.reshape(1, -1)
    return pl.pallas_call(
        _mla_proj_kernel,
        grid=(t // tm,),
        in_specs=[pl.BlockSpec((tm, MLA_Q_RANK), lambda i: (i, P_CQ // MLA_Q_RANK)),
                  pl.BlockSpec((tm, MLA_KV_RANK), lambda i: (i, P_CKV // MLA_KV_RANK)),
                  pl.BlockSpec((tm, LANES), lambda i: (i, P_KROPE // LANES)),
                  full(qn2), full(kvn2), full(wq), full(wk), full(wvt), tab, tab, tab],
        out_specs=[pl.BlockSpec((tm, h * 2 * LANES), lambda i: (i, 0)),
                   pl.BlockSpec((tm, h * MLA_NOPE_DIM), lambda i: (i, 0)),
                   pl.BlockSpec((h * MLA_V_DIM, tm), lambda i: (0, i)),
                   pl.BlockSpec((tm, LANES), lambda i: (i, 0))],
        out_shape=[jax.ShapeDtypeStruct((t, h * 2 * LANES), BF16),
                   jax.ShapeDtypeStruct((t, h * MLA_NOPE_DIM), BF16),
                   jax.ShapeDtypeStruct((h * MLA_V_DIM, t), BF16),
                   jax.ShapeDtypeStruct((t, LANES), BF16)],
        compiler_params=_cparams(("arbitrary",)),
        name="mla_proj",
    )(proj, proj, proj, qn2, kvn2, wq, wk, wvt, cc, s1, s2)


def _flash_kernel(qi_ref, kj_ref, q_ref, kn_ref, kr_ref, vt_ref, o_ref, m_sc, l_sc, acc_sc, *, tq, tk):
    s = pl.program_id(1)
    qi, kj = qi_ref[s], kj_ref[s]

    @pl.when(kj == 0)
    def _():
        m_sc[...] = jnp.full_like(m_sc, -jnp.inf)
        l_sc[...] = jnp.zeros_like(l_sc)
        acc_sc[...] = jnp.zeros_like(acc_sc)

    def step(masked):
        k = jnp.concatenate([kn_ref[...], kr_ref[...]], axis=1)
        st = _dot_nt(k, q_ref[...])
        if masked:
            key = kj * tk + lax.broadcasted_iota(jnp.int32, (tk, tq), 0)
            qry = qi * tq + lax.broadcasted_iota(jnp.int32, (tk, tq), 1)
            st = jnp.where(key <= qry, st, -jnp.inf)
        m_prev = m_sc[...]
        m_new = jnp.maximum(m_prev, jnp.max(st, axis=0, keepdims=True))
        alpha = jnp.exp2(m_prev - m_new)
        p = jnp.exp2(st - m_new)
        l_sc[...] = alpha * l_sc[...] + jnp.sum(p, axis=0, keepdims=True)
        acc_sc[...] = alpha * acc_sc[...] + _dot(vt_ref[...], p.astype(BF16))
        m_sc[...] = m_new

    last_key_of_block = kj * tk + tk - 1
    on_diag = last_key_of_block > qi * tq

    @pl.when(jnp.logical_not(on_diag))
    def _():
        step(False)

    @pl.when(on_diag)
    def _():
        step(True)

    @pl.when(last_key_of_block >= qi * tq + tq - 1)
    def _():
        o_ref[...] = (acc_sc[...] / l_sc[...]).T.astype(o_ref.dtype)


def _flash(q, kn, vt, kr, tq=512, tk=512):
    t = q.shape[0]
    assert tq % tk == 0
    pairs = [(i, j) for i in range(t // tq) for j in range((i + 1) * tq // tk)]
    qi = jnp.asarray([pr[0] for pr in pairs], jnp.int32)
    kj = jnp.asarray([pr[1] for pr in pairs], jnp.int32)
    grid_spec = pltpu.PrefetchScalarGridSpec(
        num_scalar_prefetch=2,
        grid=(MLA_HEADS, len(pairs)),
        in_specs=[pl.BlockSpec((tq, 2 * LANES), lambda h, s, qi, kj: (qi[s], h)),
                  pl.BlockSpec((tk, MLA_NOPE_DIM), lambda h, s, qi, kj: (kj[s], h)),
                  pl.BlockSpec((tk, LANES), lambda h, s, qi, kj: (kj[s], 0)),
                  pl.BlockSpec((MLA_V_DIM, tk), lambda h, s, qi, kj: (h, kj[s]))],
        out_specs=pl.BlockSpec((tq, MLA_V_DIM), lambda h, s, qi, kj: (qi[s], h)),
        scratch_shapes=[pltpu.VMEM((1, tq), F32), pltpu.VMEM((1, tq), F32), pltpu.VMEM((MLA_V_DIM, tq), F32)],
    )
    return pl.pallas_call(
        functools.partial(_flash_kernel, tq=tq, tk=tk),
        grid_spec=grid_spec,
        out_shape=jax.ShapeDtypeStruct((t, MLA_HEADS * MLA_V_DIM), BF16),
        compiler_params=_cparams(("arbitrary", "arbitrary")),
        name="flash",
    )(qi, kj, q, kn, kr, vt)


def _router_kernel(h_ref, rwt_ref, bias_ref, exp_o, pos_o, gate_o, cnt_o, carry_sc):
    e = N_EXPERTS
    per = e // N_EXPERT_GROUPS
    h_hi, h_lo = _split_bf16(h_ref[...])
    w_hi, w_lo = _split_bf16(rwt_ref[...])
    logits = _dot_nt(w_hi, h_hi) + _dot_nt(w_hi, h_lo) + _dot_nt(w_lo, h_hi)
    scores = _sigmoid(logits)
    biased = scores + bias_ref[...][:, 0:1]
    tb = biased.shape[1]
    neg = -jnp.inf
    sub = lax.broadcasted_iota(jnp.int32, (per, tb), 0)
    grp_rows = []
    for g in range(N_EXPERT_GROUPS):
        blk = biased[g * per:(g + 1) * per, :]
        m1 = jnp.max(blk, axis=0, keepdims=True)
        first = jnp.min(jnp.where(blk == m1, sub, per), axis=0, keepdims=True)
        m2 = jnp.max(jnp.where(sub == first, neg, blk), axis=0, keepdims=True)
        grp_rows.append(m1 + m2)
    grp = jnp.concatenate(grp_rows, axis=0)
    gidx = lax.broadcasted_iota(jnp.int32, grp.shape, 0)
    grank = jnp.zeros(grp.shape, jnp.int32)
    for g in range(N_EXPERT_GROUPS):
        other = grp[g:g + 1, :]
        ahead = (other > grp) | ((other == grp) & (g < gidx))
        grank = grank + jnp.where(ahead, 1, 0)
    gsel = grank < TOPK_GROUPS
    masked = jnp.concatenate(
        [jnp.where(gsel[g:g + 1, :], biased[g * per:(g + 1) * per, :], neg) for g in range(N_EXPERT_GROUPS)], axis=0)
    eidx = lax.broadcasted_iota(jnp.int32, masked.shape, 0)
    rank = jnp.zeros(masked.shape, jnp.int32)
    for j in range(e):
        other = masked[j:j + 1, :]
        ahead = (other > masked) | ((other == masked) & (j < eidx))
        rank = rank + jnp.where(ahead, 1, 0)
    chosen = rank < TOP_K
    sel = jnp.where(chosen, scores, 0.0)
    gate = sel / jnp.sum(sel, axis=0, keepdims=True) * ROUTED_SCALE

    @pl.when(pl.program_id(0) == 0)
    def _():
        carry_sc[...] = jnp.zeros_like(carry_sc)

    chosen_f = jnp.where(chosen, 1.0, 0.0)
    earlier = (lax.broadcasted_iota(jnp.int32, (tb, tb), 0) < lax.broadcasted_iota(jnp.int32, (tb, tb), 1))
    carry = carry_sc[...]
    pos = _dot(chosen_f.astype(BF16), jnp.where(earlier, 1.0, 0.0).astype(BF16)) + carry[:, 0:1]
    carry_sc[...] = carry + jnp.sum(chosen_f, axis=1, keepdims=True)
    cnt_o[...] = carry_sc[...].astype(jnp.int32)

    eidx_f = eidx.astype(F32)
    pick = lambda hit, val: jnp.sum(jnp.where(hit, val, 0.0), axis=0, keepdims=True)
    hits = [rank == k for k in range(TOP_K)]
    exp_o[...] = jnp.concatenate([pick(hit, eidx_f) for hit in hits], axis=0).astype(jnp.int32)
    pos_o[...] = jnp.concatenate([pick(hit, pos) for hit in hits], axis=0).astype(jnp.int32)
    gate_o[...] = jnp.concatenate([pick(hit, gate) for hit in hits], axis=0)


def _router(h, router_w, router_bias, tb=512):
    t, d = h.shape
    bias = jnp.broadcast_to(router_bias.astype(F32)[:, None], (N_EXPERTS, LANES))
    per_tok = pl.BlockSpec((TOP_K, tb), lambda i: (0, i))
    return pl.pallas_call(
        _router_kernel,
        grid=(t // tb,),
        in_specs=[pl.BlockSpec((tb, d), lambda i: (i, 0)),
                  pl.BlockSpec((N_EXPERTS, d), lambda i: (0, 0)),
                  pl.BlockSpec((N_EXPERTS, LANES), lambda i: (0, 0))],
        out_specs=[per_tok, per_tok, per_tok, pl.BlockSpec((N_EXPERTS, LANES), lambda i: (0, 0))],
        out_shape=[jax.ShapeDtypeStruct((TOP_K, t), jnp.int32), jax.ShapeDtypeStruct((TOP_K, t), jnp.int32),
                   jax.ShapeDtypeStruct((TOP_K, t), F32), jax.ShapeDtypeStruct((N_EXPERTS, LANES), jnp.int32)],
        scratch_shapes=[pltpu.VMEM((N_EXPERTS, LANES), F32)],
        compiler_params=_cparams(("arbitrary",)),
        name="router",
    )(h, router_w.T, bias)


HALF_MASK = 0xFFFF0000


def _pack_bf16_pairs(x):
    n = x.shape[1] // 2
    lo = pltpu.bitcast(x[:, :n].astype(BF16).astype(F32), jnp.uint32) >> 16
    hi = pltpu.bitcast(x[:, n:].astype(BF16).astype(F32), jnp.uint32) & jnp.uint32(HALF_MASK)
    return lo | hi


def _unpack_bf16_pairs(w):
    return pltpu.bitcast(w << 16, F32), pltpu.bitcast(w & jnp.uint32(HALF_MASK), F32)


def _dispatch_plan(exp_r, pos_r, cnt, bm, nb_max):
    ids = jnp.arange(N_EXPERTS, dtype=jnp.int32)
    nb = (cnt + bm - 1) // bm
    bend = jnp.sum(jnp.where(ids[None, :] <= ids[:, None], nb[None, :], 0), axis=1)
    total = bend[-1]
    slot_start = (bend - nb) * bm
    start_of = jnp.sum(jnp.where(exp_r[:, :, None] == ids, slot_start, 0), axis=-1)
    slot = (start_of + pos_r).T.reshape(-1)
    blocks = jnp.minimum(jnp.arange(nb_max, dtype=jnp.int32), total - 1)
    blk_exp = jnp.sum(jnp.where(bend[None, :] <= blocks[:, None], 1, 0), axis=1)
    return slot.astype(jnp.int32), blk_exp.astype(jnp.int32), total.reshape(1).astype(jnp.int32)


def _dispatch_kernel(slot_ref, h_ref, xs_init_hbm, xs_hbm, sem, *, tb):
    del xs_init_hbm

    def body(t, carry):
        for k in range(TOP_K):
            s = slot_ref[t * TOP_K + k]
            pltpu.make_async_copy(h_ref.at[pl.ds(t, 1), :], xs_hbm.at[pl.ds(s, 1), :], sem).start()
        return carry

    lax.fori_loop(0, tb, body, 0)
    rows = tb * TOP_K
    pltpu.make_async_copy(xs_hbm.at[pl.ds(0, rows), :], xs_hbm.at[pl.ds(0, rows), :], sem).wait()


def _dispatch(h_pk, slot, n_slots, tb):
    t, w = h_pk.shape
    xs0 = jnp.zeros((n_slots, w), jnp.uint32)
    return pl.pallas_call(
        functools.partial(_dispatch_kernel, tb=tb),
        grid=(t // tb,),
        in_specs=[pl.BlockSpec((tb * TOP_K,), lambda i: (i,), memory_space=pltpu.SMEM),
                  pl.BlockSpec((tb, w), lambda i: (i, 0)),
                  pl.BlockSpec(memory_space=pl.ANY)],
        out_specs=pl.BlockSpec(memory_space=pl.ANY),
        out_shape=jax.ShapeDtypeStruct((n_slots, w), jnp.uint32),
        scratch_shapes=[pltpu.SemaphoreType.DMA(())],
        input_output_aliases={2: 0},
        compiler_params=pltpu.CompilerParams(dimension_semantics=("arbitrary",), vmem_limit_bytes=VMEM_LIMIT,
                                             has_side_effects=True, disable_bounds_checks=True),
        name="moe_dispatch",
    )(slot, h_pk, xs0)


def _new_expert(be_ref):
    b = pl.program_id(0)
    return (b == 0) | (be_ref[b] != be_ref[jnp.maximum(b - 1, 0)])


def _expert_up_kernel(be_ref, tot_ref, xs_ref, w1_ref, w3_ref, act_ref, w13_sc):
    @pl.when(_new_expert(be_ref))
    def _():
        w13_sc[:, :D_EXPERT] = w1_ref[0, 0].astype(BF16)
        w13_sc[:, D_EXPERT:] = w3_ref[0, 0].astype(BF16)

    @pl.when(pl.program_id(0) < tot_ref[0])
    def _():
        lo, hi = _unpack_bf16_pairs(xs_ref[...])
        x = jnp.concatenate([lo.astype(BF16), hi.astype(BF16)], axis=1)
        hgu = _dot(x, w13_sc[...])
        hg, hu = hgu[:, :D_EXPERT], hgu[:, D_EXPERT:]
        act_ref[...] = (hg * _sigmoid(hg) * hu).astype(BF16)

    @pl.when(pl.program_id(0) >= tot_ref[0])
    def _():
        act_ref[...] = jnp.zeros_like(act_ref)


def _expert_down_kernel(be_ref, tot_ref, act_ref, w2_ref, ys_ref, w2_sc):
    @pl.when(_new_expert(be_ref))
    def _():
        w2_sc[...] = w2_ref[0, 0].astype(BF16)

    @pl.when(pl.program_id(0) < tot_ref[0])
    def _():
        ys_ref[...] = _pack_bf16_pairs(_dot(act_ref[...], w2_sc[...]))

    @pl.when(pl.program_id(0) >= tot_ref[0])
    def _():
        ys_ref[...] = jnp.zeros_like(ys_ref)


def _experts(xs, blk_exp, total, w1, w3, w2, layer, bm):
    n_slots, w = xs.shape
    d = 2 * w
    used = lambda b, be, tot: (jnp.minimum(b, tot[0] - 1), 0)
    every = lambda b, be, tot: (b, 0)
    wsel = lambda b, be, tot: (layer, be[b], 0, 0)
    act = pl.pallas_call(
        _expert_up_kernel,
        grid_spec=pltpu.PrefetchScalarGridSpec(
            num_scalar_prefetch=2,
            grid=(n_slots // bm,),
            in_specs=[pl.BlockSpec((bm, w), used),
                      pl.BlockSpec((1, 1, d, D_EXPERT), wsel),
                      pl.BlockSpec((1, 1, d, D_EXPERT), wsel)],
            out_specs=pl.BlockSpec((bm, D_EXPERT), every),
            scratch_shapes=[pltpu.VMEM((d, 2 * D_EXPERT), BF16)]),
        out_shape=jax.ShapeDtypeStruct((n_slots, D_EXPERT), BF16),
        compiler_params=_cparams(("arbitrary",)),
        name="moe_up",
    )(blk_exp, total, xs, w1, w3)
    return pl.pallas_call(
        _expert_down_kernel,
        grid_spec=pltpu.PrefetchScalarGridSpec(
            num_scalar_prefetch=2,
            grid=(n_slots // bm,),
            in_specs=[pl.BlockSpec((bm, D_EXPERT), used),
                      pl.BlockSpec((1, 1, D_EXPERT, d), wsel)],
            out_specs=pl.BlockSpec((bm, w), every),
            scratch_shapes=[pltpu.VMEM((D_EXPERT, d), BF16)]),
        out_shape=jax.ShapeDtypeStruct((n_slots, w), jnp.uint32),
        compiler_params=_cparams(("arbitrary",)),
        name="moe_down",
    )(blk_exp, total, act, w2)


def _combine_kernel(slot_ref, slot_next_ref, gate_ref, h_ref, hb_ref, sw13_ref, sw2_ref, g_ref, b_ref, ys_hbm,
                    of_ref, ob_ref, rows_buf, sem, *, tb, steps):
    i = pl.program_id(0)
    cur = lax.rem(i, 2)

    def issue(table_ref, buf):
        def body(t, carry):
            for k in range(TOP_K):
                s = table_ref[t * TOP_K + k]
                pltpu.make_async_copy(ys_hbm.at[pl.ds(s, 1), :], rows_buf.at[buf, k, pl.ds(t, 1), :],
                                      sem.at[buf]).start()
            return carry

        lax.fori_loop(0, tb, body, 0)

    @pl.when(i == 0)
    def _():
        issue(slot_ref, 0)

    @pl.when(i + 1 < steps)
    def _():
        issue(slot_next_ref, 1 - cur)

    hgu = _dot(hb_ref[...], sw13_ref[...])
    hg, hu = hgu[:, :D_EXPERT], hgu[:, D_EXPERT:]
    shared = _dot((hg * _sigmoid(hg) * hu).astype(BF16), sw2_ref[...])

    pltpu.make_async_copy(rows_buf.at[cur], rows_buf.at[cur], sem.at[cur]).wait()
    gate = gate_ref[...]
    acc_lo = jnp.zeros((tb, rows_buf.shape[-1]), F32)
    acc_hi = jnp.zeros((tb, rows_buf.shape[-1]), F32)
    for k in range(TOP_K):
        lo, hi = _unpack_bf16_pairs(rows_buf[cur, k])
        gk = gate[:, k:k + 1]
        acc_lo = acc_lo + gk * lo
        acc_hi = acc_hi + gk * hi
    t = ALPHA * h_ref[...] + shared + jnp.concatenate([acc_lo, acc_hi], axis=1)
    mu = jnp.mean(t, axis=-1, keepdims=True)
    c = t - mu
    var = jnp.mean(c * c, axis=-1, keepdims=True)
    out = c * lax.rsqrt(var + LN_EPS) * g_ref[...] + b_ref[...]
    of_ref[...] = out
    ob_ref[...] = out.astype(BF16)


def _combine(ys, slot, gate_tk, hf, hb, sw1, sw3, sw2, g, b, tb):
    t, d = hf.shape
    steps = t // tb
    sw13 = jnp.concatenate([sw1, sw3], axis=1).astype(BF16)
    row = pl.BlockSpec((tb, d), lambda i: (i, 0))
    full = lambda a: pl.BlockSpec(a.shape, lambda i: (0,) * a.ndim)
    g2, b2, sw2b = g.reshape(1, d), b.reshape(1, d), sw2.astype(BF16)
    return pl.pallas_call(
        functools.partial(_combine_kernel, tb=tb, steps=steps),
        grid=(steps,),
        in_specs=[pl.BlockSpec((tb * TOP_K,), lambda i: (i,), memory_space=pltpu.SMEM),
                  pl.BlockSpec((tb * TOP_K,), lambda i: (jnp.minimum(i + 1, steps - 1),), memory_space=pltpu.SMEM),
                  pl.BlockSpec((tb, TOP_K), lambda i: (i, 0)),
                  row, row, full(sw13), full(sw2b), full(g2), full(b2),
                  pl.BlockSpec(memory_space=pl.ANY)],
        out_specs=[row, row],
        out_shape=[jax.ShapeDtypeStruct((t, d), F32), jax.ShapeDtypeStruct((t, d), BF16)],
        scratch_shapes=[pltpu.VMEM((2, TOP_K, tb, ys.shape[1]), jnp.uint32), pltpu.SemaphoreType.DMA((2,))],
        compiler_params=pltpu.CompilerParams(dimension_semantics=("arbitrary",), vmem_limit_bytes=VMEM_LIMIT,
                                             disable_bounds_checks=True),
        name="moe_combine",
    )(slot, slot, gate_tk, hf, hb, sw13, sw2b, g2, b2, ys)


def _moe_ffn(hf, hb, h_pk, router_w, router_bias, w1, w3, w2, layer, sw1, sw3, sw2, ln_g, ln_b, bm):
    t = hf.shape[0]
    exp_r, pos_r, gate_r, cnt = _router(hf, router_w, router_bias, tb=min(512, t))
    nb_max = t * TOP_K // bm + N_EXPERTS
    slot, blk_exp, total = _dispatch_plan(exp_r, pos_r, cnt[:, 0], bm, nb_max)
    xs = _dispatch(h_pk, slot, nb_max * bm, tb=min(256, t // TOP_K))
    ys = _experts(xs, blk_exp, total, w1, w3, w2, layer, bm)
    return _combine(ys, slot, gate_r.T, hf, hb, sw1, sw3, sw2, ln_g, ln_b, tb=min(128, t))


def _prep_w_in(w_in, has_vres):
    d = w_in.shape[0]
    sizes = [RWKV_WIDTH, RWKV_WIDTH, RWKV_WIDTH, DECAY_LORA, AAA_LORA, GATE_LORA,
             GMLP_WIDTH, GMLP_WIDTH, MLA_Q_RANK, MLA_KV_RANK, MLA_ROPE_DIM]
    if has_vres:
        sizes.append(MV_LORA)
    offs = np.concatenate([[0], np.cumsum(sizes)])
    seg = [w_in[:, offs[i]:offs[i + 1]] for i in range(len(sizes))]
    r, k, v, w_lo, a_lo, g_lo, gu, gv, cq, ckv, kr = seg[:11]
    z = lambda n: jnp.zeros((d, n), w_in.dtype)
    v_lo = jnp.concatenate([seg[11], z(LANES - MV_LORA)], axis=1) if has_vres else z(LANES)
    cols = [gu, gv, r, k, v, w_lo, a_lo, cq, ckv, g_lo, z(GLO_PAD - GATE_LORA), kr, z(LANES - MLA_ROPE_DIM), v_lo]
    out = jnp.concatenate(cols, axis=1).astype(BF16)
    assert out.shape[1] == P_COLS
    return out


def _prep_mu(mu, mu_vres):
    out = jnp.zeros((1, P_COLS), F32)
    offs = np.concatenate([[0], np.cumsum([RWKV_WIDTH] * 3 + [DECAY_LORA, AAA_LORA, GATE_LORA])])
    for dst, i in zip([P_R, P_K, P_V, P_WLO, P_ALO, P_GLO], range(6)):
        out = lax.dynamic_update_slice(out, mu[offs[i]:offs[i + 1]].reshape(1, -1), (0, dst))
    if mu_vres is not None:
        out = lax.dynamic_update_slice(out, mu_vres.reshape(1, -1), (0, P_VLO))
    return out


def kernel(x, positions, w_in_first, w_in_rest, rwkv_mu, rwkv_mu_vres, rwkv_w0, rwkv_w2, rwkv_a0, rwkv_a2, rwkv_v0, rwkv_v2, rwkv_g2, rwkv_k_k, rwkv_k_a, rwkv_r_k, rwkv_ln_g, rwkv_ln_b, gmlp_ln_g, gmlp_ln_b, gmlp_w_s, gmlp_b_s, mla_q_norm, mla_kv_norm, mla_w_uq, mla_w_ukv, w_out, ln1_g, ln1_b, router_w, router_bias, exp_w1, exp_w3, exp_w2, shared_w1, shared_w3, shared_w2, ln2_g, ln2_b):
    b, s, d = x.shape
    t = b * s
    p = dict(rwkv_w0=rwkv_w0, rwkv_w2=rwkv_w2, rwkv_a0=rwkv_a0, rwkv_a2=rwkv_a2, rwkv_v0=rwkv_v0, rwkv_v2=rwkv_v2,
             rwkv_g2=rwkv_g2, rwkv_k_k=rwkv_k_k, rwkv_k_a=rwkv_k_a, rwkv_r_k=rwkv_r_k)
    xf = x.reshape(t, d)
    xb = xf.astype(BF16)
    v_first = None
    for l in range(DEPTH):
        has_vres = l > 0
        w_in = _prep_w_in(w_in_first if l == 0 else w_in_rest[l - 1], has_vres)
        mu_p = _prep_mu(rwkv_mu[l], rwkv_mu_vres[l - 1] if has_vres else None)
        proj = _matmul(xb, w_in, F32, tm=min(512, t), tn=1280)
        prep = _rwkv_prep(proj, mu_p, p, l, v_first, tb=min(256, t))
        if l == 0:
            v_first = prep[3]
        y_a = _rwkv_scan(prep, rwkv_ln_g[l], rwkv_ln_b[l])
        y_b = _gmlp(proj, gmlp_ln_g[l], gmlp_ln_b[l], gmlp_w_s[l], gmlp_b_s[l], tb=min(256, t))
        q, kn, vt, kr = _mla_proj(proj, positions, mla_q_norm[l], mla_kv_norm[l], mla_w_uq[l], mla_w_ukv[l],
                                  tm=min(512, t))
        y_c = _flash(q, kn, vt, kr, tq=min(1024, t), tk=min(512, t))
        ycat = jnp.concatenate([y_a, y_b, y_c], axis=1)
        mix = _matmul(ycat, w_out[l].astype(BF16), F32, tm=min(512, t), tn=1024)
        hf, hb, h_pk = _res_ln(xf, mix, ln1_g[l], ln1_b[l], tm=min(256, t))
        xf, xb = _moe_ffn(hf, hb, h_pk, router_w[l], router_bias[l],
                          exp_w1, exp_w3, exp_w2, l,
                          shared_w1[l], shared_w3[l], shared_w2[l], ln2_g[l], ln2_b[l], bm=MOE_ROW_BLOCK)
    return xf.reshape(b, s, d)
```

```python
import functools

import jax
import jax.numpy as jnp
import numpy as np
from jax import lax
from jax.experimental import pallas as pl
from jax.experimental.pallas import tpu as pltpu

F32 = jnp.float32
BF16 = jnp.bfloat16

D_MODEL = 4096
DEPTH = 2
RWKV_HEAD_DIM = 64
RWKV_WIDTH = 3 * D_MODEL // 8
DECAY_LORA = 128
AAA_LORA = 128
MV_LORA = 96
GATE_LORA = 480
GN_EPS = 64e-5
GMLP_WIDTH = D_MODEL // 4
GMLP_GROUP_DIM = 128
GMLP_GROUPS = GMLP_WIDTH // GMLP_GROUP_DIM
GMLP_CHUNK = 128
MLA_V_DIM = 128
MLA_WIDTH = D_MODEL - RWKV_WIDTH - GMLP_WIDTH
MLA_HEADS = MLA_WIDTH // MLA_V_DIM
MLA_NOPE_DIM = 128
MLA_ROPE_DIM = 64
MLA_QK_DIM = MLA_NOPE_DIM + MLA_ROPE_DIM
MLA_Q_RANK = 768
MLA_KV_RANK = 512
ROPE_THETA = 10000.0
N_EXPERTS = 64
TOP_K = 8
N_EXPERT_GROUPS = 8
TOPK_GROUPS = 4
D_EXPERT = 384
ROUTED_SCALE = 2.5
ALPHA = (2 * DEPTH) ** 0.25
LN_EPS = 1e-5
RMS_EPS = 1e-6

LANES = 128
SUBLANES = 8
VMEM_LIMIT = 56 * 1024 * 1024

P_GU, P_GV = 0, 1024
P_R, P_K, P_V = 2048, 3584, 5120
P_WLO, P_ALO = 6656, 6784
P_CQ, P_CKV = 6912, 7680
P_GLO, P_KROPE, P_VLO = 8192, 8704, 8832
P_COLS = 8960
GLO_PAD = 512
RW_CH = 512
RW_CHUNK = 64
RW_SUB = 4
MOE_ROW_BLOCK = 512


def _cparams(sem):
    return pltpu.CompilerParams(dimension_semantics=sem, vmem_limit_bytes=VMEM_LIMIT)


def _sigmoid(x):
    return 1.0 / (1.0 + jnp.exp(-x))


def _dot(a, b):
    return jnp.dot(a, b, preferred_element_type=F32)


def _dot_nt(a, b):
    return lax.dot_general(a, b, (((1,), (1,)), ((), ())), preferred_element_type=F32)


def _dot_tn(a, b):
    return lax.dot_general(a, b, (((0,), (0,)), ((), ())), preferred_element_type=F32)


def _split_bf16(x):
    hi = x.astype(BF16)
    lo = (x - hi.astype(F32)).astype(BF16)
    return hi, lo


def _mm_kernel(x_ref, w_ref, o_ref):
    o_ref[...] = _dot(x_ref[...], w_ref[...]).astype(o_ref.dtype)


def _matmul(x, w, out_dtype, tm, tn):
    m, k = x.shape
    n = w.shape[1]
    assert m % tm == 0 and n % tn == 0
    return pl.pallas_call(
        _mm_kernel,
        grid=(n // tn, m // tm),
        in_specs=[pl.BlockSpec((tm, k), lambda j, i: (i, 0)),
                  pl.BlockSpec((k, tn), lambda j, i: (0, j))],
        out_specs=pl.BlockSpec((tm, tn), lambda j, i: (i, j)),
        out_shape=jax.ShapeDtypeStruct((m, n), out_dtype),
        compiler_params=_cparams(("arbitrary", "arbitrary")),
        name="matmul",
    )(x, w)


def _res_ln_kernel(res_ref, y_ref, g_ref, b_ref, of_ref, ob_ref, opk_ref):
    t = ALPHA * res_ref[...] + y_ref[...]
    mu = jnp.mean(t, axis=-1, keepdims=True)
    c = t - mu
    var = jnp.mean(c * c, axis=-1, keepdims=True)
    out = c * lax.rsqrt(var + LN_EPS) * g_ref[...] + b_ref[...]
    of_ref[...] = out
    ob_ref[...] = out.astype(BF16)
    opk_ref[...] = _pack_bf16_pairs(out)


def _res_ln(res, y, g, b, tm=256):
    m, d = res.shape
    row = pl.BlockSpec((tm, d), lambda i: (i, 0))
    half = pl.BlockSpec((tm, d // 2), lambda i: (i, 0))
    vec = pl.BlockSpec((1, d), lambda i: (0, 0))
    return pl.pallas_call(
        _res_ln_kernel,
        grid=(m // tm,),
        in_specs=[row, row, vec, vec],
        out_specs=[row, row, half],
        out_shape=[jax.ShapeDtypeStruct((m, d), F32), jax.ShapeDtypeStruct((m, d), BF16),
                   jax.ShapeDtypeStruct((m, d // 2), jnp.uint32)],
        compiler_params=_cparams(("arbitrary",)),
        name="res_ln",
    )(res, y, g.reshape(1, d), b.reshape(1, d))


def _shift_mix(cur, prev8, mu, is_first):
    prev_row = jnp.where(is_first, 0.0, prev8[SUBLANES - 1:SUBLANES, :])
    rolled = pltpu.roll(cur, 1, 0)
    row = lax.broadcasted_iota(jnp.int32, cur.shape, 0)
    shifted = jnp.where(row == 0, prev_row, rolled)
    return cur + (shifted - cur) * mu


def _head_block_ones(width):
    r = lax.broadcasted_iota(jnp.int32, (width, width), 0) // RWKV_HEAD_DIM
    c = lax.broadcasted_iota(jnp.int32, (width, width), 1) // RWKV_HEAD_DIM
    return jnp.where(r == c, 1.0, 0.0).astype(BF16)


def _head_sum(x, ones_bd):
    w = ones_bd.shape[0]
    outs = []
    for c in range(x.shape[1] // w):
        hi, lo = _split_bf16(x[:, c * w:(c + 1) * w])
        outs.append(_dot(hi, ones_bd) + _dot(lo, ones_bd))
    return jnp.concatenate(outs, axis=1)


def _rwkv_prep_body(first, r_ref, k_ref, v_ref, rp_ref, kp_ref, vp_ref,
                    wlo_ref, alo_ref, glo_ref, wlop_ref, alop_ref, glop_ref,
                    mur_ref, muk_ref, muv_ref, muw_ref, mua_ref, mug_ref,
                    w2_ref, a2_ref, g2_ref, w0_ref, a0_ref, kk_ref, ka_ref, rk_ref, vres):
    r = _shift_mix(r_ref[...], rp_ref[...], mur_ref[...], first)
    k = _shift_mix(k_ref[...], kp_ref[...], muk_ref[...], first)
    v = _shift_mix(v_ref[...], vp_ref[...], muv_ref[...], first)
    w_lo = _shift_mix(wlo_ref[...], wlop_ref[...], muw_ref[...], first)
    a_lo = _shift_mix(alo_ref[...], alop_ref[...], mua_ref[...], first)
    g_lo = _shift_mix(glo_ref[...], glop_ref[...], mug_ref[...], first)

    z = w0_ref[...] + _dot(jnp.tanh(w_lo).astype(BF16), w2_ref[...])
    nz = -z
    softplus = jnp.maximum(nz, 0.0) + jnp.log(1.0 + jnp.exp(-jnp.abs(nz)))
    log_w = -softplus - 0.5
    ld = -jnp.exp(log_w)
    a = _sigmoid(a0_ref[...] + _dot(a_lo.astype(BF16), a2_ref[...]))
    g = _dot(_sigmoid(g_lo).astype(BF16), g2_ref[...])
    if vres is not None:
        vlo_ref, vlop_ref, muvl_ref, v2_ref, v0_ref, vf_ref = vres
        v_lo = _shift_mix(vlo_ref[...], vlop_ref[...], muvl_ref[...], first)
        mix = _sigmoid(v0_ref[...] + _dot(v_lo.astype(BF16), v2_ref[...]))
        v = v + (vf_ref[...] - v) * mix

    ones_bd = _head_block_ones(2 * LANES)
    kk = k * kk_ref[...]
    ss = _head_sum(kk * kk, ones_bd)
    kn = kk * lax.rsqrt(jnp.maximum(ss, 1e-24))
    k_mod = k * (1.0 + (a - 1.0) * ka_ref[...])
    bonus = _head_sum(r * k_mod * rk_ref[...], ones_bd) * v
    return r, ld, k_mod, v, kn, kn * a, g, bonus


def _make_rwkv_prep_kernel(has_vres):
    n_common = 26

    def kern(*refs):
        common = refs[:n_common]
        if has_vres:
            vres = refs[n_common:n_common + 6]
            outs = refs[n_common + 6:]
        else:
            vres = None
            outs = refs[n_common:]
        first = pl.program_id(0) == 0
        vals = _rwkv_prep_body(first, *common, vres)
        for o_ref, val in zip(outs, vals):
            o_ref[...] = val

    return kern


def _rwkv_prep(proj, mu_p, p, l, v_first, tb):
    t = proj.shape[0]
    has_vres = v_first is not None
    nj = RWKV_WIDTH // RW_CH
    pb = tb // SUBLANES

    def cur(width, col0):
        return pl.BlockSpec((tb, width), lambda i, j: (i, col0 // width))

    def cur_j(col0):
        return pl.BlockSpec((tb, RW_CH), lambda i, j: (i, col0 // RW_CH + j))

    def prev(width, col0):
        return pl.BlockSpec((SUBLANES, width), lambda i, j: (jnp.maximum(i * pb - 1, 0), col0 // width))

    def prev_j(col0):
        return pl.BlockSpec((SUBLANES, RW_CH), lambda i, j: (jnp.maximum(i * pb - 1, 0), col0 // RW_CH + j))

    def vec(width, col0):
        return pl.BlockSpec((1, width), lambda i, j: (0, col0 // width))

    def vec_j(col0=0):
        return pl.BlockSpec((1, RW_CH), lambda i, j: (0, col0 // RW_CH + j))

    def lora(rank):
        return pl.BlockSpec((rank, RW_CH), lambda i, j: (0, j))

    row = lambda a: a.reshape(1, -1)
    g2 = jnp.pad(p["rwkv_g2"][l], ((0, GLO_PAD - GATE_LORA), (0, 0))).astype(BF16)
    args = [proj, proj, proj, proj, proj, proj,
            proj, proj, proj, proj, proj, proj,
            mu_p, mu_p, mu_p, mu_p, mu_p, mu_p,
            p["rwkv_w2"][l].astype(BF16), p["rwkv_a2"][l].astype(BF16), g2,
            row(p["rwkv_w0"][l]), row(p["rwkv_a0"][l]), row(p["rwkv_k_k"][l]), row(p["rwkv_k_a"][l]),
            row(p["rwkv_r_k"][l])]
    in_specs = [cur_j(P_R), cur_j(P_K), cur_j(P_V), prev_j(P_R), prev_j(P_K), prev_j(P_V),
                cur(LANES, P_WLO), cur(LANES, P_ALO), cur(GLO_PAD, P_GLO),
                prev(LANES, P_WLO), prev(LANES, P_ALO), prev(GLO_PAD, P_GLO),
                vec_j(P_R), vec_j(P_K), vec_j(P_V), vec(LANES, P_WLO), vec(LANES, P_ALO), vec(GLO_PAD, P_GLO),
                lora(DECAY_LORA), lora(AAA_LORA), lora(GLO_PAD),
                vec_j(), vec_j(), vec_j(), vec_j(), vec_j()]
    if has_vres:
        v2 = jnp.pad(p["rwkv_v2"][l - 1], ((0, LANES - MV_LORA), (0, 0))).astype(BF16)
        args += [proj, proj, mu_p, v2, row(p["rwkv_v0"][l - 1]), v_first]
        in_specs += [cur(LANES, P_VLO), prev(LANES, P_VLO), vec(LANES, P_VLO), lora(LANES), vec_j(),
                     pl.BlockSpec((tb, RW_CH), lambda i, j: (i, j))]
    out_spec = pl.BlockSpec((tb, RW_CH), lambda i, j: (i, j))
    out_sds = jax.ShapeDtypeStruct((t, RWKV_WIDTH), F32)
    return pl.pallas_call(
        _make_rwkv_prep_kernel(has_vres),
        grid=(t // tb, nj),
        in_specs=in_specs,
        out_specs=[out_spec] * 8,
        out_shape=[out_sds] * 8,
        compiler_params=_cparams(("arbitrary", "arbitrary")),
        name="rwkv_prep",
    )(*args)


def _unit_lower_inverse(a_strict, c):
    row = lax.broadcasted_iota(jnp.int32, (c, c), 0)
    col = lax.broadcasted_iota(jnp.int32, (c, c), 1)
    eye = jnp.where(row == col, 1.0, 0.0)
    bd = lambda x: x.astype(BF16)
    base = SUBLANES
    same_base = row // base == col // base
    d1 = [bd(jnp.where(same_base, a, 0.0)) for a in a_strict]
    d2 = [bd(_dot(d, d)) for d in d1]
    inv = [eye + d.astype(F32) for d in d1]
    inv = [i + _dot(d, bd(i)) for i, d in zip(inv, d2)]
    d4 = [bd(_dot(d, d)) for d in d2]
    inv = [i + _dot(d, bd(i)) for i, d in zip(inv, d4)]
    blk = base
    while blk < c:
        band = (row // (2 * blk) == col // (2 * blk)) & (row // blk != col // blk)
        off = [bd(jnp.where(band, a, 0.0)) for a in a_strict]
        inv_b = [bd(i) for i in inv]
        tmp = [bd(_dot(o, i)) for o, i in zip(off, inv_b)]
        inv = [i + _dot(ib, t) for i, ib, t in zip(inv, inv_b, tmp)]
        blk *= 2
    return inv


def _rwkv_scan_kernel(r_ref, ld_ref, k_ref, v_ref, kn_ref, b_ref, g_ref, bonus_ref, lng_ref, lnb_ref,
                      y_ref, state_ref):
    c = RW_CHUNK
    n = RWKV_HEAD_DIM
    heads = RW_CH // n
    rows = ld_ref.shape[0]
    subs = rows // c

    @pl.when(pl.program_id(1) == 0)
    def _():
        state_ref[...] = jnp.zeros_like(state_ref)

    row = lax.broadcasted_iota(jnp.int32, (c, c), 0)
    col = lax.broadcasted_iota(jnp.int32, (c, c), 1)
    strict = row > col
    incl2 = (lax.broadcasted_iota(jnp.int32, (c, 2 * c), 0)
             >= lax.broadcasted_iota(jnp.int32, (c, 2 * c), 1) % c)
    brow = lax.broadcasted_iota(jnp.int32, (rows, rows), 0)
    bcol = lax.broadcasted_iota(jnp.int32, (rows, rows), 1)
    tri_incl = jnp.where((brow >= bcol) & (brow // c == bcol // c), 1.0, 0.0).astype(BF16)

    ld = ld_ref[...]
    ld_hi, ld_lo = _split_bf16(ld)
    cum = _dot(tri_incl, ld_hi) + _dot(tri_incl, ld_lo)
    g_inc = jnp.exp(cum)
    g_inv = jnp.exp(-cum)
    a_t = -kn_ref[...] * jnp.exp(cum - ld)
    b_t = b_ref[...] * g_inv
    k_t = k_ref[...] * g_inv
    r_t = r_ref[...] * g_inc
    v_all = v_ref[...]

    bd = lambda x: x.astype(BF16)
    idx = [(s, h) for s in range(subs) for h in range(heads)]
    rs = lambda s: slice(s * c, (s + 1) * c)
    ls = lambda h: slice(h * n, (h + 1) * n)
    v_h = [v_all[rs(s), ls(h)] for s, h in idx]
    a_h = [a_t[rs(s), ls(h)] for s, h in idx]
    r_h = [bd(r_t[rs(s), ls(h)]) for s, h in idx]
    ar = [bd(jnp.concatenate([a_h[i], r_t[rs(s), ls(h)]], axis=0)) for i, (s, h) in enumerate(idx)]
    bk = [bd(jnp.concatenate([b_t[rs(s), ls(h)], k_t[rs(s), ls(h)]], axis=0)) for s, h in idx]
    p1 = [_dot_nt(x, y) for x, y in zip(ar, bk)]
    a_ab = [jnp.where(strict, p[:c, :c], 0.0) for p in p1]
    a_ak = [bd(jnp.where(strict, p[:c, c:], 0.0)) for p in p1]
    a_r = [bd(jnp.where(incl2, p[c:, :], 0.0)) for p in p1]
    akv = [_dot(x, bd(y)) for x, y in zip(a_ak, v_h)]
    inv = _unit_lower_inverse(a_ab, c)
    sol = [_dot(bd(inv[i]), bd(jnp.concatenate([a_h[i], akv[i]], axis=1))) for i in range(len(idx))]

    state = [state_ref[h] for h in range(heads)]
    y_rows = []
    for s in range(subs):
        at = lambda lst, h: lst[s * heads + h]
        sb = [bd(x) for x in state]
        u = [_dot_nt(bd(at(sol, h)[:, :n]), sb[h]) + at(sol, h)[:, n:] for h in range(heads)]
        uv = [bd(jnp.concatenate([u[h], at(v_h, h)], axis=0)) for h in range(heads)]
        ys = [_dot_nt(at(r_h, h), sb[h]) + _dot(at(a_r, h), uv[h]) for h in range(heads)]
        g_last = g_inc[(s + 1) * c - 1:(s + 1) * c, :]
        state = [(state[h] + _dot_tn(uv[h], at(bk, h))) * g_last[:, ls(h)] for h in range(heads)]
        y_rows.append(jnp.concatenate(ys, axis=1))
    for h in range(heads):
        state_ref[h] = state[h]
    y = jnp.concatenate(y_rows, axis=0)

    ones_bd = _head_block_ones(2 * LANES)
    mean = _head_sum(y, ones_bd) * (1.0 / n)
    yc = y - mean
    var = _head_sum(yc * yc, ones_bd) * (1.0 / n)
    yn = yc * lax.rsqrt(var + GN_EPS) * lng_ref[...] + lnb_ref[...]
    y_ref[...] = ((yn + bonus_ref[...]) * g_ref[...]).astype(y_ref.dtype)


def _rwkv_scan(prep, ln_g, ln_b):
    t = prep[0].shape[0]
    c = RW_CHUNK * RW_SUB
    blk = pl.BlockSpec((c, RW_CH), lambda j, i: (i, j))
    vec = pl.BlockSpec((1, RW_CH), lambda j, i: (0, j))
    return pl.pallas_call(
        _rwkv_scan_kernel,
        grid=(RWKV_WIDTH // RW_CH, t // c),
        in_specs=[blk] * 8 + [vec, vec],
        out_specs=blk,
        out_shape=jax.ShapeDtypeStruct((t, RWKV_WIDTH), BF16),
        scratch_shapes=[pltpu.VMEM((RW_CH // RWKV_HEAD_DIM, RWKV_HEAD_DIM, RWKV_HEAD_DIM), F32)],
        compiler_params=_cparams(("arbitrary", "arbitrary")),
        name="rwkv_scan",
    )(*prep, ln_g.reshape(1, -1), ln_b.reshape(1, -1))


def _gelu_tanh(x):
    return 0.5 * x * (1.0 + jnp.tanh(np.sqrt(2.0 / np.pi).astype(np.float32) * (x + 0.044715 * (x * x * x))))


def _gmlp_kernel(u_ref, v_ref, lng_ref, lnb_ref, ws_ref, bs_ref, o_ref, *, chunks):
    ch = GMLP_CHUNK
    gd = GMLP_GROUP_DIM
    u = _gelu_tanh(u_ref[...])
    v = _gelu_tanh(v_ref[...])
    mu = jnp.mean(v, axis=-1, keepdims=True)
    vc = v - mu
    var = jnp.mean(vc * vc, axis=-1, keepdims=True)
    vn = (vc * lax.rsqrt(var + LN_EPS) * lng_ref[...] + lnb_ref[...]).astype(BF16)
    row = lax.broadcasted_iota(jnp.int32, (ch, ch), 0)
    col = lax.broadcasted_iota(jnp.int32, (ch, ch), 1)
    causal = row >= col
    bs = bs_ref[...]
    for g in range(GMLP_GROUPS):
        w = jnp.where(causal, ws_ref[g], 0.0).astype(BF16)
        bias = bs[:, g:g + 1]
        for n in range(chunks):
            f = _dot(w, vn[n * ch:(n + 1) * ch, g * gd:(g + 1) * gd]) + bias
            o_ref[n * ch:(n + 1) * ch, g * gd:(g + 1) * gd] = (
                u[n * ch:(n + 1) * ch, g * gd:(g + 1) * gd] * f).astype(o_ref.dtype)


def _gmlp(proj, ln_g, ln_b, w_s, b_s, tb=256):
    t = proj.shape[0]
    bs_t = jnp.pad(b_s.T, ((0, 0), (0, LANES - GMLP_GROUPS)))
    return pl.pallas_call(
        functools.partial(_gmlp_kernel, chunks=tb // GMLP_CHUNK),
        grid=(t // tb,),
        in_specs=[pl.BlockSpec((tb, GMLP_WIDTH), lambda i: (i, P_GU // GMLP_WIDTH)),
                  pl.BlockSpec((tb, GMLP_WIDTH), lambda i: (i, P_GV // GMLP_WIDTH)),
                  pl.BlockSpec((1, GMLP_WIDTH), lambda i: (0, 0)),
                  pl.BlockSpec((1, GMLP_WIDTH), lambda i: (0, 0)),
                  pl.BlockSpec((GMLP_GROUPS, GMLP_CHUNK, GMLP_CHUNK), lambda i: (0, 0, 0)),
                  pl.BlockSpec((GMLP_CHUNK, LANES), lambda i: (0, 0))],
        out_specs=pl.BlockSpec((tb, GMLP_WIDTH), lambda i: (i, 0)),
        out_shape=jax.ShapeDtypeStruct((t, GMLP_WIDTH), BF16),
        compiler_params=_cparams(("arbitrary",)),
        name="gmlp",
    )(proj, proj, ln_g.reshape(1, -1), ln_b.reshape(1, -1), w_s, bs_t)


def _rope_lanes(t, cc, s1, s2):
    return t * cc + pltpu.roll(t, LANES - MLA_ROPE_DIM // 2, 1) * s1 + pltpu.roll(t, MLA_ROPE_DIM // 2, 1) * s2


def _mla_proj_kernel(cq_ref, ckv_ref, kr_ref, qn_ref, kvn_ref, wq_ref, wk_ref, wvt_ref, cc_ref, s1_ref, s2_ref,
                     q_o, kn_o, vt_o, kr_o):
    cq = cq_ref[...]
    qn = (cq * lax.rsqrt(jnp.mean(cq * cq, axis=-1, keepdims=True) + RMS_EPS) * qn_ref[...]).astype(BF16)
    ckv = ckv_ref[...]
    kvn = (ckv * lax.rsqrt(jnp.mean(ckv * ckv, axis=-1, keepdims=True) + RMS_EPS) * kvn_ref[...]).astype(BF16)
    cc, s1, s2 = cc_ref[...], s1_ref[...], s2_ref[...]
    scale = MLA_QK_DIM ** -0.5 * np.log2(np.e)
    for h in range(MLA_HEADS):
        q = _dot(qn, wq_ref[:, 2 * LANES * h:2 * LANES * (h + 1)]) * scale
        q_o[:, 2 * LANES * h:2 * LANES * h + LANES] = q[:, :LANES].astype(BF16)
        q_o[:, 2 * LANES * h + LANES:2 * LANES * (h + 1)] = _rope_lanes(q[:, LANES:], cc, s1, s2).astype(BF16)
    kn_o[...] = _dot(kvn, wk_ref[...]).astype(BF16)
    vt_o[...] = _dot_nt(wvt_ref[...], kvn).astype(BF16)
    kr_o[...] = _rope_lanes(kr_ref[...], cc, s1, s2).astype(BF16)


def _mla_proj(proj, positions, q_norm, kv_norm, w_uq, w_ukv, tm=512):
    t = proj.shape[0]
    h = MLA_HEADS
    half = MLA_ROPE_DIM // 2
    inv_freq = jnp.power(ROPE_THETA, -jnp.arange(0, MLA_ROPE_DIM, 2, dtype=F32) / MLA_ROPE_DIM)
    ang = positions.reshape(t).astype(F32)[:, None] * inv_freq
    cos, sin = jnp.cos(ang), jnp.sin(ang)
    z = jnp.zeros((t, half), F32)
    cc = jnp.concatenate([cos, cos, z, z], axis=1)
    s1 = jnp.concatenate([-sin, z, z, z], axis=1)
    s2 = jnp.concatenate([z, sin, z, z], axis=1)
    wq = w_uq.reshape(MLA_Q_RANK, h, MLA_QK_DIM)
    wq = jnp.pad(wq, ((0, 0), (0, 0), (0, 2 * LANES - MLA_QK_DIM))).reshape(MLA_Q_RANK, h * 2 * LANES).astype(BF16)
    wkv = w_ukv.reshape(MLA_KV_RANK, h, MLA_NOPE_DIM + MLA_V_DIM)
    wk = wkv[:, :, :MLA_NOPE_DIM].reshape(MLA_KV_RANK, h * MLA_NOPE_DIM).astype(BF16)
    wvt = wkv[:, :, MLA_NOPE_DIM:].reshape(MLA_KV_RANK, h * MLA_V_DIM).T.astype(BF16)
    full = lambda a: pl.BlockSpec(a.shape, lambda i: (0,) * a.ndim)
    tab = pl.BlockSpec((tm, LANES), lambda i: (i, 0))
    qn2, kvn2 = q_norm.reshape(1, -1), kv_norm.reshape(1, -1)
    return pl.pallas_call(
        _mla_proj_kernel,
        grid=(t // tm,),
        in_specs=[pl.BlockSpec((tm, MLA_Q_RANK), lambda i: (i, P_CQ // MLA_Q_RANK)),
                  pl.BlockSpec((tm, MLA_KV_RANK), lambda i: (i, P_CKV // MLA_KV_RANK)),
                  pl.BlockSpec((tm, LANES), lambda i: (i, P_KROPE // LANES)),
                  full(qn2), full(kvn2), full(wq), full(wk), full(wvt), tab, tab, tab],
        out_specs=[pl.BlockSpec((tm, h * 2 * LANES), lambda i: (i, 0)),
                   pl.BlockSpec((tm, h * MLA_NOPE_DIM), lambda i: (i, 0)),
                   pl.BlockSpec((h * MLA_V_DIM, tm), lambda i: (0, i)),
                   pl.BlockSpec((tm, LANES), lambda i: (i, 0))],
        out_shape=[jax.ShapeDtypeStruct((t, h * 2 * LANES), BF16),
                   jax.ShapeDtypeStruct((t, h * MLA_NOPE_DIM), BF16),
                   jax.ShapeDtypeStruct((h * MLA_V_DIM, t), BF16),
                   jax.ShapeDtypeStruct((t, LANES), BF16)],
        compiler_params=_cparams(("arbitrary",)),
        name="mla_proj",
    )(proj, proj, proj, qn2, kvn2, wq, wk, wvt, cc, s1, s2)


def _flash_kernel(qi_ref, kj_ref, q_ref, kn_ref, kr_ref, vt_ref, o_ref, m_sc, l_sc, acc_sc, st_sc, *, tq, tk):
    s = pl.program_id(1)
    qi, kj = qi_ref[s], kj_ref[s]

    @pl.when(kj == 0)
    def _():
        m_sc[...] = jnp.full_like(m_sc, -jnp.inf)
        l_sc[...] = jnp.zeros_like(l_sc)
        acc_sc[...] = jnp.zeros_like(acc_sc)

    qt = 2 * LANES
    kb = 2 * LANES

    def step(masked):
        k = jnp.concatenate([kn_ref[...], kr_ref[...]], axis=1)
        m_all = m_sc[...]
        m_news, alphas = [], []
        for c0 in range(0, tq, qt):
            st = _dot_nt(k, q_ref[c0:c0 + qt, :])
            if masked:
                key = kj * tk + lax.broadcasted_iota(jnp.int32, (tk, qt), 0)
                qry = qi * tq + c0 + lax.broadcasted_iota(jnp.int32, (tk, qt), 1)
                st = jnp.where(key <= qry, st, -jnp.inf)
            st_sc[:, c0:c0 + qt] = st
            m_prev = m_all[:, c0:c0 + qt]
            m_new = jnp.maximum(m_prev, jnp.max(st, axis=0, keepdims=True))
            m_news.append(m_new)
            alphas.append(jnp.exp2(m_prev - m_new))
        pvs, sums = [], []
        for t_i, c0 in enumerate(range(0, tq, qt)):
            pv, ps = None, None
            for r0 in range(0, tk, kb):
                p = jnp.exp2(st_sc[r0:r0 + kb, c0:c0 + qt] - m_news[t_i])
                part = jnp.sum(p, axis=0, keepdims=True)
                prod = _dot(vt_ref[:, r0:r0 + kb], p.astype(BF16))
                ps = part if ps is None else ps + part
                pv = prod if pv is None else pv + prod
            pvs.append(pv)
            sums.append(ps)
        alpha = jnp.concatenate(alphas, axis=1)
        m_sc[...] = jnp.concatenate(m_news, axis=1)
        l_sc[...] = alpha * l_sc[...] + jnp.concatenate(sums, axis=1)
        acc_sc[...] = alpha * acc_sc[...] + jnp.concatenate(pvs, axis=1)

    last_key_of_block = kj * tk + tk - 1
    on_diag = last_key_of_block > qi * tq

    @pl.when(jnp.logical_not(on_diag))
    def _():
        step(False)

    @pl.when(on_diag)
    def _():
        step(True)

    @pl.when(last_key_of_block >= qi * tq + tq - 1)
    def _():
        o_ref[...] = (acc_sc[...] / l_sc[...]).T.astype(o_ref.dtype)


def _flash(q, kn, vt, kr, tq=512, tk=512):
    t = q.shape[0]
    assert tq % tk == 0
    pairs = [(i, j) for i in range(t // tq) for j in range((i + 1) * tq // tk)]
    qi = jnp.asarray([pr[0] for pr in pairs], jnp.int32)
    kj = jnp.asarray([pr[1] for pr in pairs], jnp.int32)
    grid_spec = pltpu.PrefetchScalarGridSpec(
        num_scalar_prefetch=2,
        grid=(MLA_HEADS, len(pairs)),
        in_specs=[pl.BlockSpec((tq, 2 * LANES), lambda h, s, qi, kj: (qi[s], h)),
                  pl.BlockSpec((tk, MLA_NOPE_DIM), lambda h, s, qi, kj: (kj[s], h)),
                  pl.BlockSpec((tk, LANES), lambda h, s, qi, kj: (kj[s], 0)),
                  pl.BlockSpec((MLA_V_DIM, tk), lambda h, s, qi, kj: (h, kj[s]))],
        out_specs=pl.BlockSpec((tq, MLA_V_DIM), lambda h, s, qi, kj: (qi[s], h)),
        scratch_shapes=[pltpu.VMEM((1, tq), F32), pltpu.VMEM((1, tq), F32), pltpu.VMEM((MLA_V_DIM, tq), F32),
                        pltpu.VMEM((tk, tq), F32)],
    )
    return pl.pallas_call(
        functools.partial(_flash_kernel, tq=tq, tk=tk),
        grid_spec=grid_spec,
        out_shape=jax.ShapeDtypeStruct((t, MLA_HEADS * MLA_V_DIM), BF16),
        compiler_params=_cparams(("arbitrary", "arbitrary")),
        name="flash",
    )(qi, kj, q, kn, kr, vt)


def _router_kernel(h_ref, rwt_ref, bias_ref, exp_o, pos_o, gate_o, cnt_o, carry_sc):
    e = N_EXPERTS
    per = e // N_EXPERT_GROUPS
    h_hi, h_lo = _split_bf16(h_ref[...])
    w_hi, w_lo = _split_bf16(rwt_ref[...])
    logits = _dot_nt(w_hi, h_hi) + _dot_nt(w_hi, h_lo) + _dot_nt(w_lo, h_hi)
    scores = _sigmoid(logits)
    biased = scores + bias_ref[...][:, 0:1]
    tb = biased.shape[1]
    neg = -jnp.inf
    sub = lax.broadcasted_iota(jnp.int32, (per, tb), 0)
    grp_rows = []
    for g in range(N_EXPERT_GROUPS):
        blk = biased[g * per:(g + 1) * per, :]
        m1 = jnp.max(blk, axis=0, keepdims=True)
        first = jnp.min(jnp.where(blk == m1, sub, per), axis=0, keepdims=True)
        m2 = jnp.max(jnp.where(sub == first, neg, blk), axis=0, keepdims=True)
        grp_rows.append(m1 + m2)
    grp = jnp.concatenate(grp_rows, axis=0)
    gidx = lax.broadcasted_iota(jnp.int32, grp.shape, 0)
    grank = jnp.zeros(grp.shape, jnp.int32)
    for g in range(N_EXPERT_GROUPS):
        other = grp[g:g + 1, :]
        ahead = (other > grp) | ((other == grp) & (g < gidx))
        grank = grank + jnp.where(ahead, 1, 0)
    gsel = grank < TOPK_GROUPS
    masked = jnp.concatenate(
        [jnp.where(gsel[g:g + 1, :], biased[g * per:(g + 1) * per, :], neg) for g in range(N_EXPERT_GROUPS)], axis=0)
    eidx = lax.broadcasted_iota(jnp.int32, masked.shape, 0)
    rank = jnp.zeros(masked.shape, jnp.int32)
    for j in range(e):
        other = masked[j:j + 1, :]
        ahead = (other > masked) | ((other == masked) & (j < eidx))
        rank = rank + jnp.where(ahead, 1, 0)
    chosen = rank < TOP_K
    sel = jnp.where(chosen, scores, 0.0)
    gate = sel / jnp.sum(sel, axis=0, keepdims=True) * ROUTED_SCALE

    @pl.when(pl.program_id(0) == 0)
    def _():
        carry_sc[...] = jnp.zeros_like(carry_sc)

    chosen_f = jnp.where(chosen, 1.0, 0.0)
    earlier = (lax.broadcasted_iota(jnp.int32, (tb, tb), 0) < lax.broadcasted_iota(jnp.int32, (tb, tb), 1))
    carry = carry_sc[...]
    pos = _dot(chosen_f.astype(BF16), jnp.where(earlier, 1.0, 0.0).astype(BF16)) + carry[:, 0:1]
    carry_sc[...] = carry + jnp.sum(chosen_f, axis=1, keepdims=True)
    cnt_o[...] = carry_sc[...].astype(jnp.int32)

    eidx_f = eidx.astype(F32)
    pick = lambda hit, val: jnp.sum(jnp.where(hit, val, 0.0), axis=0, keepdims=True)
    hits = [rank == k for k in range(TOP_K)]
    exp_o[...] = jnp.concatenate([pick(hit, eidx_f) for hit in hits], axis=0).astype(jnp.int32)
    pos_o[...] = jnp.concatenate([pick(hit, pos) for hit in hits], axis=0).astype(jnp.int32)
    gate_o[...] = jnp.concatenate([pick(hit, gate) for hit in hits], axis=0)


def _router(h, router_w, router_bias, tb=512):
    t, d = h.shape
    bias = jnp.broadcast_to(router_bias.astype(F32)[:, None], (N_EXPERTS, LANES))
    per_tok = pl.BlockSpec((TOP_K, tb), lambda i: (0, i))
    return pl.pallas_call(
        _router_kernel,
        grid=(t // tb,),
        in_specs=[pl.BlockSpec((tb, d), lambda i: (i, 0)),
                  pl.BlockSpec((N_EXPERTS, d), lambda i: (0, 0)),
                  pl.BlockSpec((N_EXPERTS, LANES), lambda i: (0, 0))],
        out_specs=[per_tok, per_tok, per_tok, pl.BlockSpec((N_EXPERTS, LANES), lambda i: (0, 0))],
        out_shape=[jax.ShapeDtypeStruct((TOP_K, t), jnp.int32), jax.ShapeDtypeStruct((TOP_K, t), jnp.int32),
                   jax.ShapeDtypeStruct((TOP_K, t), F32), jax.ShapeDtypeStruct((N_EXPERTS, LANES), jnp.int32)],
        scratch_shapes=[pltpu.VMEM((N_EXPERTS, LANES), F32)],
        compiler_params=_cparams(("arbitrary",)),
        name="router",
    )(h, router_w.T, bias)


HALF_MASK = 0xFFFF0000


def _pack_bf16_pairs(x):
    n = x.shape[1] // 2
    lo = pltpu.bitcast(x[:, :n].astype(BF16).astype(F32), jnp.uint32) >> 16
    hi = pltpu.bitcast(x[:, n:].astype(BF16).astype(F32), jnp.uint32) & jnp.uint32(HALF_MASK)
    return lo | hi


def _unpack_bf16_pairs(w):
    return pltpu.bitcast(w << 16, F32), pltpu.bitcast(w & jnp.uint32(HALF_MASK), F32)


def _dispatch_plan(exp_r, pos_r, cnt, bm, nb_max):
    ids = jnp.arange(N_EXPERTS, dtype=jnp.int32)
    nb = (cnt + bm - 1) // bm
    bend = jnp.sum(jnp.where(ids[None, :] <= ids[:, None], nb[None, :], 0), axis=1)
    total = bend[-1]
    slot_start = (bend - nb) * bm
    start_of = jnp.sum(jnp.where(exp_r[:, :, None] == ids, slot_start, 0), axis=-1)
    slot = (start_of + pos_r).T.reshape(-1)
    blocks = jnp.minimum(jnp.arange(nb_max, dtype=jnp.int32), total - 1)
    blk_exp = jnp.sum(jnp.where(bend[None, :] <= blocks[:, None], 1, 0), axis=1)
    return slot.astype(jnp.int32), blk_exp.astype(jnp.int32), total.reshape(1).astype(jnp.int32)


def _dispatch_kernel(slot_ref, h_ref, xs_init_hbm, xs_hbm, sem, *, tb):
    del xs_init_hbm

    def body(t, carry):
        for k in range(TOP_K):
            s = slot_ref[t * TOP_K + k]
            pltpu.make_async_copy(h_ref.at[pl.ds(t, 1), :], xs_hbm.at[pl.ds(s, 1), :], sem).start()
        return carry

    lax.fori_loop(0, tb, body, 0)
    rows = tb * TOP_K
    pltpu.make_async_copy(xs_hbm.at[pl.ds(0, rows), :], xs_hbm.at[pl.ds(0, rows), :], sem).wait()


def _dispatch(h_pk, slot, n_slots, tb):
    t, w = h_pk.shape
    xs0 = jnp.zeros((n_slots, w), jnp.uint32)
    return pl.pallas_call(
        functools.partial(_dispatch_kernel, tb=tb),
        grid=(t // tb,),
        in_specs=[pl.BlockSpec((tb * TOP_K,), lambda i: (i,), memory_space=pltpu.SMEM),
                  pl.BlockSpec((tb, w), lambda i: (i, 0)),
                  pl.BlockSpec(memory_space=pl.ANY)],
        out_specs=pl.BlockSpec(memory_space=pl.ANY),
        out_shape=jax.ShapeDtypeStruct((n_slots, w), jnp.uint32),
        scratch_shapes=[pltpu.SemaphoreType.DMA(())],
        input_output_aliases={2: 0},
        compiler_params=pltpu.CompilerParams(dimension_semantics=("arbitrary",), vmem_limit_bytes=VMEM_LIMIT,
                                             has_side_effects=True, disable_bounds_checks=True),
        name="moe_dispatch",
    )(slot, h_pk, xs0)


def _new_expert(be_ref):
    b = pl.program_id(0)
    return (b == 0) | (be_ref[b] != be_ref[jnp.maximum(b - 1, 0)])


def _expert_up_kernel(be_ref, tot_ref, xs_ref, w1_ref, w3_ref, act_ref, w13_sc):
    @pl.when(_new_expert(be_ref))
    def _():
        w13_sc[:, :D_EXPERT] = w1_ref[0, 0].astype(BF16)
        w13_sc[:, D_EXPERT:] = w3_ref[0, 0].astype(BF16)

    @pl.when(pl.program_id(0) < tot_ref[0])
    def _():
        lo, hi = _unpack_bf16_pairs(xs_ref[...])
        x = jnp.concatenate([lo.astype(BF16), hi.astype(BF16)], axis=1)
        hgu = _dot(x, w13_sc[...])
        hg, hu = hgu[:, :D_EXPERT], hgu[:, D_EXPERT:]
        act_ref[...] = (hg * _sigmoid(hg) * hu).astype(BF16)

    @pl.when(pl.program_id(0) >= tot_ref[0])
    def _():
        act_ref[...] = jnp.zeros_like(act_ref)


def _expert_down_kernel(be_ref, tot_ref, act_ref, w2_ref, ys_ref, w2_sc):
    @pl.when(_new_expert(be_ref))
    def _():
        w2_sc[...] = w2_ref[0, 0].astype(BF16)

    @pl.when(pl.program_id(0) < tot_ref[0])
    def _():
        ys_ref[...] = _pack_bf16_pairs(_dot(act_ref[...], w2_sc[...]))

    @pl.when(pl.program_id(0) >= tot_ref[0])
    def _():
        ys_ref[...] = jnp.zeros_like(ys_ref)


def _experts(xs, blk_exp, total, w1, w3, w2, layer, bm):
    n_slots, w = xs.shape
    d = 2 * w
    used = lambda b, be, tot: (jnp.minimum(b, tot[0] - 1), 0)
    every = lambda b, be, tot: (b, 0)
    wsel = lambda b, be, tot: (layer, be[b], 0, 0)
    act = pl.pallas_call(
        _expert_up_kernel,
        grid_spec=pltpu.PrefetchScalarGridSpec(
            num_scalar_prefetch=2,
            grid=(n_slots // bm,),
            in_specs=[pl.BlockSpec((bm, w), used),
                      pl.BlockSpec((1, 1, d, D_EXPERT), wsel),
                      pl.BlockSpec((1, 1, d, D_EXPERT), wsel)],
            out_specs=pl.BlockSpec((bm, D_EXPERT), every),
            scratch_shapes=[pltpu.VMEM((d, 2 * D_EXPERT), BF16)]),
        out_shape=jax.ShapeDtypeStruct((n_slots, D_EXPERT), BF16),
        compiler_params=_cparams(("arbitrary",)),
        name="moe_up",
    )(blk_exp, total, xs, w1, w3)
    return pl.pallas_call(
        _expert_down_kernel,
        grid_spec=pltpu.PrefetchScalarGridSpec(
            num_scalar_prefetch=2,
            grid=(n_slots // bm,),
            in_specs=[pl.BlockSpec((bm, D_EXPERT), used),
                      pl.BlockSpec((1, 1, D_EXPERT, d), wsel)],
            out_specs=pl.BlockSpec((bm, w), every),
            scratch_shapes=[pltpu.VMEM((D_EXPERT, d), BF16)]),
        out_shape=jax.ShapeDtypeStruct((n_slots, w), jnp.uint32),
        compiler_params=_cparams(("arbitrary",)),
        name="moe_down",
    )(blk_exp, total, act, w2)


def _combine_kernel(slot_ref, slot_next_ref, gate_ref, h_ref, hb_ref, sw13_ref, sw2_ref, g_ref, b_ref, ys_hbm,
                    of_ref, ob_ref, rows_buf, sem, *, tb, steps):
    i = pl.program_id(0)
    cur = lax.rem(i, 2)

    def issue(table_ref, buf):
        def body(t, carry):
            for k in range(TOP_K):
                s = table_ref[t * TOP_K + k]
                pltpu.make_async_copy(ys_hbm.at[pl.ds(s, 1), :], rows_buf.at[buf, k, pl.ds(t, 1), :],
                                      sem.at[buf]).start()
            return carry

        lax.fori_loop(0, tb, body, 0)

    @pl.when(i == 0)
    def _():
        issue(slot_ref, 0)

    @pl.when(i + 1 < steps)
    def _():
        issue(slot_next_ref, 1 - cur)

    hgu = _dot(hb_ref[...], sw13_ref[...])
    hg, hu = hgu[:, :D_EXPERT], hgu[:, D_EXPERT:]
    shared = _dot((hg * _sigmoid(hg) * hu).astype(BF16), sw2_ref[...])

    pltpu.make_async_copy(rows_buf.at[cur], rows_buf.at[cur], sem.at[cur]).wait()
    gate = gate_ref[...]
    acc_lo = jnp.zeros((tb, rows_buf.shape[-1]), F32)
    acc_hi = jnp.zeros((tb, rows_buf.shape[-1]), F32)
    for k in range(TOP_K):
        lo, hi = _unpack_bf16_pairs(rows_buf[cur, k])
        gk = gate[:, k:k + 1]
        acc_lo = acc_lo + gk * lo
        acc_hi = acc_hi + gk * hi
    t = ALPHA * h_ref[...] + shared + jnp.concatenate([acc_lo, acc_hi], axis=1)
    mu = jnp.mean(t, axis=-1, keepdims=True)
    c = t - mu
    var = jnp.mean(c * c, axis=-1, keepdims=True)
    out = c * lax.rsqrt(var + LN_EPS) * g_ref[...] + b_ref[...]
    of_ref[...] = out
    ob_ref[...] = out.astype(BF16)


def _combine(ys, slot, gate_tk, hf, hb, sw1, sw3, sw2, g, b, tb):
    t, d = hf.shape
    steps = t // tb
    sw13 = jnp.concatenate([sw1, sw3], axis=1).astype(BF16)
    row = pl.BlockSpec((tb, d), lambda i: (i, 0))
    full = lambda a: pl.BlockSpec(a.shape, lambda i: (0,) * a.ndim)
    g2, b2, sw2b = g.reshape(1, d), b.reshape(1, d), sw2.astype(BF16)
    return pl.pallas_call(
        functools.partial(_combine_kernel, tb=tb, steps=steps),
        grid=(steps,),
        in_specs=[pl.BlockSpec((tb * TOP_K,), lambda i: (i,), memory_space=pltpu.SMEM),
                  pl.BlockSpec((tb * TOP_K,), lambda i: (jnp.minimum(i + 1, steps - 1),), memory_space=pltpu.SMEM),
                  pl.BlockSpec((tb, TOP_K), lambda i: (i, 0)),
                  row, row, full(sw13), full(sw2b), full(g2), full(b2),
                  pl.BlockSpec(memory_space=pl.ANY)],
        out_specs=[row, row],
        out_shape=[jax.ShapeDtypeStruct((t, d), F32), jax.ShapeDtypeStruct((t, d), BF16)],
        scratch_shapes=[pltpu.VMEM((2, TOP_K, tb, ys.shape[1]), jnp.uint32), pltpu.SemaphoreType.DMA((2,))],
        compiler_params=pltpu.CompilerParams(dimension_semantics=("arbitrary",), vmem_limit_bytes=VMEM_LIMIT,
                                             disable_bounds_checks=True),
        name="moe_combine",
    )(slot, slot, gate_tk, hf, hb, sw13, sw2b, g2, b2, ys)


def _moe_ffn(hf, hb, h_pk, router_w, router_bias, w1, w3, w2, layer, sw1, sw3, sw2, ln_g, ln_b, bm):
    t = hf.shape[0]
    exp_r, pos_r, gate_r, cnt = _router(hf, router_w, router_bias, tb=min(512, t))
    nb_max = t * TOP_K // bm + N_EXPERTS
    slot, blk_exp, total = _dispatch_plan(exp_r, pos_r, cnt[:, 0], bm, nb_max)
    xs = _dispatch(h_pk, slot, nb_max * bm, tb=min(256, t // TOP_K))
    ys = _experts(xs, blk_exp, total, w1, w3, w2, layer, bm)
    return _combine(ys, slot, gate_r.T, hf, hb, sw1, sw3, sw2, ln_g, ln_b, tb=min(128, t))


def _prep_w_in(w_in, has_vres):
    d = w_in.shape[0]
    sizes = [RWKV_WIDTH, RWKV_WIDTH, RWKV_WIDTH, DECAY_LORA, AAA_LORA, GATE_LORA,
             GMLP_WIDTH, GMLP_WIDTH, MLA_Q_RANK, MLA_KV_RANK, MLA_ROPE_DIM]
    if has_vres:
        sizes.append(MV_LORA)
    offs = np.concatenate([[0], np.cumsum(sizes)])
    seg = [w_in[:, offs[i]:offs[i + 1]] for i in range(len(sizes))]
    r, k, v, w_lo, a_lo, g_lo, gu, gv, cq, ckv, kr = seg[:11]
    z = lambda n: jnp.zeros((d, n), w_in.dtype)
    v_lo = jnp.concatenate([seg[11], z(LANES - MV_LORA)], axis=1) if has_vres else z(LANES)
    cols = [gu, gv, r, k, v, w_lo, a_lo, cq, ckv, g_lo, z(GLO_PAD - GATE_LORA), kr, z(LANES - MLA_ROPE_DIM), v_lo]
    out = jnp.concatenate(cols, axis=1).astype(BF16)
    assert out.shape[1] == P_COLS
    return out


def _prep_mu(mu, mu_vres):
    out = jnp.zeros((1, P_COLS), F32)
    offs = np.concatenate([[0], np.cumsum([RWKV_WIDTH] * 3 + [DECAY_LORA, AAA_LORA, GATE_LORA])])
    for dst, i in zip([P_R, P_K, P_V, P_WLO, P_ALO, P_GLO], range(6)):
        out = lax.dynamic_update_slice(out, mu[offs[i]:offs[i + 1]].reshape(1, -1), (0, dst))
    if mu_vres is not None:
        out = lax.dynamic_update_slice(out, mu_vres.reshape(1, -1), (0, P_VLO))
    return out


def kernel(x, positions, w_in_first, w_in_rest, rwkv_mu, rwkv_mu_vres, rwkv_w0, rwkv_w2, rwkv_a0, rwkv_a2, rwkv_v0, rwkv_v2, rwkv_g2, rwkv_k_k, rwkv_k_a, rwkv_r_k, rwkv_ln_g, rwkv_ln_b, gmlp_ln_g, gmlp_ln_b, gmlp_w_s, gmlp_b_s, mla_q_norm, mla_kv_norm, mla_w_uq, mla_w_ukv, w_out, ln1_g, ln1_b, router_w, router_bias, exp_w1, exp_w3, exp_w2, shared_w1, shared_w3, shared_w2, ln2_g, ln2_b):
    b, s, d = x.shape
    t = b * s
    p = dict(rwkv_w0=rwkv_w0, rwkv_w2=rwkv_w2, rwkv_a0=rwkv_a0, rwkv_a2=rwkv_a2, rwkv_v0=rwkv_v0, rwkv_v2=rwkv_v2,
             rwkv_g2=rwkv_g2, rwkv_k_k=rwkv_k_k, rwkv_k_a=rwkv_k_a, rwkv_r_k=rwkv_r_k)
    xf = x.reshape(t, d)
    xb = xf.astype(BF16)
    v_first = None
    for l in range(DEPTH):
        has_vres = l > 0
        w_in = _prep_w_in(w_in_first if l == 0 else w_in_rest[l - 1], has_vres)
        mu_p = _prep_mu(rwkv_mu[l], rwkv_mu_vres[l - 1] if has_vres else None)
        proj = _matmul(xb, w_in, F32, tm=min(512, t), tn=1280)
        prep = _rwkv_prep(proj, mu_p, p, l, v_first, tb=min(256, t))
        if l == 0:
            v_first = prep[3]
        y_a = _rwkv_scan(prep, rwkv_ln_g[l], rwkv_ln_b[l])
        y_b = _gmlp(proj, gmlp_ln_g[l], gmlp_ln_b[l], gmlp_w_s[l], gmlp_b_s[l], tb=min(256, t))
        q, kn, vt, kr = _mla_proj(proj, positions, mla_q_norm[l], mla_kv_norm[l], mla_w_uq[l], mla_w_ukv[l],
                                  tm=min(512, t))
        y_c = _flash(q, kn, vt, kr, tq=min(1024, t), tk=min(1024, t))
        ycat = jnp.concatenate([y_a, y_b, y_c], axis=1)
        mix = _matmul(ycat, w_out[l].astype(BF16), F32, tm=min(512, t), tn=1024)
        hf, hb, h_pk = _res_ln(xf, mix, ln1_g[l], ln1_b[l], tm=min(256, t))
        xf, xb = _moe_ffn(hf, hb, h_pk, router_w[l], router_bias[l],
                          exp_w1, exp_w3, exp_w2, l,
                          shared_w1[l], shared_w3[l], shared_w2[l], ln2_g[l], ln2_b[l], bm=MOE_ROW_BLOCK)
    return xf.reshape(b, s, d)
```

```python
import functools

import jax
import jax.numpy as jnp
import numpy as np
from jax import lax
from jax.experimental import pallas as pl
from jax.experimental.pallas import tpu as pltpu

F32 = jnp.float32
BF16 = jnp.bfloat16

D_MODEL = 4096
DEPTH = 2
RWKV_HEAD_DIM = 64
RWKV_WIDTH = 3 * D_MODEL // 8
DECAY_LORA = 128
AAA_LORA = 128
MV_LORA = 96
GATE_LORA = 480
GN_EPS = 64e-5
GMLP_WIDTH = D_MODEL // 4
GMLP_GROUP_DIM = 128
GMLP_GROUPS = GMLP_WIDTH // GMLP_GROUP_DIM
GMLP_CHUNK = 128
MLA_V_DIM = 128
MLA_WIDTH = D_MODEL - RWKV_WIDTH - GMLP_WIDTH
MLA_HEADS = MLA_WIDTH // MLA_V_DIM
MLA_NOPE_DIM = 128
MLA_ROPE_DIM = 64
MLA_QK_DIM = MLA_NOPE_DIM + MLA_ROPE_DIM
MLA_Q_RANK = 768
MLA_KV_RANK = 512
ROPE_THETA = 10000.0
N_EXPERTS = 64
TOP_K = 8
N_EXPERT_GROUPS = 8
TOPK_GROUPS = 4
D_EXPERT = 384
ROUTED_SCALE = 2.5
ALPHA = (2 * DEPTH) ** 0.25
LN_EPS = 1e-5
RMS_EPS = 1e-6

LANES = 128
SUBLANES = 8
VMEM_LIMIT = 56 * 1024 * 1024

P_GU, P_GV = 0, 1024
P_R, P_K, P_V = 2048, 3584, 5120
P_WLO, P_ALO = 6656, 6784
P_CQ, P_CKV = 6912, 7680
P_GLO, P_KROPE, P_VLO = 8192, 8704, 8832
P_COLS = 8960
GLO_PAD = 512
RW_CH = 512
RW_CHUNK = 64
RW_SUB = 4
MOE_ROW_BLOCK = 512


def _cparams(sem):
    return pltpu.CompilerParams(dimension_semantics=sem, vmem_limit_bytes=VMEM_LIMIT)


def _sigmoid(x):
    return 1.0 / (1.0 + jnp.exp(-x))


def _dot(a, b):
    return jnp.dot(a, b, preferred_element_type=F32)


def _dot_nt(a, b):
    return lax.dot_general(a, b, (((1,), (1,)), ((), ())), preferred_element_type=F32)


def _dot_tn(a, b):
    return lax.dot_general(a, b, (((0,), (0,)), ((), ())), preferred_element_type=F32)


def _split_bf16(x):
    hi = x.astype(BF16)
    lo = (x - hi.astype(F32)).astype(BF16)
    return hi, lo


def _mm_kernel(x_ref, w_ref, o_ref):
    o_ref[...] = _dot(x_ref[...], w_ref[...]).astype(o_ref.dtype)


def _matmul(x, w, out_dtype, tm, tn):
    m, k = x.shape
    n = w.shape[1]
    assert m % tm == 0 and n % tn == 0
    return pl.pallas_call(
        _mm_kernel,
        grid=(n // tn, m // tm),
        in_specs=[pl.BlockSpec((tm, k), lambda j, i: (i, 0)),
                  pl.BlockSpec((k, tn), lambda j, i: (0, j))],
        out_specs=pl.BlockSpec((tm, tn), lambda j, i: (i, j)),
        out_shape=jax.ShapeDtypeStruct((m, n), out_dtype),
        compiler_params=_cparams(("arbitrary", "arbitrary")),
        name="matmul",
    )(x, w)


def _out_proj_kernel(ya_ref, yb_ref, yc_ref, wa_ref, wb_ref, wc_ref, o_ref):
    o_ref[...] = _dot(ya_ref[...], wa_ref[...]) + _dot(yb_ref[...], wb_ref[...]) + _dot(yc_ref[...], wc_ref[...])


def _out_proj(y_a, y_b, y_c, w_out, tm, tn):
    m = y_a.shape[0]
    n = w_out.shape[1]
    ka, kb = y_a.shape[1], y_b.shape[1]
    w = w_out.astype(BF16)
    parts = [w[:ka], w[ka:ka + kb], w[ka + kb:]]
    ys = [y_a, y_b, y_c]
    return pl.pallas_call(
        _out_proj_kernel,
        grid=(n // tn, m // tm),
        in_specs=[pl.BlockSpec((tm, y.shape[1]), lambda j, i: (i, 0)) for y in ys]
        + [pl.BlockSpec((p.shape[0], tn), lambda j, i: (0, j)) for p in parts],
        out_specs=pl.BlockSpec((tm, tn), lambda j, i: (i, j)),
        out_shape=jax.ShapeDtypeStruct((m, n), F32),
        compiler_params=_cparams(("arbitrary", "arbitrary")),
        name="out_proj",
    )(*ys, *parts)


def _res_ln_kernel(res_ref, y_ref, g_ref, b_ref, of_ref, ob_ref, opk_ref):
    t = ALPHA * res_ref[...] + y_ref[...]
    mu = jnp.mean(t, axis=-1, keepdims=True)
    c = t - mu
    var = jnp.mean(c * c, axis=-1, keepdims=True)
    out = c * lax.rsqrt(var + LN_EPS) * g_ref[...] + b_ref[...]
    of_ref[...] = out
    ob_ref[...] = out.astype(BF16)
    opk_ref[...] = _pack_bf16_pairs(out)


def _res_ln(res, y, g, b, tm=256):
    m, d = res.shape
    row = pl.BlockSpec((tm, d), lambda i: (i, 0))
    half = pl.BlockSpec((tm, d // 2), lambda i: (i, 0))
    vec = pl.BlockSpec((1, d), lambda i: (0, 0))
    return pl.pallas_call(
        _res_ln_kernel,
        grid=(m // tm,),
        in_specs=[row, row, vec, vec],
        out_specs=[row, row, half],
        out_shape=[jax.ShapeDtypeStruct((m, d), F32), jax.ShapeDtypeStruct((m, d), BF16),
                   jax.ShapeDtypeStruct((m, d // 2), jnp.uint32)],
        compiler_params=_cparams(("arbitrary",)),
        name="res_ln",
    )(res, y, g.reshape(1, d), b.reshape(1, d))


def _shift_mix(cur, prev8, mu, is_first):
    prev_row = jnp.where(is_first, 0.0, prev8[SUBLANES - 1:SUBLANES, :])
    rolled = pltpu.roll(cur, 1, 0)
    row = lax.broadcasted_iota(jnp.int32, cur.shape, 0)
    shifted = jnp.where(row == 0, prev_row, rolled)
    return cur + (shifted - cur) * mu


def _head_block_ones(width):
    r = lax.broadcasted_iota(jnp.int32, (width, width), 0) // RWKV_HEAD_DIM
    c = lax.broadcasted_iota(jnp.int32, (width, width), 1) // RWKV_HEAD_DIM
    return jnp.where(r == c, 1.0, 0.0).astype(BF16)


def _head_sum(x, ones_bd):
    w = ones_bd.shape[0]
    outs = []
    for c in range(x.shape[1] // w):
        hi, lo = _split_bf16(x[:, c * w:(c + 1) * w])
        outs.append(_dot(hi, ones_bd) + _dot(lo, ones_bd))
    return jnp.concatenate(outs, axis=1)


def _rwkv_prep_body(first, r_ref, k_ref, v_ref, rp_ref, kp_ref, vp_ref,
                    wlo_ref, alo_ref, glo_ref, wlop_ref, alop_ref, glop_ref,
                    mur_ref, muk_ref, muv_ref, muw_ref, mua_ref, mug_ref,
                    w2_ref, a2_ref, g2_ref, w0_ref, a0_ref, kk_ref, ka_ref, rk_ref, vres):
    r = _shift_mix(r_ref[...], rp_ref[...], mur_ref[...], first)
    k = _shift_mix(k_ref[...], kp_ref[...], muk_ref[...], first)
    v = _shift_mix(v_ref[...], vp_ref[...], muv_ref[...], first)
    w_lo = _shift_mix(wlo_ref[...], wlop_ref[...], muw_ref[...], first)
    a_lo = _shift_mix(alo_ref[...], alop_ref[...], mua_ref[...], first)
    g_lo = _shift_mix(glo_ref[...], glop_ref[...], mug_ref[...], first)

    z = w0_ref[...] + _dot(jnp.tanh(w_lo).astype(BF16), w2_ref[...])
    nz = -z
    softplus = jnp.maximum(nz, 0.0) + jnp.log(1.0 + jnp.exp(-jnp.abs(nz)))
    log_w = -softplus - 0.5
    ld = -jnp.exp(log_w)
    a = _sigmoid(a0_ref[...] + _dot(a_lo.astype(BF16), a2_ref[...]))
    g = _dot(_sigmoid(g_lo).astype(BF16), g2_ref[...])
    if vres is not None:
        vlo_ref, vlop_ref, muvl_ref, v2_ref, v0_ref, vf_ref = vres
        v_lo = _shift_mix(vlo_ref[...], vlop_ref[...], muvl_ref[...], first)
        mix = _sigmoid(v0_ref[...] + _dot(v_lo.astype(BF16), v2_ref[...]))
        v = v + (vf_ref[...] - v) * mix

    ones_bd = _head_block_ones(2 * LANES)
    kk = k * kk_ref[...]
    ss = _head_sum(kk * kk, ones_bd)
    kn = kk * lax.rsqrt(jnp.maximum(ss, 1e-24))
    k_mod = k * (1.0 + (a - 1.0) * ka_ref[...])
    bonus = _head_sum(r * k_mod * rk_ref[...], ones_bd) * v
    return r, ld, k_mod, v, kn, kn * a, g, bonus


def _make_rwkv_prep_kernel(has_vres):
    n_common = 26

    def kern(*refs):
        common = refs[:n_common]
        if has_vres:
            vres = refs[n_common:n_common + 6]
            outs = refs[n_common + 6:]
        else:
            vres = None
            outs = refs[n_common:]
        first = pl.program_id(0) == 0
        vals = _rwkv_prep_body(first, *common, vres)
        for o_ref, val in zip(outs, vals):
            o_ref[...] = val

    return kern


def _rwkv_prep(proj, mu_p, p, l, v_first, tb):
    t = proj.shape[0]
    has_vres = v_first is not None
    nj = RWKV_WIDTH // RW_CH
    pb = tb // SUBLANES

    def cur(width, col0):
        return pl.BlockSpec((tb, width), lambda i, j: (i, col0 // width))

    def cur_j(col0):
        return pl.BlockSpec((tb, RW_CH), lambda i, j: (i, col0 // RW_CH + j))

    def prev(width, col0):
        return pl.BlockSpec((SUBLANES, width), lambda i, j: (jnp.maximum(i * pb - 1, 0), col0 // width))

    def prev_j(col0):
        return pl.BlockSpec((SUBLANES, RW_CH), lambda i, j: (jnp.maximum(i * pb - 1, 0), col0 // RW_CH + j))

    def vec(width, col0):
        return pl.BlockSpec((1, width), lambda i, j: (0, col0 // width))

    def vec_j(col0=0):
        return pl.BlockSpec((1, RW_CH), lambda i, j: (0, col0 // RW_CH + j))

    def lora(rank):
        return pl.BlockSpec((rank, RW_CH), lambda i, j: (0, j))

    row = lambda a: a.reshape(1, -1)
    g2 = jnp.pad(p["rwkv_g2"][l], ((0, GLO_PAD - GATE_LORA), (0, 0))).astype(BF16)
    args = [proj, proj, proj, proj, proj, proj,
            proj, proj, proj, proj, proj, proj,
            mu_p, mu_p, mu_p, mu_p, mu_p, mu_p,
            p["rwkv_w2"][l].astype(BF16), p["rwkv_a2"][l].astype(BF16), g2,
            row(p["rwkv_w0"][l]), row(p["rwkv_a0"][l]), row(p["rwkv_k_k"][l]), row(p["rwkv_k_a"][l]),
            row(p["rwkv_r_k"][l])]
    in_specs = [cur_j(P_R), cur_j(P_K), cur_j(P_V), prev_j(P_R), prev_j(P_K), prev_j(P_V),
                cur(LANES, P_WLO), cur(LANES, P_ALO), cur(GLO_PAD, P_GLO),
                prev(LANES, P_WLO), prev(LANES, P_ALO), prev(GLO_PAD, P_GLO),
                vec_j(P_R), vec_j(P_K), vec_j(P_V), vec(LANES, P_WLO), vec(LANES, P_ALO), vec(GLO_PAD, P_GLO),
                lora(DECAY_LORA), lora(AAA_LORA), lora(GLO_PAD),
                vec_j(), vec_j(), vec_j(), vec_j(), vec_j()]
    if has_vres:
        v2 = jnp.pad(p["rwkv_v2"][l - 1], ((0, LANES - MV_LORA), (0, 0))).astype(BF16)
        args += [proj, proj, mu_p, v2, row(p["rwkv_v0"][l - 1]), v_first]
        in_specs += [cur(LANES, P_VLO), prev(LANES, P_VLO), vec(LANES, P_VLO), lora(LANES), vec_j(),
                     pl.BlockSpec((tb, RW_CH), lambda i, j: (i, j))]
    out_spec = pl.BlockSpec((tb, RW_CH), lambda i, j: (i, j))
    out_sds = jax.ShapeDtypeStruct((t, RWKV_WIDTH), F32)
    return pl.pallas_call(
        _make_rwkv_prep_kernel(has_vres),
        grid=(t // tb, nj),
        in_specs=in_specs,
        out_specs=[out_spec] * 8,
        out_shape=[out_sds] * 8,
        compiler_params=_cparams(("arbitrary", "arbitrary")),
        name="rwkv_prep",
    )(*args)


def _unit_lower_inverse(a_strict, c):
    row = lax.broadcasted_iota(jnp.int32, (c, c), 0)
    col = lax.broadcasted_iota(jnp.int32, (c, c), 1)
    eye = jnp.where(row == col, 1.0, 0.0)
    bd = lambda x: x.astype(BF16)
    base = SUBLANES
    same_base = row // base == col // base
    d1 = [bd(jnp.where(same_base, a, 0.0)) for a in a_strict]
    d2 = [bd(_dot(d, d)) for d in d1]
    inv = [eye + d.astype(F32) for d in d1]
    inv = [i + _dot(d, bd(i)) for i, d in zip(inv, d2)]
    d4 = [bd(_dot(d, d)) for d in d2]
    inv = [i + _dot(d, bd(i)) for i, d in zip(inv, d4)]
    blk = base
    while blk < c:
        band = (row // (2 * blk) == col // (2 * blk)) & (row // blk != col // blk)
        off = [bd(jnp.where(band, a, 0.0)) for a in a_strict]
        inv_b = [bd(i) for i in inv]
        tmp = [bd(_dot(o, i)) for o, i in zip(off, inv_b)]
        inv = [i + _dot(ib, t) for i, ib, t in zip(inv, inv_b, tmp)]
        blk *= 2
    return inv


def _rwkv_scan_kernel(r_ref, ld_ref, k_ref, v_ref, kn_ref, b_ref, g_ref, bonus_ref, lng_ref, lnb_ref,
                      y_ref, state_ref):
    c = RW_CHUNK
    n = RWKV_HEAD_DIM
    heads = RW_CH // n
    rows = ld_ref.shape[0]
    subs = rows // c

    @pl.when(pl.program_id(1) == 0)
    def _():
        state_ref[...] = jnp.zeros_like(state_ref)

    row = lax.broadcasted_iota(jnp.int32, (c, c), 0)
    col = lax.broadcasted_iota(jnp.int32, (c, c), 1)
    strict = row > col
    incl2 = (lax.broadcasted_iota(jnp.int32, (c, 2 * c), 0)
             >= lax.broadcasted_iota(jnp.int32, (c, 2 * c), 1) % c)
    brow = lax.broadcasted_iota(jnp.int32, (rows, rows), 0)
    bcol = lax.broadcasted_iota(jnp.int32, (rows, rows), 1)
    tri_incl = jnp.where((brow >= bcol) & (brow // c == bcol // c), 1.0, 0.0).astype(BF16)

    ld = ld_ref[...]
    ld_hi, ld_lo = _split_bf16(ld)
    cum = _dot(tri_incl, ld_hi) + _dot(tri_incl, ld_lo)
    g_inc = jnp.exp(cum)
    g_inv = jnp.exp(-cum)
    a_t = -kn_ref[...] * jnp.exp(cum - ld)
    b_t = b_ref[...] * g_inv
    k_t = k_ref[...] * g_inv
    r_t = r_ref[...] * g_inc
    v_all = v_ref[...]

    bd = lambda x: x.astype(BF16)
    idx = [(s, h) for s in range(subs) for h in range(heads)]
    rs = lambda s: slice(s * c, (s + 1) * c)
    ls = lambda h: slice(h * n, (h + 1) * n)
    v_h = [v_all[rs(s), ls(h)] for s, h in idx]
    a_h = [a_t[rs(s), ls(h)] for s, h in idx]
    r_h = [bd(r_t[rs(s), ls(h)]) for s, h in idx]
    ar = [bd(jnp.concatenate([a_h[i], r_t[rs(s), ls(h)]], axis=0)) for i, (s, h) in enumerate(idx)]
    bk = [bd(jnp.concatenate([b_t[rs(s), ls(h)], k_t[rs(s), ls(h)]], axis=0)) for s, h in idx]
    p1 = [_dot_nt(x, y) for x, y in zip(ar, bk)]
    a_ab = [jnp.where(strict, p[:c, :c], 0.0) for p in p1]
    a_ak = [bd(jnp.where(strict, p[:c, c:], 0.0)) for p in p1]
    a_r = [bd(jnp.where(incl2, p[c:, :], 0.0)) for p in p1]
    akv = [_dot(x, bd(y)) for x, y in zip(a_ak, v_h)]
    inv = _unit_lower_inverse(a_ab, c)
    sol = [_dot(bd(inv[i]), bd(jnp.concatenate([a_h[i], akv[i]], axis=1))) for i in range(len(idx))]

    state = [state_ref[h] for h in range(heads)]
    y_rows = []
    for s in range(subs):
        at = lambda lst, h: lst[s * heads + h]
        sb = [bd(x) for x in state]
        u = [_dot_nt(bd(at(sol, h)[:, :n]), sb[h]) + at(sol, h)[:, n:] for h in range(heads)]
        uv = [bd(jnp.concatenate([u[h], at(v_h, h)], axis=0)) for h in range(heads)]
        ys = [_dot_nt(at(r_h, h), sb[h]) + _dot(at(a_r, h), uv[h]) for h in range(heads)]
        g_last = g_inc[(s + 1) * c - 1:(s + 1) * c, :]
        state = [(state[h] + _dot_tn(uv[h], at(bk, h))) * g_last[:, ls(h)] for h in range(heads)]
        y_rows.append(jnp.concatenate(ys, axis=1))
    for h in range(heads):
        state_ref[h] = state[h]
    y = jnp.concatenate(y_rows, axis=0)

    ones_bd = _head_block_ones(2 * LANES)
    mean = _head_sum(y, ones_bd) * (1.0 / n)
    yc = y - mean
    var = _head_sum(yc * yc, ones_bd) * (1.0 / n)
    yn = yc * lax.rsqrt(var + GN_EPS) * lng_ref[...] + lnb_ref[...]
    y_ref[...] = ((yn + bonus_ref[...]) * g_ref[...]).astype(y_ref.dtype)


def _rwkv_scan(prep, ln_g, ln_b):
    t = prep[0].shape[0]
    c = RW_CHUNK * RW_SUB
    blk = pl.BlockSpec((c, RW_CH), lambda j, i: (i, j))
    vec = pl.BlockSpec((1, RW_CH), lambda j, i: (0, j))
    return pl.pallas_call(
        _rwkv_scan_kernel,
        grid=(RWKV_WIDTH // RW_CH, t // c),
        in_specs=[blk] * 8 + [vec, vec],
        out_specs=blk,
        out_shape=jax.ShapeDtypeStruct((t, RWKV_WIDTH), BF16),
        scratch_shapes=[pltpu.VMEM((RW_CH // RWKV_HEAD_DIM, RWKV_HEAD_DIM, RWKV_HEAD_DIM), F32)],
        compiler_params=_cparams(("arbitrary", "arbitrary")),
        name="rwkv_scan",
    )(*prep, ln_g.reshape(1, -1), ln_b.reshape(1, -1))


def _gelu_tanh(x):
    return 0.5 * x * (1.0 + jnp.tanh(np.sqrt(2.0 / np.pi).astype(np.float32) * (x + 0.044715 * (x * x * x))))


def _gmlp_kernel(u_ref, v_ref, lng_ref, lnb_ref, ws_ref, bs_ref, o_ref, *, chunks):
    ch = GMLP_CHUNK
    gd = GMLP_GROUP_DIM
    u = _gelu_tanh(u_ref[...])
    v = _gelu_tanh(v_ref[...])
    mu = jnp.mean(v, axis=-1, keepdims=True)
    vc = v - mu
    var = jnp.mean(vc * vc, axis=-1, keepdims=True)
    vn = (vc * lax.rsqrt(var + LN_EPS) * lng_ref[...] + lnb_ref[...]).astype(BF16)
    row = lax.broadcasted_iota(jnp.int32, (ch, ch), 0)
    col = lax.broadcasted_iota(jnp.int32, (ch, ch), 1)
    causal = row >= col
    bs = bs_ref[...]
    for g in range(GMLP_GROUPS):
        w = jnp.where(causal, ws_ref[g], 0.0).astype(BF16)
        bias = bs[:, g:g + 1]
        for n in range(chunks):
            f = _dot(w, vn[n * ch:(n + 1) * ch, g * gd:(g + 1) * gd]) + bias
            o_ref[n * ch:(n + 1) * ch, g * gd:(g + 1) * gd] = (
                u[n * ch:(n + 1) * ch, g * gd:(g + 1) * gd] * f).astype(o_ref.dtype)


def _gmlp(proj, ln_g, ln_b, w_s, b_s, tb=256):
    t = proj.shape[0]
    bs_t = jnp.pad(b_s.T, ((0, 0), (0, LANES - GMLP_GROUPS)))
    return pl.pallas_call(
        functools.partial(_gmlp_kernel, chunks=tb // GMLP_CHUNK),
        grid=(t // tb,),
        in_specs=[pl.BlockSpec((tb, GMLP_WIDTH), lambda i: (i, P_GU // GMLP_WIDTH)),
                  pl.BlockSpec((tb, GMLP_WIDTH), lambda i: (i, P_GV // GMLP_WIDTH)),
                  pl.BlockSpec((1, GMLP_WIDTH), lambda i: (0, 0)),
                  pl.BlockSpec((1, GMLP_WIDTH), lambda i: (0, 0)),
                  pl.BlockSpec((GMLP_GROUPS, GMLP_CHUNK, GMLP_CHUNK), lambda i: (0, 0, 0)),
                  pl.BlockSpec((GMLP_CHUNK, LANES), lambda i: (0, 0))],
        out_specs=pl.BlockSpec((tb, GMLP_WIDTH), lambda i: (i, 0)),
        out_shape=jax.ShapeDtypeStruct((t, GMLP_WIDTH), BF16),
        compiler_params=_cparams(("arbitrary",)),
        name="gmlp",
    )(proj, proj, ln_g.reshape(1, -1), ln_b.reshape(1, -1), w_s, bs_t)


def _rope_lanes(t, cc, s1, s2):
    return t * cc + pltpu.roll(t, LANES - MLA_ROPE_DIM // 2, 1) * s1 + pltpu.roll(t, MLA_ROPE_DIM // 2, 1) * s2


def _mla_proj_kernel(cq_ref, ckv_ref, kr_ref, qn_ref, kvn_ref, wq_ref, wk_ref, wvt_ref, cc_ref, s1_ref, s2_ref,
                     q_o, kn_o, vt_o, kr_o):
    cq = cq_ref[...]
    qn = (cq * lax.rsqrt(jnp.mean(cq * cq, axis=-1, keepdims=True) + RMS_EPS) * qn_ref[...]).astype(BF16)
    ckv = ckv_ref[...]
    kvn = (ckv * lax.rsqrt(jnp.mean(ckv * ckv, axis=-1, keepdims=True) + RMS_EPS) * kvn_ref[...]).astype(BF16)
    cc, s1, s2 = cc_ref[...], s1_ref[...], s2_ref[...]
    scale = MLA_QK_DIM ** -0.5 * np.log2(np.e)
    for h in range(MLA_HEADS):
        q = _dot(qn, wq_ref[:, 2 * LANES * h:2 * LANES * (h + 1)]) * scale
        q_o[:, 2 * LANES * h:2 * LANES * h + LANES] = q[:, :LANES].astype(BF16)
        q_o[:, 2 * LANES * h + LANES:2 * LANES * (h + 1)] = _rope_lanes(q[:, LANES:], cc, s1, s2).astype(BF16)
    kn_o[...] = _dot(kvn, wk_ref[...]).astype(BF16)
    vt_o[...] = _dot_nt(wvt_ref[...], kvn).astype(BF16)
    kr_o[...] = _rope_lanes(kr_ref[...], cc, s1, s2).astype(BF16)


def _mla_proj(proj, positions, q_norm, kv_norm, w_uq, w_ukv, tm=512):
    t = proj.shape[0]
    h = MLA_HEADS
    half = MLA_ROPE_DIM // 2
    inv_freq = jnp.power(ROPE_THETA, -jnp.arange(0, MLA_ROPE_DIM, 2, dtype=F32) / MLA_ROPE_DIM)
    ang = positions.reshape(t).astype(F32)[:, None] * inv_freq
    cos, sin = jnp.cos(ang), jnp.sin(ang)
    z = jnp.zeros((t, half), F32)
    cc = jnp.concatenate([cos, cos, z, z], axis=1)
    s1 = jnp.concatenate([-sin, z, z, z], axis=1)
    s2 = jnp.concatenate([z, sin, z, z], axis=1)
    wq = w_uq.reshape(MLA_Q_RANK, h, MLA_QK_DIM)
    wq = jnp.pad(wq, ((0, 0), (0, 0), (0, 2 * LANES - MLA_QK_DIM))).reshape(MLA_Q_RANK, h * 2 * LANES).astype(BF16)
    wkv = w_ukv.reshape(MLA_KV_RANK, h, MLA_NOPE_DIM + MLA_V_DIM)
    wk = wkv[:, :, :MLA_NOPE_DIM].reshape(MLA_KV_RANK, h * MLA_NOPE_DIM).astype(BF16)
    wvt = wkv[:, :, MLA_NOPE_DIM:].reshape(MLA_KV_RANK, h * MLA_V_DIM).T.astype(BF16)
    full = lambda a: pl.BlockSpec(a.shape, lambda i: (0,) * a.ndim)
    tab = pl.BlockSpec((tm, LANES), lambda i: (i, 0))
    qn2, kvn2 = q_norm.reshape(1, -1), kv_norm.reshape(1, -1)
    return pl.pallas_call(
        _mla_proj_kernel,
        grid=(t // tm,),
        in_specs=[pl.BlockSpec((tm, MLA_Q_RANK), lambda i: (i, P_CQ // MLA_Q_RANK)),
                  pl.BlockSpec((tm, MLA_KV_RANK), lambda i: (i, P_CKV // MLA_KV_RANK)),
                  pl.BlockSpec((tm, LANES), lambda i: (i, P_KROPE // LANES)),
                  full(qn2), full(kvn2), full(wq), full(wk), full(wvt), tab, tab, tab],
        out_specs=[pl.BlockSpec((tm, h * 2 * LANES), lambda i: (i, 0)),
                   pl.BlockSpec((tm, h * MLA_NOPE_DIM), lambda i: (i, 0)),
                   pl.BlockSpec((h * MLA_V_DIM, tm), lambda i: (0, i)),
                   pl.BlockSpec((tm, LANES), lambda i: (i, 0))],
        out_shape=[jax.ShapeDtypeStruct((t, h * 2 * LANES), BF16),
                   jax.ShapeDtypeStruct((t, h * MLA_NOPE_DIM), BF16),
                   jax.ShapeDtypeStruct((h * MLA_V_DIM, t), BF16),
                   jax.ShapeDtypeStruct((t, LANES), BF16)],
        compiler_params=_cparams(("arbitrary",)),
        name="mla_proj",
    )(proj, proj, proj, qn2, kvn2, wq, wk, wvt, cc, s1, s2)


def _flash_kernel(qi_ref, kj_ref, q_ref, kn_ref, kr_ref, vt_ref, o_ref, m_sc, l_sc, acc_sc, st_sc, *, tq, tk):
    s = pl.program_id(1)
    qi, kj = qi_ref[s], kj_ref[s]

    @pl.when(kj == 0)
    def _():
        m_sc[...] = jnp.full_like(m_sc, -jnp.inf)
        l_sc[...] = jnp.zeros_like(l_sc)
        acc_sc[...] = jnp.zeros_like(acc_sc)

    qt = 2 * LANES
    kb = 2 * LANES

    def step(masked):
        k = jnp.concatenate([kn_ref[...], kr_ref[...]], axis=1)
        m_all = m_sc[...]
        m_news, alphas = [], []
        for c0 in range(0, tq, qt):
            st = _dot_nt(k, q_ref[c0:c0 + qt, :])
            if masked:
                key = kj * tk + lax.broadcasted_iota(jnp.int32, (tk, qt), 0)
                qry = qi * tq + c0 + lax.broadcasted_iota(jnp.int32, (tk, qt), 1)
                st = jnp.where(key <= qry, st, -jnp.inf)
            st_sc[:, c0:c0 + qt] = st
            m_prev = m_all[:, c0:c0 + qt]
            m_new = jnp.maximum(m_prev, jnp.max(st, axis=0, keepdims=True))
            m_news.append(m_new)
            alphas.append(jnp.exp2(m_prev - m_new))
        pvs, sums = [], []
        for t_i, c0 in enumerate(range(0, tq, qt)):
            pv, ps = None, None
            for r0 in range(0, tk, kb):
                p = jnp.exp2(st_sc[r0:r0 + kb, c0:c0 + qt] - m_news[t_i])
                part = jnp.sum(p, axis=0, keepdims=True)
                prod = _dot(vt_ref[:, r0:r0 + kb], p.astype(BF16))
                ps = part if ps is None else ps + part
                pv = prod if pv is None else pv + prod
            pvs.append(pv)
            sums.append(ps)
        alpha = jnp.concatenate(alphas, axis=1)
        m_sc[...] = jnp.concatenate(m_news, axis=1)
        l_sc[...] = alpha * l_sc[...] + jnp.concatenate(sums, axis=1)
        acc_sc[...] = alpha * acc_sc[...] + jnp.concatenate(pvs, axis=1)

    last_key_of_block = kj * tk + tk - 1
    on_diag = last_key_of_block > qi * tq

    @pl.when(jnp.logical_not(on_diag))
    def _():
        step(False)

    @pl.when(on_diag)
    def _():
        step(True)

    @pl.when(last_key_of_block >= qi * tq + tq - 1)
    def _():
        o_ref[...] = (acc_sc[...] / l_sc[...]).T.astype(o_ref.dtype)


def _flash(q, kn, vt, kr, tq=512, tk=512):
    t = q.shape[0]
    assert tq % tk == 0
    pairs = [(i, j) for i in range(t // tq) for j in range((i + 1) * tq // tk)]
    qi = jnp.asarray([pr[0] for pr in pairs], jnp.int32)
    kj = jnp.asarray([pr[1] for pr in pairs], jnp.int32)
    grid_spec = pltpu.PrefetchScalarGridSpec(
        num_scalar_prefetch=2,
        grid=(MLA_HEADS, len(pairs)),
        in_specs=[pl.BlockSpec((tq, 2 * LANES), lambda h, s, qi, kj: (qi[s], h)),
                  pl.BlockSpec((tk, MLA_NOPE_DIM), lambda h, s, qi, kj: (kj[s], h)),
                  pl.BlockSpec((tk, LANES), lambda h, s, qi, kj: (kj[s], 0)),
                  pl.BlockSpec((MLA_V_DIM, tk), lambda h, s, qi, kj: (h, kj[s]))],
        out_specs=pl.BlockSpec((tq, MLA_V_DIM), lambda h, s, qi, kj: (qi[s], h)),
        scratch_shapes=[pltpu.VMEM((1, tq), F32), pltpu.VMEM((1, tq), F32), pltpu.VMEM((MLA_V_DIM, tq), F32),
                        pltpu.VMEM((tk, tq), F32)],
    )
    return pl.pallas_call(
        functools.partial(_flash_kernel, tq=tq, tk=tk),
        grid_spec=grid_spec,
        out_shape=jax.ShapeDtypeStruct((t, MLA_HEADS * MLA_V_DIM), BF16),
        compiler_params=_cparams(("arbitrary", "arbitrary")),
        name="flash",
    )(qi, kj, q, kn, kr, vt)


def _router_kernel(h_ref, rwt_ref, bias_ref, exp_o, pos_o, gate_o, cnt_o, carry_sc):
    e = N_EXPERTS
    per = e // N_EXPERT_GROUPS
    h_hi, h_lo = _split_bf16(h_ref[...])
    w_hi, w_lo = _split_bf16(rwt_ref[...])
    logits = _dot_nt(w_hi, h_hi) + _dot_nt(w_hi, h_lo) + _dot_nt(w_lo, h_hi)
    scores = _sigmoid(logits)
    biased = scores + bias_ref[...][:, 0:1]
    tb = biased.shape[1]
    neg = -jnp.inf
    sub = lax.broadcasted_iota(jnp.int32, (per, tb), 0)
    grp_rows = []
    for g in range(N_EXPERT_GROUPS):
        blk = biased[g * per:(g + 1) * per, :]
        m1 = jnp.max(blk, axis=0, keepdims=True)
        first = jnp.min(jnp.where(blk == m1, sub, per), axis=0, keepdims=True)
        m2 = jnp.max(jnp.where(sub == first, neg, blk), axis=0, keepdims=True)
        grp_rows.append(m1 + m2)
    grp = jnp.concatenate(grp_rows, axis=0)
    gidx = lax.broadcasted_iota(jnp.int32, grp.shape, 0)
    grank = jnp.zeros(grp.shape, jnp.int32)
    for g in range(N_EXPERT_GROUPS):
        other = grp[g:g + 1, :]
        ahead = (other > grp) | ((other == grp) & (g < gidx))
        grank = grank + jnp.where(ahead, 1, 0)
    gsel = grank < TOPK_GROUPS
    masked = jnp.concatenate(
        [jnp.where(gsel[g:g + 1, :], biased[g * per:(g + 1) * per, :], neg) for g in range(N_EXPERT_GROUPS)], axis=0)
    eidx = lax.broadcasted_iota(jnp.int32, masked.shape, 0)
    rank = jnp.zeros(masked.shape, jnp.int32)
    for j in range(e):
        other = masked[j:j + 1, :]
        ahead = (other > masked) | ((other == masked) & (j < eidx))
        rank = rank + jnp.where(ahead, 1, 0)
    chosen = rank < TOP_K
    sel = jnp.where(chosen, scores, 0.0)
    gate = sel / jnp.sum(sel, axis=0, keepdims=True) * ROUTED_SCALE

    @pl.when(pl.program_id(0) == 0)
    def _():
        carry_sc[...] = jnp.zeros_like(carry_sc)

    chosen_f = jnp.where(chosen, 1.0, 0.0)
    earlier = (lax.broadcasted_iota(jnp.int32, (tb, tb), 0) < lax.broadcasted_iota(jnp.int32, (tb, tb), 1))
    carry = carry_sc[...]
    pos = _dot(chosen_f.astype(BF16), jnp.where(earlier, 1.0, 0.0).astype(BF16)) + carry[:, 0:1]
    carry_sc[...] = carry + jnp.sum(chosen_f, axis=1, keepdims=True)
    cnt_o[...] = carry_sc[...].astype(jnp.int32)

    eidx_f = eidx.astype(F32)
    pick = lambda hit, val: jnp.sum(jnp.where(hit, val, 0.0), axis=0, keepdims=True)
    hits = [rank == k for k in range(TOP_K)]
    exp_o[...] = jnp.concatenate([pick(hit, eidx_f) for hit in hits], axis=0).astype(jnp.int32)
    pos_o[...] = jnp.concatenate([pick(hit, pos) for hit in hits], axis=0).astype(jnp.int32)
    gate_o[...] = jnp.concatenate([pick(hit, gate) for hit in hits], axis=0)


def _router(h, router_w, router_bias, tb=512):
    t, d = h.shape
    bias = jnp.broadcast_to(router_bias.astype(F32)[:, None], (N_EXPERTS, LANES))
    per_tok = pl.BlockSpec((TOP_K, tb), lambda i: (0, i))
    return pl.pallas_call(
        _router_kernel,
        grid=(t // tb,),
        in_specs=[pl.BlockSpec((tb, d), lambda i: (i, 0)),
                  pl.BlockSpec((N_EXPERTS, d), lambda i: (0, 0)),
                  pl.BlockSpec((N_EXPERTS, LANES), lambda i: (0, 0))],
        out_specs=[per_tok, per_tok, per_tok, pl.BlockSpec((N_EXPERTS, LANES), lambda i: (0, 0))],
        out_shape=[jax.ShapeDtypeStruct((TOP_K, t), jnp.int32), jax.ShapeDtypeStruct((TOP_K, t), jnp.int32),
                   jax.ShapeDtypeStruct((TOP_K, t), F32), jax.ShapeDtypeStruct((N_EXPERTS, LANES), jnp.int32)],
        scratch_shapes=[pltpu.VMEM((N_EXPERTS, LANES), F32)],
        compiler_params=_cparams(("arbitrary",)),
        name="router",
    )(h, router_w.T, bias)


HALF_MASK = 0xFFFF0000


def _pack_bf16_pairs(x):
    n = x.shape[1] // 2
    lo = pltpu.bitcast(x[:, :n].astype(BF16).astype(F32), jnp.uint32) >> 16
    hi = pltpu.bitcast(x[:, n:].astype(BF16).astype(F32), jnp.uint32) & jnp.uint32(HALF_MASK)
    return lo | hi


def _unpack_bf16_pairs(w):
    return pltpu.bitcast(w << 16, F32), pltpu.bitcast(w & jnp.uint32(HALF_MASK), F32)


def _dispatch_plan(exp_r, pos_r, cnt, bm, nb_max):
    ids = jnp.arange(N_EXPERTS, dtype=jnp.int32)
    nb = (cnt + bm - 1) // bm
    bend = jnp.sum(jnp.where(ids[None, :] <= ids[:, None], nb[None, :], 0), axis=1)
    total = bend[-1]
    slot_start = (bend - nb) * bm
    start_of = jnp.sum(jnp.where(exp_r[:, :, None] == ids, slot_start, 0), axis=-1)
    slot = (start_of + pos_r).T.reshape(-1)
    bidx = jnp.arange(nb_max, dtype=jnp.int32)
    blocks = jnp.minimum(bidx, total - 1)
    blk_exp = jnp.sum(jnp.where(bend[None, :] <= blocks[:, None], 1, 0), axis=1)
    first = jnp.where((bidx == 0) | (blk_exp != jnp.roll(blk_exp, 1)), 1, 0)
    turn = jnp.sum(jnp.where(bidx[None, :] <= bidx[:, None], first[None, :], 0), axis=1) - 1
    later_first = jnp.where((bidx[None, :] > bidx[:, None]) & (first[None, :] == 1), bidx[None, :], nb_max)
    nxt_blk = jnp.min(later_first, axis=1)
    nxt_exp = jnp.sum(jnp.where(bidx[None, :] == nxt_blk[:, None], blk_exp[None, :] + 1, 0), axis=1) - 1
    meta = jnp.concatenate([blk_exp, first, turn % 2, nxt_exp]).astype(jnp.int32)
    return slot.astype(jnp.int32), meta, total.reshape(1).astype(jnp.int32)


def _dispatch_kernel(slot_ref, h_ref, xs_init_hbm, xs_hbm, sem, *, tb):
    del xs_init_hbm

    def body(t, carry):
        for k in range(TOP_K):
            s = slot_ref[t * TOP_K + k]
            pltpu.make_async_copy(h_ref.at[pl.ds(t, 1), :], xs_hbm.at[pl.ds(s, 1), :], sem).start()
        return carry

    lax.fori_loop(0, tb, body, 0)
    rows = tb * TOP_K
    pltpu.make_async_copy(xs_hbm.at[pl.ds(0, rows), :], xs_hbm.at[pl.ds(0, rows), :], sem).wait()


def _dispatch(h_pk, slot, n_slots, tb):
    t, w = h_pk.shape
    xs0 = jnp.zeros((n_slots, w), jnp.uint32)
    return pl.pallas_call(
        functools.partial(_dispatch_kernel, tb=tb),
        grid=(t // tb,),
        in_specs=[pl.BlockSpec((tb * TOP_K,), lambda i: (i,), memory_space=pltpu.SMEM),
                  pl.BlockSpec((tb, w), lambda i: (i, 0)),
                  pl.BlockSpec(memory_space=pl.ANY)],
        out_specs=pl.BlockSpec(memory_space=pl.ANY),
        out_shape=jax.ShapeDtypeStruct((n_slots, w), jnp.uint32),
        scratch_shapes=[pltpu.SemaphoreType.DMA(())],
        input_output_aliases={2: 0},
        compiler_params=pltpu.CompilerParams(dimension_semantics=("arbitrary",), vmem_limit_bytes=VMEM_LIMIT,
                                             has_side_effects=True, disable_bounds_checks=True),
        name="moe_dispatch",
    )(slot, h_pk, xs0)


def _expert_meta(meta_ref, nb):
    b = pl.program_id(0)
    return meta_ref[b], meta_ref[nb + b] == 1, meta_ref[2 * nb + b], meta_ref[3 * nb + b]


def _expert_weight_turn(meta_ref, nb, layer, hbm_refs, stage_ref, sem, cast_to):
    e, first, par, nxt = _expert_meta(meta_ref, nb)

    def copies(expert, half):
        return [pltpu.make_async_copy(w.at[layer, expert], stage_ref.at[half, n], sem.at[half, n])
                for n, w in enumerate(hbm_refs)]

    @pl.when(pl.program_id(0) == 0)
    def _():
        for c in copies(e, par):
            c.start()

    @pl.when(first)
    def _():
        for c in copies(e, par):
            c.wait()
        cast_to(stage_ref.at[par])

        @pl.when(nxt >= 0)
        def _():
            for c in copies(nxt, 1 - par):
                c.start()


def _expert_up_kernel(meta_ref, tot_ref, xs_ref, w1_hbm, w3_hbm, act_ref, stage_sc, w13_sc, sem, *, nb, layer):
    def cast_to(staged):
        w13_sc[:, :D_EXPERT] = staged[0].astype(BF16)
        w13_sc[:, D_EXPERT:] = staged[1].astype(BF16)

    _expert_weight_turn(meta_ref, nb, layer, [w1_hbm, w3_hbm], stage_sc, sem, cast_to)

    @pl.when(pl.program_id(0) < tot_ref[0])
    def _():
        lo, hi = _unpack_bf16_pairs(xs_ref[...])
        x = jnp.concatenate([lo.astype(BF16), hi.astype(BF16)], axis=1)
        hgu = _dot(x, w13_sc[...])
        hg, hu = hgu[:, :D_EXPERT], hgu[:, D_EXPERT:]
        act_ref[...] = (hg * _sigmoid(hg) * hu).astype(BF16)

    @pl.when(pl.program_id(0) >= tot_ref[0])
    def _():
        act_ref[...] = jnp.zeros_like(act_ref)


def _expert_down_kernel(meta_ref, tot_ref, act_ref, w2_hbm, ys_ref, stage_sc, w2_sc, sem, *, nb, layer):
    def cast_to(staged):
        w2_sc[...] = staged[0].astype(BF16)

    _expert_weight_turn(meta_ref, nb, layer, [w2_hbm], stage_sc, sem, cast_to)

    @pl.when(pl.program_id(0) < tot_ref[0])
    def _():
        ys_ref[...] = _pack_bf16_pairs(_dot(act_ref[...], w2_sc[...]))

    @pl.when(pl.program_id(0) >= tot_ref[0])
    def _():
        ys_ref[...] = jnp.zeros_like(ys_ref)


def _experts(xs, meta, total, w1, w3, w2, layer, bm):
    n_slots, w = xs.shape
    d = 2 * w
    nb = n_slots // bm
    used = lambda b, meta, tot: (jnp.minimum(b, tot[0] - 1), 0)
    every = lambda b, meta, tot: (b, 0)
    hbm = pl.BlockSpec(memory_space=pl.ANY)
    act = pl.pallas_call(
        functools.partial(_expert_up_kernel, nb=nb, layer=layer),
        grid_spec=pltpu.PrefetchScalarGridSpec(
            num_scalar_prefetch=2,
            grid=(nb,),
            in_specs=[pl.BlockSpec((bm, w), used), hbm, hbm],
            out_specs=pl.BlockSpec((bm, D_EXPERT), every),
            scratch_shapes=[pltpu.VMEM((2, 2, d, D_EXPERT), F32), pltpu.VMEM((d, 2 * D_EXPERT), BF16),
                            pltpu.SemaphoreType.DMA((2, 2))]),
        out_shape=jax.ShapeDtypeStruct((n_slots, D_EXPERT), BF16),
        compiler_params=_cparams(("arbitrary",)),
        name="moe_up",
    )(meta, total, xs, w1, w3)
    return pl.pallas_call(
        functools.partial(_expert_down_kernel, nb=nb, layer=layer),
        grid_spec=pltpu.PrefetchScalarGridSpec(
            num_scalar_prefetch=2,
            grid=(nb,),
            in_specs=[pl.BlockSpec((bm, D_EXPERT), used), hbm],
            out_specs=pl.BlockSpec((bm, w), every),
            scratch_shapes=[pltpu.VMEM((2, 1, D_EXPERT, d), F32), pltpu.VMEM((D_EXPERT, d), BF16),
                            pltpu.SemaphoreType.DMA((2, 1))]),
        out_shape=jax.ShapeDtypeStruct((n_slots, w), jnp.uint32),
        compiler_params=_cparams(("arbitrary",)),
        name="moe_down",
    )(meta, total, act, w2)


def _combine_kernel(slot_ref, slot_next_ref, gate_ref, h_ref, hb_ref, sw13_ref, sw2_ref, g_ref, b_ref, ys_hbm,
                    of_ref, ob_ref, rows_a, rows_b, shared_sc, sem, *, tb, steps):
    i = pl.program_id(0)
    grp = 2 * SUBLANES

    def issue(table_ref, buf, buf_sem, t0, first=0, count=grp):
        for j in range(first, first + count):
            for k in range(TOP_K):
                s = table_ref[(t0 + j) * TOP_K + k]
                pltpu.make_async_copy(ys_hbm.at[pl.ds(s, 1), :], buf.at[k, pl.ds(t0 + j, 1), :], buf_sem).start()

    def wait_block(buf, buf_sem):
        pltpu.make_async_copy(buf, buf, buf_sem).wait()

    @pl.when(i == 0)
    def _():
        def first(g, carry):
            issue(slot_ref, rows_a, sem.at[0], g * grp)
            return carry

        lax.fori_loop(0, tb // grp, first, 0)

    hgu = _dot(hb_ref[...], sw13_ref[...])
    hg, hu = hgu[:, :D_EXPERT], hgu[:, D_EXPERT:]
    shared_sc[...] = _dot((hg * _sigmoid(hg) * hu).astype(BF16), sw2_ref[...])

    def run(cur, cur_sem, nxt, nxt_sem):
        wait_block(cur, cur_sem)

        def group(g, carry):
            r0 = pl.multiple_of(g * grp, grp)
            rows = pl.ds(r0, grp)
            gate = gate_ref[rows, :]
            acc_lo = jnp.zeros((grp, cur.shape[-1]), F32)
            acc_hi = jnp.zeros((grp, cur.shape[-1]), F32)
            per_k = grp // TOP_K
            for k in range(TOP_K):
                issue(slot_next_ref, nxt, nxt_sem, r0, k * per_k, per_k)
                lo, hi = _unpack_bf16_pairs(cur[k, rows, :])
                gk = gate[:, k:k + 1]
                acc_lo = acc_lo + gk * lo
                acc_hi = acc_hi + gk * hi
            t = ALPHA * h_ref[rows, :] + shared_sc[rows, :] + jnp.concatenate([acc_lo, acc_hi], axis=1)
            mu = jnp.mean(t, axis=-1, keepdims=True)
            c = t - mu
            var = jnp.mean(c * c, axis=-1, keepdims=True)
            out = c * lax.rsqrt(var + LN_EPS) * g_ref[...] + b_ref[...]
            of_ref[rows, :] = out
            ob_ref[rows, :] = out.astype(BF16)
            return carry

        lax.fori_loop(0, tb // grp, group, 0)

        @pl.when(i == steps - 1)
        def _():
            wait_block(nxt, nxt_sem)

    even = lax.rem(i, 2) == 0

    @pl.when(even)
    def _():
        run(rows_a, sem.at[0], rows_b, sem.at[1])

    @pl.when(jnp.logical_not(even))
    def _():
        run(rows_b, sem.at[1], rows_a, sem.at[0])


def _combine(ys, slot, gate_tk, hf, hb, sw1, sw3, sw2, g, b, tb):
    t, d = hf.shape
    steps = t // tb
    sw13 = jnp.concatenate([sw1, sw3], axis=1).astype(BF16)
    row = pl.BlockSpec((tb, d), lambda i: (i, 0))
    full = lambda a: pl.BlockSpec(a.shape, lambda i: (0,) * a.ndim)
    g2, b2, sw2b = g.reshape(1, d), b.reshape(1, d), sw2.astype(BF16)
    return pl.pallas_call(
        functools.partial(_combine_kernel, tb=tb, steps=steps),
        grid=(steps,),
        in_specs=[pl.BlockSpec((tb * TOP_K,), lambda i: (i,), memory_space=pltpu.SMEM),
                  pl.BlockSpec((tb * TOP_K,), lambda i: (jnp.minimum(i + 1, steps - 1),), memory_space=pltpu.SMEM),
                  pl.BlockSpec((tb, TOP_K), lambda i: (i, 0)),
                  row, row, full(sw13), full(sw2b), full(g2), full(b2),
                  pl.BlockSpec(memory_space=pl.ANY)],
        out_specs=[row, row],
        out_shape=[jax.ShapeDtypeStruct((t, d), F32), jax.ShapeDtypeStruct((t, d), BF16)],
        scratch_shapes=[pltpu.VMEM((TOP_K, tb, ys.shape[1]), jnp.uint32),
                        pltpu.VMEM((TOP_K, tb, ys.shape[1]), jnp.uint32),
                        pltpu.VMEM((tb, d), F32), pltpu.SemaphoreType.DMA((2,))],
        compiler_params=pltpu.CompilerParams(dimension_semantics=("arbitrary",), vmem_limit_bytes=VMEM_LIMIT,
                                             disable_bounds_checks=True),
        name="moe_combine",
    )(slot, slot, gate_tk, hf, hb, sw13, sw2b, g2, b2, ys)


def _moe_ffn(hf, hb, h_pk, router_w, router_bias, w1, w3, w2, layer, sw1, sw3, sw2, ln_g, ln_b, bm):
    t = hf.shape[0]
    exp_r, pos_r, gate_r, cnt = _router(hf, router_w, router_bias, tb=min(512, t))
    nb_max = t * TOP_K // bm + N_EXPERTS
    slot, meta, total = _dispatch_plan(exp_r, pos_r, cnt[:, 0], bm, nb_max)
    xs = _dispatch(h_pk, slot, nb_max * bm, tb=min(256, t // TOP_K))
    ys = _experts(xs, meta, total, w1, w3, w2, layer, bm)
    return _combine(ys, slot, gate_r.T, hf, hb, sw1, sw3, sw2, ln_g, ln_b, tb=min(128, t))


def _prep_w_in(w_in, has_vres):
    d = w_in.shape[0]
    sizes = [RWKV_WIDTH, RWKV_WIDTH, RWKV_WIDTH, DECAY_LORA, AAA_LORA, GATE_LORA,
             GMLP_WIDTH, GMLP_WIDTH, MLA_Q_RANK, MLA_KV_RANK, MLA_ROPE_DIM]
    if has_vres:
        sizes.append(MV_LORA)
    offs = np.concatenate([[0], np.cumsum(sizes)])
    seg = [w_in[:, offs[i]:offs[i + 1]] for i in range(len(sizes))]
    r, k, v, w_lo, a_lo, g_lo, gu, gv, cq, ckv, kr = seg[:11]
    z = lambda n: jnp.zeros((d, n), w_in.dtype)
    v_lo = jnp.concatenate([seg[11], z(LANES - MV_LORA)], axis=1) if has_vres else z(LANES)
    cols = [gu, gv, r, k, v, w_lo, a_lo, cq, ckv, g_lo, z(GLO_PAD - GATE_LORA), kr, z(LANES - MLA_ROPE_DIM), v_lo]
    out = jnp.concatenate(cols, axis=1).astype(BF16)
    assert out.shape[1] == P_COLS
    return out


def _prep_mu(mu, mu_vres):
    out = jnp.zeros((1, P_COLS), F32)
    offs = np.concatenate([[0], np.cumsum([RWKV_WIDTH] * 3 + [DECAY_LORA, AAA_LORA, GATE_LORA])])
    for dst, i in zip([P_R, P_K, P_V, P_WLO, P_ALO, P_GLO], range(6)):
        out = lax.dynamic_update_slice(out, mu[offs[i]:offs[i + 1]].reshape(1, -1), (0, dst))
    if mu_vres is not None:
        out = lax.dynamic_update_slice(out, mu_vres.reshape(1, -1), (0, P_VLO))
    return out


def kernel(x, positions, w_in_first, w_in_rest, rwkv_mu, rwkv_mu_vres, rwkv_w0, rwkv_w2, rwkv_a0, rwkv_a2, rwkv_v0, rwkv_v2, rwkv_g2, rwkv_k_k, rwkv_k_a, rwkv_r_k, rwkv_ln_g, rwkv_ln_b, gmlp_ln_g, gmlp_ln_b, gmlp_w_s, gmlp_b_s, mla_q_norm, mla_kv_norm, mla_w_uq, mla_w_ukv, w_out, ln1_g, ln1_b, router_w, router_bias, exp_w1, exp_w3, exp_w2, shared_w1, shared_w3, shared_w2, ln2_g, ln2_b):
    b, s, d = x.shape
    t = b * s
    p = dict(rwkv_w0=rwkv_w0, rwkv_w2=rwkv_w2, rwkv_a0=rwkv_a0, rwkv_a2=rwkv_a2, rwkv_v0=rwkv_v0, rwkv_v2=rwkv_v2,
             rwkv_g2=rwkv_g2, rwkv_k_k=rwkv_k_k, rwkv_k_a=rwkv_k_a, rwkv_r_k=rwkv_r_k)
    xf = x.reshape(t, d)
    xb = xf.astype(BF16)
    v_first = None
    for l in range(DEPTH):
        has_vres = l > 0
        w_in = _prep_w_in(w_in_first if l == 0 else w_in_rest[l - 1], has_vres)
        mu_p = _prep_mu(rwkv_mu[l], rwkv_mu_vres[l - 1] if has_vres else None)
        proj = _matmul(xb, w_in, F32, tm=min(512, t), tn=1280)
        prep = _rwkv_prep(proj, mu_p, p, l, v_first, tb=min(256, t))
        if l == 0:
            v_first = prep[3]
        y_a = _rwkv_scan(prep, rwkv_ln_g[l], rwkv_ln_b[l])
        y_b = _gmlp(proj, gmlp_ln_g[l], gmlp_ln_b[l], gmlp_w_s[l], gmlp_b_s[l], tb=min(256, t))
        q, kn, vt, kr = _mla_proj(proj, positions, mla_q_norm[l], mla_kv_norm[l], mla_w_uq[l], mla_w_ukv[l],
                                  tm=min(512, t))
        y_c = _flash(q, kn, vt, kr, tq=min(1024, t), tk=min(1024, t))
        mix = _out_proj(y_a, y_b, y_c, w_out[l], tm=min(512, t), tn=1024)
        hf, hb, h_pk = _res_ln(xf, mix, ln1_g[l], ln1_b[l], tm=min(256, t))
        xf, xb = _moe_ffn(hf, hb, h_pk, router_w[l], router_bias[l],
                          exp_w1, exp_w3, exp_w2, l,
                          shared_w1[l], shared_w3[l], shared_w2[l], ln2_g[l], ln2_b[l], bm=MOE_ROW_BLOCK)
    return xf.reshape(b, s, d)
```

```python
import functools

import jax
import jax.numpy as jnp
import numpy as np
from jax import lax
from jax.experimental import pallas as pl
from jax.experimental.pallas import tpu as pltpu

F32 = jnp.float32
BF16 = jnp.bfloat16

D_MODEL = 4096
DEPTH = 2
RWKV_HEAD_DIM = 64
RWKV_WIDTH = 3 * D_MODEL // 8
DECAY_LORA = 128
AAA_LORA = 128
MV_LORA = 96
GATE_LORA = 480
GN_EPS = 64e-5
GMLP_WIDTH = D_MODEL // 4
GMLP_GROUP_DIM = 128
GMLP_GROUPS = GMLP_WIDTH // GMLP_GROUP_DIM
GMLP_CHUNK = 128
MLA_V_DIM = 128
MLA_WIDTH = D_MODEL - RWKV_WIDTH - GMLP_WIDTH
MLA_HEADS = MLA_WIDTH // MLA_V_DIM
MLA_NOPE_DIM = 128
MLA_ROPE_DIM = 64
MLA_QK_DIM = MLA_NOPE_DIM + MLA_ROPE_DIM
MLA_Q_RANK = 768
MLA_KV_RANK = 512
ROPE_THETA = 10000.0
N_EXPERTS = 64
TOP_K = 8
N_EXPERT_GROUPS = 8
TOPK_GROUPS = 4
D_EXPERT = 384
ROUTED_SCALE = 2.5
ALPHA = (2 * DEPTH) ** 0.25
LN_EPS = 1e-5
RMS_EPS = 1e-6

LANES = 128
SUBLANES = 8
VMEM_LIMIT = 56 * 1024 * 1024

P_GU, P_GV = 0, 1024
P_R, P_K, P_V = 2048, 3584, 5120
P_WLO, P_ALO = 6656, 6784
P_CQ, P_CKV = 6912, 7680
P_GLO, P_KROPE, P_VLO = 8192, 8704, 8832
P_COLS = 8960
GLO_PAD = 512
RW_CH = 512
RW_CHUNK = 64
RW_SUB = 4
MOE_ROW_BLOCK = 512


def _cparams(sem):
    return pltpu.CompilerParams(dimension_semantics=sem, vmem_limit_bytes=VMEM_LIMIT)


def _sigmoid(x):
    return 1.0 / (1.0 + jnp.exp(-x))


def _dot(a, b):
    return jnp.dot(a, b, preferred_element_type=F32)


def _dot_nt(a, b):
    return lax.dot_general(a, b, (((1,), (1,)), ((), ())), preferred_element_type=F32)


def _dot_tn(a, b):
    return lax.dot_general(a, b, (((0,), (0,)), ((), ())), preferred_element_type=F32)


def _split_bf16(x):
    hi = x.astype(BF16)
    lo = (x - hi.astype(F32)).astype(BF16)
    return hi, lo


def _mm_kernel(x_ref, w_ref, o_ref):
    o_ref[...] = _dot(x_ref[...], w_ref[...]).astype(o_ref.dtype)


def _matmul(x, w, out_dtype, tm, tn):
    m, k = x.shape
    n = w.shape[1]
    assert m % tm == 0 and n % tn == 0
    return pl.pallas_call(
        _mm_kernel,
        grid=(n // tn, m // tm),
        in_specs=[pl.BlockSpec((tm, k), lambda j, i: (i, 0)),
                  pl.BlockSpec((k, tn), lambda j, i: (0, j))],
        out_specs=pl.BlockSpec((tm, tn), lambda j, i: (i, j)),
        out_shape=jax.ShapeDtypeStruct((m, n), out_dtype),
        compiler_params=_cparams(("arbitrary", "arbitrary")),
        name="matmul",
    )(x, w)


def _out_proj_kernel(ya_ref, yb_ref, yc_ref, wa_ref, wb_ref, wc_ref, o_ref):
    o_ref[...] = _dot(ya_ref[...], wa_ref[...]) + _dot(yb_ref[...], wb_ref[...]) + _dot(yc_ref[...], wc_ref[...])


def _out_proj(y_a, y_b, y_c, w_out, tm, tn):
    m = y_a.shape[0]
    n = w_out.shape[1]
    ka, kb = y_a.shape[1], y_b.shape[1]
    w = w_out.astype(BF16)
    parts = [w[:ka], w[ka:ka + kb], w[ka + kb:]]
    ys = [y_a, y_b, y_c]
    return pl.pallas_call(
        _out_proj_kernel,
        grid=(n // tn, m // tm),
        in_specs=[pl.BlockSpec((tm, y.shape[1]), lambda j, i: (i, 0)) for y in ys]
        + [pl.BlockSpec((p.shape[0], tn), lambda j, i: (0, j)) for p in parts],
        out_specs=pl.BlockSpec((tm, tn), lambda j, i: (i, j)),
        out_shape=jax.ShapeDtypeStruct((m, n), F32),
        compiler_params=_cparams(("arbitrary", "arbitrary")),
        name="out_proj",
    )(*ys, *parts)


def _res_ln_kernel(res_ref, y_ref, g_ref, b_ref, of_ref, ob_ref, opk_ref):
    t = ALPHA * res_ref[...] + y_ref[...]
    mu = jnp.mean(t, axis=-1, keepdims=True)
    c = t - mu
    var = jnp.mean(c * c, axis=-1, keepdims=True)
    out = c * lax.rsqrt(var + LN_EPS) * g_ref[...] + b_ref[...]
    of_ref[...] = out
    ob_ref[...] = out.astype(BF16)
    opk_ref[...] = _pack_bf16_pairs(out)


def _res_ln(res, y, g, b, tm=256):
    m, d = res.shape
    row = pl.BlockSpec((tm, d), lambda i: (i, 0))
    half = pl.BlockSpec((tm, d // 2), lambda i: (i, 0))
    vec = pl.BlockSpec((1, d), lambda i: (0, 0))
    return pl.pallas_call(
        _res_ln_kernel,
        grid=(m // tm,),
        in_specs=[row, row, vec, vec],
        out_specs=[row, row, half],
        out_shape=[jax.ShapeDtypeStruct((m, d), F32), jax.ShapeDtypeStruct((m, d), BF16),
                   jax.ShapeDtypeStruct((m, d // 2), jnp.uint32)],
        compiler_params=_cparams(("arbitrary",)),
        name="res_ln",
    )(res, y, g.reshape(1, d), b.reshape(1, d))


def _shift_mix(cur, prev8, mu, is_first):
    prev_row = jnp.where(is_first, 0.0, prev8[SUBLANES - 1:SUBLANES, :])
    rolled = pltpu.roll(cur, 1, 0)
    row = lax.broadcasted_iota(jnp.int32, cur.shape, 0)
    shifted = jnp.where(row == 0, prev_row, rolled)
    return cur + (shifted - cur) * mu


def _head_block_ones(width):
    r = lax.broadcasted_iota(jnp.int32, (width, width), 0) // RWKV_HEAD_DIM
    c = lax.broadcasted_iota(jnp.int32, (width, width), 1) // RWKV_HEAD_DIM
    return jnp.where(r == c, 1.0, 0.0).astype(BF16)


def _head_sum(x, ones_bd):
    w = ones_bd.shape[0]
    outs = []
    for c in range(x.shape[1] // w):
        hi, lo = _split_bf16(x[:, c * w:(c + 1) * w])
        outs.append(_dot(hi, ones_bd) + _dot(lo, ones_bd))
    return jnp.concatenate(outs, axis=1)


def _rwkv_prep_body(first, r_ref, k_ref, v_ref, rp_ref, kp_ref, vp_ref,
                    wlo_ref, alo_ref, glo_ref, wlop_ref, alop_ref, glop_ref,
                    mur_ref, muk_ref, muv_ref, muw_ref, mua_ref, mug_ref,
                    w2_ref, a2_ref, g2_ref, w0_ref, a0_ref, kk_ref, ka_ref, rk_ref, vres):
    r = _shift_mix(r_ref[...], rp_ref[...], mur_ref[...], first)
    k = _shift_mix(k_ref[...], kp_ref[...], muk_ref[...], first)
    v = _shift_mix(v_ref[...], vp_ref[...], muv_ref[...], first)
    w_lo = _shift_mix(wlo_ref[...], wlop_ref[...], muw_ref[...], first)
    a_lo = _shift_mix(alo_ref[...], alop_ref[...], mua_ref[...], first)
    g_lo = _shift_mix(glo_ref[...], glop_ref[...], mug_ref[...], first)

    z = w0_ref[...] + _dot(jnp.tanh(w_lo).astype(BF16), w2_ref[...])
    nz = -z
    softplus = jnp.maximum(nz, 0.0) + jnp.log(1.0 + jnp.exp(-jnp.abs(nz)))
    log_w = -softplus - 0.5
    ld = -jnp.exp(log_w)
    a = _sigmoid(a0_ref[...] + _dot(a_lo.astype(BF16), a2_ref[...]))
    g = _dot(_sigmoid(g_lo).astype(BF16), g2_ref[...])
    if vres is not None:
        vlo_ref, vlop_ref, muvl_ref, v2_ref, v0_ref, vf_ref = vres
        v_lo = _shift_mix(vlo_ref[...], vlop_ref[...], muvl_ref[...], first)
        mix = _sigmoid(v0_ref[...] + _dot(v_lo.astype(BF16), v2_ref[...]))
        v = v + (vf_ref[...] - v) * mix

    ones_bd = _head_block_ones(2 * LANES)
    kk = k * kk_ref[...]
    ss = _head_sum(kk * kk, ones_bd)
    kn = kk * lax.rsqrt(jnp.maximum(ss, 1e-24))
    k_mod = k * (1.0 + (a - 1.0) * ka_ref[...])
    bonus = _head_sum(r * k_mod * rk_ref[...], ones_bd) * v
    return r, ld, k_mod, v, kn, kn * a, g, bonus


def _make_rwkv_kernel(has_vres):
    n_common = 26

    def kern(*refs):
        common = refs[:n_common]
        rest = refs[n_common:]
        if has_vres:
            vres, rest = rest[:6], rest[6:]
        else:
            vres = None
        lng_ref, lnb_ref = rest[0], rest[1]
        outs, state_ref = rest[2:-1], rest[-1]
        first = pl.program_id(1) == 0
        r, ld, k, v, kn, b, g, bonus = _rwkv_prep_body(first, *common, vres)
        y = _rwkv_scan_block(r, ld, k, v, kn, b, state_ref)
        outs[0][...] = _rwkv_finish(y, g, bonus, lng_ref[...], lnb_ref[...]).astype(outs[0].dtype)
        if not has_vres:
            outs[1][...] = v

    return kern


def _rwkv_group(proj, mu_p, p, l, v_first, ln_g, ln_b):
    t = proj.shape[0]
    tb = RW_CHUNK * RW_SUB
    has_vres = v_first is not None
    nj = RWKV_WIDTH // RW_CH
    pb = tb // SUBLANES

    def cur(width, col0):
        return pl.BlockSpec((tb, width), lambda j, i: (i, col0 // width))

    def cur_j(col0):
        return pl.BlockSpec((tb, RW_CH), lambda j, i: (i, col0 // RW_CH + j))

    def prev(width, col0):
        return pl.BlockSpec((SUBLANES, width), lambda j, i: (jnp.maximum(i * pb - 1, 0), col0 // width))

    def prev_j(col0):
        return pl.BlockSpec((SUBLANES, RW_CH), lambda j, i: (jnp.maximum(i * pb - 1, 0), col0 // RW_CH + j))

    def vec(width, col0):
        return pl.BlockSpec((1, width), lambda j, i: (0, col0 // width))

    def vec_j(col0=0):
        return pl.BlockSpec((1, RW_CH), lambda j, i: (0, col0 // RW_CH + j))

    def lora(rank):
        return pl.BlockSpec((rank, RW_CH), lambda j, i: (0, j))

    row = lambda a: a.reshape(1, -1)
    g2 = jnp.pad(p["rwkv_g2"][l], ((0, GLO_PAD - GATE_LORA), (0, 0))).astype(BF16)
    args = [proj, proj, proj, proj, proj, proj,
            proj, proj, proj, proj, proj, proj,
            mu_p, mu_p, mu_p, mu_p, mu_p, mu_p,
            p["rwkv_w2"][l].astype(BF16), p["rwkv_a2"][l].astype(BF16), g2,
            row(p["rwkv_w0"][l]), row(p["rwkv_a0"][l]), row(p["rwkv_k_k"][l]), row(p["rwkv_k_a"][l]),
            row(p["rwkv_r_k"][l])]
    in_specs = [cur_j(P_R), cur_j(P_K), cur_j(P_V), prev_j(P_R), prev_j(P_K), prev_j(P_V),
                cur(LANES, P_WLO), cur(LANES, P_ALO), cur(GLO_PAD, P_GLO),
                prev(LANES, P_WLO), prev(LANES, P_ALO), prev(GLO_PAD, P_GLO),
                vec_j(P_R), vec_j(P_K), vec_j(P_V), vec(LANES, P_WLO), vec(LANES, P_ALO), vec(GLO_PAD, P_GLO),
                lora(DECAY_LORA), lora(AAA_LORA), lora(GLO_PAD),
                vec_j(), vec_j(), vec_j(), vec_j(), vec_j()]
    if has_vres:
        v2 = jnp.pad(p["rwkv_v2"][l - 1], ((0, LANES - MV_LORA), (0, 0))).astype(BF16)
        args += [proj, proj, mu_p, v2, row(p["rwkv_v0"][l - 1]), v_first]
        in_specs += [cur(LANES, P_VLO), prev(LANES, P_VLO), vec(LANES, P_VLO), lora(LANES), vec_j(),
                     pl.BlockSpec((tb, RW_CH), lambda j, i: (i, j))]
    args += [row(ln_g), row(ln_b)]
    in_specs += [vec_j(), vec_j()]
    out_spec = pl.BlockSpec((tb, RW_CH), lambda j, i: (i, j))
    out_specs = [out_spec]
    out_shape = [jax.ShapeDtypeStruct((t, RWKV_WIDTH), BF16)]
    if not has_vres:
        out_specs.append(out_spec)
        out_shape.append(jax.ShapeDtypeStruct((t, RWKV_WIDTH), F32))
    outs = pl.pallas_call(
        _make_rwkv_kernel(has_vres),
        grid=(nj, t // tb),
        in_specs=in_specs,
        out_specs=out_specs,
        out_shape=out_shape,
        scratch_shapes=[pltpu.VMEM((RW_CH // RWKV_HEAD_DIM, RWKV_HEAD_DIM, RWKV_HEAD_DIM), F32)],
        compiler_params=_cparams(("arbitrary", "arbitrary")),
        name="rwkv",
    )(*args)
    return outs[0], (v_first if has_vres else outs[1])


def _unit_lower_inverse(a_strict, c):
    row = lax.broadcasted_iota(jnp.int32, (c, c), 0)
    col = lax.broadcasted_iota(jnp.int32, (c, c), 1)
    eye = jnp.where(row == col, 1.0, 0.0)
    bd = lambda x: x.astype(BF16)
    base = SUBLANES
    same_base = row // base == col // base
    d1 = [bd(jnp.where(same_base, a, 0.0)) for a in a_strict]
    d2 = [bd(_dot(d, d)) for d in d1]
    inv = [eye + d.astype(F32) for d in d1]
    inv = [i + _dot(d, bd(i)) for i, d in zip(inv, d2)]
    d4 = [bd(_dot(d, d)) for d in d2]
    inv = [i + _dot(d, bd(i)) for i, d in zip(inv, d4)]
    blk = base
    while blk < c:
        band = (row // (2 * blk) == col // (2 * blk)) & (row // blk != col // blk)
        off = [bd(jnp.where(band, a, 0.0)) for a in a_strict]
        inv_b = [bd(i) for i in inv]
        tmp = [bd(_dot(o, i)) for o, i in zip(off, inv_b)]
        inv = [i + _dot(ib, t) for i, ib, t in zip(inv, inv_b, tmp)]
        blk *= 2
    return inv


def _rwkv_scan_block(r, ld, k, v_all, kn, b, state_ref):
    c = RW_CHUNK
    n = RWKV_HEAD_DIM
    heads = RW_CH // n
    rows = ld.shape[0]
    subs = rows // c

    @pl.when(pl.program_id(1) == 0)
    def _():
        state_ref[...] = jnp.zeros_like(state_ref)

    row = lax.broadcasted_iota(jnp.int32, (c, c), 0)
    col = lax.broadcasted_iota(jnp.int32, (c, c), 1)
    strict = row > col
    incl2 = (lax.broadcasted_iota(jnp.int32, (c, 2 * c), 0)
             >= lax.broadcasted_iota(jnp.int32, (c, 2 * c), 1) % c)
    brow = lax.broadcasted_iota(jnp.int32, (rows, rows), 0)
    bcol = lax.broadcasted_iota(jnp.int32, (rows, rows), 1)
    tri_incl = jnp.where((brow >= bcol) & (brow // c == bcol // c), 1.0, 0.0).astype(BF16)

    ld_hi, ld_lo = _split_bf16(ld)
    cum = _dot(tri_incl, ld_hi) + _dot(tri_incl, ld_lo)
    g_inc = jnp.exp(cum)
    g_inv = jnp.exp(-cum)
    a_t = -kn * jnp.exp(cum - ld)
    b_t = b * g_inv
    k_t = k * g_inv
    r_t = r * g_inc

    bd = lambda x: x.astype(BF16)
    idx = [(s, h) for s in range(subs) for h in range(heads)]
    rs = lambda s: slice(s * c, (s + 1) * c)
    ls = lambda h: slice(h * n, (h + 1) * n)
    v_h = [v_all[rs(s), ls(h)] for s, h in idx]
    a_h = [a_t[rs(s), ls(h)] for s, h in idx]
    r_h = [bd(r_t[rs(s), ls(h)]) for s, h in idx]
    ar = [bd(jnp.concatenate([a_h[i], r_t[rs(s), ls(h)]], axis=0)) for i, (s, h) in enumerate(idx)]
    bk = [bd(jnp.concatenate([b_t[rs(s), ls(h)], k_t[rs(s), ls(h)]], axis=0)) for s, h in idx]
    p1 = [_dot_nt(x, y) for x, y in zip(ar, bk)]
    a_ab = [jnp.where(strict, p[:c, :c], 0.0) for p in p1]
    a_ak = [bd(jnp.where(strict, p[:c, c:], 0.0)) for p in p1]
    a_r = [bd(jnp.where(incl2, p[c:, :], 0.0)) for p in p1]
    akv = [_dot(x, bd(y)) for x, y in zip(a_ak, v_h)]
    inv = _unit_lower_inverse(a_ab, c)
    sol = [_dot(bd(inv[i]), bd(jnp.concatenate([a_h[i], akv[i]], axis=1))) for i in range(len(idx))]

    state = [state_ref[h] for h in range(heads)]
    y_rows = []
    for s in range(subs):
        at = lambda lst, h: lst[s * heads + h]
        sb = [bd(x) for x in state]
        u = [_dot_nt(bd(at(sol, h)[:, :n]), sb[h]) + at(sol, h)[:, n:] for h in range(heads)]
        uv = [bd(jnp.concatenate([u[h], at(v_h, h)], axis=0)) for h in range(heads)]
        ys = [_dot_nt(at(r_h, h), sb[h]) + _dot(at(a_r, h), uv[h]) for h in range(heads)]
        g_last = g_inc[(s + 1) * c - 1:(s + 1) * c, :]
        state = [(state[h] + _dot_tn(uv[h], at(bk, h))) * g_last[:, ls(h)] for h in range(heads)]
        y_rows.append(jnp.concatenate(ys, axis=1))
    for h in range(heads):
        state_ref[h] = state[h]
    return jnp.concatenate(y_rows, axis=0)


def _rwkv_finish(y, g, bonus, ln_g, ln_b):
    n = RWKV_HEAD_DIM
    ones_bd = _head_block_ones(2 * LANES)
    mean = _head_sum(y, ones_bd) * (1.0 / n)
    yc = y - mean
    var = _head_sum(yc * yc, ones_bd) * (1.0 / n)
    yn = yc * lax.rsqrt(var + GN_EPS) * ln_g + ln_b
    return (yn + bonus) * g


def _gelu_tanh(x):
    return 0.5 * x * (1.0 + jnp.tanh(np.sqrt(2.0 / np.pi).astype(np.float32) * (x + 0.044715 * (x * x * x))))


def _gmlp_kernel(u_ref, v_ref, lng_ref, lnb_ref, ws_ref, bs_ref, o_ref, *, chunks):
    ch = GMLP_CHUNK
    gd = GMLP_GROUP_DIM
    u = _gelu_tanh(u_ref[...])
    v = _gelu_tanh(v_ref[...])
    mu = jnp.mean(v, axis=-1, keepdims=True)
    vc = v - mu
    var = jnp.mean(vc * vc, axis=-1, keepdims=True)
    vn = (vc * lax.rsqrt(var + LN_EPS) * lng_ref[...] + lnb_ref[...]).astype(BF16)
    row = lax.broadcasted_iota(jnp.int32, (ch, ch), 0)
    col = lax.broadcasted_iota(jnp.int32, (ch, ch), 1)
    causal = row >= col
    bs = bs_ref[...]
    for g in range(GMLP_GROUPS):
        w = jnp.where(causal, ws_ref[g], 0.0).astype(BF16)
        bias = bs[:, g:g + 1]
        for n in range(chunks):
            f = _dot(w, vn[n * ch:(n + 1) * ch, g * gd:(g + 1) * gd]) + bias
            o_ref[n * ch:(n + 1) * ch, g * gd:(g + 1) * gd] = (
                u[n * ch:(n + 1) * ch, g * gd:(g + 1) * gd] * f).astype(o_ref.dtype)


def _gmlp(proj, ln_g, ln_b, w_s, b_s, tb=256):
    t = proj.shape[0]
    bs_t = jnp.pad(b_s.T, ((0, 0), (0, LANES - GMLP_GROUPS)))
    return pl.pallas_call(
        functools.partial(_gmlp_kernel, chunks=tb // GMLP_CHUNK),
        grid=(t // tb,),
        in_specs=[pl.BlockSpec((tb, GMLP_WIDTH), lambda i: (i, P_GU // GMLP_WIDTH)),
                  pl.BlockSpec((tb, GMLP_WIDTH), lambda i: (i, P_GV // GMLP_WIDTH)),
                  pl.BlockSpec((1, GMLP_WIDTH), lambda i: (0, 0)),
                  pl.BlockSpec((1, GMLP_WIDTH), lambda i: (0, 0)),
                  pl.BlockSpec((GMLP_GROUPS, GMLP_CHUNK, GMLP_CHUNK), lambda i: (0, 0, 0)),
                  pl.BlockSpec((GMLP_CHUNK, LANES), lambda i: (0, 0))],
        out_specs=pl.BlockSpec((tb, GMLP_WIDTH), lambda i: (i, 0)),
        out_shape=jax.ShapeDtypeStruct((t, GMLP_WIDTH), BF16),
        compiler_params=_cparams(("arbitrary",)),
        name="gmlp",
    )(proj, proj, ln_g.reshape(1, -1), ln_b.reshape(1, -1), w_s, bs_t)


def _rope_lanes(t, cc, s1, s2):
    return t * cc + pltpu.roll(t, LANES - MLA_ROPE_DIM // 2, 1) * s1 + pltpu.roll(t, MLA_ROPE_DIM // 2, 1) * s2


def _mla_proj_kernel(cq_ref, ckv_ref, kr_ref, qn_ref, kvn_ref, wq_ref, wk_ref, wvt_ref, cc_ref, s1_ref, s2_ref,
                     q_o, kn_o, vt_o, kr_o):
    cq = cq_ref[...]
    qn = (cq * lax.rsqrt(jnp.mean(cq * cq, axis=-1, keepdims=True) + RMS_EPS) * qn_ref[...]).astype(BF16)
    ckv = ckv_ref[...]
    kvn = (ckv * lax.rsqrt(jnp.mean(ckv * ckv, axis=-1, keepdims=True) + RMS_EPS) * kvn_ref[...]).astype(BF16)
    cc, s1, s2 = cc_ref[...], s1_ref[...], s2_ref[...]
    scale = MLA_QK_DIM ** -0.5 * np.log2(np.e)
    for h in range(MLA_HEADS):
        q = _dot(qn, wq_ref[:, 2 * LANES * h:2 * LANES * (h + 1)]) * scale
        q_o[:, 2 * LANES * h:2 * LANES * h + LANES] = q[:, :LANES].astype(BF16)
        q_o[:, 2 * LANES * h + LANES:2 * LANES * (h + 1)] = _rope_lanes(q[:, LANES:], cc, s1, s2).astype(BF16)
    kn_o[...] = _dot(kvn, wk_ref[...]).astype(BF16)
    vt_o[...] = _dot_nt(wvt_ref[...], kvn).astype(BF16)
    kr_o[...] = _rope_lanes(kr_ref[...], cc, s1, s2).astype(BF16)


def _mla_proj(proj, positions, q_norm, kv_norm, w_uq, w_ukv, tm=512):
    t = proj.shape[0]
    h = MLA_HEADS
    half = MLA_ROPE_DIM // 2
    inv_freq = jnp.power(ROPE_THETA, -jnp.arange(0, MLA_ROPE_DIM, 2, dtype=F32) / MLA_ROPE_DIM)
    ang = positions.reshape(t).astype(F32)[:, None] * inv_freq
    cos, sin = jnp.cos(ang), jnp.sin(ang)
    z = jnp.zeros((t, half), F32)
    cc = jnp.concatenate([cos, cos, z, z], axis=1)
    s1 = jnp.concatenate([-sin, z, z, z], axis=1)
    s2 = jnp.concatenate([z, sin, z, z], axis=1)
    wq = w_uq.reshape(MLA_Q_RANK, h, MLA_QK_DIM)
    wq = jnp.pad(wq, ((0, 0), (0, 0), (0, 2 * LANES - MLA_QK_DIM))).reshape(MLA_Q_RANK, h * 2 * LANES).astype(BF16)
    wkv = w_ukv.reshape(MLA_KV_RANK, h, MLA_NOPE_DIM + MLA_V_DIM)
    wk = wkv[:, :, :MLA_NOPE_DIM].reshape(MLA_KV_RANK, h * MLA_NOPE_DIM).astype(BF16)
    wvt = wkv[:, :, MLA_NOPE_DIM:].reshape(MLA_KV_RANK, h * MLA_V_DIM).T.astype(BF16)
    full = lambda a: pl.BlockSpec(a.shape, lambda i: (0,) * a.ndim)
    tab = pl.BlockSpec((tm, LANES), lambda i: (i, 0))
    qn2, kvn2 = q_norm.reshape(1, -1), kv_norm.reshape(1, -1)
    return pl.pallas_call(
        _mla_proj_kernel,
        grid=(t // tm,),
        in_specs=[pl.BlockSpec((tm, MLA_Q_RANK), lambda i: (i, P_CQ // MLA_Q_RANK)),
                  pl.BlockSpec((tm, MLA_KV_RANK), lambda i: (i, P_CKV // MLA_KV_RANK)),
                  pl.BlockSpec((tm, LANES), lambda i: (i, P_KROPE // LANES)),
                  full(qn2), full(kvn2), full(wq), full(wk), full(wvt), tab, tab, tab],
        out_specs=[pl.BlockSpec((tm, h * 2 * LANES), lambda i: (i, 0)),
                   pl.BlockSpec((tm, h * MLA_NOPE_DIM), lambda i: (i, 0)),
                   pl.BlockSpec((h * MLA_V_DIM, tm), lambda i: (0, i)),
                   pl.BlockSpec((tm, LANES), lambda i: (i, 0))],
        out_shape=[jax.ShapeDtypeStruct((t, h * 2 * LANES), BF16),
                   jax.ShapeDtypeStruct((t, h * MLA_NOPE_DIM), BF16),
                   jax.ShapeDtypeStruct((h * MLA_V_DIM, t), BF16),
                   jax.ShapeDtypeStruct((t, LANES), BF16)],
        compiler_params=_cparams(("arbitrary",)),
        name="mla_proj",
    )(proj, proj, proj, qn2, kvn2, wq, wk, wvt, cc, s1, s2)


def _flash_kernel(qi_ref, kj_ref, q_ref, kn_ref, kr_ref, vt_ref, o_ref, m_sc, l_sc, acc_sc, st_sc, *, tq, tk):
    s = pl.program_id(1)
    qi, kj = qi_ref[s], kj_ref[s]

    @pl.when(kj == 0)
    def _():
        m_sc[...] = jnp.full_like(m_sc, -jnp.inf)
        l_sc[...] = jnp.zeros_like(l_sc)
        acc_sc[...] = jnp.zeros_like(acc_sc)

    qt = 2 * LANES
    kb = 2 * LANES

    def step(masked):
        k = jnp.concatenate([kn_ref[...], kr_ref[...]], axis=1)
        m_all = m_sc[...]
        m_news, alphas = [], []
        for c0 in range(0, tq, qt):
            st = _dot_nt(k, q_ref[c0:c0 + qt, :])
            if masked:
                key = kj * tk + lax.broadcasted_iota(jnp.int32, (tk, qt), 0)
                qry = qi * tq + c0 + lax.broadcasted_iota(jnp.int32, (tk, qt), 1)
                st = jnp.where(key <= qry, st, -jnp.inf)
            st_sc[:, c0:c0 + qt] = st
            m_prev = m_all[:, c0:c0 + qt]
            m_new = jnp.maximum(m_prev, jnp.max(st, axis=0, keepdims=True))
            m_news.append(m_new)
            alphas.append(jnp.exp2(m_prev - m_new))
        pvs, sums = [], []
        for t_i, c0 in enumerate(range(0, tq, qt)):
            pv, ps = None, None
            for r0 in range(0, tk, kb):
                p = jnp.exp2(st_sc[r0:r0 + kb, c0:c0 + qt] - m_news[t_i])
                part = jnp.sum(p, axis=0, keepdims=True)
                prod = _dot(vt_ref[:, r0:r0 + kb], p.astype(BF16))
                ps = part if ps is None else ps + part
                pv = prod if pv is None else pv + prod
            pvs.append(pv)
            sums.append(ps)
        alpha = jnp.concatenate(alphas, axis=1)
        m_sc[...] = jnp.concatenate(m_news, axis=1)
        l_sc[...] = alpha * l_sc[...] + jnp.concatenate(sums, axis=1)
        acc_sc[...] = alpha * acc_sc[...] + jnp.concatenate(pvs, axis=1)

    last_key_of_block = kj * tk + tk - 1
    on_diag = last_key_of_block > qi * tq

    @pl.when(jnp.logical_not(on_diag))
    def _():
        step(False)

    @pl.when(on_diag)
    def _():
        step(True)

    @pl.when(last_key_of_block >= qi * tq + tq - 1)
    def _():
        o_ref[...] = (acc_sc[...] / l_sc[...]).T.astype(o_ref.dtype)


def _flash(q, kn, vt, kr, tq=512, tk=512):
    t = q.shape[0]
    assert tq % tk == 0
    pairs = [(i, j) for i in range(t // tq) for j in range((i + 1) * tq // tk)]
    qi = jnp.asarray([pr[0] for pr in pairs], jnp.int32)
    kj = jnp.asarray([pr[1] for pr in pairs], jnp.int32)
    grid_spec = pltpu.PrefetchScalarGridSpec(
        num_scalar_prefetch=2,
        grid=(MLA_HEADS, len(pairs)),
        in_specs=[pl.BlockSpec((tq, 2 * LANES), lambda h, s, qi, kj: (qi[s], h)),
                  pl.BlockSpec((tk, MLA_NOPE_DIM), lambda h, s, qi, kj: (kj[s], h)),
                  pl.BlockSpec((tk, LANES), lambda h, s, qi, kj: (kj[s], 0)),
                  pl.BlockSpec((MLA_V_DIM, tk), lambda h, s, qi, kj: (h, kj[s]))],
        out_specs=pl.BlockSpec((tq, MLA_V_DIM), lambda h, s, qi, kj: (qi[s], h)),
        scratch_shapes=[pltpu.VMEM((1, tq), F32), pltpu.VMEM((1, tq), F32), pltpu.VMEM((MLA_V_DIM, tq), F32),
                        pltpu.VMEM((tk, tq), F32)],
    )
    return pl.pallas_call(
        functools.partial(_flash_kernel, tq=tq, tk=tk),
        grid_spec=grid_spec,
        out_shape=jax.ShapeDtypeStruct((t, MLA_HEADS * MLA_V_DIM), BF16),
        compiler_params=_cparams(("arbitrary", "arbitrary")),
        name="flash",
    )(qi, kj, q, kn, kr, vt)


def _router_kernel(h_ref, rwt_ref, bias_ref, exp_o, pos_o, gate_o, cnt_o, carry_sc):
    e = N_EXPERTS
    per = e // N_EXPERT_GROUPS
    h_hi, h_lo = _split_bf16(h_ref[...])
    w_hi, w_lo = _split_bf16(rwt_ref[...])
    logits = _dot_nt(w_hi, h_hi) + _dot_nt(w_hi, h_lo) + _dot_nt(w_lo, h_hi)
    scores = _sigmoid(logits)
    biased = scores + bias_ref[...][:, 0:1]
    tb = biased.shape[1]
    neg = -jnp.inf
    sub = lax.broadcasted_iota(jnp.int32, (per, tb), 0)
    grp_rows = []
    for g in range(N_EXPERT_GROUPS):
        blk = biased[g * per:(g + 1) * per, :]
        m1 = jnp.max(blk, axis=0, keepdims=True)
        first = jnp.min(jnp.where(blk == m1, sub, per), axis=0, keepdims=True)
        m2 = jnp.max(jnp.where(sub == first, neg, blk), axis=0, keepdims=True)
        grp_rows.append(m1 + m2)
    grp = jnp.concatenate(grp_rows, axis=0)
    gidx = lax.broadcasted_iota(jnp.int32, grp.shape, 0)
    grank = jnp.zeros(grp.shape, jnp.int32)
    for g in range(N_EXPERT_GROUPS):
        other = grp[g:g + 1, :]
        ahead = (other > grp) | ((other == grp) & (g < gidx))
        grank = grank + jnp.where(ahead, 1, 0)
    gsel = grank < TOPK_GROUPS
    masked = jnp.concatenate(
        [jnp.where(gsel[g:g + 1, :], biased[g * per:(g + 1) * per, :], neg) for g in range(N_EXPERT_GROUPS)], axis=0)
    eidx = lax.broadcasted_iota(jnp.int32, masked.shape, 0)
    rank = jnp.zeros(masked.shape, jnp.int32)
    for j in range(e):
        other = masked[j:j + 1, :]
        ahead = (other > masked) | ((other == masked) & (j < eidx))
        rank = rank + jnp.where(ahead, 1, 0)
    chosen = rank < TOP_K
    sel = jnp.where(chosen, scores, 0.0)
    gate = sel / jnp.sum(sel, axis=0, keepdims=True) * ROUTED_SCALE

    @pl.when(pl.program_id(0) == 0)
    def _():
        carry_sc[...] = jnp.zeros_like(carry_sc)

    chosen_f = jnp.where(chosen, 1.0, 0.0)
    earlier = (lax.broadcasted_iota(jnp.int32, (tb, tb), 0) < lax.broadcasted_iota(jnp.int32, (tb, tb), 1))
    carry = carry_sc[...]
    pos = _dot(chosen_f.astype(BF16), jnp.where(earlier, 1.0, 0.0).astype(BF16)) + carry[:, 0:1]
    carry_sc[...] = carry + jnp.sum(chosen_f, axis=1, keepdims=True)
    cnt_o[...] = carry_sc[...].astype(jnp.int32)

    eidx_f = eidx.astype(F32)
    pick = lambda hit, val: jnp.sum(jnp.where(hit, val, 0.0), axis=0, keepdims=True)
    hits = [rank == k for k in range(TOP_K)]
    exp_o[...] = jnp.concatenate([pick(hit, eidx_f) for hit in hits], axis=0).astype(jnp.int32)
    pos_o[...] = jnp.concatenate([pick(hit, pos) for hit in hits], axis=0).astype(jnp.int32)
    gate_o[...] = jnp.concatenate([pick(hit, gate) for hit in hits], axis=0)


def _router(h, router_w, router_bias, tb=512):
    t, d = h.shape
    bias = jnp.broadcast_to(router_bias.astype(F32)[:, None], (N_EXPERTS, LANES))
    per_tok = pl.BlockSpec((TOP_K, tb), lambda i: (0, i))
    return pl.pallas_call(
        _router_kernel,
        grid=(t // tb,),
        in_specs=[pl.BlockSpec((tb, d), lambda i: (i, 0)),
                  pl.BlockSpec((N_EXPERTS, d), lambda i: (0, 0)),
                  pl.BlockSpec((N_EXPERTS, LANES), lambda i: (0, 0))],
        out_specs=[per_tok, per_tok, per_tok, pl.BlockSpec((N_EXPERTS, LANES), lambda i: (0, 0))],
        out_shape=[jax.ShapeDtypeStruct((TOP_K, t), jnp.int32), jax.ShapeDtypeStruct((TOP_K, t), jnp.int32),
                   jax.ShapeDtypeStruct((TOP_K, t), F32), jax.ShapeDtypeStruct((N_EXPERTS, LANES), jnp.int32)],
        scratch_shapes=[pltpu.VMEM((N_EXPERTS, LANES), F32)],
        compiler_params=_cparams(("arbitrary",)),
        name="router",
    )(h, router_w.T, bias)


HALF_MASK = 0xFFFF0000


def _pack_bf16_pairs(x):
    n = x.shape[1] // 2
    lo = pltpu.bitcast(x[:, :n].astype(BF16).astype(F32), jnp.uint32) >> 16
    hi = pltpu.bitcast(x[:, n:].astype(BF16).astype(F32), jnp.uint32) & jnp.uint32(HALF_MASK)
    return lo | hi


def _unpack_bf16_pairs(w):
    return pltpu.bitcast(w << 16, F32), pltpu.bitcast(w & jnp.uint32(HALF_MASK), F32)


def _dispatch_plan(exp_r, pos_r, cnt, bm, nb_max):
    ids = jnp.arange(N_EXPERTS, dtype=jnp.int32)
    nb = (cnt + bm - 1) // bm
    bend = jnp.sum(jnp.where(ids[None, :] <= ids[:, None], nb[None, :], 0), axis=1)
    total = bend[-1]
    slot_start = (bend - nb) * bm
    start_of = jnp.sum(jnp.where(exp_r[:, :, None] == ids, slot_start, 0), axis=-1)
    slot = (start_of + pos_r).T.reshape(-1)
    bidx = jnp.arange(nb_max, dtype=jnp.int32)
    blocks = jnp.minimum(bidx, total - 1)
    blk_exp = jnp.sum(jnp.where(bend[None, :] <= blocks[:, None], 1, 0), axis=1)
    first = jnp.where((bidx == 0) | (blk_exp != jnp.roll(blk_exp, 1)), 1, 0)
    turn = jnp.sum(jnp.where(bidx[None, :] <= bidx[:, None], first[None, :], 0), axis=1) - 1
    later_first = jnp.where((bidx[None, :] > bidx[:, None]) & (first[None, :] == 1), bidx[None, :], nb_max)
    nxt_blk = jnp.min(later_first, axis=1)
    nxt_exp = jnp.sum(jnp.where(bidx[None, :] == nxt_blk[:, None], blk_exp[None, :] + 1, 0), axis=1) - 1
    meta = jnp.concatenate([blk_exp, first, turn % 2, nxt_exp]).astype(jnp.int32)
    lo = jnp.concatenate([slot_start + cnt, (total * bm).reshape(1)])
    hi = jnp.concatenate([slot_start + nb * bm, jnp.full((1,), nb_max * bm, jnp.int32)])
    gaps = jnp.stack([lo, hi], axis=1).reshape(-1).astype(jnp.int32)
    return slot.astype(jnp.int32), meta, total.reshape(1).astype(jnp.int32), gaps


ZERO_ROWS = 256


def _dispatch_kernel(slot_ref, gaps_ref, h_ref, xs_hbm, zero_sc, sem, zsem, *, tb, steps, n_gaps):
    i = pl.program_id(0)

    def fill_gaps(act):
        def gap(r, carry):
            lo, hi = gaps_ref[2 * r], gaps_ref[2 * r + 1]
            lo_tile = jnp.minimum((lo + SUBLANES - 1) // SUBLANES * SUBLANES, hi)

            def single(row, carry2):
                act(pltpu.make_async_copy(zero_sc.at[pl.ds(0, 1), :], xs_hbm.at[pl.ds(row, 1), :], zsem))
                return carry2

            lax.fori_loop(lo, lo_tile, single, 0)
            whole = (hi - lo_tile) // ZERO_ROWS

            def chunk(c, carry2):
                start = pl.multiple_of(lo_tile + c * ZERO_ROWS, SUBLANES)
                act(pltpu.make_async_copy(zero_sc, xs_hbm.at[pl.ds(start, ZERO_ROWS), :], zsem))
                return carry2

            lax.fori_loop(0, whole, chunk, 0)
            off = lo_tile + whole * ZERO_ROWS
            rem = hi - off
            size = ZERO_ROWS // 2
            while size >= SUBLANES:
                take = (rem & size) != 0

                @pl.when(take)
                def _(off=off, size=size):
                    start = pl.multiple_of(off, SUBLANES)
                    act(pltpu.make_async_copy(zero_sc.at[pl.ds(0, size), :], xs_hbm.at[pl.ds(start, size), :], zsem))

                off = off + jnp.where(take, size, 0)
                size //= 2
            return carry

        lax.fori_loop(0, n_gaps, gap, 0)

    @pl.when(i == 0)
    def _():
        zero_sc[...] = jnp.zeros_like(zero_sc)
        fill_gaps(lambda c: c.start())

    def body(t, carry):
        for k in range(TOP_K):
            s = slot_ref[t * TOP_K + k]
            pltpu.make_async_copy(h_ref.at[pl.ds(t, 1), :], xs_hbm.at[pl.ds(s, 1), :], sem).start()
        return carry

    lax.fori_loop(0, tb, body, 0)
    rows = tb * TOP_K
    pltpu.make_async_copy(xs_hbm.at[pl.ds(0, rows), :], xs_hbm.at[pl.ds(0, rows), :], sem).wait()

    @pl.when(i == steps - 1)
    def _():
        fill_gaps(lambda c: c.wait())


def _dispatch(h_pk, slot, gaps, n_slots, tb):
    t, w = h_pk.shape
    steps = t // tb
    return pl.pallas_call(
        functools.partial(_dispatch_kernel, tb=tb, steps=steps, n_gaps=gaps.shape[0] // 2),
        grid=(steps,),
        in_specs=[pl.BlockSpec((tb * TOP_K,), lambda i: (i,), memory_space=pltpu.SMEM),
                  pl.BlockSpec(memory_space=pltpu.SMEM),
                  pl.BlockSpec((tb, w), lambda i: (i, 0))],
        out_specs=pl.BlockSpec(memory_space=pl.ANY),
        out_shape=jax.ShapeDtypeStruct((n_slots, w), jnp.uint32),
        scratch_shapes=[pltpu.VMEM((ZERO_ROWS, w), jnp.uint32), pltpu.SemaphoreType.DMA(()),
                        pltpu.SemaphoreType.DMA(())],
        compiler_params=pltpu.CompilerParams(dimension_semantics=("arbitrary",), vmem_limit_bytes=VMEM_LIMIT,
                                             has_side_effects=True, disable_bounds_checks=True),
        name="moe_dispatch",
    )(slot, gaps, h_pk)


def _expert_meta(meta_ref, nb):
    b = pl.program_id(0)
    return meta_ref[b], meta_ref[nb + b] == 1, meta_ref[2 * nb + b], meta_ref[3 * nb + b]


def _expert_weight_turn(meta_ref, nb, layer, hbm_refs, stage_ref, sem, cast_to):
    e, first, par, nxt = _expert_meta(meta_ref, nb)

    def copies(expert, half):
        return [pltpu.make_async_copy(w.at[layer, expert], stage_ref.at[half, n], sem.at[half, n])
                for n, w in enumerate(hbm_refs)]

    @pl.when(pl.program_id(0) == 0)
    def _():
        for c in copies(e, par):
            c.start()

    @pl.when(first)
    def _():
        for c in copies(e, par):
            c.wait()
        cast_to(stage_ref.at[par])

        @pl.when(nxt >= 0)
        def _():
            for c in copies(nxt, 1 - par):
                c.start()


def _expert_up_kernel(meta_ref, tot_ref, xs_ref, w1_hbm, w3_hbm, act_ref, stage_sc, w13_sc, sem, *, nb, layer):
    def cast_to(staged):
        w13_sc[:, :D_EXPERT] = staged[0].astype(BF16)
        w13_sc[:, D_EXPERT:] = staged[1].astype(BF16)

    _expert_weight_turn(meta_ref, nb, layer, [w1_hbm, w3_hbm], stage_sc, sem, cast_to)

    @pl.when(pl.program_id(0) < tot_ref[0])
    def _():
        lo, hi = _unpack_bf16_pairs(xs_ref[...])
        x = jnp.concatenate([lo.astype(BF16), hi.astype(BF16)], axis=1)
        hgu = _dot(x, w13_sc[...])
        hg, hu = hgu[:, :D_EXPERT], hgu[:, D_EXPERT:]
        act_ref[...] = (hg * _sigmoid(hg) * hu).astype(BF16)

    @pl.when(pl.program_id(0) >= tot_ref[0])
    def _():
        act_ref[...] = jnp.zeros_like(act_ref)


def _expert_down_kernel(meta_ref, tot_ref, act_ref, w2_hbm, ys_ref, stage_sc, w2_sc, sem, *, nb, layer):
    def cast_to(staged):
        w2_sc[...] = staged[0].astype(BF16)

    _expert_weight_turn(meta_ref, nb, layer, [w2_hbm], stage_sc, sem, cast_to)

    @pl.when(pl.program_id(0) < tot_ref[0])
    def _():
        ys_ref[...] = _pack_bf16_pairs(_dot(act_ref[...], w2_sc[...]))

    @pl.when(pl.program_id(0) >= tot_ref[0])
    def _():
        ys_ref[...] = jnp.zeros_like(ys_ref)


def _experts(xs, meta, total, w1, w3, w2, layer, bm):
    n_slots, w = xs.shape
    d = 2 * w
    nb = n_slots // bm
    used = lambda b, meta, tot: (jnp.minimum(b, tot[0] - 1), 0)
    every = lambda b, meta, tot: (b, 0)
    hbm = pl.BlockSpec(memory_space=pl.ANY)
    act = pl.pallas_call(
        functools.partial(_expert_up_kernel, nb=nb, layer=layer),
        grid_spec=pltpu.PrefetchScalarGridSpec(
            num_scalar_prefetch=2,
            grid=(nb,),
            in_specs=[pl.BlockSpec((bm, w), used), hbm, hbm],
            out_specs=pl.BlockSpec((bm, D_EXPERT), every),
            scratch_shapes=[pltpu.VMEM((2, 2, d, D_EXPERT), F32), pltpu.VMEM((d, 2 * D_EXPERT), BF16),
                            pltpu.SemaphoreType.DMA((2, 2))]),
        out_shape=jax.ShapeDtypeStruct((n_slots, D_EXPERT), BF16),
        compiler_params=_cparams(("arbitrary",)),
        name="moe_up",
    )(meta, total, xs, w1, w3)
    return pl.pallas_call(
        functools.partial(_expert_down_kernel, nb=nb, layer=layer),
        grid_spec=pltpu.PrefetchScalarGridSpec(
            num_scalar_prefetch=2,
            grid=(nb,),
            in_specs=[pl.BlockSpec((bm, D_EXPERT), used), hbm],
            out_specs=pl.BlockSpec((bm, w), every),
            scratch_shapes=[pltpu.VMEM((2, 1, D_EXPERT, d), F32), pltpu.VMEM((D_EXPERT, d), BF16),
                            pltpu.SemaphoreType.DMA((2, 1))]),
        out_shape=jax.ShapeDtypeStruct((n_slots, w), jnp.uint32),
        compiler_params=_cparams(("arbitrary",)),
        name="moe_down",
    )(meta, total, act, w2)


def _combine_kernel(slot_ref, slot_next_ref, gate_ref, h_ref, hb_ref, sw13_ref, sw2_ref, g_ref, b_ref, ys_hbm,
                    of_ref, ob_ref, rows_a, rows_b, shared_sc, sem, *, tb, steps):
    i = pl.program_id(0)
    grp = 2 * SUBLANES

    def issue(table_ref, buf, buf_sem, t0, first=0, count=grp):
        for j in range(first, first + count):
            for k in range(TOP_K):
                s = table_ref[(t0 + j) * TOP_K + k]
                pltpu.make_async_copy(ys_hbm.at[pl.ds(s, 1), :], buf.at[k, pl.ds(t0 + j, 1), :], buf_sem).start()

    def wait_block(buf, buf_sem):
        pltpu.make_async_copy(buf, buf, buf_sem).wait()

    @pl.when(i == 0)
    def _():
        def first(g, carry):
            issue(slot_ref, rows_a, sem.at[0], g * grp)
            return carry

        lax.fori_loop(0, tb // grp, first, 0)

    hgu = _dot(hb_ref[...], sw13_ref[...])
    hg, hu = hgu[:, :D_EXPERT], hgu[:, D_EXPERT:]
    shared_sc[...] = _dot((hg * _sigmoid(hg) * hu).astype(BF16), sw2_ref[...])

    def run(cur, cur_sem, nxt, nxt_sem):
        wait_block(cur, cur_sem)

        def group(g, carry):
            r0 = pl.multiple_of(g * grp, grp)
            rows = pl.ds(r0, grp)
            gate = gate_ref[rows, :]
            acc_lo = jnp.zeros((grp, cur.shape[-1]), F32)
            acc_hi = jnp.zeros((grp, cur.shape[-1]), F32)
            per_k = grp // TOP_K
            for k in range(TOP_K):
                issue(slot_next_ref, nxt, nxt_sem, r0, k * per_k, per_k)
                lo, hi = _unpack_bf16_pairs(cur[k, rows, :])
                gk = gate[:, k:k + 1]
                acc_lo = acc_lo + gk * lo
                acc_hi = acc_hi + gk * hi
            t = ALPHA * h_ref[rows, :] + shared_sc[rows, :] + jnp.concatenate([acc_lo, acc_hi], axis=1)
            mu = jnp.mean(t, axis=-1, keepdims=True)
            c = t - mu
            var = jnp.mean(c * c, axis=-1, keepdims=True)
            out = c * lax.rsqrt(var + LN_EPS) * g_ref[...] + b_ref[...]
            of_ref[rows, :] = out
            ob_ref[rows, :] = out.astype(BF16)
            return carry

        lax.fori_loop(0, tb // grp, group, 0)

        @pl.when(i == steps - 1)
        def _():
            wait_block(nxt, nxt_sem)

    even = lax.rem(i, 2) == 0

    @pl.when(even)
    def _():
        run(rows_a, sem.at[0], rows_b, sem.at[1])

    @pl.when(jnp.logical_not(even))
    def _():
        run(rows_b, sem.at[1], rows_a, sem.at[0])


def _combine(ys, slot, gate_tk, hf, hb, sw1, sw3, sw2, g, b, tb):
    t, d = hf.shape
    steps = t // tb
    sw13 = jnp.concatenate([sw1, sw3], axis=1).astype(BF16)
    row = pl.BlockSpec((tb, d), lambda i: (i, 0))
    full = lambda a: pl.BlockSpec(a.shape, lambda i: (0,) * a.ndim)
    g2, b2, sw2b = g.reshape(1, d), b.reshape(1, d), sw2.astype(BF16)
    return pl.pallas_call(
        functools.partial(_combine_kernel, tb=tb, steps=steps),
        grid=(steps,),
        in_specs=[pl.BlockSpec((tb * TOP_K,), lambda i: (i,), memory_space=pltpu.SMEM),
                  pl.BlockSpec((tb * TOP_K,), lambda i: (jnp.minimum(i + 1, steps - 1),), memory_space=pltpu.SMEM),
                  pl.BlockSpec((tb, TOP_K), lambda i: (i, 0)),
                  row, row, full(sw13), full(sw2b), full(g2), full(b2),
                  pl.BlockSpec(memory_space=pl.ANY)],
        out_specs=[row, row],
        out_shape=[jax.ShapeDtypeStruct((t, d), F32), jax.ShapeDtypeStruct((t, d), BF16)],
        scratch_shapes=[pltpu.VMEM((TOP_K, tb, ys.shape[1]), jnp.uint32),
                        pltpu.VMEM((TOP_K, tb, ys.shape[1]), jnp.uint32),
                        pltpu.VMEM((tb, d), F32), pltpu.SemaphoreType.DMA((2,))],
        compiler_params=pltpu.CompilerParams(dimension_semantics=("arbitrary",), vmem_limit_bytes=VMEM_LIMIT,
                                             disable_bounds_checks=True),
        name="moe_combine",
    )(slot, slot, gate_tk, hf, hb, sw13, sw2b, g2, b2, ys)


def _moe_ffn(hf, hb, h_pk, router_w, router_bias, w1, w3, w2, layer, sw1, sw3, sw2, ln_g, ln_b, bm):
    t = hf.shape[0]
    exp_r, pos_r, gate_r, cnt = _router(hf, router_w, router_bias, tb=min(512, t))
    nb_max = t * TOP_K // bm + N_EXPERTS
    slot, meta, total, gaps = _dispatch_plan(exp_r, pos_r, cnt[:, 0], bm, nb_max)
    xs = _dispatch(h_pk, slot, gaps, nb_max * bm, tb=min(256, t // TOP_K))
    ys = _experts(xs, meta, total, w1, w3, w2, layer, bm)
    return _combine(ys, slot, gate_r.T, hf, hb, sw1, sw3, sw2, ln_g, ln_b, tb=min(128, t))


def _prep_w_in(w_in, has_vres):
    d = w_in.shape[0]
    sizes = [RWKV_WIDTH, RWKV_WIDTH, RWKV_WIDTH, DECAY_LORA, AAA_LORA, GATE_LORA,
             GMLP_WIDTH, GMLP_WIDTH, MLA_Q_RANK, MLA_KV_RANK, MLA_ROPE_DIM]
    if has_vres:
        sizes.append(MV_LORA)
    offs = np.concatenate([[0], np.cumsum(sizes)])
    seg = [w_in[:, offs[i]:offs[i + 1]] for i in range(len(sizes))]
    r, k, v, w_lo, a_lo, g_lo, gu, gv, cq, ckv, kr = seg[:11]
    z = lambda n: jnp.zeros((d, n), w_in.dtype)
    v_lo = jnp.concatenate([seg[11], z(LANES - MV_LORA)], axis=1) if has_vres else z(LANES)
    cols = [gu, gv, r, k, v, w_lo, a_lo, cq, ckv, g_lo, z(GLO_PAD - GATE_LORA), kr, z(LANES - MLA_ROPE_DIM), v_lo]
    out = jnp.concatenate(cols, axis=1).astype(BF16)
    assert out.shape[1] == P_COLS
    return out


def _prep_mu(mu, mu_vres):
    out = jnp.zeros((1, P_COLS), F32)
    offs = np.concatenate([[0], np.cumsum([RWKV_WIDTH] * 3 + [DECAY_LORA, AAA_LORA, GATE_LORA])])
    for dst, i in zip([P_R, P_K, P_V, P_WLO, P_ALO, P_GLO], range(6)):
        out = lax.dynamic_update_slice(out, mu[offs[i]:offs[i + 1]].reshape(1, -1), (0, dst))
    if mu_vres is not None:
        out = lax.dynamic_update_slice(out, mu_vres.reshape(1, -1), (0, P_VLO))
    return out


def kernel(x, positions, w_in_first, w_in_rest, rwkv_mu, rwkv_mu_vres, rwkv_w0, rwkv_w2, rwkv_a0, rwkv_a2, rwkv_v0, rwkv_v2, rwkv_g2, rwkv_k_k, rwkv_k_a, rwkv_r_k, rwkv_ln_g, rwkv_ln_b, gmlp_ln_g, gmlp_ln_b, gmlp_w_s, gmlp_b_s, mla_q_norm, mla_kv_norm, mla_w_uq, mla_w_ukv, w_out, ln1_g, ln1_b, router_w, router_bias, exp_w1, exp_w3, exp_w2, shared_w1, shared_w3, shared_w2, ln2_g, ln2_b):
    b, s, d = x.shape
    t = b * s
    p = dict(rwkv_w0=rwkv_w0, rwkv_w2=rwkv_w2, rwkv_a0=rwkv_a0, rwkv_a2=rwkv_a2, rwkv_v0=rwkv_v0, rwkv_v2=rwkv_v2,
             rwkv_g2=rwkv_g2, rwkv_k_k=rwkv_k_k, rwkv_k_a=rwkv_k_a, rwkv_r_k=rwkv_r_k)
    xf = x.reshape(t, d)
    xb = xf.astype(BF16)
    v_first = None
    for l in range(DEPTH):
        has_vres = l > 0
        w_in = _prep_w_in(w_in_first if l == 0 else w_in_rest[l - 1], has_vres)
        mu_p = _prep_mu(rwkv_mu[l], rwkv_mu_vres[l - 1] if has_vres else None)
        proj = _matmul(xb, w_in, F32, tm=min(1024, t), tn=1280)
        y_a, v_first = _rwkv_group(proj, mu_p, p, l, v_first, rwkv_ln_g[l], rwkv_ln_b[l])
        y_b = _gmlp(proj, gmlp_ln_g[l], gmlp_ln_b[l], gmlp_w_s[l], gmlp_b_s[l], tb=min(256, t))
        q, kn, vt, kr = _mla_proj(proj, positions, mla_q_norm[l], mla_kv_norm[l], mla_w_uq[l], mla_w_ukv[l],
                                  tm=min(512, t))
        y_c = _flash(q, kn, vt, kr, tq=min(1024, t), tk=min(1024, t))
        mix = _out_proj(y_a, y_b, y_c, w_out[l], tm=min(1024, t), tn=1024)
        hf, hb, h_pk = _res_ln(xf, mix, ln1_g[l], ln1_b[l], tm=min(256, t))
        xf, xb = _moe_ffn(hf, hb, h_pk, router_w[l], router_bias[l],
                          exp_w1, exp_w3, exp_w2, l,
                          shared_w1[l], shared_w3[l], shared_w2[l], ln2_g[l], ln2_b[l], bm=MOE_ROW_BLOCK)
    return xf.reshape(b, s, d)
```

```python
import functools

import jax
import jax.numpy as jnp
import numpy as np
from jax import lax
from jax.experimental import pallas as pl
from jax.experimental.pallas import tpu as pltpu

F32 = jnp.float32
BF16 = jnp.bfloat16

D_MODEL = 4096
DEPTH = 2
RWKV_HEAD_DIM = 64
RWKV_WIDTH = 3 * D_MODEL // 8
DECAY_LORA = 128
AAA_LORA = 128
MV_LORA = 96
GATE_LORA = 480
GN_EPS = 64e-5
GMLP_WIDTH = D_MODEL // 4
GMLP_GROUP_DIM = 128
GMLP_GROUPS = GMLP_WIDTH // GMLP_GROUP_DIM
GMLP_CHUNK = 128
MLA_V_DIM = 128
MLA_WIDTH = D_MODEL - RWKV_WIDTH - GMLP_WIDTH
MLA_HEADS = MLA_WIDTH // MLA_V_DIM
MLA_NOPE_DIM = 128
MLA_ROPE_DIM = 64
MLA_QK_DIM = MLA_NOPE_DIM + MLA_ROPE_DIM
MLA_Q_RANK = 768
MLA_KV_RANK = 512
ROPE_THETA = 10000.0
N_EXPERTS = 64
TOP_K = 8
N_EXPERT_GROUPS = 8
TOPK_GROUPS = 4
D_EXPERT = 384
ROUTED_SCALE = 2.5
ALPHA = (2 * DEPTH) ** 0.25
LN_EPS = 1e-5
RMS_EPS = 1e-6

LANES = 128
SUBLANES = 8
VMEM_LIMIT = 56 * 1024 * 1024

P_GU, P_GV = 0, 1024
P_R, P_K, P_V = 2048, 3584, 5120
P_WLO, P_ALO = 6656, 6784
P_CQ, P_CKV = 6912, 7680
P_GLO, P_KROPE, P_VLO = 8192, 8704, 8832
P_COLS = 8960
GLO_PAD = 512
RW_CH = 512
RW_CHUNK = 64
RW_SUB = 4
MOE_ROW_BLOCK = 512


def _cparams(sem):
    return pltpu.CompilerParams(dimension_semantics=sem, vmem_limit_bytes=VMEM_LIMIT)


def _sigmoid(x):
    return 1.0 / (1.0 + jnp.exp(-x))


def _dot(a, b):
    return jnp.dot(a, b, preferred_element_type=F32)


def _dot_nt(a, b):
    return lax.dot_general(a, b, (((1,), (1,)), ((), ())), preferred_element_type=F32)


def _dot_tn(a, b):
    return lax.dot_general(a, b, (((0,), (0,)), ((), ())), preferred_element_type=F32)


def _split_bf16(x):
    hi = x.astype(BF16)
    lo = (x - hi.astype(F32)).astype(BF16)
    return hi, lo


def _mm_kernel(x_ref, w_ref, o_ref):
    o_ref[...] = _dot(x_ref[...], w_ref[...]).astype(o_ref.dtype)


def _matmul(x, w, out_dtype, tm, tn):
    m, k = x.shape
    n = w.shape[1]
    assert m % tm == 0 and n % tn == 0
    return pl.pallas_call(
        _mm_kernel,
        grid=(n // tn, m // tm),
        in_specs=[pl.BlockSpec((tm, k), lambda j, i: (i, 0)),
                  pl.BlockSpec((k, tn), lambda j, i: (0, j))],
        out_specs=pl.BlockSpec((tm, tn), lambda j, i: (i, j)),
        out_shape=jax.ShapeDtypeStruct((m, n), out_dtype),
        compiler_params=_cparams(("arbitrary", "arbitrary")),
        name="matmul",
    )(x, w)


def _out_proj_kernel(ya_ref, yb_ref, yc_ref, wa_ref, wb_ref, wc_ref, o_ref):
    o_ref[...] = _dot(ya_ref[...], wa_ref[...]) + _dot(yb_ref[...], wb_ref[...]) + _dot(yc_ref[...], wc_ref[...])


def _out_proj(y_a, y_b, y_c, w_out, tm, tn):
    m = y_a.shape[0]
    n = w_out.shape[1]
    ka, kb = y_a.shape[1], y_b.shape[1]
    w = w_out.astype(BF16)
    parts = [w[:ka], w[ka:ka + kb], w[ka + kb:]]
    ys = [y_a, y_b, y_c]
    return pl.pallas_call(
        _out_proj_kernel,
        grid=(n // tn, m // tm),
        in_specs=[pl.BlockSpec((tm, y.shape[1]), lambda j, i: (i, 0)) for y in ys]
        + [pl.BlockSpec((p.shape[0], tn), lambda j, i: (0, j)) for p in parts],
        out_specs=pl.BlockSpec((tm, tn), lambda j, i: (i, j)),
        out_shape=jax.ShapeDtypeStruct((m, n), F32),
        compiler_params=_cparams(("arbitrary", "arbitrary")),
        name="out_proj",
    )(*ys, *parts)


def _res_ln_kernel(res_ref, y_ref, g_ref, b_ref, of_ref, ob_ref, opk_ref):
    t = ALPHA * res_ref[...] + y_ref[...]
    mu = jnp.mean(t, axis=-1, keepdims=True)
    c = t - mu
    var = jnp.mean(c * c, axis=-1, keepdims=True)
    out = c * lax.rsqrt(var + LN_EPS) * g_ref[...] + b_ref[...]
    of_ref[...] = out
    ob_ref[...] = out.astype(BF16)
    opk_ref[...] = _pack_bf16_pairs(out)


def _res_ln(res, y, g, b, tm=256):
    m, d = res.shape
    row = pl.BlockSpec((tm, d), lambda i: (i, 0))
    half = pl.BlockSpec((tm, d // 2), lambda i: (i, 0))
    vec = pl.BlockSpec((1, d), lambda i: (0, 0))
    return pl.pallas_call(
        _res_ln_kernel,
        grid=(m // tm,),
        in_specs=[row, row, vec, vec],
        out_specs=[row, row, half],
        out_shape=[jax.ShapeDtypeStruct((m, d), F32), jax.ShapeDtypeStruct((m, d), BF16),
                   jax.ShapeDtypeStruct((m, d // 2), jnp.uint32)],
        compiler_params=_cparams(("arbitrary",)),
        name="res_ln",
    )(res, y, g.reshape(1, d), b.reshape(1, d))


def _shift_mix(cur, prev8, mu, is_first):
    prev_row = jnp.where(is_first, 0.0, prev8[SUBLANES - 1:SUBLANES, :])
    rolled = pltpu.roll(cur, 1, 0)
    row = lax.broadcasted_iota(jnp.int32, cur.shape, 0)
    shifted = jnp.where(row == 0, prev_row, rolled)
    return cur + (shifted - cur) * mu


def _head_block_ones(width):
    r = lax.broadcasted_iota(jnp.int32, (width, width), 0) // RWKV_HEAD_DIM
    c = lax.broadcasted_iota(jnp.int32, (width, width), 1) // RWKV_HEAD_DIM
    return jnp.where(r == c, 1.0, 0.0).astype(BF16)


def _head_sum(x, ones_bd):
    w = ones_bd.shape[0]
    outs = []
    for c in range(x.shape[1] // w):
        hi, lo = _split_bf16(x[:, c * w:(c + 1) * w])
        outs.append(_dot(hi, ones_bd) + _dot(lo, ones_bd))
    return jnp.concatenate(outs, axis=1)


def _rwkv_prep_body(first, r_ref, k_ref, v_ref, rp_ref, kp_ref, vp_ref,
                    wlo_ref, alo_ref, glo_ref, wlop_ref, alop_ref, glop_ref,
                    mur_ref, muk_ref, muv_ref, muw_ref, mua_ref, mug_ref,
                    w2_ref, a2_ref, g2_ref, w0_ref, a0_ref, kk_ref, ka_ref, rk_ref, vres):
    r = _shift_mix(r_ref[...], rp_ref[...], mur_ref[...], first)
    k = _shift_mix(k_ref[...], kp_ref[...], muk_ref[...], first)
    v = _shift_mix(v_ref[...], vp_ref[...], muv_ref[...], first)
    w_lo = _shift_mix(wlo_ref[...], wlop_ref[...], muw_ref[...], first)
    a_lo = _shift_mix(alo_ref[...], alop_ref[...], mua_ref[...], first)
    g_lo = _shift_mix(glo_ref[...], glop_ref[...], mug_ref[...], first)

    z = w0_ref[...] + _dot(jnp.tanh(w_lo).astype(BF16), w2_ref[...])
    nz = -z
    softplus = jnp.maximum(nz, 0.0) + jnp.log(1.0 + jnp.exp(-jnp.abs(nz)))
    log_w = -softplus - 0.5
    ld = -jnp.exp(log_w)
    a = _sigmoid(a0_ref[...] + _dot(a_lo.astype(BF16), a2_ref[...]))
    g = _dot(_sigmoid(g_lo).astype(BF16), g2_ref[...])
    if vres is not None:
        vlo_ref, vlop_ref, muvl_ref, v2_ref, v0_ref, vf_ref = vres
        v_lo = _shift_mix(vlo_ref[...], vlop_ref[...], muvl_ref[...], first)
        mix = _sigmoid(v0_ref[...] + _dot(v_lo.astype(BF16), v2_ref[...]))
        v = v + (vf_ref[...] - v) * mix

    ones_bd = _head_block_ones(2 * LANES)
    kk = k * kk_ref[...]
    ss = _head_sum(kk * kk, ones_bd)
    kn = kk * lax.rsqrt(jnp.maximum(ss, 1e-24))
    k_mod = k * (1.0 + (a - 1.0) * ka_ref[...])
    bonus = _head_sum(r * k_mod * rk_ref[...], ones_bd) * v
    return r, ld, k_mod, v, kn, kn * a, g, bonus


def _make_rwkv_kernel(has_vres):
    n_common = 26

    def kern(*refs):
        common = refs[:n_common]
        rest = refs[n_common:]
        if has_vres:
            vres, rest = rest[:6], rest[6:]
        else:
            vres = None
        lng_ref, lnb_ref = rest[0], rest[1]
        outs, state_ref = rest[2:-1], rest[-1]
        first = pl.program_id(1) == 0
        r, ld, k, v, kn, b, g, bonus = _rwkv_prep_body(first, *common, vres)
        y = _rwkv_scan_block(r, ld, k, v, kn, b, state_ref)
        outs[0][...] = _rwkv_finish(y, g, bonus, lng_ref[...], lnb_ref[...]).astype(outs[0].dtype)
        if not has_vres:
            outs[1][...] = v

    return kern


def _rwkv_group(proj, mu_p, p, l, v_first, ln_g, ln_b):
    t = proj.shape[0]
    tb = RW_CHUNK * RW_SUB
    has_vres = v_first is not None
    nj = RWKV_WIDTH // RW_CH
    pb = tb // SUBLANES

    def cur(width, col0):
        return pl.BlockSpec((tb, width), lambda j, i: (i, col0 // width))

    def cur_j(col0):
        return pl.BlockSpec((tb, RW_CH), lambda j, i: (i, col0 // RW_CH + j))

    def prev(width, col0):
        return pl.BlockSpec((SUBLANES, width), lambda j, i: (jnp.maximum(i * pb - 1, 0), col0 // width))

    def prev_j(col0):
        return pl.BlockSpec((SUBLANES, RW_CH), lambda j, i: (jnp.maximum(i * pb - 1, 0), col0 // RW_CH + j))

    def vec(width, col0):
        return pl.BlockSpec((1, width), lambda j, i: (0, col0 // width))

    def vec_j(col0=0):
        return pl.BlockSpec((1, RW_CH), lambda j, i: (0, col0 // RW_CH + j))

    def lora(rank):
        return pl.BlockSpec((rank, RW_CH), lambda j, i: (0, j))

    row = lambda a: a.reshape(1, -1)
    g2 = jnp.pad(p["rwkv_g2"][l], ((0, GLO_PAD - GATE_LORA), (0, 0))).astype(BF16)
    args = [proj, proj, proj, proj, proj, proj,
            proj, proj, proj, proj, proj, proj,
            mu_p, mu_p, mu_p, mu_p, mu_p, mu_p,
            p["rwkv_w2"][l].astype(BF16), p["rwkv_a2"][l].astype(BF16), g2,
            row(p["rwkv_w0"][l]), row(p["rwkv_a0"][l]), row(p["rwkv_k_k"][l]), row(p["rwkv_k_a"][l]),
            row(p["rwkv_r_k"][l])]
    in_specs = [cur_j(P_R), cur_j(P_K), cur_j(P_V), prev_j(P_R), prev_j(P_K), prev_j(P_V),
                cur(LANES, P_WLO), cur(LANES, P_ALO), cur(GLO_PAD, P_GLO),
                prev(LANES, P_WLO), prev(LANES, P_ALO), prev(GLO_PAD, P_GLO),
                vec_j(P_R), vec_j(P_K), vec_j(P_V), vec(LANES, P_WLO), vec(LANES, P_ALO), vec(GLO_PAD, P_GLO),
                lora(DECAY_LORA), lora(AAA_LORA), lora(GLO_PAD),
                vec_j(), vec_j(), vec_j(), vec_j(), vec_j()]
    if has_vres:
        v2 = jnp.pad(p["rwkv_v2"][l - 1], ((0, LANES - MV_LORA), (0, 0))).astype(BF16)
        args += [proj, proj, mu_p, v2, row(p["rwkv_v0"][l - 1]), v_first]
        in_specs += [cur(LANES, P_VLO), prev(LANES, P_VLO), vec(LANES, P_VLO), lora(LANES), vec_j(),
                     pl.BlockSpec((tb, RW_CH), lambda j, i: (i, j))]
    args += [row(ln_g), row(ln_b)]
    in_specs += [vec_j(), vec_j()]
    out_spec = pl.BlockSpec((tb, RW_CH), lambda j, i: (i, j))
    out_specs = [out_spec]
    out_shape = [jax.ShapeDtypeStruct((t, RWKV_WIDTH), BF16)]
    if not has_vres:
        out_specs.append(out_spec)
        out_shape.append(jax.ShapeDtypeStruct((t, RWKV_WIDTH), F32))
    outs = pl.pallas_call(
        _make_rwkv_kernel(has_vres),
        grid=(nj, t // tb),
        in_specs=in_specs,
        out_specs=out_specs,
        out_shape=out_shape,
        scratch_shapes=[pltpu.VMEM((RW_CH // RWKV_HEAD_DIM, RWKV_HEAD_DIM, RWKV_HEAD_DIM), F32)],
        compiler_params=_cparams(("arbitrary", "arbitrary")),
        name="rwkv",
    )(*args)
    return outs[0], (v_first if has_vres else outs[1])


def _unit_lower_inverse(a_strict, c):
    row = lax.broadcasted_iota(jnp.int32, (c, c), 0)
    col = lax.broadcasted_iota(jnp.int32, (c, c), 1)
    eye = jnp.where(row == col, 1.0, 0.0)
    bd = lambda x: x.astype(BF16)
    base = SUBLANES
    same_base = row // base == col // base
    d1 = [bd(jnp.where(same_base, a, 0.0)) for a in a_strict]
    d2 = [bd(_dot(d, d)) for d in d1]
    inv = [eye + d.astype(F32) for d in d1]
    inv = [i + _dot(d, bd(i)) for i, d in zip(inv, d2)]
    d4 = [bd(_dot(d, d)) for d in d2]
    inv = [i + _dot(d, bd(i)) for i, d in zip(inv, d4)]
    blk = base
    while blk < c:
        band = (row // (2 * blk) == col // (2 * blk)) & (row // blk != col // blk)
        off = [bd(jnp.where(band, a, 0.0)) for a in a_strict]
        inv_b = [bd(i) for i in inv]
        tmp = [bd(_dot(o, i)) for o, i in zip(off, inv_b)]
        inv = [i + _dot(ib, t) for i, ib, t in zip(inv, inv_b, tmp)]
        blk *= 2
    return inv


def _rwkv_scan_block(r, ld, k, v_all, kn, b, state_ref):
    c = RW_CHUNK
    n = RWKV_HEAD_DIM
    heads = RW_CH // n
    rows = ld.shape[0]
    subs = rows // c

    @pl.when(pl.program_id(1) == 0)
    def _():
        state_ref[...] = jnp.zeros_like(state_ref)

    row = lax.broadcasted_iota(jnp.int32, (c, c), 0)
    col = lax.broadcasted_iota(jnp.int32, (c, c), 1)
    strict = row > col
    incl2 = (lax.broadcasted_iota(jnp.int32, (c, 2 * c), 0)
             >= lax.broadcasted_iota(jnp.int32, (c, 2 * c), 1) % c)
    brow = lax.broadcasted_iota(jnp.int32, (rows, rows), 0)
    bcol = lax.broadcasted_iota(jnp.int32, (rows, rows), 1)
    tri_incl = jnp.where((brow >= bcol) & (brow // c == bcol // c), 1.0, 0.0).astype(BF16)

    ld_hi, ld_lo = _split_bf16(ld)
    cum = _dot(tri_incl, ld_hi) + _dot(tri_incl, ld_lo)
    g_inc = jnp.exp(cum)
    g_inv = jnp.exp(-cum)
    a_t = -kn * jnp.exp(cum - ld)
    b_t = b * g_inv
    k_t = k * g_inv
    r_t = r * g_inc

    bd = lambda x: x.astype(BF16)
    idx = [(s, h) for s in range(subs) for h in range(heads)]
    rs = lambda s: slice(s * c, (s + 1) * c)
    ls = lambda h: slice(h * n, (h + 1) * n)
    v_h = [v_all[rs(s), ls(h)] for s, h in idx]
    a_h = [a_t[rs(s), ls(h)] for s, h in idx]
    r_h = [bd(r_t[rs(s), ls(h)]) for s, h in idx]
    ar = [bd(jnp.concatenate([a_h[i], r_t[rs(s), ls(h)]], axis=0)) for i, (s, h) in enumerate(idx)]
    bk = [bd(jnp.concatenate([b_t[rs(s), ls(h)], k_t[rs(s), ls(h)]], axis=0)) for s, h in idx]
    p1 = [_dot_nt(x, y) for x, y in zip(ar, bk)]
    a_ab = [jnp.where(strict, p[:c, :c], 0.0) for p in p1]
    a_ak = [bd(jnp.where(strict, p[:c, c:], 0.0)) for p in p1]
    a_r = [bd(jnp.where(incl2, p[c:, :], 0.0)) for p in p1]
    akv = [_dot(x, bd(y)) for x, y in zip(a_ak, v_h)]
    inv = _unit_lower_inverse(a_ab, c)
    sol = [_dot(bd(inv[i]), bd(jnp.concatenate([a_h[i], akv[i]], axis=1))) for i in range(len(idx))]

    state = [state_ref[h] for h in range(heads)]
    y_rows = []
    for s in range(subs):
        at = lambda lst, h: lst[s * heads + h]
        sb = [bd(x) for x in state]
        u = [_dot_nt(bd(at(sol, h)[:, :n]), sb[h]) + at(sol, h)[:, n:] for h in range(heads)]
        uv = [bd(jnp.concatenate([u[h], at(v_h, h)], axis=0)) for h in range(heads)]
        ys = [_dot_nt(at(r_h, h), sb[h]) + _dot(at(a_r, h), uv[h]) for h in range(heads)]
        g_last = g_inc[(s + 1) * c - 1:(s + 1) * c, :]
        state = [(state[h] + _dot_tn(uv[h], at(bk, h))) * g_last[:, ls(h)] for h in range(heads)]
        y_rows.append(jnp.concatenate(ys, axis=1))
    for h in range(heads):
        state_ref[h] = state[h]
    return jnp.concatenate(y_rows, axis=0)


def _rwkv_finish(y, g, bonus, ln_g, ln_b):
    n = RWKV_HEAD_DIM
    ones_bd = _head_block_ones(2 * LANES)
    mean = _head_sum(y, ones_bd) * (1.0 / n)
    yc = y - mean
    var = _head_sum(yc * yc, ones_bd) * (1.0 / n)
    yn = yc * lax.rsqrt(var + GN_EPS) * ln_g + ln_b
    return (yn + bonus) * g


def _gelu_tanh(x):
    return 0.5 * x * (1.0 + jnp.tanh(np.sqrt(2.0 / np.pi).astype(np.float32) * (x + 0.044715 * (x * x * x))))


def _gmlp_kernel(u_ref, v_ref, lng_ref, lnb_ref, ws_ref, bs_ref, o_ref, *, chunks):
    ch = GMLP_CHUNK
    gd = GMLP_GROUP_DIM
    u = _gelu_tanh(u_ref[...])
    v = _gelu_tanh(v_ref[...])
    mu = jnp.mean(v, axis=-1, keepdims=True)
    vc = v - mu
    var = jnp.mean(vc * vc, axis=-1, keepdims=True)
    vn = (vc * lax.rsqrt(var + LN_EPS) * lng_ref[...] + lnb_ref[...]).astype(BF16)
    row = lax.broadcasted_iota(jnp.int32, (ch, ch), 0)
    col = lax.broadcasted_iota(jnp.int32, (ch, ch), 1)
    causal = row >= col
    bs = bs_ref[...]
    for g in range(GMLP_GROUPS):
        w = jnp.where(causal, ws_ref[g], 0.0).astype(BF16)
        bias = bs[:, g:g + 1]
        for n in range(chunks):
            f = _dot(w, vn[n * ch:(n + 1) * ch, g * gd:(g + 1) * gd]) + bias
            o_ref[n * ch:(n + 1) * ch, g * gd:(g + 1) * gd] = (
                u[n * ch:(n + 1) * ch, g * gd:(g + 1) * gd] * f).astype(o_ref.dtype)


def _gmlp(proj, ln_g, ln_b, w_s, b_s, tb=256):
    t = proj.shape[0]
    bs_t = jnp.pad(b_s.T, ((0, 0), (0, LANES - GMLP_GROUPS)))
    return pl.pallas_call(
        functools.partial(_gmlp_kernel, chunks=tb // GMLP_CHUNK),
        grid=(t // tb,),
        in_specs=[pl.BlockSpec((tb, GMLP_WIDTH), lambda i: (i, P_GU // GMLP_WIDTH)),
                  pl.BlockSpec((tb, GMLP_WIDTH), lambda i: (i, P_GV // GMLP_WIDTH)),
                  pl.BlockSpec((1, GMLP_WIDTH), lambda i: (0, 0)),
                  pl.BlockSpec((1, GMLP_WIDTH), lambda i: (0, 0)),
                  pl.BlockSpec((GMLP_GROUPS, GMLP_CHUNK, GMLP_CHUNK), lambda i: (0, 0, 0)),
                  pl.BlockSpec((GMLP_CHUNK, LANES), lambda i: (0, 0))],
        out_specs=pl.BlockSpec((tb, GMLP_WIDTH), lambda i: (i, 0)),
        out_shape=jax.ShapeDtypeStruct((t, GMLP_WIDTH), BF16),
        compiler_params=_cparams(("arbitrary",)),
        name="gmlp",
    )(proj, proj, ln_g.reshape(1, -1), ln_b.reshape(1, -1), w_s, bs_t)


def _rope_lanes(t, cc, s1, s2):
    return t * cc + pltpu.roll(t, LANES - MLA_ROPE_DIM // 2, 1) * s1 + pltpu.roll(t, MLA_ROPE_DIM // 2, 1) * s2


def _mla_proj_kernel(cq_ref, ckv_ref, kr_ref, qn_ref, kvn_ref, wq_ref, wk_ref, wvt_ref, cc_ref, s1_ref, s2_ref,
                     q_o, kn_o, vt_o, kr_o):
    cq = cq_ref[...]
    qn = (cq * lax.rsqrt(jnp.mean(cq * cq, axis=-1, keepdims=True) + RMS_EPS) * qn_ref[...]).astype(BF16)
    ckv = ckv_ref[...]
    kvn = (ckv * lax.rsqrt(jnp.mean(ckv * ckv, axis=-1, keepdims=True) + RMS_EPS) * kvn_ref[...]).astype(BF16)
    cc, s1, s2 = cc_ref[...], s1_ref[...], s2_ref[...]
    scale = MLA_QK_DIM ** -0.5 * np.log2(np.e)
    for h in range(MLA_HEADS):
        q = _dot(qn, wq_ref[:, 2 * LANES * h:2 * LANES * (h + 1)]) * scale
        q_o[:, 2 * LANES * h:2 * LANES * h + LANES] = q[:, :LANES].astype(BF16)
        q_o[:, 2 * LANES * h + LANES:2 * LANES * (h + 1)] = _rope_lanes(q[:, LANES:], cc, s1, s2).astype(BF16)
    kn_o[...] = _dot(kvn, wk_ref[...]).astype(BF16)
    vt_o[...] = _dot_nt(wvt_ref[...], kvn).astype(BF16)
    kr_o[...] = _rope_lanes(kr_ref[...], cc, s1, s2).astype(BF16)


def _mla_proj(proj, positions, q_norm, kv_norm, w_uq, w_ukv, tm=512):
    t = proj.shape[0]
    h = MLA_HEADS
    half = MLA_ROPE_DIM // 2
    inv_freq = jnp.power(ROPE_THETA, -jnp.arange(0, MLA_ROPE_DIM, 2, dtype=F32) / MLA_ROPE_DIM)
    ang = positions.reshape(t).astype(F32)[:, None] * inv_freq
    cos, sin = jnp.cos(ang), jnp.sin(ang)
    z = jnp.zeros((t, half), F32)
    cc = jnp.concatenate([cos, cos, z, z], axis=1)
    s1 = jnp.concatenate([-sin, z, z, z], axis=1)
    s2 = jnp.concatenate([z, sin, z, z], axis=1)
    wq = w_uq.reshape(MLA_Q_RANK, h, MLA_QK_DIM)
    wq = jnp.pad(wq, ((0, 0), (0, 0), (0, 2 * LANES - MLA_QK_DIM))).reshape(MLA_Q_RANK, h * 2 * LANES).astype(BF16)
    wkv = w_ukv.reshape(MLA_KV_RANK, h, MLA_NOPE_DIM + MLA_V_DIM)
    wk = wkv[:, :, :MLA_NOPE_DIM].reshape(MLA_KV_RANK, h * MLA_NOPE_DIM).astype(BF16)
    wvt = wkv[:, :, MLA_NOPE_DIM:].reshape(MLA_KV_RANK, h * MLA_V_DIM).T.astype(BF16)
    full = lambda a: pl.BlockSpec(a.shape, lambda i: (0,) * a.ndim)
    tab = pl.BlockSpec((tm, LANES), lambda i: (i, 0))
    qn2, kvn2 = q_norm.reshape(1, -1), kv_norm.reshape(1, -1)
    return pl.pallas_call(
        _mla_proj_kernel,
        grid=(t // tm,),
        in_specs=[pl.BlockSpec((tm, MLA_Q_RANK), lambda i: (i, P_CQ // MLA_Q_RANK)),
                  pl.BlockSpec((tm, MLA_KV_RANK), lambda i: (i, P_CKV // MLA_KV_RANK)),
                  pl.BlockSpec((tm, LANES), lambda i: (i, P_KROPE // LANES)),
                  full(qn2), full(kvn2), full(wq), full(wk), full(wvt), tab, tab, tab],
        out_specs=[pl.BlockSpec((tm, h * 2 * LANES), lambda i: (i, 0)),
                   pl.BlockSpec((tm, h * MLA_NOPE_DIM), lambda i: (i, 0)),
                   pl.BlockSpec((h * MLA_V_DIM, tm), lambda i: (0, i)),
                   pl.BlockSpec((tm, LANES), lambda i: (i, 0))],
        out_shape=[jax.ShapeDtypeStruct((t, h * 2 * LANES), BF16),
                   jax.ShapeDtypeStruct((t, h * MLA_NOPE_DIM), BF16),
                   jax.ShapeDtypeStruct((h * MLA_V_DIM, t), BF16),
                   jax.ShapeDtypeStruct((t, LANES), BF16)],
        compiler_params=_cparams(("arbitrary",)),
        name="mla_proj",
    )(proj, proj, proj, qn2, kvn2, wq, wk, wvt, cc, s1, s2)


def _flash_kernel(qi_ref, kj_ref, q_ref, kn_ref, kr_ref, vt_ref, o_ref, m_sc, l_sc, acc_sc, st_sc, *, tq, tk):
    s = pl.program_id(1)
    qi, kj = qi_ref[s], kj_ref[s]

    @pl.when(kj == 0)
    def _():
        m_sc[...] = jnp.full_like(m_sc, -jnp.inf)
        l_sc[...] = jnp.zeros_like(l_sc)
        acc_sc[...] = jnp.zeros_like(acc_sc)

    qt = 2 * LANES
    kb = 2 * LANES

    def step(masked):
        k = jnp.concatenate([kn_ref[...], kr_ref[...]], axis=1)
        m_all = m_sc[...]
        m_news, alphas = [], []
        for c0 in range(0, tq, qt):
            st = _dot_nt(k, q_ref[c0:c0 + qt, :])
            if masked:
                key = kj * tk + lax.broadcasted_iota(jnp.int32, (tk, qt), 0)
                qry = qi * tq + c0 + lax.broadcasted_iota(jnp.int32, (tk, qt), 1)
                st = jnp.where(key <= qry, st, -jnp.inf)
            st_sc[:, c0:c0 + qt] = st
            m_prev = m_all[:, c0:c0 + qt]
            m_new = jnp.maximum(m_prev, jnp.max(st, axis=0, keepdims=True))
            m_news.append(m_new)
            alphas.append(jnp.exp2(m_prev - m_new))
        pvs, sums = [], []
        for t_i, c0 in enumerate(range(0, tq, qt)):
            pv, ps = None, None
            for r0 in range(0, tk, kb):
                p = jnp.exp2(st_sc[r0:r0 + kb, c0:c0 + qt] - m_news[t_i])
                part = jnp.sum(p, axis=0, keepdims=True)
                prod = _dot(vt_ref[:, r0:r0 + kb], p.astype(BF16))
                ps = part if ps is None else ps + part
                pv = prod if pv is None else pv + prod
            pvs.append(pv)
            sums.append(ps)
        alpha = jnp.concatenate(alphas, axis=1)
        m_sc[...] = jnp.concatenate(m_news, axis=1)
        l_sc[...] = alpha * l_sc[...] + jnp.concatenate(sums, axis=1)
        acc_sc[...] = alpha * acc_sc[...] + jnp.concatenate(pvs, axis=1)

    last_key_of_block = kj * tk + tk - 1
    on_diag = last_key_of_block > qi * tq

    @pl.when(jnp.logical_not(on_diag))
    def _():
        step(False)

    @pl.when(on_diag)
    def _():
        step(True)

    @pl.when(last_key_of_block >= qi * tq + tq - 1)
    def _():
        o_ref[...] = (acc_sc[...] / l_sc[...]).T.astype(o_ref.dtype)


def _flash(q, kn, vt, kr, tq=512, tk=512):
    t = q.shape[0]
    assert tq % tk == 0
    pairs = [(i, j) for i in range(t // tq) for j in range((i + 1) * tq // tk)]
    qi = jnp.asarray([pr[0] for pr in pairs], jnp.int32)
    kj = jnp.asarray([pr[1] for pr in pairs], jnp.int32)
    grid_spec = pltpu.PrefetchScalarGridSpec(
        num_scalar_prefetch=2,
        grid=(MLA_HEADS, len(pairs)),
        in_specs=[pl.BlockSpec((tq, 2 * LANES), lambda h, s, qi, kj: (qi[s], h)),
                  pl.BlockSpec((tk, MLA_NOPE_DIM), lambda h, s, qi, kj: (kj[s], h)),
                  pl.BlockSpec((tk, LANES), lambda h, s, qi, kj: (kj[s], 0)),
                  pl.BlockSpec((MLA_V_DIM, tk), lambda h, s, qi, kj: (h, kj[s]))],
        out_specs=pl.BlockSpec((tq, MLA_V_DIM), lambda h, s, qi, kj: (qi[s], h)),
        scratch_shapes=[pltpu.VMEM((1, tq), F32), pltpu.VMEM((1, tq), F32), pltpu.VMEM((MLA_V_DIM, tq), F32),
                        pltpu.VMEM((tk, tq), F32)],
    )
    return pl.pallas_call(
        functools.partial(_flash_kernel, tq=tq, tk=tk),
        grid_spec=grid_spec,
        out_shape=jax.ShapeDtypeStruct((t, MLA_HEADS * MLA_V_DIM), BF16),
        compiler_params=_cparams(("arbitrary", "arbitrary")),
        name="flash",
    )(qi, kj, q, kn, kr, vt)


def _router_kernel(h_ref, rwt_ref, bias_ref, exp_o, pos_o, gate_o, cnt_o, carry_sc):
    e = N_EXPERTS
    per = e // N_EXPERT_GROUPS
    h_hi, h_lo = _split_bf16(h_ref[...])
    w_hi, w_lo = _split_bf16(rwt_ref[...])
    logits = _dot_nt(w_hi, h_hi) + _dot_nt(w_hi, h_lo) + _dot_nt(w_lo, h_hi)
    scores = _sigmoid(logits)
    biased = scores + bias_ref[...][:, 0:1]
    tb = biased.shape[1]
    neg = -jnp.inf
    sub = lax.broadcasted_iota(jnp.int32, (per, tb), 0)
    grp_rows = []
    for g in range(N_EXPERT_GROUPS):
        blk = biased[g * per:(g + 1) * per, :]
        m1 = jnp.max(blk, axis=0, keepdims=True)
        first = jnp.min(jnp.where(blk == m1, sub, per), axis=0, keepdims=True)
        m2 = jnp.max(jnp.where(sub == first, neg, blk), axis=0, keepdims=True)
        grp_rows.append(m1 + m2)
    grp = jnp.concatenate(grp_rows, axis=0)
    gidx = lax.broadcasted_iota(jnp.int32, grp.shape, 0)
    grank = jnp.zeros(grp.shape, jnp.int32)
    for g in range(N_EXPERT_GROUPS):
        other = grp[g:g + 1, :]
        ahead = (other > grp) | ((other == grp) & (g < gidx))
        grank = grank + jnp.where(ahead, 1, 0)
    gsel = grank < TOPK_GROUPS
    masked = jnp.concatenate(
        [jnp.where(gsel[g:g + 1, :], biased[g * per:(g + 1) * per, :], neg) for g in range(N_EXPERT_GROUPS)], axis=0)
    eidx = lax.broadcasted_iota(jnp.int32, masked.shape, 0)
    rank = jnp.zeros(masked.shape, jnp.int32)
    for j in range(e):
        other = masked[j:j + 1, :]
        ahead = (other > masked) | ((other == masked) & (j < eidx))
        rank = rank + jnp.where(ahead, 1, 0)
    chosen = rank < TOP_K
    sel = jnp.where(chosen, scores, 0.0)
    gate = sel / jnp.sum(sel, axis=0, keepdims=True) * ROUTED_SCALE

    @pl.when(pl.program_id(0) == 0)
    def _():
        carry_sc[...] = jnp.zeros_like(carry_sc)

    chosen_f = jnp.where(chosen, 1.0, 0.0)
    earlier = (lax.broadcasted_iota(jnp.int32, (tb, tb), 0) < lax.broadcasted_iota(jnp.int32, (tb, tb), 1))
    carry = carry_sc[...]
    pos = _dot(chosen_f.astype(BF16), jnp.where(earlier, 1.0, 0.0).astype(BF16)) + carry[:, 0:1]
    carry_sc[...] = carry + jnp.sum(chosen_f, axis=1, keepdims=True)
    cnt_o[...] = carry_sc[...].astype(jnp.int32)

    eidx_f = eidx.astype(F32)
    pick = lambda hit, val: jnp.sum(jnp.where(hit, val, 0.0), axis=0, keepdims=True)
    hits = [rank == k for k in range(TOP_K)]
    exp_o[...] = jnp.concatenate([pick(hit, eidx_f) for hit in hits], axis=0).astype(jnp.int32)
    pos_o[...] = jnp.concatenate([pick(hit, pos) for hit in hits], axis=0).astype(jnp.int32)
    gate_o[...] = jnp.concatenate([pick(hit, gate) for hit in hits], axis=0)


def _router(h, router_w, router_bias, tb=512):
    t, d = h.shape
    bias = jnp.broadcast_to(router_bias.astype(F32)[:, None], (N_EXPERTS, LANES))
    per_tok = pl.BlockSpec((TOP_K, tb), lambda i: (0, i))
    return pl.pallas_call(
        _router_kernel,
        grid=(t // tb,),
        in_specs=[pl.BlockSpec((tb, d), lambda i: (i, 0)),
                  pl.BlockSpec((N_EXPERTS, d), lambda i: (0, 0)),
                  pl.BlockSpec((N_EXPERTS, LANES), lambda i: (0, 0))],
        out_specs=[per_tok, per_tok, per_tok, pl.BlockSpec((N_EXPERTS, LANES), lambda i: (0, 0))],
        out_shape=[jax.ShapeDtypeStruct((TOP_K, t), jnp.int32), jax.ShapeDtypeStruct((TOP_K, t), jnp.int32),
                   jax.ShapeDtypeStruct((TOP_K, t), F32), jax.ShapeDtypeStruct((N_EXPERTS, LANES), jnp.int32)],
        scratch_shapes=[pltpu.VMEM((N_EXPERTS, LANES), F32)],
        compiler_params=_cparams(("arbitrary",)),
        name="router",
    )(h, router_w.T, bias)


HALF_MASK = 0xFFFF0000


def _pack_bf16_pairs(x):
    n = x.shape[1] // 2
    lo = pltpu.bitcast(x[:, :n].astype(BF16).astype(F32), jnp.uint32) >> 16
    hi = pltpu.bitcast(x[:, n:].astype(BF16).astype(F32), jnp.uint32) & jnp.uint32(HALF_MASK)
    return lo | hi


def _unpack_bf16_pairs(w):
    return pltpu.bitcast(w << 16, F32), pltpu.bitcast(w & jnp.uint32(HALF_MASK), F32)


def _dispatch_plan(exp_r, pos_r, cnt, bm, nb_max):
    ids = jnp.arange(N_EXPERTS, dtype=jnp.int32)
    nb = (cnt + bm - 1) // bm
    bend = jnp.sum(jnp.where(ids[None, :] <= ids[:, None], nb[None, :], 0), axis=1)
    total = bend[-1]
    slot_start = (bend - nb) * bm
    start_of = jnp.sum(jnp.where(exp_r[:, :, None] == ids, slot_start, 0), axis=-1)
    slot = (start_of + pos_r).T.reshape(-1)
    bidx = jnp.arange(nb_max, dtype=jnp.int32)
    blocks = jnp.minimum(bidx, total - 1)
    blk_exp = jnp.sum(jnp.where(bend[None, :] <= blocks[:, None], 1, 0), axis=1)
    first = jnp.where((bidx == 0) | (blk_exp != jnp.roll(blk_exp, 1)), 1, 0)
    turn = jnp.sum(jnp.where(bidx[None, :] <= bidx[:, None], first[None, :], 0), axis=1) - 1
    later_first = jnp.where((bidx[None, :] > bidx[:, None]) & (first[None, :] == 1), bidx[None, :], nb_max)
    nxt_blk = jnp.min(later_first, axis=1)
    nxt_exp = jnp.sum(jnp.where(bidx[None, :] == nxt_blk[:, None], blk_exp[None, :] + 1, 0), axis=1) - 1
    meta = jnp.concatenate([blk_exp, first, turn % 2, nxt_exp]).astype(jnp.int32)
    lo = jnp.concatenate([slot_start + cnt, (total * bm).reshape(1)])
    hi = jnp.concatenate([slot_start + nb * bm, jnp.full((1,), nb_max * bm, jnp.int32)])
    gaps = jnp.stack([lo, hi], axis=1).reshape(-1).astype(jnp.int32)
    return slot.astype(jnp.int32), meta, total.reshape(1).astype(jnp.int32), gaps


ZERO_ROWS = 256


def _dispatch_kernel(slot_ref, gaps_ref, h_ref, xs_hbm, zero_sc, sem, zsem, *, tb, steps, n_gaps):
    i = pl.program_id(0)

    def fill_gaps(act):
        def gap(r, carry):
            lo, hi = gaps_ref[2 * r], gaps_ref[2 * r + 1]
            lo_tile = jnp.minimum((lo + SUBLANES - 1) // SUBLANES * SUBLANES, hi)

            def single(row, carry2):
                act(pltpu.make_async_copy(zero_sc.at[pl.ds(0, 1), :], xs_hbm.at[pl.ds(row, 1), :], zsem))
                return carry2

            lax.fori_loop(lo, lo_tile, single, 0)
            whole = (hi - lo_tile) // ZERO_ROWS

            def chunk(c, carry2):
                start = pl.multiple_of(lo_tile + c * ZERO_ROWS, SUBLANES)
                act(pltpu.make_async_copy(zero_sc, xs_hbm.at[pl.ds(start, ZERO_ROWS), :], zsem))
                return carry2

            lax.fori_loop(0, whole, chunk, 0)
            off = lo_tile + whole * ZERO_ROWS
            rem = hi - off
            size = ZERO_ROWS // 2
            while size >= SUBLANES:
                take = (rem & size) != 0

                @pl.when(take)
                def _(off=off, size=size):
                    start = pl.multiple_of(off, SUBLANES)
                    act(pltpu.make_async_copy(zero_sc.at[pl.ds(0, size), :], xs_hbm.at[pl.ds(start, size), :], zsem))

                off = off + jnp.where(take, size, 0)
                size //= 2
            return carry

        lax.fori_loop(0, n_gaps, gap, 0)

    @pl.when(i == 0)
    def _():
        zero_sc[...] = jnp.zeros_like(zero_sc)
        fill_gaps(lambda c: c.start())

    def body(t, carry):
        for k in range(TOP_K):
            s = slot_ref[t * TOP_K + k]
            pltpu.make_async_copy(h_ref.at[pl.ds(t, 1), :], xs_hbm.at[pl.ds(s, 1), :], sem).start()
        return carry

    lax.fori_loop(0, tb, body, 0)
    rows = tb * TOP_K
    pltpu.make_async_copy(xs_hbm.at[pl.ds(0, rows), :], xs_hbm.at[pl.ds(0, rows), :], sem).wait()

    @pl.when(i == steps - 1)
    def _():
        fill_gaps(lambda c: c.wait())


def _dispatch(h_pk, slot, gaps, n_slots, tb):
    t, w = h_pk.shape
    steps = t // tb
    return pl.pallas_call(
        functools.partial(_dispatch_kernel, tb=tb, steps=steps, n_gaps=gaps.shape[0] // 2),
        grid=(steps,),
        in_specs=[pl.BlockSpec((tb * TOP_K,), lambda i: (i,), memory_space=pltpu.SMEM),
                  pl.BlockSpec(memory_space=pltpu.SMEM),
                  pl.BlockSpec((tb, w), lambda i: (i, 0))],
        out_specs=pl.BlockSpec(memory_space=pl.ANY),
        out_shape=jax.ShapeDtypeStruct((n_slots, w), jnp.uint32),
        scratch_shapes=[pltpu.VMEM((ZERO_ROWS, w), jnp.uint32), pltpu.SemaphoreType.DMA(()),
                        pltpu.SemaphoreType.DMA(())],
        compiler_params=pltpu.CompilerParams(dimension_semantics=("arbitrary",), vmem_limit_bytes=VMEM_LIMIT,
                                             has_side_effects=True, disable_bounds_checks=True),
        name="moe_dispatch",
    )(slot, gaps, h_pk)


def _expert_meta(meta_ref, nb):
    b = pl.program_id(0)
    return meta_ref[b], meta_ref[nb + b] == 1, meta_ref[2 * nb + b], meta_ref[3 * nb + b]


def _expert_weight_turn(meta_ref, nb, layer, hbm_refs, stage_ref, sem, cast_to):
    e, first, par, nxt = _expert_meta(meta_ref, nb)

    def copies(expert, half):
        return [pltpu.make_async_copy(w.at[layer, expert], stage_ref.at[half, n], sem.at[half, n])
                for n, w in enumerate(hbm_refs)]

    @pl.when(pl.program_id(0) == 0)
    def _():
        for c in copies(e, par):
            c.start()

    @pl.when(first)
    def _():
        for c in copies(e, par):
            c.wait()
        cast_to(stage_ref.at[par])

        @pl.when(nxt >= 0)
        def _():
            for c in copies(nxt, 1 - par):
                c.start()


def _expert_up_kernel(meta_ref, tot_ref, xs_ref, w1_hbm, w3_hbm, act_ref, stage_sc, w13_sc, sem, *, nb, layer):
    def cast_to(staged):
        w13_sc[:, :D_EXPERT] = staged[0].astype(BF16)
        w13_sc[:, D_EXPERT:] = staged[1].astype(BF16)

    _expert_weight_turn(meta_ref, nb, layer, [w1_hbm, w3_hbm], stage_sc, sem, cast_to)

    @pl.when(pl.program_id(0) < tot_ref[0])
    def _():
        lo, hi = _unpack_bf16_pairs(xs_ref[...])
        x = jnp.concatenate([lo.astype(BF16), hi.astype(BF16)], axis=1)
        hgu = _dot(x, w13_sc[...])
        hg, hu = hgu[:, :D_EXPERT], hgu[:, D_EXPERT:]
        act_ref[...] = (hg * _sigmoid(hg) * hu).astype(BF16)

    @pl.when(pl.program_id(0) >= tot_ref[0])
    def _():
        act_ref[...] = jnp.zeros_like(act_ref)


def _expert_down_kernel(meta_ref, tot_ref, act_ref, w2_hbm, ys_ref, stage_sc, w2_sc, sem, *, nb, layer):
    def cast_to(staged):
        w2_sc[...] = staged[0].astype(BF16)

    _expert_weight_turn(meta_ref, nb, layer, [w2_hbm], stage_sc, sem, cast_to)

    @pl.when(pl.program_id(0) < tot_ref[0])
    def _():
        ys_ref[...] = _pack_bf16_pairs(_dot(act_ref[...], w2_sc[...]))

    @pl.when(pl.program_id(0) >= tot_ref[0])
    def _():
        ys_ref[...] = jnp.zeros_like(ys_ref)


def _experts(xs, meta, total, w1, w3, w2, layer, bm):
    n_slots, w = xs.shape
    d = 2 * w
    nb = n_slots // bm
    used = lambda b, meta, tot: (jnp.minimum(b, tot[0] - 1), 0)
    every = lambda b, meta, tot: (b, 0)
    hbm = pl.BlockSpec(memory_space=pl.ANY)
    act = pl.pallas_call(
        functools.partial(_expert_up_kernel, nb=nb, layer=layer),
        grid_spec=pltpu.PrefetchScalarGridSpec(
            num_scalar_prefetch=2,
            grid=(nb,),
            in_specs=[pl.BlockSpec((bm, w), used), hbm, hbm],
            out_specs=pl.BlockSpec((bm, D_EXPERT), every),
            scratch_shapes=[pltpu.VMEM((2, 2, d, D_EXPERT), F32), pltpu.VMEM((d, 2 * D_EXPERT), BF16),
                            pltpu.SemaphoreType.DMA((2, 2))]),
        out_shape=jax.ShapeDtypeStruct((n_slots, D_EXPERT), BF16),
        compiler_params=_cparams(("arbitrary",)),
        name="moe_up",
    )(meta, total, xs, w1, w3)
    return pl.pallas_call(
        functools.partial(_expert_down_kernel, nb=nb, layer=layer),
        grid_spec=pltpu.PrefetchScalarGridSpec(
            num_scalar_prefetch=2,
            grid=(nb,),
            in_specs=[pl.BlockSpec((bm, D_EXPERT), used), hbm],
            out_specs=pl.BlockSpec((bm, w), every),
            scratch_shapes=[pltpu.VMEM((2, 1, D_EXPERT, d), F32), pltpu.VMEM((D_EXPERT, d), BF16),
                            pltpu.SemaphoreType.DMA((2, 1))]),
        out_shape=jax.ShapeDtypeStruct((n_slots, w), jnp.uint32),
        compiler_params=_cparams(("arbitrary",)),
        name="moe_down",
    )(meta, total, act, w2)


def _combine_kernel(slot_ref, slot_next_ref, gate_ref, h_ref, hb_ref, sw13_ref, sw2_ref, g_ref, b_ref, ys_hbm,
                    of_ref, ob_ref, rows_a, rows_b, shared_sc, sem, *, tb, steps):
    i = pl.program_id(0)
    grp = 2 * SUBLANES

    def issue(table_ref, buf, buf_sem, t0, first=0, count=grp):
        for j in range(first, first + count):
            for k in range(TOP_K):
                s = table_ref[(t0 + j) * TOP_K + k]
                pltpu.make_async_copy(ys_hbm.at[pl.ds(s, 1), :], buf.at[k, pl.ds(t0 + j, 1), :], buf_sem).start()

    def wait_block(buf, buf_sem):
        pltpu.make_async_copy(buf, buf, buf_sem).wait()

    @pl.when(i == 0)
    def _():
        def first(g, carry):
            issue(slot_ref, rows_a, sem.at[0], g * grp)
            return carry

        lax.fori_loop(0, tb // grp, first, 0)

    hgu = _dot(hb_ref[...], sw13_ref[...])
    hg, hu = hgu[:, :D_EXPERT], hgu[:, D_EXPERT:]
    shared_sc[...] = _dot((hg * _sigmoid(hg) * hu).astype(BF16), sw2_ref[...])

    def run(cur, cur_sem, nxt, nxt_sem):
        wait_block(cur, cur_sem)

        def group(g, carry):
            r0 = pl.multiple_of(g * grp, grp)
            rows = pl.ds(r0, grp)
            gate = gate_ref[rows, :]
            acc_lo = jnp.zeros((grp, cur.shape[-1]), F32)
            acc_hi = jnp.zeros((grp, cur.shape[-1]), F32)
            per_k = grp // TOP_K
            for k in range(TOP_K):
                issue(slot_next_ref, nxt, nxt_sem, r0, k * per_k, per_k)
                lo, hi = _unpack_bf16_pairs(cur[k, rows, :])
                gk = gate[:, k:k + 1]
                acc_lo = acc_lo + gk * lo
                acc_hi = acc_hi + gk * hi
            t = ALPHA * h_ref[rows, :] + shared_sc[rows, :] + jnp.concatenate([acc_lo, acc_hi], axis=1)
            mu = jnp.mean(t, axis=-1, keepdims=True)
            c = t - mu
            var = jnp.mean(c * c, axis=-1, keepdims=True)
            out = c * lax.rsqrt(var + LN_EPS) * g_ref[...] + b_ref[...]
            of_ref[rows, :] = out
            ob_ref[rows, :] = out.astype(BF16)
            return carry

        lax.fori_loop(0, tb // grp, group, 0)

        @pl.when(i == steps - 1)
        def _():
            wait_block(nxt, nxt_sem)

    even = lax.rem(i, 2) == 0

    @pl.when(even)
    def _():
        run(rows_a, sem.at[0], rows_b, sem.at[1])

    @pl.when(jnp.logical_not(even))
    def _():
        run(rows_b, sem.at[1], rows_a, sem.at[0])


def _combine(ys, slot, gate_tk, hf, hb, sw1, sw3, sw2, g, b, tb):
    t, d = hf.shape
    steps = t // tb
    sw13 = jnp.concatenate([sw1, sw3], axis=1).astype(BF16)
    row = pl.BlockSpec((tb, d), lambda i: (i, 0))
    full = lambda a: pl.BlockSpec(a.shape, lambda i: (0,) * a.ndim)
    g2, b2, sw2b = g.reshape(1, d), b.reshape(1, d), sw2.astype(BF16)
    return pl.pallas_call(
        functools.partial(_combine_kernel, tb=tb, steps=steps),
        grid=(steps,),
        in_specs=[pl.BlockSpec((tb * TOP_K,), lambda i: (i,), memory_space=pltpu.SMEM),
                  pl.BlockSpec((tb * TOP_K,), lambda i: (jnp.minimum(i + 1, steps - 1),), memory_space=pltpu.SMEM),
                  pl.BlockSpec((tb, TOP_K), lambda i: (i, 0)),
                  row, row, full(sw13), full(sw2b), full(g2), full(b2),
                  pl.BlockSpec(memory_space=pl.ANY)],
        out_specs=[row, row],
        out_shape=[jax.ShapeDtypeStruct((t, d), F32), jax.ShapeDtypeStruct((t, d), BF16)],
        scratch_shapes=[pltpu.VMEM((TOP_K, tb, ys.shape[1]), jnp.uint32),
                        pltpu.VMEM((TOP_K, tb, ys.shape[1]), jnp.uint32),
                        pltpu.VMEM((tb, d), F32), pltpu.SemaphoreType.DMA((2,))],
        compiler_params=pltpu.CompilerParams(dimension_semantics=("arbitrary",), vmem_limit_bytes=VMEM_LIMIT,
                                             disable_bounds_checks=True),
        name="moe_combine",
    )(slot, slot, gate_tk, hf, hb, sw13, sw2b, g2, b2, ys)


def _moe_ffn(hf, hb, h_pk, router_w, router_bias, w1, w3, w2, layer, sw1, sw3, sw2, ln_g, ln_b, bm):
    t = hf.shape[0]
    exp_r, pos_r, gate_r, cnt = _router(hf, router_w, router_bias, tb=min(512, t))
    nb_max = t * TOP_K // bm + N_EXPERTS
    slot, meta, total, gaps = _dispatch_plan(exp_r, pos_r, cnt[:, 0], bm, nb_max)
    xs = _dispatch(h_pk, slot, gaps, nb_max * bm, tb=min(256, t // TOP_K))
    ys = _experts(xs, meta, total, w1, w3, w2, layer, bm)
    return _combine(ys, slot, gate_r.T, hf, hb, sw1, sw3, sw2, ln_g, ln_b, tb=min(128, t))


def _w_in_layout_kernel(w_ref, o_ref, *, segments):
    o_ref[...] = jnp.zeros_like(o_ref)
    for src, dst, width in segments:
        o_ref[:, dst:dst + width] = w_ref[:, src:src + width].astype(BF16)


def _prep_w_in(w_in, has_vres, tr=256):
    d, cols = w_in.shape
    sizes = [RWKV_WIDTH, RWKV_WIDTH, RWKV_WIDTH, DECAY_LORA, AAA_LORA, GATE_LORA,
             GMLP_WIDTH, GMLP_WIDTH, MLA_Q_RANK, MLA_KV_RANK, MLA_ROPE_DIM]
    dsts = [P_R, P_K, P_V, P_WLO, P_ALO, P_GLO, P_GU, P_GV, P_CQ, P_CKV, P_KROPE]
    if has_vres:
        sizes.append(MV_LORA)
        dsts.append(P_VLO)
    srcs = np.concatenate([[0], np.cumsum(sizes)])[:-1]
    segments = tuple((int(s), int(t), int(n)) for s, t, n in zip(srcs, dsts, sizes))
    return pl.pallas_call(
        functools.partial(_w_in_layout_kernel, segments=segments),
        grid=(d // tr,),
        in_specs=[pl.BlockSpec((tr, cols), lambda i: (i, 0))],
        out_specs=pl.BlockSpec((tr, P_COLS), lambda i: (i, 0)),
        out_shape=jax.ShapeDtypeStruct((d, P_COLS), BF16),
        compiler_params=_cparams(("arbitrary",)),
        name="w_in_layout",
    )(w_in)


def _prep_mu(mu, mu_vres):
    out = jnp.zeros((1, P_COLS), F32)
    offs = np.concatenate([[0], np.cumsum([RWKV_WIDTH] * 3 + [DECAY_LORA, AAA_LORA, GATE_LORA])])
    for dst, i in zip([P_R, P_K, P_V, P_WLO, P_ALO, P_GLO], range(6)):
        out = lax.dynamic_update_slice(out, mu[offs[i]:offs[i + 1]].reshape(1, -1), (0, dst))
    if mu_vres is not None:
        out = lax.dynamic_update_slice(out, mu_vres.reshape(1, -1), (0, P_VLO))
    return out


def kernel(x, positions, w_in_first, w_in_rest, rwkv_mu, rwkv_mu_vres, rwkv_w0, rwkv_w2, rwkv_a0, rwkv_a2, rwkv_v0, rwkv_v2, rwkv_g2, rwkv_k_k, rwkv_k_a, rwkv_r_k, rwkv_ln_g, rwkv_ln_b, gmlp_ln_g, gmlp_ln_b, gmlp_w_s, gmlp_b_s, mla_q_norm, mla_kv_norm, mla_w_uq, mla_w_ukv, w_out, ln1_g, ln1_b, router_w, router_bias, exp_w1, exp_w3, exp_w2, shared_w1, shared_w3, shared_w2, ln2_g, ln2_b):
    b, s, d = x.shape
    t = b * s
    p = dict(rwkv_w0=rwkv_w0, rwkv_w2=rwkv_w2, rwkv_a0=rwkv_a0, rwkv_a2=rwkv_a2, rwkv_v0=rwkv_v0, rwkv_v2=rwkv_v2,
             rwkv_g2=rwkv_g2, rwkv_k_k=rwkv_k_k, rwkv_k_a=rwkv_k_a, rwkv_r_k=rwkv_r_k)
    xf = x.reshape(t, d)
    xb = xf.astype(BF16)
    v_first = None
    for l in range(DEPTH):
        has_vres = l > 0
        w_in = _prep_w_in(w_in_first if l == 0 else w_in_rest[l - 1], has_vres)
        mu_p = _prep_mu(rwkv_mu[l], rwkv_mu_vres[l - 1] if has_vres else None)
        proj = _matmul(xb, w_in, F32, tm=min(1024, t), tn=1280)
        y_a, v_first = _rwkv_group(proj, mu_p, p, l, v_first, rwkv_ln_g[l], rwkv_ln_b[l])
        y_b = _gmlp(proj, gmlp_ln_g[l], gmlp_ln_b[l], gmlp_w_s[l], gmlp_b_s[l], tb=min(256, t))
        q, kn, vt, kr = _mla_proj(proj, positions, mla_q_norm[l], mla_kv_norm[l], mla_w_uq[l], mla_w_ukv[l],
                                  tm=min(512, t))
        y_c = _flash(q, kn, vt, kr, tq=min(1024, t), tk=min(1024, t))
        mix = _out_proj(y_a, y_b, y_c, w_out[l], tm=min(1024, t), tn=1024)
        hf, hb, h_pk = _res_ln(xf, mix, ln1_g[l], ln1_b[l], tm=min(256, t))
        xf, xb = _moe_ffn(hf, hb, h_pk, router_w[l], router_bias[l],
                          exp_w1, exp_w3, exp_w2, l,
                          shared_w1[l], shared_w3[l], shared_w2[l], ln2_g[l], ln2_b[l], bm=MOE_ROW_BLOCK)
    return xf.reshape(b, s, d)
```

```python
import functools

import jax
import jax.numpy as jnp
import numpy as np
from jax import lax
from jax.experimental import pallas as pl
from jax.experimental.pallas import tpu as pltpu

F32 = jnp.float32
BF16 = jnp.bfloat16

D_MODEL = 4096
DEPTH = 2
RWKV_HEAD_DIM = 64
RWKV_WIDTH = 3 * D_MODEL // 8
DECAY_LORA = 128
AAA_LORA = 128
MV_LORA = 96
GATE_LORA = 480
GN_EPS = 64e-5
GMLP_WIDTH = D_MODEL // 4
GMLP_GROUP_DIM = 128
GMLP_GROUPS = GMLP_WIDTH // GMLP_GROUP_DIM
GMLP_CHUNK = 128
MLA_V_DIM = 128
MLA_WIDTH = D_MODEL - RWKV_WIDTH - GMLP_WIDTH
MLA_HEADS = MLA_WIDTH // MLA_V_DIM
MLA_NOPE_DIM = 128
MLA_ROPE_DIM = 64
MLA_QK_DIM = MLA_NOPE_DIM + MLA_ROPE_DIM
MLA_Q_RANK = 768
MLA_KV_RANK = 512
ROPE_THETA = 10000.0
N_EXPERTS = 64
TOP_K = 8
N_EXPERT_GROUPS = 8
TOPK_GROUPS = 4
D_EXPERT = 384
ROUTED_SCALE = 2.5
ALPHA = (2 * DEPTH) ** 0.25
LN_EPS = 1e-5
RMS_EPS = 1e-6

LANES = 128
SUBLANES = 8
VMEM_LIMIT = 56 * 1024 * 1024

P_GU, P_GV = 0, 1024
P_R, P_K, P_V = 2048, 3584, 5120
P_WLO, P_ALO = 6656, 6784
P_CQ, P_CKV = 6912, 7680
P_GLO, P_KROPE, P_VLO = 8192, 8704, 8832
P_COLS = 8960
GLO_PAD = 512
RW_CH = 512
RW_CHUNK = 64
RW_SUB = 4
MOE_ROW_BLOCK = 512


def _cparams(sem):
    return pltpu.CompilerParams(dimension_semantics=sem, vmem_limit_bytes=VMEM_LIMIT)


def _sigmoid(x):
    return 1.0 / (1.0 + jnp.exp(-x))


def _dot(a, b):
    return jnp.dot(a, b, preferred_element_type=F32)


def _dot_nt(a, b):
    return lax.dot_general(a, b, (((1,), (1,)), ((), ())), preferred_element_type=F32)


def _dot_tn(a, b):
    return lax.dot_general(a, b, (((0,), (0,)), ((), ())), preferred_element_type=F32)


def _split_bf16(x):
    hi = x.astype(BF16)
    lo = (x - hi.astype(F32)).astype(BF16)
    return hi, lo


def _mm_kernel(x_ref, w_ref, o_ref):
    o_ref[...] = _dot(x_ref[...], w_ref[...]).astype(o_ref.dtype)


def _matmul(x, w, out_dtype, tm, tn):
    m, k = x.shape
    n = w.shape[1]
    assert m % tm == 0 and n % tn == 0
    return pl.pallas_call(
        _mm_kernel,
        grid=(n // tn, m // tm),
        in_specs=[pl.BlockSpec((tm, k), lambda j, i: (i, 0)),
                  pl.BlockSpec((k, tn), lambda j, i: (0, j))],
        out_specs=pl.BlockSpec((tm, tn), lambda j, i: (i, j)),
        out_shape=jax.ShapeDtypeStruct((m, n), out_dtype),
        compiler_params=_cparams(("arbitrary", "arbitrary")),
        name="matmul",
    )(x, w)


def _out_proj_kernel(ya_ref, yb_ref, yc_ref, w_ref, o_ref, wb_sc):
    @pl.when(pl.program_id(1) == 0)
    def _():
        wb_sc[...] = w_ref[0].astype(BF16)

    ka, kb = ya_ref.shape[1], yb_ref.shape[1]
    o_ref[...] = (_dot(ya_ref[...], wb_sc[:ka, :]) + _dot(yb_ref[...], wb_sc[ka:ka + kb, :])
                  + _dot(yc_ref[...], wb_sc[ka + kb:, :]))


def _out_proj(y_a, y_b, y_c, w_out, layer, tm, tn):
    m = y_a.shape[0]
    k, n = w_out.shape[1:]
    ys = [y_a, y_b, y_c]
    return pl.pallas_call(
        _out_proj_kernel,
        grid=(n // tn, m // tm),
        in_specs=[pl.BlockSpec((tm, y.shape[1]), lambda j, i: (i, 0)) for y in ys]
        + [pl.BlockSpec((1, k, tn), lambda j, i: (layer, 0, j))],
        out_specs=pl.BlockSpec((tm, tn), lambda j, i: (i, j)),
        out_shape=jax.ShapeDtypeStruct((m, n), F32),
        scratch_shapes=[pltpu.VMEM((k, tn), BF16)],
        compiler_params=_cparams(("arbitrary", "arbitrary")),
        name="out_proj",
    )(*ys, w_out)


def _res_ln_kernel(res_ref, y_ref, g_ref, b_ref, of_ref, ob_ref, opk_ref):
    t = ALPHA * res_ref[...] + y_ref[...]
    mu = jnp.mean(t, axis=-1, keepdims=True)
    c = t - mu
    var = jnp.mean(c * c, axis=-1, keepdims=True)
    out = c * lax.rsqrt(var + LN_EPS) * g_ref[...] + b_ref[...]
    of_ref[...] = out
    ob_ref[...] = out.astype(BF16)
    opk_ref[...] = _pack_bf16_pairs(out)


def _res_ln(res, y, g, b, tm=256):
    m, d = res.shape
    row = pl.BlockSpec((tm, d), lambda i: (i, 0))
    half = pl.BlockSpec((tm, d // 2), lambda i: (i, 0))
    vec = pl.BlockSpec((1, d), lambda i: (0, 0))
    return pl.pallas_call(
        _res_ln_kernel,
        grid=(m // tm,),
        in_specs=[row, row, vec, vec],
        out_specs=[row, row, half],
        out_shape=[jax.ShapeDtypeStruct((m, d), F32), jax.ShapeDtypeStruct((m, d), BF16),
                   jax.ShapeDtypeStruct((m, d // 2), jnp.uint32)],
        compiler_params=_cparams(("arbitrary",)),
        name="res_ln",
    )(res, y, g.reshape(1, d), b.reshape(1, d))


def _shift_mix(cur, prev8, mu, is_first):
    prev_row = jnp.where(is_first, 0.0, prev8[SUBLANES - 1:SUBLANES, :])
    rolled = pltpu.roll(cur, 1, 0)
    row = lax.broadcasted_iota(jnp.int32, cur.shape, 0)
    shifted = jnp.where(row == 0, prev_row, rolled)
    return cur + (shifted - cur) * mu


def _head_block_ones(width):
    r = lax.broadcasted_iota(jnp.int32, (width, width), 0) // RWKV_HEAD_DIM
    c = lax.broadcasted_iota(jnp.int32, (width, width), 1) // RWKV_HEAD_DIM
    return jnp.where(r == c, 1.0, 0.0).astype(BF16)


def _head_sum(x, ones_bd):
    w = ones_bd.shape[0]
    outs = []
    for c in range(x.shape[1] // w):
        hi, lo = _split_bf16(x[:, c * w:(c + 1) * w])
        outs.append(_dot(hi, ones_bd) + _dot(lo, ones_bd))
    return jnp.concatenate(outs, axis=1)


def _rwkv_prep_body(first, r_ref, k_ref, v_ref, rp_ref, kp_ref, vp_ref,
                    wlo_ref, alo_ref, glo_ref, wlop_ref, alop_ref, glop_ref,
                    mur_ref, muk_ref, muv_ref, muw_ref, mua_ref, mug_ref,
                    w2_ref, a2_ref, g2_ref, w0_ref, a0_ref, kk_ref, ka_ref, rk_ref, vres):
    r = _shift_mix(r_ref[...], rp_ref[...], mur_ref[...], first)
    k = _shift_mix(k_ref[...], kp_ref[...], muk_ref[...], first)
    v = _shift_mix(v_ref[...], vp_ref[...], muv_ref[...], first)
    w_lo = _shift_mix(wlo_ref[...], wlop_ref[...], muw_ref[...], first)
    a_lo = _shift_mix(alo_ref[...], alop_ref[...], mua_ref[...], first)
    g_lo = _shift_mix(glo_ref[...], glop_ref[...], mug_ref[...], first)

    z = w0_ref[...] + _dot(jnp.tanh(w_lo).astype(BF16), w2_ref[...])
    nz = -z
    softplus = jnp.maximum(nz, 0.0) + jnp.log(1.0 + jnp.exp(-jnp.abs(nz)))
    log_w = -softplus - 0.5
    ld = -jnp.exp(log_w)
    a = _sigmoid(a0_ref[...] + _dot(a_lo.astype(BF16), a2_ref[...]))
    g = _dot(_sigmoid(g_lo).astype(BF16), g2_ref[...])
    if vres is not None:
        vlo_ref, vlop_ref, muvl_ref, v2_ref, v0_ref, vf_ref = vres
        v_lo = _shift_mix(vlo_ref[...], vlop_ref[...], muvl_ref[...], first)
        mix = _sigmoid(v0_ref[...] + _dot(v_lo.astype(BF16), v2_ref[...]))
        v = v + (vf_ref[...] - v) * mix

    ones_bd = _head_block_ones(2 * LANES)
    kk = k * kk_ref[...]
    ss = _head_sum(kk * kk, ones_bd)
    kn = kk * lax.rsqrt(jnp.maximum(ss, 1e-24))
    k_mod = k * (1.0 + (a - 1.0) * ka_ref[...])
    bonus = _head_sum(r * k_mod * rk_ref[...], ones_bd) * v
    return r, ld, k_mod, v, kn, kn * a, g, bonus


def _make_rwkv_kernel(has_vres):
    n_common = 26

    def kern(*refs):
        common = refs[:n_common]
        rest = refs[n_common:]
        if has_vres:
            vres, rest = rest[:6], rest[6:]
        else:
            vres = None
        lng_ref, lnb_ref = rest[0], rest[1]
        outs, state_ref = rest[2:-1], rest[-1]
        first = pl.program_id(1) == 0
        r, ld, k, v, kn, b, g, bonus = _rwkv_prep_body(first, *common, vres)
        y = _rwkv_scan_block(r, ld, k, v, kn, b, state_ref)
        outs[0][...] = _rwkv_finish(y, g, bonus, lng_ref[...], lnb_ref[...]).astype(outs[0].dtype)
        if not has_vres:
            outs[1][...] = v

    return kern


def _rwkv_group(proj, mu_p, p, l, v_first, ln_g, ln_b):
    t = proj.shape[0]
    tb = RW_CHUNK * RW_SUB
    has_vres = v_first is not None
    nj = RWKV_WIDTH // RW_CH
    pb = tb // SUBLANES

    def cur(width, col0):
        return pl.BlockSpec((tb, width), lambda j, i: (i, col0 // width))

    def cur_j(col0):
        return pl.BlockSpec((tb, RW_CH), lambda j, i: (i, col0 // RW_CH + j))

    def prev(width, col0):
        return pl.BlockSpec((SUBLANES, width), lambda j, i: (jnp.maximum(i * pb - 1, 0), col0 // width))

    def prev_j(col0):
        return pl.BlockSpec((SUBLANES, RW_CH), lambda j, i: (jnp.maximum(i * pb - 1, 0), col0 // RW_CH + j))

    def vec(width, col0):
        return pl.BlockSpec((1, width), lambda j, i: (0, col0 // width))

    def vec_j(col0=0):
        return pl.BlockSpec((1, RW_CH), lambda j, i: (0, col0 // RW_CH + j))

    def lora(rank):
        return pl.BlockSpec((rank, RW_CH), lambda j, i: (0, j))

    row = lambda a: a.reshape(1, -1)
    g2 = jnp.pad(p["rwkv_g2"][l], ((0, GLO_PAD - GATE_LORA), (0, 0))).astype(BF16)
    args = [proj, proj, proj, proj, proj, proj,
            proj, proj, proj, proj, proj, proj,
            mu_p, mu_p, mu_p, mu_p, mu_p, mu_p,
            p["rwkv_w2"][l].astype(BF16), p["rwkv_a2"][l].astype(BF16), g2,
            row(p["rwkv_w0"][l]), row(p["rwkv_a0"][l]), row(p["rwkv_k_k"][l]), row(p["rwkv_k_a"][l]),
            row(p["rwkv_r_k"][l])]
    in_specs = [cur_j(P_R), cur_j(P_K), cur_j(P_V), prev_j(P_R), prev_j(P_K), prev_j(P_V),
                cur(LANES, P_WLO), cur(LANES, P_ALO), cur(GLO_PAD, P_GLO),
                prev(LANES, P_WLO), prev(LANES, P_ALO), prev(GLO_PAD, P_GLO),
                vec_j(P_R), vec_j(P_K), vec_j(P_V), vec(LANES, P_WLO), vec(LANES, P_ALO), vec(GLO_PAD, P_GLO),
                lora(DECAY_LORA), lora(AAA_LORA), lora(GLO_PAD),
                vec_j(), vec_j(), vec_j(), vec_j(), vec_j()]
    if has_vres:
        v2 = jnp.pad(p["rwkv_v2"][l - 1], ((0, LANES - MV_LORA), (0, 0))).astype(BF16)
        args += [proj, proj, mu_p, v2, row(p["rwkv_v0"][l - 1]), v_first]
        in_specs += [cur(LANES, P_VLO), prev(LANES, P_VLO), vec(LANES, P_VLO), lora(LANES), vec_j(),
                     pl.BlockSpec((tb, RW_CH), lambda j, i: (i, j))]
    args += [row(ln_g), row(ln_b)]
    in_specs += [vec_j(), vec_j()]
    out_spec = pl.BlockSpec((tb, RW_CH), lambda j, i: (i, j))
    out_specs = [out_spec]
    out_shape = [jax.ShapeDtypeStruct((t, RWKV_WIDTH), BF16)]
    if not has_vres:
        out_specs.append(out_spec)
        out_shape.append(jax.ShapeDtypeStruct((t, RWKV_WIDTH), F32))
    outs = pl.pallas_call(
        _make_rwkv_kernel(has_vres),
        grid=(nj, t // tb),
        in_specs=in_specs,
        out_specs=out_specs,
        out_shape=out_shape,
        scratch_shapes=[pltpu.VMEM((RW_CH // RWKV_HEAD_DIM, RWKV_HEAD_DIM, RWKV_HEAD_DIM), F32)],
        compiler_params=_cparams(("arbitrary", "arbitrary")),
        name="rwkv",
    )(*args)
    return outs[0], (v_first if has_vres else outs[1])


def _unit_lower_inverse(a_strict, c):
    row = lax.broadcasted_iota(jnp.int32, (c, c), 0)
    col = lax.broadcasted_iota(jnp.int32, (c, c), 1)
    eye = jnp.where(row == col, 1.0, 0.0)
    bd = lambda x: x.astype(BF16)
    base = SUBLANES
    same_base = row // base == col // base
    d1 = [bd(jnp.where(same_base, a, 0.0)) for a in a_strict]
    d2 = [bd(_dot(d, d)) for d in d1]
    inv = [eye + d.astype(F32) for d in d1]
    inv = [i + _dot(d, bd(i)) for i, d in zip(inv, d2)]
    d4 = [bd(_dot(d, d)) for d in d2]
    inv = [i + _dot(d, bd(i)) for i, d in zip(inv, d4)]
    blk = base
    while blk < c:
        band = (row // (2 * blk) == col // (2 * blk)) & (row // blk != col // blk)
        off = [bd(jnp.where(band, a, 0.0)) for a in a_strict]
        inv_b = [bd(i) for i in inv]
        tmp = [bd(_dot(o, i)) for o, i in zip(off, inv_b)]
        inv = [i + _dot(ib, t) for i, ib, t in zip(inv, inv_b, tmp)]
        blk *= 2
    return inv


def _rwkv_scan_block(r, ld, k, v_all, kn, b, state_ref):
    c = RW_CHUNK
    n = RWKV_HEAD_DIM
    heads = RW_CH // n
    rows = ld.shape[0]
    subs = rows // c

    @pl.when(pl.program_id(1) == 0)
    def _():
        state_ref[...] = jnp.zeros_like(state_ref)

    row = lax.broadcasted_iota(jnp.int32, (c, c), 0)
    col = lax.broadcasted_iota(jnp.int32, (c, c), 1)
    strict = row > col
    incl2 = (lax.broadcasted_iota(jnp.int32, (c, 2 * c), 0)
             >= lax.broadcasted_iota(jnp.int32, (c, 2 * c), 1) % c)
    brow = lax.broadcasted_iota(jnp.int32, (rows, rows), 0)
    bcol = lax.broadcasted_iota(jnp.int32, (rows, rows), 1)
    tri_incl = jnp.where((brow >= bcol) & (brow // c == bcol // c), 1.0, 0.0).astype(BF16)

    ld_hi, ld_lo = _split_bf16(ld)
    cum = _dot(tri_incl, ld_hi) + _dot(tri_incl, ld_lo)
    g_inc = jnp.exp(cum)
    g_inv = jnp.exp(-cum)
    a_t = -kn * jnp.exp(cum - ld)
    b_t = b * g_inv
    k_t = k * g_inv
    r_t = r * g_inc

    bd = lambda x: x.astype(BF16)
    idx = [(s, h) for s in range(subs) for h in range(heads)]
    rs = lambda s: slice(s * c, (s + 1) * c)
    ls = lambda h: slice(h * n, (h + 1) * n)
    v_h = [v_all[rs(s), ls(h)] for s, h in idx]
    a_h = [a_t[rs(s), ls(h)] for s, h in idx]
    r_h = [bd(r_t[rs(s), ls(h)]) for s, h in idx]
    ar = [bd(jnp.concatenate([a_h[i], r_t[rs(s), ls(h)]], axis=0)) for i, (s, h) in enumerate(idx)]
    bk = [bd(jnp.concatenate([b_t[rs(s), ls(h)], k_t[rs(s), ls(h)]], axis=0)) for s, h in idx]
    p1 = [_dot_nt(x, y) for x, y in zip(ar, bk)]
    a_ab = [jnp.where(strict, p[:c, :c], 0.0) for p in p1]
    a_ak = [bd(jnp.where(strict, p[:c, c:], 0.0)) for p in p1]
    a_r = [bd(jnp.where(incl2, p[c:, :], 0.0)) for p in p1]
    akv = [_dot(x, bd(y)) for x, y in zip(a_ak, v_h)]
    inv = _unit_lower_inverse(a_ab, c)
    sol = [_dot(bd(inv[i]), bd(jnp.concatenate([a_h[i], akv[i]], axis=1))) for i in range(len(idx))]

    state = [state_ref[h] for h in range(heads)]
    y_rows = []
    for s in range(subs):
        at = lambda lst, h: lst[s * heads + h]
        sb = [bd(x) for x in state]
        u = [_dot_nt(bd(at(sol, h)[:, :n]), sb[h]) + at(sol, h)[:, n:] for h in range(heads)]
        uv = [bd(jnp.concatenate([u[h], at(v_h, h)], axis=0)) for h in range(heads)]
        ys = [_dot_nt(at(r_h, h), sb[h]) + _dot(at(a_r, h), uv[h]) for h in range(heads)]
        g_last = g_inc[(s + 1) * c - 1:(s + 1) * c, :]
        state = [(state[h] + _dot_tn(uv[h], at(bk, h))) * g_last[:, ls(h)] for h in range(heads)]
        y_rows.append(jnp.concatenate(ys, axis=1))
    for h in range(heads):
        state_ref[h] = state[h]
    return jnp.concatenate(y_rows, axis=0)


def _rwkv_finish(y, g, bonus, ln_g, ln_b):
    n = RWKV_HEAD_DIM
    ones_bd = _head_block_ones(2 * LANES)
    mean = _head_sum(y, ones_bd) * (1.0 / n)
    yc = y - mean
    var = _head_sum(yc * yc, ones_bd) * (1.0 / n)
    yn = yc * lax.rsqrt(var + GN_EPS) * ln_g + ln_b
    return (yn + bonus) * g


def _gelu_tanh(x):
    return 0.5 * x * (1.0 + jnp.tanh(np.sqrt(2.0 / np.pi).astype(np.float32) * (x + 0.044715 * (x * x * x))))


def _gmlp_kernel(u_ref, v_ref, lng_ref, lnb_ref, ws_ref, bs_ref, o_ref, *, chunks):
    ch = GMLP_CHUNK
    gd = GMLP_GROUP_DIM
    u = _gelu_tanh(u_ref[...])
    v = _gelu_tanh(v_ref[...])
    mu = jnp.mean(v, axis=-1, keepdims=True)
    vc = v - mu
    var = jnp.mean(vc * vc, axis=-1, keepdims=True)
    vn = (vc * lax.rsqrt(var + LN_EPS) * lng_ref[...] + lnb_ref[...]).astype(BF16)
    row = lax.broadcasted_iota(jnp.int32, (ch, ch), 0)
    col = lax.broadcasted_iota(jnp.int32, (ch, ch), 1)
    causal = row >= col
    bs = bs_ref[...]
    for g in range(GMLP_GROUPS):
        w = jnp.where(causal, ws_ref[g], 0.0).astype(BF16)
        bias = bs[:, g:g + 1]
        for n in range(chunks):
            f = _dot(w, vn[n * ch:(n + 1) * ch, g * gd:(g + 1) * gd]) + bias
            o_ref[n * ch:(n + 1) * ch, g * gd:(g + 1) * gd] = (
                u[n * ch:(n + 1) * ch, g * gd:(g + 1) * gd] * f).astype(o_ref.dtype)


def _gmlp(proj, ln_g, ln_b, w_s, b_s, tb=256):
    t = proj.shape[0]
    bs_t = jnp.pad(b_s.T, ((0, 0), (0, LANES - GMLP_GROUPS)))
    return pl.pallas_call(
        functools.partial(_gmlp_kernel, chunks=tb // GMLP_CHUNK),
        grid=(t // tb,),
        in_specs=[pl.BlockSpec((tb, GMLP_WIDTH), lambda i: (i, P_GU // GMLP_WIDTH)),
                  pl.BlockSpec((tb, GMLP_WIDTH), lambda i: (i, P_GV // GMLP_WIDTH)),
                  pl.BlockSpec((1, GMLP_WIDTH), lambda i: (0, 0)),
                  pl.BlockSpec((1, GMLP_WIDTH), lambda i: (0, 0)),
                  pl.BlockSpec((GMLP_GROUPS, GMLP_CHUNK, GMLP_CHUNK), lambda i: (0, 0, 0)),
                  pl.BlockSpec((GMLP_CHUNK, LANES), lambda i: (0, 0))],
        out_specs=pl.BlockSpec((tb, GMLP_WIDTH), lambda i: (i, 0)),
        out_shape=jax.ShapeDtypeStruct((t, GMLP_WIDTH), BF16),
        compiler_params=_cparams(("arbitrary",)),
        name="gmlp",
    )(proj, proj, ln_g.reshape(1, -1), ln_b.reshape(1, -1), w_s, bs_t)


def _rope_lanes(t, cc, s1, s2):
    return t * cc + pltpu.roll(t, LANES - MLA_ROPE_DIM // 2, 1) * s1 + pltpu.roll(t, MLA_ROPE_DIM // 2, 1) * s2


def _mla_proj_kernel(cq_ref, ckv_ref, kr_ref, qn_ref, kvn_ref, wq_ref, wk_ref, wvt_ref, cc_ref, s1_ref, s2_ref,
                     q_o, kn_o, vt_o, kr_o):
    cq = cq_ref[...]
    qn = (cq * lax.rsqrt(jnp.mean(cq * cq, axis=-1, keepdims=True) + RMS_EPS) * qn_ref[...]).astype(BF16)
    ckv = ckv_ref[...]
    kvn = (ckv * lax.rsqrt(jnp.mean(ckv * ckv, axis=-1, keepdims=True) + RMS_EPS) * kvn_ref[...]).astype(BF16)
    cc, s1, s2 = cc_ref[...], s1_ref[...], s2_ref[...]
    scale = MLA_QK_DIM ** -0.5 * np.log2(np.e)
    for h in range(MLA_HEADS):
        q = _dot(qn, wq_ref[:, 2 * LANES * h:2 * LANES * (h + 1)]) * scale
        q_o[:, 2 * LANES * h:2 * LANES * h + LANES] = q[:, :LANES].astype(BF16)
        q_o[:, 2 * LANES * h + LANES:2 * LANES * (h + 1)] = _rope_lanes(q[:, LANES:], cc, s1, s2).astype(BF16)
    kn_o[...] = _dot(kvn, wk_ref[...]).astype(BF16)
    vt_o[...] = _dot_nt(wvt_ref[...], kvn).astype(BF16)
    kr_o[...] = _rope_lanes(kr_ref[...], cc, s1, s2).astype(BF16)


def _mla_proj(proj, positions, q_norm, kv_norm, w_uq, w_ukv, tm=512):
    t = proj.shape[0]
    h = MLA_HEADS
    half = MLA_ROPE_DIM // 2
    inv_freq = jnp.power(ROPE_THETA, -jnp.arange(0, MLA_ROPE_DIM, 2, dtype=F32) / MLA_ROPE_DIM)
    ang = positions.reshape(t).astype(F32)[:, None] * inv_freq
    cos, sin = jnp.cos(ang), jnp.sin(ang)
    z = jnp.zeros((t, half), F32)
    cc = jnp.concatenate([cos, cos, z, z], axis=1)
    s1 = jnp.concatenate([-sin, z, z, z], axis=1)
    s2 = jnp.concatenate([z, sin, z, z], axis=1)
    wq = w_uq.reshape(MLA_Q_RANK, h, MLA_QK_DIM)
    wq = jnp.pad(wq, ((0, 0), (0, 0), (0, 2 * LANES - MLA_QK_DIM))).reshape(MLA_Q_RANK, h * 2 * LANES).astype(BF16)
    wkv = w_ukv.reshape(MLA_KV_RANK, h, MLA_NOPE_DIM + MLA_V_DIM)
    wk = wkv[:, :, :MLA_NOPE_DIM].reshape(MLA_KV_RANK, h * MLA_NOPE_DIM).astype(BF16)
    wvt = wkv[:, :, MLA_NOPE_DIM:].reshape(MLA_KV_RANK, h * MLA_V_DIM).T.astype(BF16)
    full = lambda a: pl.BlockSpec(a.shape, lambda i: (0,) * a.ndim)
    tab = pl.BlockSpec((tm, LANES), lambda i: (i, 0))
    qn2, kvn2 = q_norm.reshape(1, -1), kv_norm.reshape(1, -1)
    return pl.pallas_call(
        _mla_proj_kernel,
        grid=(t // tm,),
        in_specs=[pl.BlockSpec((tm, MLA_Q_RANK), lambda i: (i, P_CQ // MLA_Q_RANK)),
                  pl.BlockSpec((tm, MLA_KV_RANK), lambda i: (i, P_CKV // MLA_KV_RANK)),
                  pl.BlockSpec((tm, LANES), lambda i: (i, P_KROPE // LANES)),
                  full(qn2), full(kvn2), full(wq), full(wk), full(wvt), tab, tab, tab],
        out_specs=[pl.BlockSpec((tm, h * 2 * LANES), lambda i: (i, 0)),
                   pl.BlockSpec((tm, h * MLA_NOPE_DIM), lambda i: (i, 0)),
                   pl.BlockSpec((h * MLA_V_DIM, tm), lambda i: (0, i)),
                   pl.BlockSpec((tm, LANES), lambda i: (i, 0))],
        out_shape=[jax.ShapeDtypeStruct((t, h * 2 * LANES), BF16),
                   jax.ShapeDtypeStruct((t, h * MLA_NOPE_DIM), BF16),
                   jax.ShapeDtypeStruct((h * MLA_V_DIM, t), BF16),
                   jax.ShapeDtypeStruct((t, LANES), BF16)],
        compiler_params=_cparams(("arbitrary",)),
        name="mla_proj",
    )(proj, proj, proj, qn2, kvn2, wq, wk, wvt, cc, s1, s2)


def _flash_kernel(qi_ref, kj_ref, q_ref, kn_ref, kr_ref, vt_ref, o_ref, m_sc, l_sc, acc_sc, st_sc, *, tq, tk):
    s = pl.program_id(1)
    qi, kj = qi_ref[s], kj_ref[s]

    @pl.when(kj == 0)
    def _():
        m_sc[...] = jnp.full_like(m_sc, -jnp.inf)
        l_sc[...] = jnp.zeros_like(l_sc)
        acc_sc[...] = jnp.zeros_like(acc_sc)

    qt = 2 * LANES
    kb = 2 * LANES

    def step(masked):
        k = jnp.concatenate([kn_ref[...], kr_ref[...]], axis=1)
        m_all = m_sc[...]
        m_news, alphas = [], []
        for c0 in range(0, tq, qt):
            st = _dot_nt(k, q_ref[c0:c0 + qt, :])
            if masked:
                key = kj * tk + lax.broadcasted_iota(jnp.int32, (tk, qt), 0)
                qry = qi * tq + c0 + lax.broadcasted_iota(jnp.int32, (tk, qt), 1)
                st = jnp.where(key <= qry, st, -jnp.inf)
            st_sc[:, c0:c0 + qt] = st
            m_prev = m_all[:, c0:c0 + qt]
            m_new = jnp.maximum(m_prev, jnp.max(st, axis=0, keepdims=True))
            m_news.append(m_new)
            alphas.append(jnp.exp2(m_prev - m_new))
        pvs, sums = [], []
        for t_i, c0 in enumerate(range(0, tq, qt)):
            pv, ps = None, None
            for r0 in range(0, tk, kb):
                p = jnp.exp2(st_sc[r0:r0 + kb, c0:c0 + qt] - m_news[t_i])
                part = jnp.sum(p, axis=0, keepdims=True)
                prod = _dot(vt_ref[:, r0:r0 + kb], p.astype(BF16))
                ps = part if ps is None else ps + part
                pv = prod if pv is None else pv + prod
            pvs.append(pv)
            sums.append(ps)
        alpha = jnp.concatenate(alphas, axis=1)
        m_sc[...] = jnp.concatenate(m_news, axis=1)
        l_sc[...] = alpha * l_sc[...] + jnp.concatenate(sums, axis=1)
        acc_sc[...] = alpha * acc_sc[...] + jnp.concatenate(pvs, axis=1)

    last_key_of_block = kj * tk + tk - 1
    on_diag = last_key_of_block > qi * tq

    @pl.when(jnp.logical_not(on_diag))
    def _():
        step(False)

    @pl.when(on_diag)
    def _():
        step(True)

    @pl.when(last_key_of_block >= qi * tq + tq - 1)
    def _():
        o_ref[...] = (acc_sc[...] / l_sc[...]).T.astype(o_ref.dtype)


def _flash(q, kn, vt, kr, tq=512, tk=512):
    t = q.shape[0]
    assert tq % tk == 0
    pairs = [(i, j) for i in range(t // tq) for j in range((i + 1) * tq // tk)]
    qi = jnp.asarray([pr[0] for pr in pairs], jnp.int32)
    kj = jnp.asarray([pr[1] for pr in pairs], jnp.int32)
    grid_spec = pltpu.PrefetchScalarGridSpec(
        num_scalar_prefetch=2,
        grid=(MLA_HEADS, len(pairs)),
        in_specs=[pl.BlockSpec((tq, 2 * LANES), lambda h, s, qi, kj: (qi[s], h)),
                  pl.BlockSpec((tk, MLA_NOPE_DIM), lambda h, s, qi, kj: (kj[s], h)),
                  pl.BlockSpec((tk, LANES), lambda h, s, qi, kj: (kj[s], 0)),
                  pl.BlockSpec((MLA_V_DIM, tk), lambda h, s, qi, kj: (h, kj[s]))],
        out_specs=pl.BlockSpec((tq, MLA_V_DIM), lambda h, s, qi, kj: (qi[s], h)),
        scratch_shapes=[pltpu.VMEM((1, tq), F32), pltpu.VMEM((1, tq), F32), pltpu.VMEM((MLA_V_DIM, tq), F32),
                        pltpu.VMEM((tk, tq), F32)],
    )
    return pl.pallas_call(
        functools.partial(_flash_kernel, tq=tq, tk=tk),
        grid_spec=grid_spec,
        out_shape=jax.ShapeDtypeStruct((t, MLA_HEADS * MLA_V_DIM), BF16),
        compiler_params=_cparams(("arbitrary", "arbitrary")),
        name="flash",
    )(qi, kj, q, kn, kr, vt)


def _router_kernel(h_ref, rwt_ref, bias_ref, exp_o, pos_o, gate_o, cnt_o, carry_sc):
    e = N_EXPERTS
    per = e // N_EXPERT_GROUPS
    h_hi, h_lo = _split_bf16(h_ref[...])
    w_hi, w_lo = _split_bf16(rwt_ref[...])
    logits = _dot_nt(w_hi, h_hi) + _dot_nt(w_hi, h_lo) + _dot_nt(w_lo, h_hi)
    scores = _sigmoid(logits)
    biased = scores + bias_ref[...][:, 0:1]
    tb = biased.shape[1]
    neg = -jnp.inf
    sub = lax.broadcasted_iota(jnp.int32, (per, tb), 0)
    grp_rows = []
    for g in range(N_EXPERT_GROUPS):
        blk = biased[g * per:(g + 1) * per, :]
        m1 = jnp.max(blk, axis=0, keepdims=True)
        first = jnp.min(jnp.where(blk == m1, sub, per), axis=0, keepdims=True)
        m2 = jnp.max(jnp.where(sub == first, neg, blk), axis=0, keepdims=True)
        grp_rows.append(m1 + m2)
    grp = jnp.concatenate(grp_rows, axis=0)
    gidx = lax.broadcasted_iota(jnp.int32, grp.shape, 0)
    grank = jnp.zeros(grp.shape, jnp.int32)
    for g in range(N_EXPERT_GROUPS):
        other = grp[g:g + 1, :]
        ahead = (other > grp) | ((other == grp) & (g < gidx))
        grank = grank + jnp.where(ahead, 1, 0)
    gsel = grank < TOPK_GROUPS
    masked = jnp.concatenate(
        [jnp.where(gsel[g:g + 1, :], biased[g * per:(g + 1) * per, :], neg) for g in range(N_EXPERT_GROUPS)], axis=0)
    eidx = lax.broadcasted_iota(jnp.int32, masked.shape, 0)
    rank = jnp.zeros(masked.shape, jnp.int32)
    for j in range(e):
        other = masked[j:j + 1, :]
        ahead = (other > masked) | ((other == masked) & (j < eidx))
        rank = rank + jnp.where(ahead, 1, 0)
    chosen = rank < TOP_K
    sel = jnp.where(chosen, scores, 0.0)
    gate = sel / jnp.sum(sel, axis=0, keepdims=True) * ROUTED_SCALE

    @pl.when(pl.program_id(0) == 0)
    def _():
        carry_sc[...] = jnp.zeros_like(carry_sc)

    chosen_f = jnp.where(chosen, 1.0, 0.0)
    earlier = (lax.broadcasted_iota(jnp.int32, (tb, tb), 0) < lax.broadcasted_iota(jnp.int32, (tb, tb), 1))
    carry = carry_sc[...]
    pos = _dot(chosen_f.astype(BF16), jnp.where(earlier, 1.0, 0.0).astype(BF16)) + carry[:, 0:1]
    carry_sc[...] = carry + jnp.sum(chosen_f, axis=1, keepdims=True)
    cnt_o[...] = carry_sc[...].astype(jnp.int32)

    eidx_f = eidx.astype(F32)
    pick = lambda hit, val: jnp.sum(jnp.where(hit, val, 0.0), axis=0, keepdims=True)
    hits = [rank == k for k in range(TOP_K)]
    exp_o[...] = jnp.concatenate([pick(hit, eidx_f) for hit in hits], axis=0).astype(jnp.int32)
    pos_o[...] = jnp.concatenate([pick(hit, pos) for hit in hits], axis=0).astype(jnp.int32)
    gate_o[...] = jnp.concatenate([pick(hit, gate) for hit in hits], axis=0)


def _router(h, router_w, router_bias, tb=512):
    t, d = h.shape
    bias = jnp.broadcast_to(router_bias.astype(F32)[:, None], (N_EXPERTS, LANES))
    per_tok = pl.BlockSpec((TOP_K, tb), lambda i: (0, i))
    return pl.pallas_call(
        _router_kernel,
        grid=(t // tb,),
        in_specs=[pl.BlockSpec((tb, d), lambda i: (i, 0)),
                  pl.BlockSpec((N_EXPERTS, d), lambda i: (0, 0)),
                  pl.BlockSpec((N_EXPERTS, LANES), lambda i: (0, 0))],
        out_specs=[per_tok, per_tok, per_tok, pl.BlockSpec((N_EXPERTS, LANES), lambda i: (0, 0))],
        out_shape=[jax.ShapeDtypeStruct((TOP_K, t), jnp.int32), jax.ShapeDtypeStruct((TOP_K, t), jnp.int32),
                   jax.ShapeDtypeStruct((TOP_K, t), F32), jax.ShapeDtypeStruct((N_EXPERTS, LANES), jnp.int32)],
        scratch_shapes=[pltpu.VMEM((N_EXPERTS, LANES), F32)],
        compiler_params=_cparams(("arbitrary",)),
        name="router",
    )(h, router_w.T, bias)


HALF_MASK = 0xFFFF0000


def _pack_bf16_pairs(x):
    n = x.shape[1] // 2
    lo = pltpu.bitcast(x[:, :n].astype(BF16).astype(F32), jnp.uint32) >> 16
    hi = pltpu.bitcast(x[:, n:].astype(BF16).astype(F32), jnp.uint32) & jnp.uint32(HALF_MASK)
    return lo | hi


def _unpack_bf16_pairs(w):
    return pltpu.bitcast(w << 16, F32), pltpu.bitcast(w & jnp.uint32(HALF_MASK), F32)


def _dispatch_plan(exp_r, pos_r, cnt, bm, nb_max):
    ids = jnp.arange(N_EXPERTS, dtype=jnp.int32)
    nb = (cnt + bm - 1) // bm
    bend = jnp.sum(jnp.where(ids[None, :] <= ids[:, None], nb[None, :], 0), axis=1)
    total = bend[-1]
    slot_start = (bend - nb) * bm
    start_of = jnp.sum(jnp.where(exp_r[:, :, None] == ids, slot_start, 0), axis=-1)
    slot = (start_of + pos_r).T.reshape(-1)
    bidx = jnp.arange(nb_max, dtype=jnp.int32)
    blocks = jnp.minimum(bidx, total - 1)
    blk_exp = jnp.sum(jnp.where(bend[None, :] <= blocks[:, None], 1, 0), axis=1)
    first = jnp.where((bidx == 0) | (blk_exp != jnp.roll(blk_exp, 1)), 1, 0)
    turn = jnp.sum(jnp.where(bidx[None, :] <= bidx[:, None], first[None, :], 0), axis=1) - 1
    later_first = jnp.where((bidx[None, :] > bidx[:, None]) & (first[None, :] == 1), bidx[None, :], nb_max)
    nxt_blk = jnp.min(later_first, axis=1)
    nxt_exp = jnp.sum(jnp.where(bidx[None, :] == nxt_blk[:, None], blk_exp[None, :] + 1, 0), axis=1) - 1
    meta = jnp.concatenate([blk_exp, first, turn % 2, nxt_exp]).astype(jnp.int32)
    lo = jnp.concatenate([slot_start + cnt, (total * bm).reshape(1)])
    hi = jnp.concatenate([slot_start + nb * bm, jnp.full((1,), nb_max * bm, jnp.int32)])
    gaps = jnp.stack([lo, hi], axis=1).reshape(-1).astype(jnp.int32)
    return slot.astype(jnp.int32), meta, total.reshape(1).astype(jnp.int32), gaps


ZERO_ROWS = 256


def _dispatch_kernel(slot_ref, gaps_ref, h_ref, xs_hbm, zero_sc, sem, zsem, *, tb, steps, n_gaps):
    i = pl.program_id(0)

    def fill_gaps(act):
        def gap(r, carry):
            lo, hi = gaps_ref[2 * r], gaps_ref[2 * r + 1]
            lo_tile = jnp.minimum((lo + SUBLANES - 1) // SUBLANES * SUBLANES, hi)

            def single(row, carry2):
                act(pltpu.make_async_copy(zero_sc.at[pl.ds(0, 1), :], xs_hbm.at[pl.ds(row, 1), :], zsem))
                return carry2

            lax.fori_loop(lo, lo_tile, single, 0)
            whole = (hi - lo_tile) // ZERO_ROWS

            def chunk(c, carry2):
                start = pl.multiple_of(lo_tile + c * ZERO_ROWS, SUBLANES)
                act(pltpu.make_async_copy(zero_sc, xs_hbm.at[pl.ds(start, ZERO_ROWS), :], zsem))
                return carry2

            lax.fori_loop(0, whole, chunk, 0)
            off = lo_tile + whole * ZERO_ROWS
            rem = hi - off
            size = ZERO_ROWS // 2
            while size >= SUBLANES:
                take = (rem & size) != 0

                @pl.when(take)
                def _(off=off, size=size):
                    start = pl.multiple_of(off, SUBLANES)
                    act(pltpu.make_async_copy(zero_sc.at[pl.ds(0, size), :], xs_hbm.at[pl.ds(start, size), :], zsem))

                off = off + jnp.where(take, size, 0)
                size //= 2
            return carry

        lax.fori_loop(0, n_gaps, gap, 0)

    @pl.when(i == 0)
    def _():
        zero_sc[...] = jnp.zeros_like(zero_sc)
        fill_gaps(lambda c: c.start())

    def body(t, carry):
        for k in range(TOP_K):
            s = slot_ref[t * TOP_K + k]
            pltpu.make_async_copy(h_ref.at[pl.ds(t, 1), :], xs_hbm.at[pl.ds(s, 1), :], sem).start()
        return carry

    lax.fori_loop(0, tb, body, 0)
    rows = tb * TOP_K
    pltpu.make_async_copy(xs_hbm.at[pl.ds(0, rows), :], xs_hbm.at[pl.ds(0, rows), :], sem).wait()

    @pl.when(i == steps - 1)
    def _():
        fill_gaps(lambda c: c.wait())


def _dispatch(h_pk, slot, gaps, n_slots, tb):
    t, w = h_pk.shape
    steps = t // tb
    return pl.pallas_call(
        functools.partial(_dispatch_kernel, tb=tb, steps=steps, n_gaps=gaps.shape[0] // 2),
        grid=(steps,),
        in_specs=[pl.BlockSpec((tb * TOP_K,), lambda i: (i,), memory_space=pltpu.SMEM),
                  pl.BlockSpec(memory_space=pltpu.SMEM),
                  pl.BlockSpec((tb, w), lambda i: (i, 0))],
        out_specs=pl.BlockSpec(memory_space=pl.ANY),
        out_shape=jax.ShapeDtypeStruct((n_slots, w), jnp.uint32),
        scratch_shapes=[pltpu.VMEM((ZERO_ROWS, w), jnp.uint32), pltpu.SemaphoreType.DMA(()),
                        pltpu.SemaphoreType.DMA(())],
        compiler_params=pltpu.CompilerParams(dimension_semantics=("arbitrary",), vmem_limit_bytes=VMEM_LIMIT,
                                             has_side_effects=True, disable_bounds_checks=True),
        name="moe_dispatch",
    )(slot, gaps, h_pk)


def _expert_meta(meta_ref, nb):
    b = pl.program_id(0)
    return meta_ref[b], meta_ref[nb + b] == 1, meta_ref[2 * nb + b], meta_ref[3 * nb + b]


def _expert_weight_turn(meta_ref, nb, layer, hbm_refs, stage_ref, sem, cast_to):
    e, first, par, nxt = _expert_meta(meta_ref, nb)

    def copies(expert, half):
        return [pltpu.make_async_copy(w.at[layer, expert], stage_ref.at[half, n], sem.at[half, n])
                for n, w in enumerate(hbm_refs)]

    @pl.when(pl.program_id(0) == 0)
    def _():
        for c in copies(e, par):
            c.start()

    @pl.when(first)
    def _():
        for c in copies(e, par):
            c.wait()
        cast_to(stage_ref.at[par])

        @pl.when(nxt >= 0)
        def _():
            for c in copies(nxt, 1 - par):
                c.start()


def _expert_up_kernel(meta_ref, tot_ref, xs_ref, w1_hbm, w3_hbm, act_ref, stage_sc, w13_sc, sem, *, nb, layer):
    def cast_to(staged):
        w13_sc[:, :D_EXPERT] = staged[0].astype(BF16)
        w13_sc[:, D_EXPERT:] = staged[1].astype(BF16)

    _expert_weight_turn(meta_ref, nb, layer, [w1_hbm, w3_hbm], stage_sc, sem, cast_to)

    @pl.when(pl.program_id(0) < tot_ref[0])
    def _():
        lo, hi = _unpack_bf16_pairs(xs_ref[...])
        x = jnp.concatenate([lo.astype(BF16), hi.astype(BF16)], axis=1)
        hgu = _dot(x, w13_sc[...])
        hg, hu = hgu[:, :D_EXPERT], hgu[:, D_EXPERT:]
        act_ref[...] = (hg * _sigmoid(hg) * hu).astype(BF16)

    @pl.when(pl.program_id(0) >= tot_ref[0])
    def _():
        act_ref[...] = jnp.zeros_like(act_ref)


def _expert_down_kernel(meta_ref, tot_ref, act_ref, w2_hbm, ys_ref, stage_sc, w2_sc, sem, *, nb, layer):
    def cast_to(staged):
        w2_sc[...] = staged[0].astype(BF16)

    _expert_weight_turn(meta_ref, nb, layer, [w2_hbm], stage_sc, sem, cast_to)

    @pl.when(pl.program_id(0) < tot_ref[0])
    def _():
        ys_ref[...] = _pack_bf16_pairs(_dot(act_ref[...], w2_sc[...]))

    @pl.when(pl.program_id(0) >= tot_ref[0])
    def _():
        ys_ref[...] = jnp.zeros_like(ys_ref)


def _experts(xs, meta, total, w1, w3, w2, layer, bm):
    n_slots, w = xs.shape
    d = 2 * w
    nb = n_slots // bm
    used = lambda b, meta, tot: (jnp.minimum(b, tot[0] - 1), 0)
    every = lambda b, meta, tot: (b, 0)
    hbm = pl.BlockSpec(memory_space=pl.ANY)
    act = pl.pallas_call(
        functools.partial(_expert_up_kernel, nb=nb, layer=layer),
        grid_spec=pltpu.PrefetchScalarGridSpec(
            num_scalar_prefetch=2,
            grid=(nb,),
            in_specs=[pl.BlockSpec((bm, w), used), hbm, hbm],
            out_specs=pl.BlockSpec((bm, D_EXPERT), every),
            scratch_shapes=[pltpu.VMEM((2, 2, d, D_EXPERT), F32), pltpu.VMEM((d, 2 * D_EXPERT), BF16),
                            pltpu.SemaphoreType.DMA((2, 2))]),
        out_shape=jax.ShapeDtypeStruct((n_slots, D_EXPERT), BF16),
        compiler_params=_cparams(("arbitrary",)),
        name="moe_up",
    )(meta, total, xs, w1, w3)
    return pl.pallas_call(
        functools.partial(_expert_down_kernel, nb=nb, layer=layer),
        grid_spec=pltpu.PrefetchScalarGridSpec(
            num_scalar_prefetch=2,
            grid=(nb,),
            in_specs=[pl.BlockSpec((bm, D_EXPERT), used), hbm],
            out_specs=pl.BlockSpec((bm, w), every),
            scratch_shapes=[pltpu.VMEM((2, 1, D_EXPERT, d), F32), pltpu.VMEM((D_EXPERT, d), BF16),
                            pltpu.SemaphoreType.DMA((2, 1))]),
        out_shape=jax.ShapeDtypeStruct((n_slots, w), jnp.uint32),
        compiler_params=_cparams(("arbitrary",)),
        name="moe_down",
    )(meta, total, act, w2)


def _combine_kernel(slot_ref, slot_next_ref, gate_ref, h_ref, hb_ref, sw13_ref, sw2_ref, g_ref, b_ref, ys_hbm,
                    of_ref, ob_ref, rows_a, rows_b, shared_sc, sem, *, tb, steps):
    i = pl.program_id(0)
    grp = 2 * SUBLANES

    def issue(table_ref, buf, buf_sem, t0, first=0, count=grp):
        for j in range(first, first + count):
            for k in range(TOP_K):
                s = table_ref[(t0 + j) * TOP_K + k]
                pltpu.make_async_copy(ys_hbm.at[pl.ds(s, 1), :], buf.at[k, pl.ds(t0 + j, 1), :], buf_sem).start()

    def wait_block(buf, buf_sem):
        pltpu.make_async_copy(buf, buf, buf_sem).wait()

    @pl.when(i == 0)
    def _():
        def first(g, carry):
            issue(slot_ref, rows_a, sem.at[0], g * grp)
            return carry

        lax.fori_loop(0, tb // grp, first, 0)

    hgu = _dot(hb_ref[...], sw13_ref[...])
    hg, hu = hgu[:, :D_EXPERT], hgu[:, D_EXPERT:]
    shared_sc[...] = _dot((hg * _sigmoid(hg) * hu).astype(BF16), sw2_ref[...])

    def run(cur, cur_sem, nxt, nxt_sem):
        wait_block(cur, cur_sem)

        def group(g, carry):
            r0 = pl.multiple_of(g * grp, grp)
            rows = pl.ds(r0, grp)
            gate = gate_ref[rows, :]
            acc_lo = jnp.zeros((grp, cur.shape[-1]), F32)
            acc_hi = jnp.zeros((grp, cur.shape[-1]), F32)
            per_k = grp // TOP_K
            for k in range(TOP_K):
                issue(slot_next_ref, nxt, nxt_sem, r0, k * per_k, per_k)
                lo, hi = _unpack_bf16_pairs(cur[k, rows, :])
                gk = gate[:, k:k + 1]
                acc_lo = acc_lo + gk * lo
                acc_hi = acc_hi + gk * hi
            t = ALPHA * h_ref[rows, :] + shared_sc[rows, :] + jnp.concatenate([acc_lo, acc_hi], axis=1)
            mu = jnp.mean(t, axis=-1, keepdims=True)
            c = t - mu
            var = jnp.mean(c * c, axis=-1, keepdims=True)
            out = c * lax.rsqrt(var + LN_EPS) * g_ref[...] + b_ref[...]
            of_ref[rows, :] = out
            ob_ref[rows, :] = out.astype(BF16)
            return carry

        lax.fori_loop(0, tb // grp, group, 0)

        @pl.when(i == steps - 1)
        def _():
            wait_block(nxt, nxt_sem)

    even = lax.rem(i, 2) == 0

    @pl.when(even)
    def _():
        run(rows_a, sem.at[0], rows_b, sem.at[1])

    @pl.when(jnp.logical_not(even))
    def _():
        run(rows_b, sem.at[1], rows_a, sem.at[0])


def _combine(ys, slot, gate_tk, hf, hb, sw1, sw3, sw2, g, b, tb):
    t, d = hf.shape
    steps = t // tb
    sw13 = jnp.concatenate([sw1, sw3], axis=1).astype(BF16)
    row = pl.BlockSpec((tb, d), lambda i: (i, 0))
    full = lambda a: pl.BlockSpec(a.shape, lambda i: (0,) * a.ndim)
    g2, b2, sw2b = g.reshape(1, d), b.reshape(1, d), sw2.astype(BF16)
    return pl.pallas_call(
        functools.partial(_combine_kernel, tb=tb, steps=steps),
        grid=(steps,),
        in_specs=[pl.BlockSpec((tb * TOP_K,), lambda i: (i,), memory_space=pltpu.SMEM),
                  pl.BlockSpec((tb * TOP_K,), lambda i: (jnp.minimum(i + 1, steps - 1),), memory_space=pltpu.SMEM),
                  pl.BlockSpec((tb, TOP_K), lambda i: (i, 0)),
                  row, row, full(sw13), full(sw2b), full(g2), full(b2),
                  pl.BlockSpec(memory_space=pl.ANY)],
        out_specs=[row, row],
        out_shape=[jax.ShapeDtypeStruct((t, d), F32), jax.ShapeDtypeStruct((t, d), BF16)],
        scratch_shapes=[pltpu.VMEM((TOP_K, tb, ys.shape[1]), jnp.uint32),
                        pltpu.VMEM((TOP_K, tb, ys.shape[1]), jnp.uint32),
                        pltpu.VMEM((tb, d), F32), pltpu.SemaphoreType.DMA((2,))],
        compiler_params=pltpu.CompilerParams(dimension_semantics=("arbitrary",), vmem_limit_bytes=VMEM_LIMIT,
                                             disable_bounds_checks=True),
        name="moe_combine",
    )(slot, slot, gate_tk, hf, hb, sw13, sw2b, g2, b2, ys)


def _moe_ffn(hf, hb, h_pk, router_w, router_bias, w1, w3, w2, layer, sw1, sw3, sw2, ln_g, ln_b, bm):
    t = hf.shape[0]
    exp_r, pos_r, gate_r, cnt = _router(hf, router_w, router_bias, tb=min(512, t))
    nb_max = t * TOP_K // bm + N_EXPERTS
    slot, meta, total, gaps = _dispatch_plan(exp_r, pos_r, cnt[:, 0], bm, nb_max)
    xs = _dispatch(h_pk, slot, gaps, nb_max * bm, tb=min(256, t // TOP_K))
    ys = _experts(xs, meta, total, w1, w3, w2, layer, bm)
    return _combine(ys, slot, gate_r.T, hf, hb, sw1, sw3, sw2, ln_g, ln_b, tb=min(128, t))


def _w_in_layout_kernel(w_ref, o_ref, *, segments):
    o_ref[...] = jnp.zeros_like(o_ref)
    for src, dst, width in segments:
        o_ref[:, dst:dst + width] = w_ref[0, :, src:src + width].astype(BF16)


def _prep_w_in(w_stack, index, has_vres, tr=256):
    _, d, cols = w_stack.shape
    sizes = [RWKV_WIDTH, RWKV_WIDTH, RWKV_WIDTH, DECAY_LORA, AAA_LORA, GATE_LORA,
             GMLP_WIDTH, GMLP_WIDTH, MLA_Q_RANK, MLA_KV_RANK, MLA_ROPE_DIM]
    dsts = [P_R, P_K, P_V, P_WLO, P_ALO, P_GLO, P_GU, P_GV, P_CQ, P_CKV, P_KROPE]
    if has_vres:
        sizes.append(MV_LORA)
        dsts.append(P_VLO)
    srcs = np.concatenate([[0], np.cumsum(sizes)])[:-1]
    segments = tuple((int(s), int(t), int(n)) for s, t, n in zip(srcs, dsts, sizes))
    return pl.pallas_call(
        functools.partial(_w_in_layout_kernel, segments=segments),
        grid=(d // tr,),
        in_specs=[pl.BlockSpec((1, tr, cols), lambda i: (index, i, 0))],
        out_specs=pl.BlockSpec((tr, P_COLS), lambda i: (i, 0)),
        out_shape=jax.ShapeDtypeStruct((d, P_COLS), BF16),
        compiler_params=_cparams(("arbitrary",)),
        name="w_in_layout",
    )(w_stack)


def _prep_mu(mu, mu_vres):
    out = jnp.zeros((1, P_COLS), F32)
    offs = np.concatenate([[0], np.cumsum([RWKV_WIDTH] * 3 + [DECAY_LORA, AAA_LORA, GATE_LORA])])
    for dst, i in zip([P_R, P_K, P_V, P_WLO, P_ALO, P_GLO], range(6)):
        out = lax.dynamic_update_slice(out, mu[offs[i]:offs[i + 1]].reshape(1, -1), (0, dst))
    if mu_vres is not None:
        out = lax.dynamic_update_slice(out, mu_vres.reshape(1, -1), (0, P_VLO))
    return out


def kernel(x, positions, w_in_first, w_in_rest, rwkv_mu, rwkv_mu_vres, rwkv_w0, rwkv_w2, rwkv_a0, rwkv_a2, rwkv_v0, rwkv_v2, rwkv_g2, rwkv_k_k, rwkv_k_a, rwkv_r_k, rwkv_ln_g, rwkv_ln_b, gmlp_ln_g, gmlp_ln_b, gmlp_w_s, gmlp_b_s, mla_q_norm, mla_kv_norm, mla_w_uq, mla_w_ukv, w_out, ln1_g, ln1_b, router_w, router_bias, exp_w1, exp_w3, exp_w2, shared_w1, shared_w3, shared_w2, ln2_g, ln2_b):
    b, s, d = x.shape
    t = b * s
    p = dict(rwkv_w0=rwkv_w0, rwkv_w2=rwkv_w2, rwkv_a0=rwkv_a0, rwkv_a2=rwkv_a2, rwkv_v0=rwkv_v0, rwkv_v2=rwkv_v2,
             rwkv_g2=rwkv_g2, rwkv_k_k=rwkv_k_k, rwkv_k_a=rwkv_k_a, rwkv_r_k=rwkv_r_k)
    xf = x.reshape(t, d)
    xb = xf.astype(BF16)
    v_first = None
    for l in range(DEPTH):
        has_vres = l > 0
        w_in = _prep_w_in(w_in_first[None], 0, False) if l == 0 else _prep_w_in(w_in_rest, l - 1, True)
        mu_p = _prep_mu(rwkv_mu[l], rwkv_mu_vres[l - 1] if has_vres else None)
        proj = _matmul(xb, w_in, F32, tm=min(1024, t), tn=1280)
        y_a, v_first = _rwkv_group(proj, mu_p, p, l, v_first, rwkv_ln_g[l], rwkv_ln_b[l])
        y_b = _gmlp(proj, gmlp_ln_g[l], gmlp_ln_b[l], gmlp_w_s[l], gmlp_b_s[l], tb=min(256, t))
        q, kn, vt, kr = _mla_proj(proj, positions, mla_q_norm[l], mla_kv_norm[l], mla_w_uq[l], mla_w_ukv[l],
                                  tm=min(512, t))
        y_c = _flash(q, kn, vt, kr, tq=min(1024, t), tk=min(1024, t))
        mix = _out_proj(y_a, y_b, y_c, w_out, l, tm=min(512, t), tn=1024)
        hf, hb, h_pk = _res_ln(xf, mix, ln1_g[l], ln1_b[l], tm=min(256, t))
        xf, xb = _moe_ffn(hf, hb, h_pk, router_w[l], router_bias[l],
                          exp_w1, exp_w3, exp_w2, l,
                          shared_w1[l], shared_w3[l], shared_w2[l], ln2_g[l], ln2_b[l], bm=MOE_ROW_BLOCK)
    return xf.reshape(b, s, d)
```

```python
import functools

import jax
import jax.numpy as jnp
import numpy as np
from jax import lax
from jax.experimental import pallas as pl
from jax.experimental.pallas import tpu as pltpu

F32 = jnp.float32
BF16 = jnp.bfloat16

D_MODEL = 4096
DEPTH = 2
RWKV_HEAD_DIM = 64
RWKV_WIDTH = 3 * D_MODEL // 8
DECAY_LORA = 128
AAA_LORA = 128
MV_LORA = 96
GATE_LORA = 480
GN_EPS = 64e-5
GMLP_WIDTH = D_MODEL // 4
GMLP_GROUP_DIM = 128
GMLP_GROUPS = GMLP_WIDTH // GMLP_GROUP_DIM
GMLP_CHUNK = 128
MLA_V_DIM = 128
MLA_WIDTH = D_MODEL - RWKV_WIDTH - GMLP_WIDTH
MLA_HEADS = MLA_WIDTH // MLA_V_DIM
MLA_NOPE_DIM = 128
MLA_ROPE_DIM = 64
MLA_QK_DIM = MLA_NOPE_DIM + MLA_ROPE_DIM
MLA_Q_RANK = 768
MLA_KV_RANK = 512
ROPE_THETA = 10000.0
N_EXPERTS = 64
TOP_K = 8
N_EXPERT_GROUPS = 8
TOPK_GROUPS = 4
D_EXPERT = 384
ROUTED_SCALE = 2.5
ALPHA = (2 * DEPTH) ** 0.25
LN_EPS = 1e-5
RMS_EPS = 1e-6

LANES = 128
SUBLANES = 8
VMEM_LIMIT = 56 * 1024 * 1024

P_GU, P_GV = 0, 1024
P_R, P_K, P_V = 2048, 3584, 5120
P_WLO, P_ALO = 6656, 6784
P_CQ, P_CKV = 6912, 7680
P_GLO, P_KROPE, P_VLO = 8192, 8704, 8832
P_COLS = 8960
GLO_PAD = 512
RW_CH = 512
RW_CHUNK = 64
RW_SUB = 4
MOE_ROW_BLOCK = 512


def _cparams(sem):
    return pltpu.CompilerParams(dimension_semantics=sem, vmem_limit_bytes=VMEM_LIMIT)


def _sigmoid(x):
    return 1.0 / (1.0 + jnp.exp(-x))


def _dot(a, b):
    return jnp.dot(a, b, preferred_element_type=F32)


def _dot_nt(a, b):
    return lax.dot_general(a, b, (((1,), (1,)), ((), ())), preferred_element_type=F32)


def _dot_tn(a, b):
    return lax.dot_general(a, b, (((0,), (0,)), ((), ())), preferred_element_type=F32)


def _split_bf16(x):
    hi = x.astype(BF16)
    lo = (x - hi.astype(F32)).astype(BF16)
    return hi, lo


def _mm_kernel(x_ref, w_ref, o_ref):
    o_ref[...] = _dot(x_ref[...], w_ref[...]).astype(o_ref.dtype)


def _matmul(x, w, out_dtype, tm, tn):
    m, k = x.shape
    n = w.shape[1]
    assert m % tm == 0 and n % tn == 0
    return pl.pallas_call(
        _mm_kernel,
        grid=(n // tn, m // tm),
        in_specs=[pl.BlockSpec((tm, k), lambda j, i: (i, 0)),
                  pl.BlockSpec((k, tn), lambda j, i: (0, j))],
        out_specs=pl.BlockSpec((tm, tn), lambda j, i: (i, j)),
        out_shape=jax.ShapeDtypeStruct((m, n), out_dtype),
        compiler_params=_cparams(("arbitrary", "arbitrary")),
        name="matmul",
    )(x, w)


def _out_proj_kernel(ya_ref, yb_ref, yc_ref, w_ref, o_ref, wb_sc):
    @pl.when(pl.program_id(1) == 0)
    def _():
        wb_sc[...] = w_ref[0].astype(BF16)

    ka, kb = ya_ref.shape[1], yb_ref.shape[1]
    o_ref[...] = (_dot(ya_ref[...], wb_sc[:ka, :]) + _dot(yb_ref[...], wb_sc[ka:ka + kb, :])
                  + _dot(yc_ref[...], wb_sc[ka + kb:, :]))


def _out_proj(y_a, y_b, y_c, w_out, layer, tm, tn):
    m = y_a.shape[0]
    k, n = w_out.shape[1:]
    ys = [y_a, y_b, y_c]
    return pl.pallas_call(
        _out_proj_kernel,
        grid=(n // tn, m // tm),
        in_specs=[pl.BlockSpec((tm, y.shape[1]), lambda j, i: (i, 0)) for y in ys]
        + [pl.BlockSpec((1, k, tn), lambda j, i: (layer, 0, j))],
        out_specs=pl.BlockSpec((tm, tn), lambda j, i: (i, j)),
        out_shape=jax.ShapeDtypeStruct((m, n), F32),
        scratch_shapes=[pltpu.VMEM((k, tn), BF16)],
        compiler_params=_cparams(("arbitrary", "arbitrary")),
        name="out_proj",
    )(*ys, w_out)


def _res_ln_kernel(res_ref, y_ref, g_ref, b_ref, of_ref, ob_ref, opk_ref):
    t = ALPHA * res_ref[...] + y_ref[...]
    mu = jnp.mean(t, axis=-1, keepdims=True)
    c = t - mu
    var = jnp.mean(c * c, axis=-1, keepdims=True)
    out = c * lax.rsqrt(var + LN_EPS) * g_ref[...] + b_ref[...]
    of_ref[...] = out
    ob_ref[...] = out.astype(BF16)
    opk_ref[...] = _pack_bf16_pairs(out)


def _res_ln(res, y, g, b, tm=256):
    m, d = res.shape
    row = pl.BlockSpec((tm, d), lambda i: (i, 0))
    half = pl.BlockSpec((tm, d // 2), lambda i: (i, 0))
    vec = pl.BlockSpec((1, d), lambda i: (0, 0))
    return pl.pallas_call(
        _res_ln_kernel,
        grid=(m // tm,),
        in_specs=[row, row, vec, vec],
        out_specs=[row, row, half],
        out_shape=[jax.ShapeDtypeStruct((m, d), F32), jax.ShapeDtypeStruct((m, d), BF16),
                   jax.ShapeDtypeStruct((m, d // 2), jnp.uint32)],
        compiler_params=_cparams(("arbitrary",)),
        name="res_ln",
    )(res, y, g.reshape(1, d), b.reshape(1, d))


def _shift_mix(cur, prev8, mu, is_first):
    prev_row = jnp.where(is_first, 0.0, prev8[SUBLANES - 1:SUBLANES, :])
    rolled = pltpu.roll(cur, 1, 0)
    row = lax.broadcasted_iota(jnp.int32, cur.shape, 0)
    shifted = jnp.where(row == 0, prev_row, rolled)
    return cur + (shifted - cur) * mu


def _head_block_ones(width):
    r = lax.broadcasted_iota(jnp.int32, (width, width), 0) // RWKV_HEAD_DIM
    c = lax.broadcasted_iota(jnp.int32, (width, width), 1) // RWKV_HEAD_DIM
    return jnp.where(r == c, 1.0, 0.0).astype(BF16)


def _head_sum(x, ones_bd):
    w = ones_bd.shape[0]
    outs = []
    for c in range(x.shape[1] // w):
        hi, lo = _split_bf16(x[:, c * w:(c + 1) * w])
        outs.append(_dot(hi, ones_bd) + _dot(lo, ones_bd))
    return jnp.concatenate(outs, axis=1)


def _rwkv_prep_body(first, r_ref, k_ref, v_ref, rp_ref, kp_ref, vp_ref,
                    wlo_ref, alo_ref, glo_ref, wlop_ref, alop_ref, glop_ref,
                    mur_ref, muk_ref, muv_ref, muw_ref, mua_ref, mug_ref,
                    w2_ref, a2_ref, g2_ref, w0_ref, a0_ref, kk_ref, ka_ref, rk_ref, vres):
    r = _shift_mix(r_ref[...], rp_ref[...], mur_ref[...], first)
    k = _shift_mix(k_ref[...], kp_ref[...], muk_ref[...], first)
    v = _shift_mix(v_ref[...], vp_ref[...], muv_ref[...], first)
    w_lo = _shift_mix(wlo_ref[...], wlop_ref[...], muw_ref[...], first)
    a_lo = _shift_mix(alo_ref[...], alop_ref[...], mua_ref[...], first)
    g_lo = _shift_mix(glo_ref[...], glop_ref[...], mug_ref[...], first)

    z = w0_ref[...] + _dot(jnp.tanh(w_lo).astype(BF16), w2_ref[...])
    nz = -z
    softplus = jnp.maximum(nz, 0.0) + jnp.log(1.0 + jnp.exp(-jnp.abs(nz)))
    log_w = -softplus - 0.5
    ld = -jnp.exp(log_w)
    a = _sigmoid(a0_ref[...] + _dot(a_lo.astype(BF16), a2_ref[...]))
    g = _dot(_sigmoid(g_lo).astype(BF16), g2_ref[...])
    if vres is not None:
        vlo_ref, vlop_ref, muvl_ref, v2_ref, v0_ref, vf_ref = vres
        v_lo = _shift_mix(vlo_ref[...], vlop_ref[...], muvl_ref[...], first)
        mix = _sigmoid(v0_ref[...] + _dot(v_lo.astype(BF16), v2_ref[...]))
        v = v + (vf_ref[...] - v) * mix

    ones_bd = _head_block_ones(2 * LANES)
    kk = k * kk_ref[...]
    ss = _head_sum(kk * kk, ones_bd)
    kn = kk * lax.rsqrt(jnp.maximum(ss, 1e-24))
    k_mod = k * (1.0 + (a - 1.0) * ka_ref[...])
    bonus = _head_sum(r * k_mod * rk_ref[...], ones_bd) * v
    return r, ld, k_mod, v, kn, kn * a, g, bonus


def _make_rwkv_kernel(has_vres):
    n_common = 26

    def kern(*refs):
        common = refs[:n_common]
        rest = refs[n_common:]
        if has_vres:
            vres, rest = rest[:6], rest[6:]
        else:
            vres = None
        lng_ref, lnb_ref = rest[0], rest[1]
        outs, state_ref = rest[2:-1], rest[-1]
        first = pl.program_id(1) == 0
        r, ld, k, v, kn, b, g, bonus = _rwkv_prep_body(first, *common, vres)
        y = _rwkv_scan_block(r, ld, k, v, kn, b, state_ref)
        outs[0][...] = _rwkv_finish(y, g, bonus, lng_ref[...], lnb_ref[...]).astype(outs[0].dtype)
        if not has_vres:
            outs[1][...] = v

    return kern


def _rwkv_group(proj, mu_p, p, l, v_first, ln_g, ln_b):
    t = proj.shape[0]
    tb = RW_CHUNK * RW_SUB
    has_vres = v_first is not None
    nj = RWKV_WIDTH // RW_CH
    pb = tb // SUBLANES

    def cur(width, col0):
        return pl.BlockSpec((tb, width), lambda j, i: (i, col0 // width))

    def cur_j(col0):
        return pl.BlockSpec((tb, RW_CH), lambda j, i: (i, col0 // RW_CH + j))

    def prev(width, col0):
        return pl.BlockSpec((SUBLANES, width), lambda j, i: (jnp.maximum(i * pb - 1, 0), col0 // width))

    def prev_j(col0):
        return pl.BlockSpec((SUBLANES, RW_CH), lambda j, i: (jnp.maximum(i * pb - 1, 0), col0 // RW_CH + j))

    def vec(width, col0):
        return pl.BlockSpec((1, width), lambda j, i: (0, col0 // width))

    def vec_j(col0=0):
        return pl.BlockSpec((1, RW_CH), lambda j, i: (0, col0 // RW_CH + j))

    def lora(rank):
        return pl.BlockSpec((rank, RW_CH), lambda j, i: (0, j))

    row = lambda a: a.reshape(1, -1)
    g2 = jnp.pad(p["rwkv_g2"][l], ((0, GLO_PAD - GATE_LORA), (0, 0))).astype(BF16)
    args = [proj, proj, proj, proj, proj, proj,
            proj, proj, proj, proj, proj, proj,
            mu_p, mu_p, mu_p, mu_p, mu_p, mu_p,
            p["rwkv_w2"][l].astype(BF16), p["rwkv_a2"][l].astype(BF16), g2,
            row(p["rwkv_w0"][l]), row(p["rwkv_a0"][l]), row(p["rwkv_k_k"][l]), row(p["rwkv_k_a"][l]),
            row(p["rwkv_r_k"][l])]
    in_specs = [cur_j(P_R), cur_j(P_K), cur_j(P_V), prev_j(P_R), prev_j(P_K), prev_j(P_V),
                cur(LANES, P_WLO), cur(LANES, P_ALO), cur(GLO_PAD, P_GLO),
                prev(LANES, P_WLO), prev(LANES, P_ALO), prev(GLO_PAD, P_GLO),
                vec_j(P_R), vec_j(P_K), vec_j(P_V), vec(LANES, P_WLO), vec(LANES, P_ALO), vec(GLO_PAD, P_GLO),
                lora(DECAY_LORA), lora(AAA_LORA), lora(GLO_PAD),
                vec_j(), vec_j(), vec_j(), vec_j(), vec_j()]
    if has_vres:
        v2 = jnp.pad(p["rwkv_v2"][l - 1], ((0, LANES - MV_LORA), (0, 0))).astype(BF16)
        args += [proj, proj, mu_p, v2, row(p["rwkv_v0"][l - 1]), v_first]
        in_specs += [cur(LANES, P_VLO), prev(LANES, P_VLO), vec(LANES, P_VLO), lora(LANES), vec_j(),
                     pl.BlockSpec((tb, RW_CH), lambda j, i: (i, j))]
    args += [row(ln_g), row(ln_b)]
    in_specs += [vec_j(), vec_j()]
    out_spec = pl.BlockSpec((tb, RW_CH), lambda j, i: (i, j))
    out_specs = [out_spec]
    out_shape = [jax.ShapeDtypeStruct((t, RWKV_WIDTH), BF16)]
    if not has_vres:
        out_specs.append(out_spec)
        out_shape.append(jax.ShapeDtypeStruct((t, RWKV_WIDTH), F32))
    outs = pl.pallas_call(
        _make_rwkv_kernel(has_vres),
        grid=(nj, t // tb),
        in_specs=in_specs,
        out_specs=out_specs,
        out_shape=out_shape,
        scratch_shapes=[pltpu.VMEM((RW_CH // RWKV_HEAD_DIM, RWKV_HEAD_DIM, RWKV_HEAD_DIM), F32)],
        compiler_params=_cparams(("arbitrary", "arbitrary")),
        name="rwkv",
    )(*args)
    return outs[0], (v_first if has_vres else outs[1])


def _unit_lower_inverse(a_strict, c):
    row = lax.broadcasted_iota(jnp.int32, (c, c), 0)
    col = lax.broadcasted_iota(jnp.int32, (c, c), 1)
    eye = jnp.where(row == col, 1.0, 0.0)
    bd = lambda x: x.astype(BF16)
    base = SUBLANES
    same_base = row // base == col // base
    d1 = [bd(jnp.where(same_base, a, 0.0)) for a in a_strict]
    d2 = [bd(_dot(d, d)) for d in d1]
    inv = [eye + d.astype(F32) for d in d1]
    inv = [i + _dot(d, bd(i)) for i, d in zip(inv, d2)]
    d4 = [bd(_dot(d, d)) for d in d2]
    inv = [i + _dot(d, bd(i)) for i, d in zip(inv, d4)]
    blk = base
    while blk < c:
        band = (row // (2 * blk) == col // (2 * blk)) & (row // blk != col // blk)
        off = [bd(jnp.where(band, a, 0.0)) for a in a_strict]
        inv_b = [bd(i) for i in inv]
        tmp = [bd(_dot(o, i)) for o, i in zip(off, inv_b)]
        inv = [i + _dot(ib, t) for i, ib, t in zip(inv, inv_b, tmp)]
        blk *= 2
    return inv


def _rwkv_scan_block(r, ld, k, v_all, kn, b, state_ref):
    c = RW_CHUNK
    n = RWKV_HEAD_DIM
    heads = RW_CH // n
    rows = ld.shape[0]
    subs = rows // c

    @pl.when(pl.program_id(1) == 0)
    def _():
        state_ref[...] = jnp.zeros_like(state_ref)

    row = lax.broadcasted_iota(jnp.int32, (c, c), 0)
    col = lax.broadcasted_iota(jnp.int32, (c, c), 1)
    strict = row > col
    incl2 = (lax.broadcasted_iota(jnp.int32, (c, 2 * c), 0)
             >= lax.broadcasted_iota(jnp.int32, (c, 2 * c), 1) % c)
    brow = lax.broadcasted_iota(jnp.int32, (rows, rows), 0)
    bcol = lax.broadcasted_iota(jnp.int32, (rows, rows), 1)
    tri_incl = jnp.where((brow >= bcol) & (brow // c == bcol // c), 1.0, 0.0).astype(BF16)

    ld_hi, ld_lo = _split_bf16(ld)
    cum = _dot(tri_incl, ld_hi) + _dot(tri_incl, ld_lo)
    g_inc = jnp.exp(cum)
    g_inv = jnp.exp(-cum)
    a_t = -kn * jnp.exp(cum - ld)
    b_t = b * g_inv
    k_t = k * g_inv
    r_t = r * g_inc

    bd = lambda x: x.astype(BF16)
    idx = [(s, h) for s in range(subs) for h in range(heads)]
    rs = lambda s: slice(s * c, (s + 1) * c)
    ls = lambda h: slice(h * n, (h + 1) * n)
    v_h = [v_all[rs(s), ls(h)] for s, h in idx]
    a_h = [a_t[rs(s), ls(h)] for s, h in idx]
    r_h = [bd(r_t[rs(s), ls(h)]) for s, h in idx]
    ar = [bd(jnp.concatenate([a_h[i], r_t[rs(s), ls(h)]], axis=0)) for i, (s, h) in enumerate(idx)]
    bk = [bd(jnp.concatenate([b_t[rs(s), ls(h)], k_t[rs(s), ls(h)]], axis=0)) for s, h in idx]
    p1 = [_dot_nt(x, y) for x, y in zip(ar, bk)]
    a_ab = [jnp.where(strict, p[:c, :c], 0.0) for p in p1]
    a_ak = [bd(jnp.where(strict, p[:c, c:], 0.0)) for p in p1]
    a_r = [bd(jnp.where(incl2, p[c:, :], 0.0)) for p in p1]
    akv = [_dot(x, bd(y)) for x, y in zip(a_ak, v_h)]
    inv = _unit_lower_inverse(a_ab, c)
    sol = [_dot(bd(inv[i]), bd(jnp.concatenate([a_h[i], akv[i]], axis=1))) for i in range(len(idx))]

    state = [state_ref[h] for h in range(heads)]
    y_rows = []
    for s in range(subs):
        at = lambda lst, h: lst[s * heads + h]
        sb = [bd(x) for x in state]
        u = [_dot_nt(bd(at(sol, h)[:, :n]), sb[h]) + at(sol, h)[:, n:] for h in range(heads)]
        uv = [bd(jnp.concatenate([u[h], at(v_h, h)], axis=0)) for h in range(heads)]
        ys = [_dot_nt(at(r_h, h), sb[h]) + _dot(at(a_r, h), uv[h]) for h in range(heads)]
        g_last = g_inc[(s + 1) * c - 1:(s + 1) * c, :]
        state = [(state[h] + _dot_tn(uv[h], at(bk, h))) * g_last[:, ls(h)] for h in range(heads)]
        y_rows.append(jnp.concatenate(ys, axis=1))
    for h in range(heads):
        state_ref[h] = state[h]
    return jnp.concatenate(y_rows, axis=0)


def _rwkv_finish(y, g, bonus, ln_g, ln_b):
    n = RWKV_HEAD_DIM
    ones_bd = _head_block_ones(2 * LANES)
    mean = _head_sum(y, ones_bd) * (1.0 / n)
    yc = y - mean
    var = _head_sum(yc * yc, ones_bd) * (1.0 / n)
    yn = yc * lax.rsqrt(var + GN_EPS) * ln_g + ln_b
    return (yn + bonus) * g


def _gelu_tanh(x):
    return 0.5 * x * (1.0 + jnp.tanh(np.sqrt(2.0 / np.pi).astype(np.float32) * (x + 0.044715 * (x * x * x))))


def _gmlp_kernel(u_ref, v_ref, lng_ref, lnb_ref, ws_ref, bs_ref, o_ref, *, chunks):
    ch = GMLP_CHUNK
    gd = GMLP_GROUP_DIM
    u = _gelu_tanh(u_ref[...])
    v = _gelu_tanh(v_ref[...])
    mu = jnp.mean(v, axis=-1, keepdims=True)
    vc = v - mu
    var = jnp.mean(vc * vc, axis=-1, keepdims=True)
    vn = (vc * lax.rsqrt(var + LN_EPS) * lng_ref[...] + lnb_ref[...]).astype(BF16)
    row = lax.broadcasted_iota(jnp.int32, (ch, ch), 0)
    col = lax.broadcasted_iota(jnp.int32, (ch, ch), 1)
    causal = row >= col
    bs = bs_ref[...]
    for g in range(GMLP_GROUPS):
        w = jnp.where(causal, ws_ref[g], 0.0).astype(BF16)
        bias = bs[:, g:g + 1]
        for n in range(chunks):
            f = _dot(w, vn[n * ch:(n + 1) * ch, g * gd:(g + 1) * gd]) + bias
            o_ref[n * ch:(n + 1) * ch, g * gd:(g + 1) * gd] = (
                u[n * ch:(n + 1) * ch, g * gd:(g + 1) * gd] * f).astype(o_ref.dtype)


def _gmlp(proj, ln_g, ln_b, w_s, b_s, tb=256):
    t = proj.shape[0]
    bs_t = jnp.pad(b_s.T, ((0, 0), (0, LANES - GMLP_GROUPS)))
    return pl.pallas_call(
        functools.partial(_gmlp_kernel, chunks=tb // GMLP_CHUNK),
        grid=(t // tb,),
        in_specs=[pl.BlockSpec((tb, GMLP_WIDTH), lambda i: (i, P_GU // GMLP_WIDTH)),
                  pl.BlockSpec((tb, GMLP_WIDTH), lambda i: (i, P_GV // GMLP_WIDTH)),
                  pl.BlockSpec((1, GMLP_WIDTH), lambda i: (0, 0)),
                  pl.BlockSpec((1, GMLP_WIDTH), lambda i: (0, 0)),
                  pl.BlockSpec((GMLP_GROUPS, GMLP_CHUNK, GMLP_CHUNK), lambda i: (0, 0, 0)),
                  pl.BlockSpec((GMLP_CHUNK, LANES), lambda i: (0, 0))],
        out_specs=pl.BlockSpec((tb, GMLP_WIDTH), lambda i: (i, 0)),
        out_shape=jax.ShapeDtypeStruct((t, GMLP_WIDTH), BF16),
        compiler_params=_cparams(("arbitrary",)),
        name="gmlp",
    )(proj, proj, ln_g.reshape(1, -1), ln_b.reshape(1, -1), w_s, bs_t)


def _rope_lanes(t, cc, s1, s2):
    return t * cc + pltpu.roll(t, LANES - MLA_ROPE_DIM // 2, 1) * s1 + pltpu.roll(t, MLA_ROPE_DIM // 2, 1) * s2


def _mla_proj_kernel(cq_ref, ckv_ref, kr_ref, qn_ref, kvn_ref, wq_ref, wk_ref, wvt_ref, cc_ref, s1_ref, s2_ref,
                     q_o, kn_o, vt_o, kr_o):
    cq = cq_ref[...]
    qn = (cq * lax.rsqrt(jnp.mean(cq * cq, axis=-1, keepdims=True) + RMS_EPS) * qn_ref[...]).astype(BF16)
    ckv = ckv_ref[...]
    kvn = (ckv * lax.rsqrt(jnp.mean(ckv * ckv, axis=-1, keepdims=True) + RMS_EPS) * kvn_ref[...]).astype(BF16)
    cc, s1, s2 = cc_ref[...], s1_ref[...], s2_ref[...]
    scale = MLA_QK_DIM ** -0.5 * np.log2(np.e)
    for h in range(MLA_HEADS):
        q = _dot(qn, wq_ref[:, 2 * LANES * h:2 * LANES * (h + 1)]) * scale
        q_o[:, 2 * LANES * h:2 * LANES * h + LANES] = q[:, :LANES].astype(BF16)
        q_o[:, 2 * LANES * h + LANES:2 * LANES * (h + 1)] = _rope_lanes(q[:, LANES:], cc, s1, s2).astype(BF16)
    kn_o[...] = _dot(kvn, wk_ref[...]).astype(BF16)
    vt_o[...] = _dot_nt(wvt_ref[...], kvn).astype(BF16)
    kr_o[...] = _rope_lanes(kr_ref[...], cc, s1, s2).astype(BF16)


def _mla_proj(proj, positions, q_norm, kv_norm, w_uq, w_ukv, tm=512):
    t = proj.shape[0]
    h = MLA_HEADS
    half = MLA_ROPE_DIM // 2
    inv_freq = jnp.power(ROPE_THETA, -jnp.arange(0, MLA_ROPE_DIM, 2, dtype=F32) / MLA_ROPE_DIM)
    ang = positions.reshape(t).astype(F32)[:, None] * inv_freq
    cos, sin = jnp.cos(ang), jnp.sin(ang)
    z = jnp.zeros((t, half), F32)
    cc = jnp.concatenate([cos, cos, z, z], axis=1)
    s1 = jnp.concatenate([-sin, z, z, z], axis=1)
    s2 = jnp.concatenate([z, sin, z, z], axis=1)
    wq = w_uq.reshape(MLA_Q_RANK, h, MLA_QK_DIM)
    wq = jnp.pad(wq, ((0, 0), (0, 0), (0, 2 * LANES - MLA_QK_DIM))).reshape(MLA_Q_RANK, h * 2 * LANES).astype(BF16)
    wkv = w_ukv.reshape(MLA_KV_RANK, h, MLA_NOPE_DIM + MLA_V_DIM)
    wk = wkv[:, :, :MLA_NOPE_DIM].reshape(MLA_KV_RANK, h * MLA_NOPE_DIM).astype(BF16)
    wvt = wkv[:, :, MLA_NOPE_DIM:].reshape(MLA_KV_RANK, h * MLA_V_DIM).T.astype(BF16)
    full = lambda a: pl.BlockSpec(a.shape, lambda i: (0,) * a.ndim)
    tab = pl.BlockSpec((tm, LANES), lambda i: (i, 0))
    qn2, kvn2 = q_norm.reshape(1, -1), kv_norm.reshape(1, -1)
    return pl.pallas_call(
        _mla_proj_kernel,
        grid=(t // tm,),
        in_specs=[pl.BlockSpec((tm, MLA_Q_RANK), lambda i: (i, P_CQ // MLA_Q_RANK)),
                  pl.BlockSpec((tm, MLA_KV_RANK), lambda i: (i, P_CKV // MLA_KV_RANK)),
                  pl.BlockSpec((tm, LANES), lambda i: (i, P_KROPE // LANES)),
                  full(qn2), full(kvn2), full(wq), full(wk), full(wvt), tab, tab, tab],
        out_specs=[pl.BlockSpec((tm, h * 2 * LANES), lambda i: (i, 0)),
                   pl.BlockSpec((tm, h * MLA_NOPE_DIM), lambda i: (i, 0)),
                   pl.BlockSpec((h * MLA_V_DIM, tm), lambda i: (0, i)),
                   pl.BlockSpec((tm, LANES), lambda i: (i, 0))],
        out_shape=[jax.ShapeDtypeStruct((t, h * 2 * LANES), BF16),
                   jax.ShapeDtypeStruct((t, h * MLA_NOPE_DIM), BF16),
                   jax.ShapeDtypeStruct((h * MLA_V_DIM, t), BF16),
                   jax.ShapeDtypeStruct((t, LANES), BF16)],
        compiler_params=_cparams(("arbitrary",)),
        name="mla_proj",
    )(proj, proj, proj, qn2, kvn2, wq, wk, wvt, cc, s1, s2)


def _flash_kernel(qi_ref, kj_ref, q_ref, kn_ref, kr_ref, vt_ref, o_ref, m_sc, l_sc, acc_sc, st_sc, *, tq, tk):
    s = pl.program_id(1)
    qi, kj = qi_ref[s], kj_ref[s]

    @pl.when(kj == 0)
    def _():
        m_sc[...] = jnp.full_like(m_sc, -jnp.inf)
        l_sc[...] = jnp.zeros_like(l_sc)
        acc_sc[...] = jnp.zeros_like(acc_sc)

    qt = 2 * LANES
    kb = 2 * LANES

    def step(masked):
        k = jnp.concatenate([kn_ref[...], kr_ref[...]], axis=1)
        m_all = m_sc[...]
        m_news, alphas = [], []
        for c0 in range(0, tq, qt):
            st = _dot_nt(k, q_ref[c0:c0 + qt, :])
            if masked:
                key = kj * tk + lax.broadcasted_iota(jnp.int32, (tk, qt), 0)
                qry = qi * tq + c0 + lax.broadcasted_iota(jnp.int32, (tk, qt), 1)
                st = jnp.where(key <= qry, st, -jnp.inf)
            st_sc[:, c0:c0 + qt] = st
            m_prev = m_all[:, c0:c0 + qt]
            m_new = jnp.maximum(m_prev, jnp.max(st, axis=0, keepdims=True))
            m_news.append(m_new)
            alphas.append(jnp.exp2(m_prev - m_new))
        pvs, sums = [], []
        for t_i, c0 in enumerate(range(0, tq, qt)):
            pv, ps = None, None
            for r0 in range(0, tk, kb):
                p = jnp.exp2(st_sc[r0:r0 + kb, c0:c0 + qt] - m_news[t_i])
                part = jnp.sum(p, axis=0, keepdims=True)
                prod = _dot(vt_ref[:, r0:r0 + kb], p.astype(BF16))
                ps = part if ps is None else ps + part
                pv = prod if pv is None else pv + prod
            pvs.append(pv)
            sums.append(ps)
        alpha = jnp.concatenate(alphas, axis=1)
        m_sc[...] = jnp.concatenate(m_news, axis=1)
        l_sc[...] = alpha * l_sc[...] + jnp.concatenate(sums, axis=1)
        acc_sc[...] = alpha * acc_sc[...] + jnp.concatenate(pvs, axis=1)

    last_key_of_block = kj * tk + tk - 1
    on_diag = last_key_of_block > qi * tq

    @pl.when(jnp.logical_not(on_diag))
    def _():
        step(False)

    @pl.when(on_diag)
    def _():
        step(True)

    @pl.when(last_key_of_block >= qi * tq + tq - 1)
    def _():
        o_ref[...] = (acc_sc[...] / l_sc[...]).T.astype(o_ref.dtype)


def _flash(q, kn, vt, kr, tq=512, tk=512):
    t = q.shape[0]
    assert tq % tk == 0
    pairs = [(i, j) for i in range(t // tq) for j in range((i + 1) * tq // tk)]
    qi = jnp.asarray([pr[0] for pr in pairs], jnp.int32)
    kj = jnp.asarray([pr[1] for pr in pairs], jnp.int32)
    grid_spec = pltpu.PrefetchScalarGridSpec(
        num_scalar_prefetch=2,
        grid=(MLA_HEADS, len(pairs)),
        in_specs=[pl.BlockSpec((tq, 2 * LANES), lambda h, s, qi, kj: (qi[s], h)),
                  pl.BlockSpec((tk, MLA_NOPE_DIM), lambda h, s, qi, kj: (kj[s], h)),
                  pl.BlockSpec((tk, LANES), lambda h, s, qi, kj: (kj[s], 0)),
                  pl.BlockSpec((MLA_V_DIM, tk), lambda h, s, qi, kj: (h, kj[s]))],
        out_specs=pl.BlockSpec((tq, MLA_V_DIM), lambda h, s, qi, kj: (qi[s], h)),
        scratch_shapes=[pltpu.VMEM((1, tq), F32), pltpu.VMEM((1, tq), F32), pltpu.VMEM((MLA_V_DIM, tq), F32),
                        pltpu.VMEM((tk, tq), F32)],
    )
    return pl.pallas_call(
        functools.partial(_flash_kernel, tq=tq, tk=tk),
        grid_spec=grid_spec,
        out_shape=jax.ShapeDtypeStruct((t, MLA_HEADS * MLA_V_DIM), BF16),
        compiler_params=_cparams(("arbitrary", "arbitrary")),
        name="flash",
    )(qi, kj, q, kn, kr, vt)


def _router_kernel(h_ref, rwt_ref, bias_ref, exp_o, pos_o, gate_o, cnt_o, carry_sc):
    e = N_EXPERTS
    per = e // N_EXPERT_GROUPS
    h_hi, h_lo = _split_bf16(h_ref[...])
    w_hi, w_lo = _split_bf16(rwt_ref[...])
    logits = _dot_nt(w_hi, h_hi) + _dot_nt(w_hi, h_lo) + _dot_nt(w_lo, h_hi)
    scores = _sigmoid(logits)
    biased = scores + bias_ref[...][:, 0:1]
    tb = biased.shape[1]
    neg = -jnp.inf
    sub = lax.broadcasted_iota(jnp.int32, (per, tb), 0)
    grp_rows = []
    for g in range(N_EXPERT_GROUPS):
        blk = biased[g * per:(g + 1) * per, :]
        m1 = jnp.max(blk, axis=0, keepdims=True)
        first = jnp.min(jnp.where(blk == m1, sub, per), axis=0, keepdims=True)
        m2 = jnp.max(jnp.where(sub == first, neg, blk), axis=0, keepdims=True)
        grp_rows.append(m1 + m2)
    grp = jnp.concatenate(grp_rows, axis=0)
    gidx = lax.broadcasted_iota(jnp.int32, grp.shape, 0)
    grank = jnp.zeros(grp.shape, jnp.int32)
    for g in range(N_EXPERT_GROUPS):
        other = grp[g:g + 1, :]
        ahead = (other > grp) | ((other == grp) & (g < gidx))
        grank = grank + jnp.where(ahead, 1, 0)
    gsel = grank < TOPK_GROUPS
    masked = jnp.concatenate(
        [jnp.where(gsel[g:g + 1, :], biased[g * per:(g + 1) * per, :], neg) for g in range(N_EXPERT_GROUPS)], axis=0)
    eidx = lax.broadcasted_iota(jnp.int32, masked.shape, 0)
    rank = jnp.zeros(masked.shape, jnp.int32)
    for j in range(e):
        other = masked[j:j + 1, :]
        ahead = (other > masked) | ((other == masked) & (j < eidx))
        rank = rank + jnp.where(ahead, 1, 0)
    chosen = rank < TOP_K
    sel = jnp.where(chosen, scores, 0.0)
    gate = sel / jnp.sum(sel, axis=0, keepdims=True) * ROUTED_SCALE

    @pl.when(pl.program_id(0) == 0)
    def _():
        carry_sc[...] = jnp.zeros_like(carry_sc)

    chosen_f = jnp.where(chosen, 1.0, 0.0)
    earlier = (lax.broadcasted_iota(jnp.int32, (tb, tb), 0) < lax.broadcasted_iota(jnp.int32, (tb, tb), 1))
    carry = carry_sc[...]
    pos = _dot(chosen_f.astype(BF16), jnp.where(earlier, 1.0, 0.0).astype(BF16)) + carry[:, 0:1]
    carry_sc[...] = carry + jnp.sum(chosen_f, axis=1, keepdims=True)
    cnt_o[...] = carry_sc[...].astype(jnp.int32)

    eidx_f = eidx.astype(F32)
    pick = lambda hit, val: jnp.sum(jnp.where(hit, val, 0.0), axis=0, keepdims=True)
    hits = [rank == k for k in range(TOP_K)]
    exp_o[...] = jnp.concatenate([pick(hit, eidx_f) for hit in hits], axis=0).astype(jnp.int32)
    pos_o[...] = jnp.concatenate([pick(hit, pos) for hit in hits], axis=0).astype(jnp.int32)
    gate_o[...] = jnp.concatenate([pick(hit, gate) for hit in hits], axis=0)


def _router(h, router_w, router_bias, tb=512):
    t, d = h.shape
    bias = jnp.broadcast_to(router_bias.astype(F32)[:, None], (N_EXPERTS, LANES))
    per_tok = pl.BlockSpec((TOP_K, tb), lambda i: (0, i))
    return pl.pallas_call(
        _router_kernel,
        grid=(t // tb,),
        in_specs=[pl.BlockSpec((tb, d), lambda i: (i, 0)),
                  pl.BlockSpec((N_EXPERTS, d), lambda i: (0, 0)),
                  pl.BlockSpec((N_EXPERTS, LANES), lambda i: (0, 0))],
        out_specs=[per_tok, per_tok, per_tok, pl.BlockSpec((N_EXPERTS, LANES), lambda i: (0, 0))],
        out_shape=[jax.ShapeDtypeStruct((TOP_K, t), jnp.int32), jax.ShapeDtypeStruct((TOP_K, t), jnp.int32),
                   jax.ShapeDtypeStruct((TOP_K, t), F32), jax.ShapeDtypeStruct((N_EXPERTS, LANES), jnp.int32)],
        scratch_shapes=[pltpu.VMEM((N_EXPERTS, LANES), F32)],
        compiler_params=_cparams(("arbitrary",)),
        name="router",
    )(h, router_w.T, bias)


HALF_MASK = 0xFFFF0000


def _pack_bf16_pairs(x):
    n = x.shape[1] // 2
    lo = pltpu.bitcast(x[:, :n].astype(BF16).astype(F32), jnp.uint32) >> 16
    hi = pltpu.bitcast(x[:, n:].astype(BF16).astype(F32), jnp.uint32) & jnp.uint32(HALF_MASK)
    return lo | hi


def _unpack_bf16_pairs(w):
    return pltpu.bitcast(w << 16, F32), pltpu.bitcast(w & jnp.uint32(HALF_MASK), F32)


def _dispatch_plan(exp_r, pos_r, cnt, bm, nb_max):
    ids = jnp.arange(N_EXPERTS, dtype=jnp.int32)
    nb = (cnt + bm - 1) // bm
    bend = jnp.sum(jnp.where(ids[None, :] <= ids[:, None], nb[None, :], 0), axis=1)
    total = bend[-1]
    slot_start = (bend - nb) * bm
    start_of = jnp.sum(jnp.where(exp_r[:, :, None] == ids, slot_start, 0), axis=-1)
    slot = (start_of + pos_r).T.reshape(-1)
    bidx = jnp.arange(nb_max, dtype=jnp.int32)
    blocks = jnp.minimum(bidx, total - 1)
    blk_exp = jnp.sum(jnp.where(bend[None, :] <= blocks[:, None], 1, 0), axis=1)
    first = jnp.where((bidx == 0) | (blk_exp != jnp.roll(blk_exp, 1)), 1, 0)
    turn = jnp.sum(jnp.where(bidx[None, :] <= bidx[:, None], first[None, :], 0), axis=1) - 1
    later_first = jnp.where((bidx[None, :] > bidx[:, None]) & (first[None, :] == 1), bidx[None, :], nb_max)
    nxt_blk = jnp.min(later_first, axis=1)
    nxt_exp = jnp.sum(jnp.where(bidx[None, :] == nxt_blk[:, None], blk_exp[None, :] + 1, 0), axis=1) - 1
    meta = jnp.concatenate([blk_exp, first, turn % 2, nxt_exp]).astype(jnp.int32)
    lo = jnp.concatenate([slot_start + cnt, (total * bm).reshape(1)])
    hi = jnp.concatenate([slot_start + nb * bm, jnp.full((1,), nb_max * bm, jnp.int32)])
    gaps = jnp.stack([lo, hi], axis=1).reshape(-1).astype(jnp.int32)
    return slot.astype(jnp.int32), meta, total.reshape(1).astype(jnp.int32), gaps


ZERO_ROWS = 256


def _dispatch_kernel(slot_ref, gaps_ref, h_ref, xs_hbm, zero_sc, sem, zsem, *, tb, steps, n_gaps):
    i = pl.program_id(0)

    def fill_gaps(act):
        def gap(r, carry):
            lo, hi = gaps_ref[2 * r], gaps_ref[2 * r + 1]
            lo_tile = jnp.minimum((lo + SUBLANES - 1) // SUBLANES * SUBLANES, hi)

            def single(row, carry2):
                act(pltpu.make_async_copy(zero_sc.at[pl.ds(0, 1), :], xs_hbm.at[pl.ds(row, 1), :], zsem))
                return carry2

            lax.fori_loop(lo, lo_tile, single, 0)
            whole = (hi - lo_tile) // ZERO_ROWS

            def chunk(c, carry2):
                start = pl.multiple_of(lo_tile + c * ZERO_ROWS, SUBLANES)
                act(pltpu.make_async_copy(zero_sc, xs_hbm.at[pl.ds(start, ZERO_ROWS), :], zsem))
                return carry2

            lax.fori_loop(0, whole, chunk, 0)
            off = lo_tile + whole * ZERO_ROWS
            rem = hi - off
            size = ZERO_ROWS // 2
            while size >= SUBLANES:
                take = (rem & size) != 0

                @pl.when(take)
                def _(off=off, size=size):
                    start = pl.multiple_of(off, SUBLANES)
                    act(pltpu.make_async_copy(zero_sc.at[pl.ds(0, size), :], xs_hbm.at[pl.ds(start, size), :], zsem))

                off = off + jnp.where(take, size, 0)
                size //= 2
            return carry

        lax.fori_loop(0, n_gaps, gap, 0)

    @pl.when(i == 0)
    def _():
        zero_sc[...] = jnp.zeros_like(zero_sc)
        fill_gaps(lambda c: c.start())

    def body(t, carry):
        for k in range(TOP_K):
            s = slot_ref[t * TOP_K + k]
            pltpu.make_async_copy(h_ref.at[pl.ds(t, 1), :], xs_hbm.at[pl.ds(s, 1), :], sem).start()
        return carry

    lax.fori_loop(0, tb, body, 0)
    rows = tb * TOP_K
    pltpu.make_async_copy(xs_hbm.at[pl.ds(0, rows), :], xs_hbm.at[pl.ds(0, rows), :], sem).wait()

    @pl.when(i == steps - 1)
    def _():
        fill_gaps(lambda c: c.wait())


def _dispatch(h_pk, slot, gaps, n_slots, tb):
    t, w = h_pk.shape
    steps = t // tb
    return pl.pallas_call(
        functools.partial(_dispatch_kernel, tb=tb, steps=steps, n_gaps=gaps.shape[0] // 2),
        grid=(steps,),
        in_specs=[pl.BlockSpec((tb * TOP_K,), lambda i: (i,), memory_space=pltpu.SMEM),
                  pl.BlockSpec(memory_space=pltpu.SMEM),
                  pl.BlockSpec((tb, w), lambda i: (i, 0))],
        out_specs=pl.BlockSpec(memory_space=pl.ANY),
        out_shape=jax.ShapeDtypeStruct((n_slots, w), jnp.uint32),
        scratch_shapes=[pltpu.VMEM((ZERO_ROWS, w), jnp.uint32), pltpu.SemaphoreType.DMA(()),
                        pltpu.SemaphoreType.DMA(())],
        compiler_params=pltpu.CompilerParams(dimension_semantics=("arbitrary",), vmem_limit_bytes=VMEM_LIMIT,
                                             has_side_effects=True, disable_bounds_checks=True),
        name="moe_dispatch",
    )(slot, gaps, h_pk)


def _expert_meta(meta_ref, nb):
    b = pl.program_id(0)
    return meta_ref[b], meta_ref[nb + b] == 1, meta_ref[2 * nb + b], meta_ref[3 * nb + b]


def _expert_weight_turn(meta_ref, nb, layer, hbm_refs, stage_ref, sem, cast_to):
    e, first, par, nxt = _expert_meta(meta_ref, nb)

    def copies(expert, half):
        return [pltpu.make_async_copy(w.at[layer, expert], stage_ref.at[half, n], sem.at[half, n])
                for n, w in enumerate(hbm_refs)]

    @pl.when(pl.program_id(0) == 0)
    def _():
        for c in copies(e, par):
            c.start()

    @pl.when(first)
    def _():
        for c in copies(e, par):
            c.wait()
        cast_to(stage_ref.at[par])

        @pl.when(nxt >= 0)
        def _():
            for c in copies(nxt, 1 - par):
                c.start()


def _expert_up_kernel(meta_ref, tot_ref, xs_ref, w1_hbm, w3_hbm, act_ref, stage_sc, w13_sc, sem, *, nb, layer):
    def cast_to(staged):
        w13_sc[:, :D_EXPERT] = staged[0].astype(BF16)
        w13_sc[:, D_EXPERT:] = staged[1].astype(BF16)

    _expert_weight_turn(meta_ref, nb, layer, [w1_hbm, w3_hbm], stage_sc, sem, cast_to)

    @pl.when(pl.program_id(0) < tot_ref[0])
    def _():
        lo, hi = _unpack_bf16_pairs(xs_ref[...])
        x = jnp.concatenate([lo.astype(BF16), hi.astype(BF16)], axis=1)
        hgu = _dot(x, w13_sc[...])
        hg, hu = hgu[:, :D_EXPERT], hgu[:, D_EXPERT:]
        act_ref[...] = (hg * _sigmoid(hg) * hu).astype(BF16)

    @pl.when(pl.program_id(0) >= tot_ref[0])
    def _():
        act_ref[...] = jnp.zeros_like(act_ref)


def _expert_down_kernel(meta_ref, tot_ref, act_ref, w2_hbm, ys_ref, stage_sc, w2_sc, sem, *, nb, layer):
    def cast_to(staged):
        w2_sc[...] = staged[0].astype(BF16)

    _expert_weight_turn(meta_ref, nb, layer, [w2_hbm], stage_sc, sem, cast_to)

    @pl.when(pl.program_id(0) < tot_ref[0])
    def _():
        ys_ref[...] = _pack_bf16_pairs(_dot(act_ref[...], w2_sc[...]))

    @pl.when(pl.program_id(0) >= tot_ref[0])
    def _():
        ys_ref[...] = jnp.zeros_like(ys_ref)


def _experts(xs, meta, total, w1, w3, w2, layer, bm):
    n_slots, w = xs.shape
    d = 2 * w
    nb = n_slots // bm
    used = lambda b, meta, tot: (jnp.minimum(b, tot[0] - 1), 0)
    every = lambda b, meta, tot: (b, 0)
    hbm = pl.BlockSpec(memory_space=pl.ANY)
    act = pl.pallas_call(
        functools.partial(_expert_up_kernel, nb=nb, layer=layer),
        grid_spec=pltpu.PrefetchScalarGridSpec(
            num_scalar_prefetch=2,
            grid=(nb,),
            in_specs=[pl.BlockSpec((bm, w), used), hbm, hbm],
            out_specs=pl.BlockSpec((bm, D_EXPERT), every),
            scratch_shapes=[pltpu.VMEM((2, 2, d, D_EXPERT), F32), pltpu.VMEM((d, 2 * D_EXPERT), BF16),
                            pltpu.SemaphoreType.DMA((2, 2))]),
        out_shape=jax.ShapeDtypeStruct((n_slots, D_EXPERT), BF16),
        compiler_params=_cparams(("arbitrary",)),
        name="moe_up",
    )(meta, total, xs, w1, w3)
    return pl.pallas_call(
        functools.partial(_expert_down_kernel, nb=nb, layer=layer),
        grid_spec=pltpu.PrefetchScalarGridSpec(
            num_scalar_prefetch=2,
            grid=(nb,),
            in_specs=[pl.BlockSpec((bm, D_EXPERT), used), hbm],
            out_specs=pl.BlockSpec((bm, w), every),
            scratch_shapes=[pltpu.VMEM((2, 1, D_EXPERT, d), F32), pltpu.VMEM((D_EXPERT, d), BF16),
                            pltpu.SemaphoreType.DMA((2, 1))]),
        out_shape=jax.ShapeDtypeStruct((n_slots, w), jnp.uint32),
        compiler_params=_cparams(("arbitrary",)),
        name="moe_down",
    )(meta, total, act, w2)


def _combine_kernel(slot_ref, slot_next_ref, gate_ref, h_ref, hb_ref, sw13_ref, sw2_ref, g_ref, b_ref, ys_hbm,
                    of_ref, ob_ref, rows_a, rows_b, shared_sc, sem, *, tb, steps):
    i = pl.program_id(0)
    grp = 2 * SUBLANES

    def issue(table_ref, buf, buf_sem, t0, first=0, count=grp):
        for j in range(first, first + count):
            for k in range(TOP_K):
                s = table_ref[(t0 + j) * TOP_K + k]
                pltpu.make_async_copy(ys_hbm.at[pl.ds(s, 1), :], buf.at[k, pl.ds(t0 + j, 1), :], buf_sem).start()

    def wait_block(buf, buf_sem):
        pltpu.make_async_copy(buf, buf, buf_sem).wait()

    @pl.when(i == 0)
    def _():
        def first(g, carry):
            issue(slot_ref, rows_a, sem.at[0], g * grp)
            return carry

        lax.fori_loop(0, tb // grp, first, 0)

    hgu = _dot(hb_ref[...], sw13_ref[...])
    hg, hu = hgu[:, :D_EXPERT], hgu[:, D_EXPERT:]
    shared_sc[...] = _dot((hg * _sigmoid(hg) * hu).astype(BF16), sw2_ref[...])

    def run(cur, cur_sem, nxt, nxt_sem):
        wait_block(cur, cur_sem)

        def group(g, carry):
            r0 = pl.multiple_of(g * grp, grp)
            rows = pl.ds(r0, grp)
            gate = gate_ref[rows, :]
            acc_lo = jnp.zeros((grp, cur.shape[-1]), F32)
            acc_hi = jnp.zeros((grp, cur.shape[-1]), F32)
            per_k = grp // TOP_K
            for k in range(TOP_K):
                issue(slot_next_ref, nxt, nxt_sem, r0, k * per_k, per_k)
                lo, hi = _unpack_bf16_pairs(cur[k, rows, :])
                gk = gate[:, k:k + 1]
                acc_lo = acc_lo + gk * lo
                acc_hi = acc_hi + gk * hi
            t = ALPHA * h_ref[rows, :] + shared_sc[rows, :] + jnp.concatenate([acc_lo, acc_hi], axis=1)
            mu = jnp.mean(t, axis=-1, keepdims=True)
            c = t - mu
            var = jnp.mean(c * c, axis=-1, keepdims=True)
            out = c * lax.rsqrt(var + LN_EPS) * g_ref[...] + b_ref[...]
            of_ref[rows, :] = out
            ob_ref[rows, :] = out.astype(BF16)
            return carry

        lax.fori_loop(0, tb // grp, group, 0)

        @pl.when(i == steps - 1)
        def _():
            wait_block(nxt, nxt_sem)

    even = lax.rem(i, 2) == 0

    @pl.when(even)
    def _():
        run(rows_a, sem.at[0], rows_b, sem.at[1])

    @pl.when(jnp.logical_not(even))
    def _():
        run(rows_b, sem.at[1], rows_a, sem.at[0])


def _combine(ys, slot, gate_tk, hf, hb, sw1, sw3, sw2, g, b, tb):
    t, d = hf.shape
    steps = t // tb
    sw13 = jnp.concatenate([sw1, sw3], axis=1).astype(BF16)
    row = pl.BlockSpec((tb, d), lambda i: (i, 0))
    full = lambda a: pl.BlockSpec(a.shape, lambda i: (0,) * a.ndim)
    g2, b2, sw2b = g.reshape(1, d), b.reshape(1, d), sw2.astype(BF16)
    return pl.pallas_call(
        functools.partial(_combine_kernel, tb=tb, steps=steps),
        grid=(steps,),
        in_specs=[pl.BlockSpec((tb * TOP_K,), lambda i: (i,), memory_space=pltpu.SMEM),
                  pl.BlockSpec((tb * TOP_K,), lambda i: (jnp.minimum(i + 1, steps - 1),), memory_space=pltpu.SMEM),
                  pl.BlockSpec((tb, TOP_K), lambda i: (i, 0)),
                  row, row, full(sw13), full(sw2b), full(g2), full(b2),
                  pl.BlockSpec(memory_space=pl.ANY)],
        out_specs=[row, row],
        out_shape=[jax.ShapeDtypeStruct((t, d), F32), jax.ShapeDtypeStruct((t, d), BF16)],
        scratch_shapes=[pltpu.VMEM((TOP_K, tb, ys.shape[1]), jnp.uint32),
                        pltpu.VMEM((TOP_K, tb, ys.shape[1]), jnp.uint32),
                        pltpu.VMEM((tb, d), F32), pltpu.SemaphoreType.DMA((2,))],
        compiler_params=pltpu.CompilerParams(dimension_semantics=("arbitrary",), vmem_limit_bytes=VMEM_LIMIT,
                                             disable_bounds_checks=True),
        name="moe_combine",
    )(slot, slot, gate_tk, hf, hb, sw13, sw2b, g2, b2, ys)


def _moe_ffn(hf, hb, h_pk, router_w, router_bias, w1, w3, w2, layer, sw1, sw3, sw2, ln_g, ln_b, bm):
    t = hf.shape[0]
    exp_r, pos_r, gate_r, cnt = _router(hf, router_w, router_bias, tb=min(512, t))
    nb_max = t * TOP_K // bm + N_EXPERTS
    slot, meta, total, gaps = _dispatch_plan(exp_r, pos_r, cnt[:, 0], bm, nb_max)
    xs = _dispatch(h_pk, slot, gaps, nb_max * bm, tb=min(256, t // TOP_K))
    ys = _experts(xs, meta, total, w1, w3, w2, layer, bm)
    return _combine(ys, slot, gate_r.T, hf, hb, sw1, sw3, sw2, ln_g, ln_b, tb=min(128, t))


def _w_in_layout_kernel(w_ref, o_ref, *, segments):
    o_ref[...] = jnp.zeros_like(o_ref)
    w = w_ref if len(w_ref.shape) == 2 else w_ref.at[0]
    for src, dst, width in segments:
        o_ref[:, dst:dst + width] = w[:, src:src + width].astype(BF16)


def _prep_w_in(w_in, index, has_vres, tr=256):
    d, cols = w_in.shape[-2:]
    if w_in.ndim == 2:
        w_spec = pl.BlockSpec((tr, cols), lambda i: (i, 0))
    else:
        w_spec = pl.BlockSpec((1, tr, cols), lambda i: (index, i, 0))
    sizes = [RWKV_WIDTH, RWKV_WIDTH, RWKV_WIDTH, DECAY_LORA, AAA_LORA, GATE_LORA,
             GMLP_WIDTH, GMLP_WIDTH, MLA_Q_RANK, MLA_KV_RANK, MLA_ROPE_DIM]
    dsts = [P_R, P_K, P_V, P_WLO, P_ALO, P_GLO, P_GU, P_GV, P_CQ, P_CKV, P_KROPE]
    if has_vres:
        sizes.append(MV_LORA)
        dsts.append(P_VLO)
    srcs = np.concatenate([[0], np.cumsum(sizes)])[:-1]
    segments = tuple((int(s), int(t), int(n)) for s, t, n in zip(srcs, dsts, sizes))
    return pl.pallas_call(
        functools.partial(_w_in_layout_kernel, segments=segments),
        grid=(d // tr,),
        in_specs=[w_spec],
        out_specs=pl.BlockSpec((tr, P_COLS), lambda i: (i, 0)),
        out_shape=jax.ShapeDtypeStruct((d, P_COLS), BF16),
        compiler_params=_cparams(("arbitrary",)),
        name="w_in_layout",
    )(w_in)


def _prep_mu(mu, mu_vres):
    out = jnp.zeros((1, P_COLS), F32)
    offs = np.concatenate([[0], np.cumsum([RWKV_WIDTH] * 3 + [DECAY_LORA, AAA_LORA, GATE_LORA])])
    for dst, i in zip([P_R, P_K, P_V, P_WLO, P_ALO, P_GLO], range(6)):
        out = lax.dynamic_update_slice(out, mu[offs[i]:offs[i + 1]].reshape(1, -1), (0, dst))
    if mu_vres is not None:
        out = lax.dynamic_update_slice(out, mu_vres.reshape(1, -1), (0, P_VLO))
    return out


def kernel(x, positions, w_in_first, w_in_rest, rwkv_mu, rwkv_mu_vres, rwkv_w0, rwkv_w2, rwkv_a0, rwkv_a2, rwkv_v0, rwkv_v2, rwkv_g2, rwkv_k_k, rwkv_k_a, rwkv_r_k, rwkv_ln_g, rwkv_ln_b, gmlp_ln_g, gmlp_ln_b, gmlp_w_s, gmlp_b_s, mla_q_norm, mla_kv_norm, mla_w_uq, mla_w_ukv, w_out, ln1_g, ln1_b, router_w, router_bias, exp_w1, exp_w3, exp_w2, shared_w1, shared_w3, shared_w2, ln2_g, ln2_b):
    b, s, d = x.shape
    t = b * s
    p = dict(rwkv_w0=rwkv_w0, rwkv_w2=rwkv_w2, rwkv_a0=rwkv_a0, rwkv_a2=rwkv_a2, rwkv_v0=rwkv_v0, rwkv_v2=rwkv_v2,
             rwkv_g2=rwkv_g2, rwkv_k_k=rwkv_k_k, rwkv_k_a=rwkv_k_a, rwkv_r_k=rwkv_r_k)
    xf = x.reshape(t, d)
    xb = xf.astype(BF16)
    v_first = None
    for l in range(DEPTH):
        has_vres = l > 0
        w_in = _prep_w_in(w_in_first, 0, False) if l == 0 else _prep_w_in(w_in_rest, l - 1, True)
        mu_p = _prep_mu(rwkv_mu[l], rwkv_mu_vres[l - 1] if has_vres else None)
        proj = _matmul(xb, w_in, F32, tm=min(1024, t), tn=1280)
        y_a, v_first = _rwkv_group(proj, mu_p, p, l, v_first, rwkv_ln_g[l], rwkv_ln_b[l])
        y_b = _gmlp(proj, gmlp_ln_g[l], gmlp_ln_b[l], gmlp_w_s[l], gmlp_b_s[l], tb=min(256, t))
        q, kn, vt, kr = _mla_proj(proj, positions, mla_q_norm[l], mla_kv_norm[l], mla_w_uq[l], mla_w_ukv[l],
                                  tm=min(512, t))
        y_c = _flash(q, kn, vt, kr, tq=min(1024, t), tk=min(1024, t))
        mix = _out_proj(y_a, y_b, y_c, w_out, l, tm=min(512, t), tn=1024)
        hf, hb, h_pk = _res_ln(xf, mix, ln1_g[l], ln1_b[l], tm=min(256, t))
        xf, xb = _moe_ffn(hf, hb, h_pk, router_w[l], router_bias[l],
                          exp_w1, exp_w3, exp_w2, l,
                          shared_w1[l], shared_w3[l], shared_w2[l], ln2_g[l], ln2_b[l], bm=MOE_ROW_BLOCK)
    return xf.reshape(b, s, d)
```

```python
import functools

import jax
import jax.numpy as jnp
import numpy as np
from jax import lax
from jax.experimental import pallas as pl
from jax.experimental.pallas import tpu as pltpu

F32 = jnp.float32
BF16 = jnp.bfloat16

D_MODEL = 4096
DEPTH = 2
RWKV_HEAD_DIM = 64
RWKV_WIDTH = 3 * D_MODEL // 8
DECAY_LORA = 128
AAA_LORA = 128
MV_LORA = 96
GATE_LORA = 480
GN_EPS = 64e-5
GMLP_WIDTH = D_MODEL // 4
GMLP_GROUP_DIM = 128
GMLP_GROUPS = GMLP_WIDTH // GMLP_GROUP_DIM
GMLP_CHUNK = 128
MLA_V_DIM = 128
MLA_WIDTH = D_MODEL - RWKV_WIDTH - GMLP_WIDTH
MLA_HEADS = MLA_WIDTH // MLA_V_DIM
MLA_NOPE_DIM = 128
MLA_ROPE_DIM = 64
MLA_QK_DIM = MLA_NOPE_DIM + MLA_ROPE_DIM
MLA_Q_RANK = 768
MLA_KV_RANK = 512
ROPE_THETA = 10000.0
N_EXPERTS = 64
TOP_K = 8
N_EXPERT_GROUPS = 8
TOPK_GROUPS = 4
D_EXPERT = 384
ROUTED_SCALE = 2.5
ALPHA = (2 * DEPTH) ** 0.25
LN_EPS = 1e-5
RMS_EPS = 1e-6

LANES = 128
SUBLANES = 8
VMEM_LIMIT = 56 * 1024 * 1024

P_GU, P_GV = 0, 1024
P_R, P_K, P_V = 2048, 3584, 5120
P_WLO, P_ALO = 6656, 6784
P_CQ, P_CKV = 6912, 7680
P_GLO, P_KROPE, P_VLO = 8192, 8704, 8832
P_COLS = 8960
GLO_PAD = 512
RW_CH = 512
RW_CHUNK = 64
RW_SUB = 4
MOE_ROW_BLOCK = 256


def _cparams(sem):
    return pltpu.CompilerParams(dimension_semantics=sem, vmem_limit_bytes=VMEM_LIMIT)


def _sigmoid(x):
    return 1.0 / (1.0 + jnp.exp(-x))


def _dot(a, b):
    return jnp.dot(a, b, preferred_element_type=F32)


def _dot_nt(a, b):
    return lax.dot_general(a, b, (((1,), (1,)), ((), ())), preferred_element_type=F32)


def _dot_tn(a, b):
    return lax.dot_general(a, b, (((0,), (0,)), ((), ())), preferred_element_type=F32)


def _split_bf16(x):
    hi = x.astype(BF16)
    lo = (x - hi.astype(F32)).astype(BF16)
    return hi, lo


def _mm_kernel(x_ref, w_ref, o_ref):
    o_ref[...] = _dot(x_ref[...], w_ref[...]).astype(o_ref.dtype)


def _matmul(x, w, out_dtype, tm, tn):
    m, k = x.shape
    n = w.shape[1]
    assert m % tm == 0 and n % tn == 0
    return pl.pallas_call(
        _mm_kernel,
        grid=(n // tn, m // tm),
        in_specs=[pl.BlockSpec((tm, k), lambda j, i: (i, 0)),
                  pl.BlockSpec((k, tn), lambda j, i: (0, j))],
        out_specs=pl.BlockSpec((tm, tn), lambda j, i: (i, j)),
        out_shape=jax.ShapeDtypeStruct((m, n), out_dtype),
        compiler_params=_cparams(("arbitrary", "arbitrary")),
        name="matmul",
    )(x, w)


def _out_proj_kernel(ya_ref, yb_ref, yc_ref, w_ref, o_ref, wb_sc):
    @pl.when(pl.program_id(1) == 0)
    def _():
        wb_sc[...] = w_ref[0].astype(BF16)

    ka, kb = ya_ref.shape[1], yb_ref.shape[1]
    o_ref[...] = (_dot(ya_ref[...], wb_sc[:ka, :]) + _dot(yb_ref[...], wb_sc[ka:ka + kb, :])
                  + _dot(yc_ref[...], wb_sc[ka + kb:, :]))


def _out_proj(y_a, y_b, y_c, w_out, layer, tm, tn):
    m = y_a.shape[0]
    k, n = w_out.shape[1:]
    ys = [y_a, y_b, y_c]
    return pl.pallas_call(
        _out_proj_kernel,
        grid=(n // tn, m // tm),
        in_specs=[pl.BlockSpec((tm, y.shape[1]), lambda j, i: (i, 0)) for y in ys]
        + [pl.BlockSpec((1, k, tn), lambda j, i: (layer, 0, j))],
        out_specs=pl.BlockSpec((tm, tn), lambda j, i: (i, j)),
        out_shape=jax.ShapeDtypeStruct((m, n), F32),
        scratch_shapes=[pltpu.VMEM((k, tn), BF16)],
        compiler_params=_cparams(("arbitrary", "arbitrary")),
        name="out_proj",
    )(*ys, w_out)


def _res_ln_kernel(res_ref, y_ref, g_ref, b_ref, of_ref, ob_ref, opk_ref):
    t = ALPHA * res_ref[...] + y_ref[...]
    mu = jnp.mean(t, axis=-1, keepdims=True)
    c = t - mu
    var = jnp.mean(c * c, axis=-1, keepdims=True)
    out = c * lax.rsqrt(var + LN_EPS) * g_ref[...] + b_ref[...]
    of_ref[...] = out
    ob_ref[...] = out.astype(BF16)
    opk_ref[...] = _pack_bf16_pairs(out)


def _res_ln(res, y, g, b, tm=256):
    m, d = res.shape
    row = pl.BlockSpec((tm, d), lambda i: (i, 0))
    half = pl.BlockSpec((tm, d // 2), lambda i: (i, 0))
    vec = pl.BlockSpec((1, d), lambda i: (0, 0))
    return pl.pallas_call(
        _res_ln_kernel,
        grid=(m // tm,),
        in_specs=[row, row, vec, vec],
        out_specs=[row, row, half],
        out_shape=[jax.ShapeDtypeStruct((m, d), F32), jax.ShapeDtypeStruct((m, d), BF16),
                   jax.ShapeDtypeStruct((m, d // 2), jnp.uint32)],
        compiler_params=_cparams(("arbitrary",)),
        name="res_ln",
    )(res, y, g.reshape(1, d), b.reshape(1, d))


def _shift_mix(cur, prev8, mu, is_first):
    prev_row = jnp.where(is_first, 0.0, prev8[SUBLANES - 1:SUBLANES, :])
    rolled = pltpu.roll(cur, 1, 0)
    row = lax.broadcasted_iota(jnp.int32, cur.shape, 0)
    shifted = jnp.where(row == 0, prev_row, rolled)
    return cur + (shifted - cur) * mu


def _head_block_ones(width):
    r = lax.broadcasted_iota(jnp.int32, (width, width), 0) // RWKV_HEAD_DIM
    c = lax.broadcasted_iota(jnp.int32, (width, width), 1) // RWKV_HEAD_DIM
    return jnp.where(r == c, 1.0, 0.0).astype(BF16)


def _head_sum(x, ones_bd):
    w = ones_bd.shape[0]
    outs = []
    for c in range(x.shape[1] // w):
        hi, lo = _split_bf16(x[:, c * w:(c + 1) * w])
        outs.append(_dot(hi, ones_bd) + _dot(lo, ones_bd))
    return jnp.concatenate(outs, axis=1)


def _rwkv_prep_body(first, r_ref, k_ref, v_ref, rp_ref, kp_ref, vp_ref,
                    wlo_ref, alo_ref, glo_ref, wlop_ref, alop_ref, glop_ref,
                    mur_ref, muk_ref, muv_ref, muw_ref, mua_ref, mug_ref,
                    w2_ref, a2_ref, g2_ref, w0_ref, a0_ref, kk_ref, ka_ref, rk_ref, vres):
    r = _shift_mix(r_ref[...], rp_ref[...], mur_ref[...], first)
    k = _shift_mix(k_ref[...], kp_ref[...], muk_ref[...], first)
    v = _shift_mix(v_ref[...], vp_ref[...], muv_ref[...], first)
    w_lo = _shift_mix(wlo_ref[...], wlop_ref[...], muw_ref[...], first)
    a_lo = _shift_mix(alo_ref[...], alop_ref[...], mua_ref[...], first)
    g_lo = _shift_mix(glo_ref[...], glop_ref[...], mug_ref[...], first)

    z = w0_ref[...] + _dot(jnp.tanh(w_lo).astype(BF16), w2_ref[...])
    nz = -z
    softplus = jnp.maximum(nz, 0.0) + jnp.log(1.0 + jnp.exp(-jnp.abs(nz)))
    log_w = -softplus - 0.5
    ld = -jnp.exp(log_w)
    a = _sigmoid(a0_ref[...] + _dot(a_lo.astype(BF16), a2_ref[...]))
    g = _dot(_sigmoid(g_lo).astype(BF16), g2_ref[...])
    if vres is not None:
        vlo_ref, vlop_ref, muvl_ref, v2_ref, v0_ref, vf_ref = vres
        v_lo = _shift_mix(vlo_ref[...], vlop_ref[...], muvl_ref[...], first)
        mix = _sigmoid(v0_ref[...] + _dot(v_lo.astype(BF16), v2_ref[...]))
        v = v + (vf_ref[...] - v) * mix

    ones_bd = _head_block_ones(2 * LANES)
    kk = k * kk_ref[...]
    ss = _head_sum(kk * kk, ones_bd)
    kn = kk * lax.rsqrt(jnp.maximum(ss, 1e-24))
    k_mod = k * (1.0 + (a - 1.0) * ka_ref[...])
    bonus = _head_sum(r * k_mod * rk_ref[...], ones_bd) * v
    return r, ld, k_mod, v, kn, kn * a, g, bonus


def _make_rwkv_kernel(has_vres):
    n_common = 26

    def kern(*refs):
        common = refs[:n_common]
        rest = refs[n_common:]
        if has_vres:
            vres, rest = rest[:6], rest[6:]
        else:
            vres = None
        lng_ref, lnb_ref = rest[0], rest[1]
        outs, state_ref = rest[2:-1], rest[-1]
        first = pl.program_id(1) == 0
        r, ld, k, v, kn, b, g, bonus = _rwkv_prep_body(first, *common, vres)
        y = _rwkv_scan_block(r, ld, k, v, kn, b, state_ref)
        outs[0][...] = _rwkv_finish(y, g, bonus, lng_ref[...], lnb_ref[...]).astype(outs[0].dtype)
        if not has_vres:
            outs[1][...] = v

    return kern


def _rwkv_group(proj, mu_p, p, l, v_first, ln_g, ln_b):
    t = proj.shape[0]
    tb = RW_CHUNK * RW_SUB
    has_vres = v_first is not None
    nj = RWKV_WIDTH // RW_CH
    pb = tb // SUBLANES

    def cur(width, col0):
        return pl.BlockSpec((tb, width), lambda j, i: (i, col0 // width))

    def cur_j(col0):
        return pl.BlockSpec((tb, RW_CH), lambda j, i: (i, col0 // RW_CH + j))

    def prev(width, col0):
        return pl.BlockSpec((SUBLANES, width), lambda j, i: (jnp.maximum(i * pb - 1, 0), col0 // width))

    def prev_j(col0):
        return pl.BlockSpec((SUBLANES, RW_CH), lambda j, i: (jnp.maximum(i * pb - 1, 0), col0 // RW_CH + j))

    def vec(width, col0):
        return pl.BlockSpec((1, width), lambda j, i: (0, col0 // width))

    def vec_j(col0=0):
        return pl.BlockSpec((1, RW_CH), lambda j, i: (0, col0 // RW_CH + j))

    def lora(rank):
        return pl.BlockSpec((rank, RW_CH), lambda j, i: (0, j))

    row = lambda a: a.reshape(1, -1)
    g2 = jnp.pad(p["rwkv_g2"][l], ((0, GLO_PAD - GATE_LORA), (0, 0))).astype(BF16)
    args = [proj, proj, proj, proj, proj, proj,
            proj, proj, proj, proj, proj, proj,
            mu_p, mu_p, mu_p, mu_p, mu_p, mu_p,
            p["rwkv_w2"][l].astype(BF16), p["rwkv_a2"][l].astype(BF16), g2,
            row(p["rwkv_w0"][l]), row(p["rwkv_a0"][l]), row(p["rwkv_k_k"][l]), row(p["rwkv_k_a"][l]),
            row(p["rwkv_r_k"][l])]
    in_specs = [cur_j(P_R), cur_j(P_K), cur_j(P_V), prev_j(P_R), prev_j(P_K), prev_j(P_V),
                cur(LANES, P_WLO), cur(LANES, P_ALO), cur(GLO_PAD, P_GLO),
                prev(LANES, P_WLO), prev(LANES, P_ALO), prev(GLO_PAD, P_GLO),
                vec_j(P_R), vec_j(P_K), vec_j(P_V), vec(LANES, P_WLO), vec(LANES, P_ALO), vec(GLO_PAD, P_GLO),
                lora(DECAY_LORA), lora(AAA_LORA), lora(GLO_PAD),
                vec_j(), vec_j(), vec_j(), vec_j(), vec_j()]
    if has_vres:
        v2 = jnp.pad(p["rwkv_v2"][l - 1], ((0, LANES - MV_LORA), (0, 0))).astype(BF16)
        args += [proj, proj, mu_p, v2, row(p["rwkv_v0"][l - 1]), v_first]
        in_specs += [cur(LANES, P_VLO), prev(LANES, P_VLO), vec(LANES, P_VLO), lora(LANES), vec_j(),
                     pl.BlockSpec((tb, RW_CH), lambda j, i: (i, j))]
    args += [row(ln_g), row(ln_b)]
    in_specs += [vec_j(), vec_j()]
    out_spec = pl.BlockSpec((tb, RW_CH), lambda j, i: (i, j))
    out_specs = [out_spec]
    out_shape = [jax.ShapeDtypeStruct((t, RWKV_WIDTH), BF16)]
    if not has_vres:
        out_specs.append(out_spec)
        out_shape.append(jax.ShapeDtypeStruct((t, RWKV_WIDTH), F32))
    outs = pl.pallas_call(
        _make_rwkv_kernel(has_vres),
        grid=(nj, t // tb),
        in_specs=in_specs,
        out_specs=out_specs,
        out_shape=out_shape,
        scratch_shapes=[pltpu.VMEM((RW_CH // RWKV_HEAD_DIM, RWKV_HEAD_DIM, RWKV_HEAD_DIM), F32)],
        compiler_params=_cparams(("arbitrary", "arbitrary")),
        name="rwkv",
    )(*args)
    return outs[0], (v_first if has_vres else outs[1])


def _unit_lower_inverse(a_strict, c):
    row = lax.broadcasted_iota(jnp.int32, (c, c), 0)
    col = lax.broadcasted_iota(jnp.int32, (c, c), 1)
    eye = jnp.where(row == col, 1.0, 0.0)
    bd = lambda x: x.astype(BF16)
    base = SUBLANES
    same_base = row // base == col // base
    d1 = [bd(jnp.where(same_base, a, 0.0)) for a in a_strict]
    d2 = [bd(_dot(d, d)) for d in d1]
    inv = [eye + d.astype(F32) for d in d1]
    inv = [i + _dot(d, bd(i)) for i, d in zip(inv, d2)]
    d4 = [bd(_dot(d, d)) for d in d2]
    inv = [i + _dot(d, bd(i)) for i, d in zip(inv, d4)]
    blk = base
    while blk < c:
        band = (row // (2 * blk) == col // (2 * blk)) & (row // blk != col // blk)
        off = [bd(jnp.where(band, a, 0.0)) for a in a_strict]
        inv_b = [bd(i) for i in inv]
        tmp = [bd(_dot(o, i)) for o, i in zip(off, inv_b)]
        inv = [i + _dot(ib, t) for i, ib, t in zip(inv, inv_b, tmp)]
        blk *= 2
    return inv


def _rwkv_scan_block(r, ld, k, v_all, kn, b, state_ref):
    c = RW_CHUNK
    n = RWKV_HEAD_DIM
    heads = RW_CH // n
    rows = ld.shape[0]
    subs = rows // c

    @pl.when(pl.program_id(1) == 0)
    def _():
        state_ref[...] = jnp.zeros_like(state_ref)

    row = lax.broadcasted_iota(jnp.int32, (c, c), 0)
    col = lax.broadcasted_iota(jnp.int32, (c, c), 1)
    strict = row > col
    incl2 = (lax.broadcasted_iota(jnp.int32, (c, 2 * c), 0)
             >= lax.broadcasted_iota(jnp.int32, (c, 2 * c), 1) % c)
    brow = lax.broadcasted_iota(jnp.int32, (rows, rows), 0)
    bcol = lax.broadcasted_iota(jnp.int32, (rows, rows), 1)
    tri_incl = jnp.where((brow >= bcol) & (brow // c == bcol // c), 1.0, 0.0).astype(BF16)

    ld_hi, ld_lo = _split_bf16(ld)
    cum = _dot(tri_incl, ld_hi) + _dot(tri_incl, ld_lo)
    g_inc = jnp.exp(cum)
    g_inv = jnp.exp(-cum)
    a_t = -kn * jnp.exp(cum - ld)
    b_t = b * g_inv
    k_t = k * g_inv
    r_t = r * g_inc

    bd = lambda x: x.astype(BF16)
    idx = [(s, h) for s in range(subs) for h in range(heads)]
    rs = lambda s: slice(s * c, (s + 1) * c)
    ls = lambda h: slice(h * n, (h + 1) * n)
    v_h = [v_all[rs(s), ls(h)] for s, h in idx]
    a_h = [a_t[rs(s), ls(h)] for s, h in idx]
    r_h = [bd(r_t[rs(s), ls(h)]) for s, h in idx]
    ar = [bd(jnp.concatenate([a_h[i], r_t[rs(s), ls(h)]], axis=0)) for i, (s, h) in enumerate(idx)]
    bk = [bd(jnp.concatenate([b_t[rs(s), ls(h)], k_t[rs(s), ls(h)]], axis=0)) for s, h in idx]
    p1 = [_dot_nt(x, y) for x, y in zip(ar, bk)]
    a_ab = [jnp.where(strict, p[:c, :c], 0.0) for p in p1]
    a_ak = [bd(jnp.where(strict, p[:c, c:], 0.0)) for p in p1]
    a_r = [bd(jnp.where(incl2, p[c:, :], 0.0)) for p in p1]
    akv = [_dot(x, bd(y)) for x, y in zip(a_ak, v_h)]
    inv = _unit_lower_inverse(a_ab, c)
    sol = [_dot(bd(inv[i]), bd(jnp.concatenate([a_h[i], akv[i]], axis=1))) for i in range(len(idx))]

    state = [state_ref[h] for h in range(heads)]
    y_rows = []
    for s in range(subs):
        at = lambda lst, h: lst[s * heads + h]
        sb = [bd(x) for x in state]
        u = [_dot_nt(bd(at(sol, h)[:, :n]), sb[h]) + at(sol, h)[:, n:] for h in range(heads)]
        uv = [bd(jnp.concatenate([u[h], at(v_h, h)], axis=0)) for h in range(heads)]
        ys = [_dot_nt(at(r_h, h), sb[h]) + _dot(at(a_r, h), uv[h]) for h in range(heads)]
        g_last = g_inc[(s + 1) * c - 1:(s + 1) * c, :]
        state = [(state[h] + _dot_tn(uv[h], at(bk, h))) * g_last[:, ls(h)] for h in range(heads)]
        y_rows.append(jnp.concatenate(ys, axis=1))
    for h in range(heads):
        state_ref[h] = state[h]
    return jnp.concatenate(y_rows, axis=0)


def _rwkv_finish(y, g, bonus, ln_g, ln_b):
    n = RWKV_HEAD_DIM
    ones_bd = _head_block_ones(2 * LANES)
    mean = _head_sum(y, ones_bd) * (1.0 / n)
    yc = y - mean
    var = _head_sum(yc * yc, ones_bd) * (1.0 / n)
    yn = yc * lax.rsqrt(var + GN_EPS) * ln_g + ln_b
    return (yn + bonus) * g


def _gelu_tanh(x):
    return 0.5 * x * (1.0 + jnp.tanh(np.sqrt(2.0 / np.pi).astype(np.float32) * (x + 0.044715 * (x * x * x))))


def _gmlp_kernel(u_ref, v_ref, lng_ref, lnb_ref, ws_ref, bs_ref, o_ref, *, chunks):
    ch = GMLP_CHUNK
    gd = GMLP_GROUP_DIM
    u = _gelu_tanh(u_ref[...])
    v = _gelu_tanh(v_ref[...])
    mu = jnp.mean(v, axis=-1, keepdims=True)
    vc = v - mu
    var = jnp.mean(vc * vc, axis=-1, keepdims=True)
    vn = (vc * lax.rsqrt(var + LN_EPS) * lng_ref[...] + lnb_ref[...]).astype(BF16)
    row = lax.broadcasted_iota(jnp.int32, (ch, ch), 0)
    col = lax.broadcasted_iota(jnp.int32, (ch, ch), 1)
    causal = row >= col
    bs = bs_ref[...]
    for g in range(GMLP_GROUPS):
        w = jnp.where(causal, ws_ref[g], 0.0).astype(BF16)
        bias = bs[:, g:g + 1]
        for n in range(chunks):
            f = _dot(w, vn[n * ch:(n + 1) * ch, g * gd:(g + 1) * gd]) + bias
            o_ref[n * ch:(n + 1) * ch, g * gd:(g + 1) * gd] = (
                u[n * ch:(n + 1) * ch, g * gd:(g + 1) * gd] * f).astype(o_ref.dtype)


def _gmlp(proj, ln_g, ln_b, w_s, b_s, tb=256):
    t = proj.shape[0]
    bs_t = jnp.pad(b_s.T, ((0, 0), (0, LANES - GMLP_GROUPS)))
    return pl.pallas_call(
        functools.partial(_gmlp_kernel, chunks=tb // GMLP_CHUNK),
        grid=(t // tb,),
        in_specs=[pl.BlockSpec((tb, GMLP_WIDTH), lambda i: (i, P_GU // GMLP_WIDTH)),
                  pl.BlockSpec((tb, GMLP_WIDTH), lambda i: (i, P_GV // GMLP_WIDTH)),
                  pl.BlockSpec((1, GMLP_WIDTH), lambda i: (0, 0)),
                  pl.BlockSpec((1, GMLP_WIDTH), lambda i: (0, 0)),
                  pl.BlockSpec((GMLP_GROUPS, GMLP_CHUNK, GMLP_CHUNK), lambda i: (0, 0, 0)),
                  pl.BlockSpec((GMLP_CHUNK, LANES), lambda i: (0, 0))],
        out_specs=pl.BlockSpec((tb, GMLP_WIDTH), lambda i: (i, 0)),
        out_shape=jax.ShapeDtypeStruct((t, GMLP_WIDTH), BF16),
        compiler_params=_cparams(("arbitrary",)),
        name="gmlp",
    )(proj, proj, ln_g.reshape(1, -1), ln_b.reshape(1, -1), w_s, bs_t)


def _rope_lanes(t, cc, s1, s2):
    return t * cc + pltpu.roll(t, LANES - MLA_ROPE_DIM // 2, 1) * s1 + pltpu.roll(t, MLA_ROPE_DIM // 2, 1) * s2


def _mla_proj_kernel(cq_ref, ckv_ref, kr_ref, qn_ref, kvn_ref, wq_ref, wk_ref, wvt_ref, cc_ref, s1_ref, s2_ref,
                     q_o, kn_o, vt_o, kr_o):
    cq = cq_ref[...]
    qn = (cq * lax.rsqrt(jnp.mean(cq * cq, axis=-1, keepdims=True) + RMS_EPS) * qn_ref[...]).astype(BF16)
    ckv = ckv_ref[...]
    kvn = (ckv * lax.rsqrt(jnp.mean(ckv * ckv, axis=-1, keepdims=True) + RMS_EPS) * kvn_ref[...]).astype(BF16)
    cc, s1, s2 = cc_ref[...], s1_ref[...], s2_ref[...]
    scale = MLA_QK_DIM ** -0.5 * np.log2(np.e)
    for h in range(MLA_HEADS):
        q = _dot(qn, wq_ref[:, 2 * LANES * h:2 * LANES * (h + 1)]) * scale
        q_o[:, 2 * LANES * h:2 * LANES * h + LANES] = q[:, :LANES].astype(BF16)
        q_o[:, 2 * LANES * h + LANES:2 * LANES * (h + 1)] = _rope_lanes(q[:, LANES:], cc, s1, s2).astype(BF16)
    kn_o[...] = _dot(kvn, wk_ref[...]).astype(BF16)
    vt_o[...] = _dot_nt(wvt_ref[...], kvn).astype(BF16)
    kr_o[...] = _rope_lanes(kr_ref[...], cc, s1, s2).astype(BF16)


def _mla_proj(proj, positions, q_norm, kv_norm, w_uq, w_ukv, tm=512):
    t = proj.shape[0]
    h = MLA_HEADS
    half = MLA_ROPE_DIM // 2
    inv_freq = jnp.power(ROPE_THETA, -jnp.arange(0, MLA_ROPE_DIM, 2, dtype=F32) / MLA_ROPE_DIM)
    ang = positions.reshape(t).astype(F32)[:, None] * inv_freq
    cos, sin = jnp.cos(ang), jnp.sin(ang)
    z = jnp.zeros((t, half), F32)
    cc = jnp.concatenate([cos, cos, z, z], axis=1)
    s1 = jnp.concatenate([-sin, z, z, z], axis=1)
    s2 = jnp.concatenate([z, sin, z, z], axis=1)
    wq = w_uq.reshape(MLA_Q_RANK, h, MLA_QK_DIM)
    wq = jnp.pad(wq, ((0, 0), (0, 0), (0, 2 * LANES - MLA_QK_DIM))).reshape(MLA_Q_RANK, h * 2 * LANES).astype(BF16)
    wkv = w_ukv.reshape(MLA_KV_RANK, h, MLA_NOPE_DIM + MLA_V_DIM)
    wk = wkv[:, :, :MLA_NOPE_DIM].reshape(MLA_KV_RANK, h * MLA_NOPE_DIM).astype(BF16)
    wvt = wkv[:, :, MLA_NOPE_DIM:].reshape(MLA_KV_RANK, h * MLA_V_DIM).T.astype(BF16)
    full = lambda a: pl.BlockSpec(a.shape, lambda i: (0,) * a.ndim)
    tab = pl.BlockSpec((tm, LANES), lambda i: (i, 0))
    qn2, kvn2 = q_norm.reshape(1, -1), kv_norm.reshape(1, -1)
    return pl.pallas_call(
        _mla_proj_kernel,
        grid=(t // tm,),
        in_specs=[pl.BlockSpec((tm, MLA_Q_RANK), lambda i: (i, P_CQ // MLA_Q_RANK)),
                  pl.BlockSpec((tm, MLA_KV_RANK), lambda i: (i, P_CKV // MLA_KV_RANK)),
                  pl.BlockSpec((tm, LANES), lambda i: (i, P_KROPE // LANES)),
                  full(qn2), full(kvn2), full(wq), full(wk), full(wvt), tab, tab, tab],
        out_specs=[pl.BlockSpec((tm, h * 2 * LANES), lambda i: (i, 0)),
                   pl.BlockSpec((tm, h * MLA_NOPE_DIM), lambda i: (i, 0)),
                   pl.BlockSpec((h * MLA_V_DIM, tm), lambda i: (0, i)),
                   pl.BlockSpec((tm, LANES), lambda i: (i, 0))],
        out_shape=[jax.ShapeDtypeStruct((t, h * 2 * LANES), BF16),
                   jax.ShapeDtypeStruct((t, h * MLA_NOPE_DIM), BF16),
                   jax.ShapeDtypeStruct((h * MLA_V_DIM, t), BF16),
                   jax.ShapeDtypeStruct((t, LANES), BF16)],
        compiler_params=_cparams(("arbitrary",)),
        name="mla_proj",
    )(proj, proj, proj, qn2, kvn2, wq, wk, wvt, cc, s1, s2)


def _flash_kernel(qi_ref, kj_ref, q_ref, kn_ref, kr_ref, vt_ref, o_ref, m_sc, l_sc, acc_sc, st_sc, *, tq, tk):
    s = pl.program_id(1)
    qi, kj = qi_ref[s], kj_ref[s]

    @pl.when(kj == 0)
    def _():
        m_sc[...] = jnp.full_like(m_sc, -jnp.inf)
        l_sc[...] = jnp.zeros_like(l_sc)
        acc_sc[...] = jnp.zeros_like(acc_sc)

    qt = 2 * LANES
    kb = 2 * LANES

    def step(masked):
        k = jnp.concatenate([kn_ref[...], kr_ref[...]], axis=1)
        m_all = m_sc[...]
        m_news, alphas = [], []
        for c0 in range(0, tq, qt):
            st = _dot_nt(k, q_ref[c0:c0 + qt, :])
            if masked:
                key = kj * tk + lax.broadcasted_iota(jnp.int32, (tk, qt), 0)
                qry = qi * tq + c0 + lax.broadcasted_iota(jnp.int32, (tk, qt), 1)
                st = jnp.where(key <= qry, st, -jnp.inf)
            st_sc[:, c0:c0 + qt] = st
            m_prev = m_all[:, c0:c0 + qt]
            m_new = jnp.maximum(m_prev, jnp.max(st, axis=0, keepdims=True))
            m_news.append(m_new)
            alphas.append(jnp.exp2(m_prev - m_new))
        pvs, sums = [], []
        for t_i, c0 in enumerate(range(0, tq, qt)):
            pv, ps = None, None
            for r0 in range(0, tk, kb):
                p = jnp.exp2(st_sc[r0:r0 + kb, c0:c0 + qt] - m_news[t_i])
                part = jnp.sum(p, axis=0, keepdims=True)
                prod = _dot(vt_ref[:, r0:r0 + kb], p.astype(BF16))
                ps = part if ps is None else ps + part
                pv = prod if pv is None else pv + prod
            pvs.append(pv)
            sums.append(ps)
        alpha = jnp.concatenate(alphas, axis=1)
        m_sc[...] = jnp.concatenate(m_news, axis=1)
        l_sc[...] = alpha * l_sc[...] + jnp.concatenate(sums, axis=1)
        acc_sc[...] = alpha * acc_sc[...] + jnp.concatenate(pvs, axis=1)

    last_key_of_block = kj * tk + tk - 1
    on_diag = last_key_of_block > qi * tq

    @pl.when(jnp.logical_not(on_diag))
    def _():
        step(False)

    @pl.when(on_diag)
    def _():
        step(True)

    @pl.when(last_key_of_block >= qi * tq + tq - 1)
    def _():
        o_ref[...] = (acc_sc[...] / l_sc[...]).T.astype(o_ref.dtype)


def _flash(q, kn, vt, kr, tq=512, tk=512):
    t = q.shape[0]
    assert tq % tk == 0
    pairs = [(i, j) for i in range(t // tq) for j in range((i + 1) * tq // tk)]
    qi = jnp.asarray([pr[0] for pr in pairs], jnp.int32)
    kj = jnp.asarray([pr[1] for pr in pairs], jnp.int32)
    grid_spec = pltpu.PrefetchScalarGridSpec(
        num_scalar_prefetch=2,
        grid=(MLA_HEADS, len(pairs)),
        in_specs=[pl.BlockSpec((tq, 2 * LANES), lambda h, s, qi, kj: (qi[s], h)),
                  pl.BlockSpec((tk, MLA_NOPE_DIM), lambda h, s, qi, kj: (kj[s], h)),
                  pl.BlockSpec((tk, LANES), lambda h, s, qi, kj: (kj[s], 0)),
                  pl.BlockSpec((MLA_V_DIM, tk), lambda h, s, qi, kj: (h, kj[s]))],
        out_specs=pl.BlockSpec((tq, MLA_V_DIM), lambda h, s, qi, kj: (qi[s], h)),
        scratch_shapes=[pltpu.VMEM((1, tq), F32), pltpu.VMEM((1, tq), F32), pltpu.VMEM((MLA_V_DIM, tq), F32),
                        pltpu.VMEM((tk, tq), F32)],
    )
    return pl.pallas_call(
        functools.partial(_flash_kernel, tq=tq, tk=tk),
        grid_spec=grid_spec,
        out_shape=jax.ShapeDtypeStruct((t, MLA_HEADS * MLA_V_DIM), BF16),
        compiler_params=_cparams(("arbitrary", "arbitrary")),
        name="flash",
    )(qi, kj, q, kn, kr, vt)


def _router_kernel(h_ref, rwt_ref, bias_ref, exp_o, pos_o, gate_o, cnt_o, carry_sc):
    e = N_EXPERTS
    per = e // N_EXPERT_GROUPS
    h_hi, h_lo = _split_bf16(h_ref[...])
    w_hi, w_lo = _split_bf16(rwt_ref[...])
    logits = _dot_nt(w_hi, h_hi) + _dot_nt(w_hi, h_lo) + _dot_nt(w_lo, h_hi)
    scores = _sigmoid(logits)
    biased = scores + bias_ref[...][:, 0:1]
    tb = biased.shape[1]
    neg = -jnp.inf
    sub = lax.broadcasted_iota(jnp.int32, (per, tb), 0)
    grp_rows = []
    for g in range(N_EXPERT_GROUPS):
        blk = biased[g * per:(g + 1) * per, :]
        m1 = jnp.max(blk, axis=0, keepdims=True)
        first = jnp.min(jnp.where(blk == m1, sub, per), axis=0, keepdims=True)
        m2 = jnp.max(jnp.where(sub == first, neg, blk), axis=0, keepdims=True)
        grp_rows.append(m1 + m2)
    grp = jnp.concatenate(grp_rows, axis=0)
    gidx = lax.broadcasted_iota(jnp.int32, grp.shape, 0)
    grank = jnp.zeros(grp.shape, jnp.int32)
    for g in range(N_EXPERT_GROUPS):
        other = grp[g:g + 1, :]
        ahead = (other > grp) | ((other == grp) & (g < gidx))
        grank = grank + jnp.where(ahead, 1, 0)
    gsel = grank < TOPK_GROUPS
    masked = jnp.concatenate(
        [jnp.where(gsel[g:g + 1, :], biased[g * per:(g + 1) * per, :], neg) for g in range(N_EXPERT_GROUPS)], axis=0)
    eidx = lax.broadcasted_iota(jnp.int32, masked.shape, 0)
    rank = jnp.zeros(masked.shape, jnp.int32)
    for j in range(e):
        other = masked[j:j + 1, :]
        ahead = (other > masked) | ((other == masked) & (j < eidx))
        rank = rank + jnp.where(ahead, 1, 0)
    chosen = rank < TOP_K
    sel = jnp.where(chosen, scores, 0.0)
    gate = sel / jnp.sum(sel, axis=0, keepdims=True) * ROUTED_SCALE

    @pl.when(pl.program_id(0) == 0)
    def _():
        carry_sc[...] = jnp.zeros_like(carry_sc)

    chosen_f = jnp.where(chosen, 1.0, 0.0)
    earlier = (lax.broadcasted_iota(jnp.int32, (tb, tb), 0) < lax.broadcasted_iota(jnp.int32, (tb, tb), 1))
    carry = carry_sc[...]
    pos = _dot(chosen_f.astype(BF16), jnp.where(earlier, 1.0, 0.0).astype(BF16)) + carry[:, 0:1]
    carry_sc[...] = carry + jnp.sum(chosen_f, axis=1, keepdims=True)
    cnt_o[...] = carry_sc[...].astype(jnp.int32)

    eidx_f = eidx.astype(F32)
    pick = lambda hit, val: jnp.sum(jnp.where(hit, val, 0.0), axis=0, keepdims=True)
    hits = [rank == k for k in range(TOP_K)]
    exp_o[...] = jnp.concatenate([pick(hit, eidx_f) for hit in hits], axis=0).astype(jnp.int32)
    pos_o[...] = jnp.concatenate([pick(hit, pos) for hit in hits], axis=0).astype(jnp.int32)
    gate_o[...] = jnp.concatenate([pick(hit, gate) for hit in hits], axis=0)


def _router(h, router_w, router_bias, tb=512):
    t, d = h.shape
    bias = jnp.broadcast_to(router_bias.astype(F32)[:, None], (N_EXPERTS, LANES))
    per_tok = pl.BlockSpec((TOP_K, tb), lambda i: (0, i))
    return pl.pallas_call(
        _router_kernel,
        grid=(t // tb,),
        in_specs=[pl.BlockSpec((tb, d), lambda i: (i, 0)),
                  pl.BlockSpec((N_EXPERTS, d), lambda i: (0, 0)),
                  pl.BlockSpec((N_EXPERTS, LANES), lambda i: (0, 0))],
        out_specs=[per_tok, per_tok, per_tok, pl.BlockSpec((N_EXPERTS, LANES), lambda i: (0, 0))],
        out_shape=[jax.ShapeDtypeStruct((TOP_K, t), jnp.int32), jax.ShapeDtypeStruct((TOP_K, t), jnp.int32),
                   jax.ShapeDtypeStruct((TOP_K, t), F32), jax.ShapeDtypeStruct((N_EXPERTS, LANES), jnp.int32)],
        scratch_shapes=[pltpu.VMEM((N_EXPERTS, LANES), F32)],
        compiler_params=_cparams(("arbitrary",)),
        name="router",
    )(h, router_w.T, bias)


HALF_MASK = 0xFFFF0000


def _pack_bf16_pairs(x):
    n = x.shape[1] // 2
    lo = pltpu.bitcast(x[:, :n].astype(BF16).astype(F32), jnp.uint32) >> 16
    hi = pltpu.bitcast(x[:, n:].astype(BF16).astype(F32), jnp.uint32) & jnp.uint32(HALF_MASK)
    return lo | hi


def _unpack_bf16_pairs(w):
    return pltpu.bitcast(w << 16, F32), pltpu.bitcast(w & jnp.uint32(HALF_MASK), F32)


def _dispatch_plan(exp_r, pos_r, cnt, bm, nb_max):
    ids = jnp.arange(N_EXPERTS, dtype=jnp.int32)
    nb = (cnt + bm - 1) // bm
    bend = jnp.sum(jnp.where(ids[None, :] <= ids[:, None], nb[None, :], 0), axis=1)
    total = bend[-1]
    slot_start = (bend - nb) * bm
    start_of = jnp.sum(jnp.where(exp_r[:, :, None] == ids, slot_start, 0), axis=-1)
    slot = (start_of + pos_r).T.reshape(-1)
    bidx = jnp.arange(nb_max, dtype=jnp.int32)
    blocks = jnp.minimum(bidx, total - 1)
    blk_exp = jnp.sum(jnp.where(bend[None, :] <= blocks[:, None], 1, 0), axis=1)
    first = jnp.where((bidx == 0) | (blk_exp != jnp.roll(blk_exp, 1)), 1, 0)
    turn = jnp.sum(jnp.where(bidx[None, :] <= bidx[:, None], first[None, :], 0), axis=1) - 1
    later_first = jnp.where((bidx[None, :] > bidx[:, None]) & (first[None, :] == 1), bidx[None, :], nb_max)
    nxt_blk = jnp.min(later_first, axis=1)
    nxt_exp = jnp.sum(jnp.where(bidx[None, :] == nxt_blk[:, None], blk_exp[None, :] + 1, 0), axis=1) - 1
    meta = jnp.concatenate([blk_exp, first, turn % 2, nxt_exp]).astype(jnp.int32)
    lo = jnp.concatenate([slot_start + cnt, (total * bm).reshape(1)])
    hi = jnp.concatenate([slot_start + nb * bm, jnp.full((1,), nb_max * bm, jnp.int32)])
    gaps = jnp.stack([lo, hi], axis=1).reshape(-1).astype(jnp.int32)
    return slot.astype(jnp.int32), meta, total.reshape(1).astype(jnp.int32), gaps


ZERO_ROWS = 256


def _dispatch_kernel(slot_ref, gaps_ref, h_ref, xs_hbm, zero_sc, sem, zsem, *, tb, steps, n_gaps):
    i = pl.program_id(0)

    def fill_gaps(act):
        def gap(r, carry):
            lo, hi = gaps_ref[2 * r], gaps_ref[2 * r + 1]
            lo_tile = jnp.minimum((lo + SUBLANES - 1) // SUBLANES * SUBLANES, hi)

            def single(row, carry2):
                act(pltpu.make_async_copy(zero_sc.at[pl.ds(0, 1), :], xs_hbm.at[pl.ds(row, 1), :], zsem))
                return carry2

            lax.fori_loop(lo, lo_tile, single, 0)
            whole = (hi - lo_tile) // ZERO_ROWS

            def chunk(c, carry2):
                start = pl.multiple_of(lo_tile + c * ZERO_ROWS, SUBLANES)
                act(pltpu.make_async_copy(zero_sc, xs_hbm.at[pl.ds(start, ZERO_ROWS), :], zsem))
                return carry2

            lax.fori_loop(0, whole, chunk, 0)
            off = lo_tile + whole * ZERO_ROWS
            rem = hi - off
            size = ZERO_ROWS // 2
            while size >= SUBLANES:
                take = (rem & size) != 0

                @pl.when(take)
                def _(off=off, size=size):
                    start = pl.multiple_of(off, SUBLANES)
                    act(pltpu.make_async_copy(zero_sc.at[pl.ds(0, size), :], xs_hbm.at[pl.ds(start, size), :], zsem))

                off = off + jnp.where(take, size, 0)
                size //= 2
            return carry

        lax.fori_loop(0, n_gaps, gap, 0)

    @pl.when(i == 0)
    def _():
        zero_sc[...] = jnp.zeros_like(zero_sc)
        fill_gaps(lambda c: c.start())

    def body(t, carry):
        for k in range(TOP_K):
            s = slot_ref[t * TOP_K + k]
            pltpu.make_async_copy(h_ref.at[pl.ds(t, 1), :], xs_hbm.at[pl.ds(s, 1), :], sem).start()
        return carry

    lax.fori_loop(0, tb, body, 0)
    rows = tb * TOP_K
    pltpu.make_async_copy(xs_hbm.at[pl.ds(0, rows), :], xs_hbm.at[pl.ds(0, rows), :], sem).wait()

    @pl.when(i == steps - 1)
    def _():
        fill_gaps(lambda c: c.wait())


def _dispatch(h_pk, slot, gaps, n_slots, tb):
    t, w = h_pk.shape
    steps = t // tb
    return pl.pallas_call(
        functools.partial(_dispatch_kernel, tb=tb, steps=steps, n_gaps=gaps.shape[0] // 2),
        grid=(steps,),
        in_specs=[pl.BlockSpec((tb * TOP_K,), lambda i: (i,), memory_space=pltpu.SMEM),
                  pl.BlockSpec(memory_space=pltpu.SMEM),
                  pl.BlockSpec((tb, w), lambda i: (i, 0))],
        out_specs=pl.BlockSpec(memory_space=pl.ANY),
        out_shape=jax.ShapeDtypeStruct((n_slots, w), jnp.uint32),
        scratch_shapes=[pltpu.VMEM((ZERO_ROWS, w), jnp.uint32), pltpu.SemaphoreType.DMA(()),
                        pltpu.SemaphoreType.DMA(())],
        compiler_params=pltpu.CompilerParams(dimension_semantics=("arbitrary",), vmem_limit_bytes=VMEM_LIMIT,
                                             has_side_effects=True, disable_bounds_checks=True),
        name="moe_dispatch",
    )(slot, gaps, h_pk)


def _expert_meta(meta_ref, nb):
    b = pl.program_id(0)
    return meta_ref[b], meta_ref[nb + b] == 1, meta_ref[2 * nb + b], meta_ref[3 * nb + b]


def _expert_weight_turn(meta_ref, nb, layer, hbm_refs, stage_ref, sem, cast_to):
    e, first, par, nxt = _expert_meta(meta_ref, nb)

    def copies(expert, half):
        return [pltpu.make_async_copy(w.at[layer, expert], stage_ref.at[half, n], sem.at[half, n])
                for n, w in enumerate(hbm_refs)]

    @pl.when(pl.program_id(0) == 0)
    def _():
        for c in copies(e, par):
            c.start()

    @pl.when(first)
    def _():
        for c in copies(e, par):
            c.wait()
        cast_to(stage_ref.at[par])

        @pl.when(nxt >= 0)
        def _():
            for c in copies(nxt, 1 - par):
                c.start()


def _expert_up_kernel(meta_ref, tot_ref, xs_ref, w1_hbm, w3_hbm, act_ref, stage_sc, w13_sc, sem, *, nb, layer):
    def cast_to(staged):
        w13_sc[:, :D_EXPERT] = staged[0].astype(BF16)
        w13_sc[:, D_EXPERT:] = staged[1].astype(BF16)

    _expert_weight_turn(meta_ref, nb, layer, [w1_hbm, w3_hbm], stage_sc, sem, cast_to)

    @pl.when(pl.program_id(0) < tot_ref[0])
    def _():
        lo, hi = _unpack_bf16_pairs(xs_ref[...])
        x = jnp.concatenate([lo.astype(BF16), hi.astype(BF16)], axis=1)
        hgu = _dot(x, w13_sc[...])
        hg, hu = hgu[:, :D_EXPERT], hgu[:, D_EXPERT:]
        act_ref[...] = (hg * _sigmoid(hg) * hu).astype(BF16)

    @pl.when(pl.program_id(0) >= tot_ref[0])
    def _():
        act_ref[...] = jnp.zeros_like(act_ref)


def _expert_down_kernel(meta_ref, tot_ref, act_ref, w2_hbm, ys_ref, stage_sc, w2_sc, sem, *, nb, layer):
    def cast_to(staged):
        w2_sc[...] = staged[0].astype(BF16)

    _expert_weight_turn(meta_ref, nb, layer, [w2_hbm], stage_sc, sem, cast_to)

    @pl.when(pl.program_id(0) < tot_ref[0])
    def _():
        ys_ref[...] = _pack_bf16_pairs(_dot(act_ref[...], w2_sc[...]))

    @pl.when(pl.program_id(0) >= tot_ref[0])
    def _():
        ys_ref[...] = jnp.zeros_like(ys_ref)


def _experts(xs, meta, total, w1, w3, w2, layer, bm):
    n_slots, w = xs.shape
    d = 2 * w
    nb = n_slots // bm
    used = lambda b, meta, tot: (jnp.minimum(b, tot[0] - 1), 0)
    every = lambda b, meta, tot: (b, 0)
    hbm = pl.BlockSpec(memory_space=pl.ANY)
    act = pl.pallas_call(
        functools.partial(_expert_up_kernel, nb=nb, layer=layer),
        grid_spec=pltpu.PrefetchScalarGridSpec(
            num_scalar_prefetch=2,
            grid=(nb,),
            in_specs=[pl.BlockSpec((bm, w), used), hbm, hbm],
            out_specs=pl.BlockSpec((bm, D_EXPERT), every),
            scratch_shapes=[pltpu.VMEM((2, 2, d, D_EXPERT), F32), pltpu.VMEM((d, 2 * D_EXPERT), BF16),
                            pltpu.SemaphoreType.DMA((2, 2))]),
        out_shape=jax.ShapeDtypeStruct((n_slots, D_EXPERT), BF16),
        compiler_params=_cparams(("arbitrary",)),
        name="moe_up",
    )(meta, total, xs, w1, w3)
    return pl.pallas_call(
        functools.partial(_expert_down_kernel, nb=nb, layer=layer),
        grid_spec=pltpu.PrefetchScalarGridSpec(
            num_scalar_prefetch=2,
            grid=(nb,),
            in_specs=[pl.BlockSpec((bm, D_EXPERT), used), hbm],
            out_specs=pl.BlockSpec((bm, w), every),
            scratch_shapes=[pltpu.VMEM((2, 1, D_EXPERT, d), F32), pltpu.VMEM((D_EXPERT, d), BF16),
                            pltpu.SemaphoreType.DMA((2, 1))]),
        out_shape=jax.ShapeDtypeStruct((n_slots, w), jnp.uint32),
        compiler_params=_cparams(("arbitrary",)),
        name="moe_down",
    )(meta, total, act, w2)


def _combine_kernel(slot_ref, slot_next_ref, gate_ref, h_ref, hb_ref, sw13_ref, sw2_ref, g_ref, b_ref, ys_hbm,
                    of_ref, ob_ref, rows_a, rows_b, shared_sc, sem, *, tb, steps):
    i = pl.program_id(0)
    grp = 2 * SUBLANES

    def issue(table_ref, buf, buf_sem, t0, first=0, count=grp):
        for j in range(first, first + count):
            for k in range(TOP_K):
                s = table_ref[(t0 + j) * TOP_K + k]
                pltpu.make_async_copy(ys_hbm.at[pl.ds(s, 1), :], buf.at[k, pl.ds(t0 + j, 1), :], buf_sem).start()

    def wait_block(buf, buf_sem):
        pltpu.make_async_copy(buf, buf, buf_sem).wait()

    @pl.when(i == 0)
    def _():
        def first(g, carry):
            issue(slot_ref, rows_a, sem.at[0], g * grp)
            return carry

        lax.fori_loop(0, tb // grp, first, 0)

    hgu = _dot(hb_ref[...], sw13_ref[...])
    hg, hu = hgu[:, :D_EXPERT], hgu[:, D_EXPERT:]
    shared_sc[...] = _dot((hg * _sigmoid(hg) * hu).astype(BF16), sw2_ref[...])

    def run(cur, cur_sem, nxt, nxt_sem):
        wait_block(cur, cur_sem)

        def group(g, carry):
            r0 = pl.multiple_of(g * grp, grp)
            rows = pl.ds(r0, grp)
            gate = gate_ref[rows, :]
            acc_lo = jnp.zeros((grp, cur.shape[-1]), F32)
            acc_hi = jnp.zeros((grp, cur.shape[-1]), F32)
            per_k = grp // TOP_K
            for k in range(TOP_K):
                issue(slot_next_ref, nxt, nxt_sem, r0, k * per_k, per_k)
                lo, hi = _unpack_bf16_pairs(cur[k, rows, :])
                gk = gate[:, k:k + 1]
                acc_lo = acc_lo + gk * lo
                acc_hi = acc_hi + gk * hi
            t = ALPHA * h_ref[rows, :] + shared_sc[rows, :] + jnp.concatenate([acc_lo, acc_hi], axis=1)
            mu = jnp.mean(t, axis=-1, keepdims=True)
            c = t - mu
            var = jnp.mean(c * c, axis=-1, keepdims=True)
            out = c * lax.rsqrt(var + LN_EPS) * g_ref[...] + b_ref[...]
            of_ref[rows, :] = out
            ob_ref[rows, :] = out.astype(BF16)
            return carry

        lax.fori_loop(0, tb // grp, group, 0)

        @pl.when(i == steps - 1)
        def _():
            wait_block(nxt, nxt_sem)

    even = lax.rem(i, 2) == 0

    @pl.when(even)
    def _():
        run(rows_a, sem.at[0], rows_b, sem.at[1])

    @pl.when(jnp.logical_not(even))
    def _():
        run(rows_b, sem.at[1], rows_a, sem.at[0])


def _combine(ys, slot, gate_tk, hf, hb, sw1, sw3, sw2, g, b, tb):
    t, d = hf.shape
    steps = t // tb
    sw13 = jnp.concatenate([sw1, sw3], axis=1).astype(BF16)
    row = pl.BlockSpec((tb, d), lambda i: (i, 0))
    full = lambda a: pl.BlockSpec(a.shape, lambda i: (0,) * a.ndim)
    g2, b2, sw2b = g.reshape(1, d), b.reshape(1, d), sw2.astype(BF16)
    return pl.pallas_call(
        functools.partial(_combine_kernel, tb=tb, steps=steps),
        grid=(steps,),
        in_specs=[pl.BlockSpec((tb * TOP_K,), lambda i: (i,), memory_space=pltpu.SMEM),
                  pl.BlockSpec((tb * TOP_K,), lambda i: (jnp.minimum(i + 1, steps - 1),), memory_space=pltpu.SMEM),
                  pl.BlockSpec((tb, TOP_K), lambda i: (i, 0)),
                  row, row, full(sw13), full(sw2b), full(g2), full(b2),
                  pl.BlockSpec(memory_space=pl.ANY)],
        out_specs=[row, row],
        out_shape=[jax.ShapeDtypeStruct((t, d), F32), jax.ShapeDtypeStruct((t, d), BF16)],
        scratch_shapes=[pltpu.VMEM((TOP_K, tb, ys.shape[1]), jnp.uint32),
                        pltpu.VMEM((TOP_K, tb, ys.shape[1]), jnp.uint32),
                        pltpu.VMEM((tb, d), F32), pltpu.SemaphoreType.DMA((2,))],
        compiler_params=pltpu.CompilerParams(dimension_semantics=("arbitrary",), vmem_limit_bytes=VMEM_LIMIT,
                                             disable_bounds_checks=True),
        name="moe_combine",
    )(slot, slot, gate_tk, hf, hb, sw13, sw2b, g2, b2, ys)


def _moe_ffn(hf, hb, h_pk, router_w, router_bias, w1, w3, w2, layer, sw1, sw3, sw2, ln_g, ln_b, bm):
    t = hf.shape[0]
    exp_r, pos_r, gate_r, cnt = _router(hf, router_w, router_bias, tb=min(512, t))
    nb_max = t * TOP_K // bm + N_EXPERTS
    slot, meta, total, gaps = _dispatch_plan(exp_r, pos_r, cnt[:, 0], bm, nb_max)
    xs = _dispatch(h_pk, slot, gaps, nb_max * bm, tb=min(256, t // TOP_K))
    ys = _experts(xs, meta, total, w1, w3, w2, layer, bm)
    return _combine(ys, slot, gate_r.T, hf, hb, sw1, sw3, sw2, ln_g, ln_b, tb=min(128, t))


def _w_in_layout_kernel(w_ref, o_ref, *, segments):
    o_ref[...] = jnp.zeros_like(o_ref)
    w = w_ref if len(w_ref.shape) == 2 else w_ref.at[0]
    for src, dst, width in segments:
        o_ref[:, dst:dst + width] = w[:, src:src + width].astype(BF16)


def _prep_w_in(w_in, index, has_vres, tr=256):
    d, cols = w_in.shape[-2:]
    if w_in.ndim == 2:
        w_spec = pl.BlockSpec((tr, cols), lambda i: (i, 0))
    else:
        w_spec = pl.BlockSpec((1, tr, cols), lambda i: (index, i, 0))
    sizes = [RWKV_WIDTH, RWKV_WIDTH, RWKV_WIDTH, DECAY_LORA, AAA_LORA, GATE_LORA,
             GMLP_WIDTH, GMLP_WIDTH, MLA_Q_RANK, MLA_KV_RANK, MLA_ROPE_DIM]
    dsts = [P_R, P_K, P_V, P_WLO, P_ALO, P_GLO, P_GU, P_GV, P_CQ, P_CKV, P_KROPE]
    if has_vres:
        sizes.append(MV_LORA)
        dsts.append(P_VLO)
    srcs = np.concatenate([[0], np.cumsum(sizes)])[:-1]
    segments = tuple((int(s), int(t), int(n)) for s, t, n in zip(srcs, dsts, sizes))
    return pl.pallas_call(
        functools.partial(_w_in_layout_kernel, segments=segments),
        grid=(d // tr,),
        in_specs=[w_spec],
        out_specs=pl.BlockSpec((tr, P_COLS), lambda i: (i, 0)),
        out_shape=jax.ShapeDtypeStruct((d, P_COLS), BF16),
        compiler_params=_cparams(("arbitrary",)),
        name="w_in_layout",
    )(w_in)


def _prep_mu(mu, mu_vres):
    out = jnp.zeros((1, P_COLS), F32)
    offs = np.concatenate([[0], np.cumsum([RWKV_WIDTH] * 3 + [DECAY_LORA, AAA_LORA, GATE_LORA])])
    for dst, i in zip([P_R, P_K, P_V, P_WLO, P_ALO, P_GLO], range(6)):
        out = lax.dynamic_update_slice(out, mu[offs[i]:offs[i + 1]].reshape(1, -1), (0, dst))
    if mu_vres is not None:
        out = lax.dynamic_update_slice(out, mu_vres.reshape(1, -1), (0, P_VLO))
    return out


def kernel(x, positions, w_in_first, w_in_rest, rwkv_mu, rwkv_mu_vres, rwkv_w0, rwkv_w2, rwkv_a0, rwkv_a2, rwkv_v0, rwkv_v2, rwkv_g2, rwkv_k_k, rwkv_k_a, rwkv_r_k, rwkv_ln_g, rwkv_ln_b, gmlp_ln_g, gmlp_ln_b, gmlp_w_s, gmlp_b_s, mla_q_norm, mla_kv_norm, mla_w_uq, mla_w_ukv, w_out, ln1_g, ln1_b, router_w, router_bias, exp_w1, exp_w3, exp_w2, shared_w1, shared_w3, shared_w2, ln2_g, ln2_b):
    b, s, d = x.shape
    t = b * s
    p = dict(rwkv_w0=rwkv_w0, rwkv_w2=rwkv_w2, rwkv_a0=rwkv_a0, rwkv_a2=rwkv_a2, rwkv_v0=rwkv_v0, rwkv_v2=rwkv_v2,
             rwkv_g2=rwkv_g2, rwkv_k_k=rwkv_k_k, rwkv_k_a=rwkv_k_a, rwkv_r_k=rwkv_r_k)
    xf = x.reshape(t, d)
    xb = xf.astype(BF16)
    v_first = None
    for l in range(DEPTH):
        has_vres = l > 0
        w_in = _prep_w_in(w_in_first, 0, False) if l == 0 else _prep_w_in(w_in_rest, l - 1, True)
        mu_p = _prep_mu(rwkv_mu[l], rwkv_mu_vres[l - 1] if has_vres else None)
        proj = _matmul(xb, w_in, F32, tm=min(1024, t), tn=1280)
        y_a, v_first = _rwkv_group(proj, mu_p, p, l, v_first, rwkv_ln_g[l], rwkv_ln_b[l])
        y_b = _gmlp(proj, gmlp_ln_g[l], gmlp_ln_b[l], gmlp_w_s[l], gmlp_b_s[l], tb=min(256, t))
        q, kn, vt, kr = _mla_proj(proj, positions, mla_q_norm[l], mla_kv_norm[l], mla_w_uq[l], mla_w_ukv[l],
                                  tm=min(512, t))
        y_c = _flash(q, kn, vt, kr, tq=min(1024, t), tk=min(1024, t))
        mix = _out_proj(y_a, y_b, y_c, w_out, l, tm=min(512, t), tn=1024)
        hf, hb, h_pk = _res_ln(xf, mix, ln1_g[l], ln1_b[l], tm=min(256, t))
        xf, xb = _moe_ffn(hf, hb, h_pk, router_w[l], router_bias[l],
                          exp_w1, exp_w3, exp_w2, l,
                          shared_w1[l], shared_w3[l], shared_w2[l], ln2_g[l], ln2_b[l], bm=MOE_ROW_BLOCK)
    return xf.reshape(b, s, d)
```

```python
import functools

import jax
import jax.numpy as jnp
import numpy as np
from jax import lax
from jax.experimental import pallas as pl
from jax.experimental.pallas import tpu as pltpu

F32 = jnp.float32
BF16 = jnp.bfloat16

D_MODEL = 4096
DEPTH = 2
RWKV_HEAD_DIM = 64
RWKV_WIDTH = 3 * D_MODEL // 8
DECAY_LORA = 128
AAA_LORA = 128
MV_LORA = 96
GATE_LORA = 480
GN_EPS = 64e-5
GMLP_WIDTH = D_MODEL // 4
GMLP_GROUP_DIM = 128
GMLP_GROUPS = GMLP_WIDTH // GMLP_GROUP_DIM
GMLP_CHUNK = 128
MLA_V_DIM = 128
MLA_WIDTH = D_MODEL - RWKV_WIDTH - GMLP_WIDTH
MLA_HEADS = MLA_WIDTH // MLA_V_DIM
MLA_NOPE_DIM = 128
MLA_ROPE_DIM = 64
MLA_QK_DIM = MLA_NOPE_DIM + MLA_ROPE_DIM
MLA_Q_RANK = 768
MLA_KV_RANK = 512
ROPE_THETA = 10000.0
N_EXPERTS = 64
TOP_K = 8
N_EXPERT_GROUPS = 8
TOPK_GROUPS = 4
D_EXPERT = 384
ROUTED_SCALE = 2.5
ALPHA = (2 * DEPTH) ** 0.25
LN_EPS = 1e-5
RMS_EPS = 1e-6

LANES = 128
SUBLANES = 8
VMEM_LIMIT = 56 * 1024 * 1024

P_GU, P_GV = 0, 1024
P_R, P_K, P_V = 2048, 3584, 5120
P_WLO, P_ALO = 6656, 6784
P_CQ, P_CKV = 6912, 7680
P_GLO, P_KROPE, P_VLO = 8192, 8704, 8832
P_COLS = 8960
GLO_PAD = 512
RW_CH = 512
RW_CHUNK = 64
RW_SUB = 4
MOE_ROW_BLOCK = 512


def _cparams(sem):
    return pltpu.CompilerParams(dimension_semantics=sem, vmem_limit_bytes=VMEM_LIMIT)


def _sigmoid(x):
    return 1.0 / (1.0 + jnp.exp(-x))


def _dot(a, b):
    return jnp.dot(a, b, preferred_element_type=F32)


def _dot_nt(a, b):
    return lax.dot_general(a, b, (((1,), (1,)), ((), ())), preferred_element_type=F32)


def _dot_tn(a, b):
    return lax.dot_general(a, b, (((0,), (0,)), ((), ())), preferred_element_type=F32)


def _split_bf16(x):
    hi = x.astype(BF16)
    lo = (x - hi.astype(F32)).astype(BF16)
    return hi, lo


def _mm_kernel(x_ref, w_ref, o_ref):
    o_ref[...] = _dot(x_ref[...], w_ref[...]).astype(o_ref.dtype)


def _matmul(x, w, out_dtype, tm, tn):
    m, k = x.shape
    n = w.shape[1]
    assert m % tm == 0 and n % tn == 0
    return pl.pallas_call(
        _mm_kernel,
        grid=(n // tn, m // tm),
        in_specs=[pl.BlockSpec((tm, k), lambda j, i: (i, 0)),
                  pl.BlockSpec((k, tn), lambda j, i: (0, j))],
        out_specs=pl.BlockSpec((tm, tn), lambda j, i: (i, j)),
        out_shape=jax.ShapeDtypeStruct((m, n), out_dtype),
        compiler_params=_cparams(("arbitrary", "arbitrary")),
        name="matmul",
    )(x, w)


def _out_proj_kernel(ya_ref, yb_ref, yc_ref, w_ref, o_ref, wb_sc):
    @pl.when(pl.program_id(1) == 0)
    def _():
        wb_sc[...] = w_ref[0].astype(BF16)

    ka, kb = ya_ref.shape[1], yb_ref.shape[1]
    o_ref[...] = (_dot(ya_ref[...], wb_sc[:ka, :]) + _dot(yb_ref[...], wb_sc[ka:ka + kb, :])
                  + _dot(yc_ref[...], wb_sc[ka + kb:, :]))


def _out_proj(y_a, y_b, y_c, w_out, layer, tm, tn):
    m = y_a.shape[0]
    k, n = w_out.shape[1:]
    ys = [y_a, y_b, y_c]
    return pl.pallas_call(
        _out_proj_kernel,
        grid=(n // tn, m // tm),
        in_specs=[pl.BlockSpec((tm, y.shape[1]), lambda j, i: (i, 0)) for y in ys]
        + [pl.BlockSpec((1, k, tn), lambda j, i: (layer, 0, j))],
        out_specs=pl.BlockSpec((tm, tn), lambda j, i: (i, j)),
        out_shape=jax.ShapeDtypeStruct((m, n), F32),
        scratch_shapes=[pltpu.VMEM((k, tn), BF16)],
        compiler_params=_cparams(("arbitrary", "arbitrary")),
        name="out_proj",
    )(*ys, w_out)


def _res_ln_kernel(res_ref, y_ref, g_ref, b_ref, of_ref, opk_ref):
    t = ALPHA * res_ref[...] + y_ref[...]
    mu = jnp.mean(t, axis=-1, keepdims=True)
    c = t - mu
    var = jnp.mean(c * c, axis=-1, keepdims=True)
    out = c * lax.rsqrt(var + LN_EPS) * g_ref[...] + b_ref[...]
    of_ref[...] = out
    opk_ref[...] = _pack_bf16_pairs(out)


def _res_ln(res, y, g, b, tm=256):
    m, d = res.shape
    row = pl.BlockSpec((tm, d), lambda i: (i, 0))
    half = pl.BlockSpec((tm, d // 2), lambda i: (i, 0))
    vec = pl.BlockSpec((1, d), lambda i: (0, 0))
    return pl.pallas_call(
        _res_ln_kernel,
        grid=(m // tm,),
        in_specs=[row, row, vec, vec],
        out_specs=[row, half],
        out_shape=[jax.ShapeDtypeStruct((m, d), F32), jax.ShapeDtypeStruct((m, d // 2), jnp.uint32)],
        compiler_params=_cparams(("arbitrary",)),
        name="res_ln",
    )(res, y, g.reshape(1, d), b.reshape(1, d))


def _shift_mix(cur, prev8, mu, is_first):
    prev_row = jnp.where(is_first, 0.0, prev8[SUBLANES - 1:SUBLANES, :])
    rolled = pltpu.roll(cur, 1, 0)
    row = lax.broadcasted_iota(jnp.int32, cur.shape, 0)
    shifted = jnp.where(row == 0, prev_row, rolled)
    return cur + (shifted - cur) * mu


def _head_block_ones(width):
    r = lax.broadcasted_iota(jnp.int32, (width, width), 0) // RWKV_HEAD_DIM
    c = lax.broadcasted_iota(jnp.int32, (width, width), 1) // RWKV_HEAD_DIM
    return jnp.where(r == c, 1.0, 0.0).astype(BF16)


def _head_sum(x, ones_bd):
    w = ones_bd.shape[0]
    outs = []
    for c in range(x.shape[1] // w):
        hi, lo = _split_bf16(x[:, c * w:(c + 1) * w])
        outs.append(_dot(hi, ones_bd) + _dot(lo, ones_bd))
    return jnp.concatenate(outs, axis=1)


def _rwkv_prep_body(first, r_ref, k_ref, v_ref, rp_ref, kp_ref, vp_ref,
                    wlo_ref, alo_ref, glo_ref, wlop_ref, alop_ref, glop_ref,
                    mur_ref, muk_ref, muv_ref, muw_ref, mua_ref, mug_ref,
                    w2_ref, a2_ref, g2_ref, w0_ref, a0_ref, kk_ref, ka_ref, rk_ref, vres):
    r = _shift_mix(r_ref[...], rp_ref[...], mur_ref[...], first)
    k = _shift_mix(k_ref[...], kp_ref[...], muk_ref[...], first)
    v = _shift_mix(v_ref[...], vp_ref[...], muv_ref[...], first)
    w_lo = _shift_mix(wlo_ref[...], wlop_ref[...], muw_ref[...], first)
    a_lo = _shift_mix(alo_ref[...], alop_ref[...], mua_ref[...], first)
    g_lo = _shift_mix(glo_ref[...], glop_ref[...], mug_ref[...], first)

    z = w0_ref[...] + _dot(jnp.tanh(w_lo).astype(BF16), w2_ref[...])
    nz = -z
    softplus = jnp.maximum(nz, 0.0) + jnp.log(1.0 + jnp.exp(-jnp.abs(nz)))
    log_w = -softplus - 0.5
    ld = -jnp.exp(log_w)
    a = _sigmoid(a0_ref[...] + _dot(a_lo.astype(BF16), a2_ref[...]))
    g = _dot(_sigmoid(g_lo).astype(BF16), g2_ref[...])
    if vres is not None:
        vlo_ref, vlop_ref, muvl_ref, v2_ref, v0_ref, vf_ref = vres
        v_lo = _shift_mix(vlo_ref[...], vlop_ref[...], muvl_ref[...], first)
        mix = _sigmoid(v0_ref[...] + _dot(v_lo.astype(BF16), v2_ref[...]))
        v = v + (vf_ref[...] - v) * mix

    ones_bd = _head_block_ones(2 * LANES)
    kk = k * kk_ref[...]
    ss = _head_sum(kk * kk, ones_bd)
    kn = kk * lax.rsqrt(jnp.maximum(ss, 1e-24))
    k_mod = k * (1.0 + (a - 1.0) * ka_ref[...])
    bonus = _head_sum(r * k_mod * rk_ref[...], ones_bd) * v
    return r, ld, k_mod, v, kn, kn * a, g, bonus


def _make_rwkv_kernel(has_vres):
    n_common = 26

    def kern(*refs):
        common = refs[:n_common]
        rest = refs[n_common:]
        if has_vres:
            vres, rest = rest[:6], rest[6:]
        else:
            vres = None
        lng_ref, lnb_ref = rest[0], rest[1]
        outs, state_ref = rest[2:-1], rest[-1]
        first = pl.program_id(1) == 0
        r, ld, k, v, kn, b, g, bonus = _rwkv_prep_body(first, *common, vres)
        y = _rwkv_scan_block(r, ld, k, v, kn, b, state_ref)
        outs[0][...] = _rwkv_finish(y, g, bonus, lng_ref[...], lnb_ref[...]).astype(outs[0].dtype)
        if not has_vres:
            outs[1][...] = v

    return kern


def _rwkv_group(proj, mu_p, p, l, v_first, ln_g, ln_b):
    t = proj.shape[0]
    tb = RW_CHUNK * RW_SUB
    has_vres = v_first is not None
    nj = RWKV_WIDTH // RW_CH
    pb = tb // SUBLANES

    def cur(width, col0):
        return pl.BlockSpec((tb, width), lambda j, i: (i, col0 // width))

    def cur_j(col0):
        return pl.BlockSpec((tb, RW_CH), lambda j, i: (i, col0 // RW_CH + j))

    def prev(width, col0):
        return pl.BlockSpec((SUBLANES, width), lambda j, i: (jnp.maximum(i * pb - 1, 0), col0 // width))

    def prev_j(col0):
        return pl.BlockSpec((SUBLANES, RW_CH), lambda j, i: (jnp.maximum(i * pb - 1, 0), col0 // RW_CH + j))

    def vec(width, col0):
        return pl.BlockSpec((1, width), lambda j, i: (0, col0 // width))

    def vec_j(col0=0):
        return pl.BlockSpec((1, RW_CH), lambda j, i: (0, col0 // RW_CH + j))

    def lora(rank):
        return pl.BlockSpec((rank, RW_CH), lambda j, i: (0, j))

    row = lambda a: a.reshape(1, -1)
    g2 = jnp.pad(p["rwkv_g2"][l], ((0, GLO_PAD - GATE_LORA), (0, 0))).astype(BF16)
    args = [proj, proj, proj, proj, proj, proj,
            proj, proj, proj, proj, proj, proj,
            mu_p, mu_p, mu_p, mu_p, mu_p, mu_p,
            p["rwkv_w2"][l].astype(BF16), p["rwkv_a2"][l].astype(BF16), g2,
            row(p["rwkv_w0"][l]), row(p["rwkv_a0"][l]), row(p["rwkv_k_k"][l]), row(p["rwkv_k_a"][l]),
            row(p["rwkv_r_k"][l])]
    in_specs = [cur_j(P_R), cur_j(P_K), cur_j(P_V), prev_j(P_R), prev_j(P_K), prev_j(P_V),
                cur(LANES, P_WLO), cur(LANES, P_ALO), cur(GLO_PAD, P_GLO),
                prev(LANES, P_WLO), prev(LANES, P_ALO), prev(GLO_PAD, P_GLO),
                vec_j(P_R), vec_j(P_K), vec_j(P_V), vec(LANES, P_WLO), vec(LANES, P_ALO), vec(GLO_PAD, P_GLO),
                lora(DECAY_LORA), lora(AAA_LORA), lora(GLO_PAD),
                vec_j(), vec_j(), vec_j(), vec_j(), vec_j()]
    if has_vres:
        v2 = jnp.pad(p["rwkv_v2"][l - 1], ((0, LANES - MV_LORA), (0, 0))).astype(BF16)
        args += [proj, proj, mu_p, v2, row(p["rwkv_v0"][l - 1]), v_first]
        in_specs += [cur(LANES, P_VLO), prev(LANES, P_VLO), vec(LANES, P_VLO), lora(LANES), vec_j(),
                     pl.BlockSpec((tb, RW_CH), lambda j, i: (i, j))]
    args += [row(ln_g), row(ln_b)]
    in_specs += [vec_j(), vec_j()]
    out_spec = pl.BlockSpec((tb, RW_CH), lambda j, i: (i, j))
    out_specs = [out_spec]
    out_shape = [jax.ShapeDtypeStruct((t, RWKV_WIDTH), BF16)]
    if not has_vres:
        out_specs.append(out_spec)
        out_shape.append(jax.ShapeDtypeStruct((t, RWKV_WIDTH), F32))
    outs = pl.pallas_call(
        _make_rwkv_kernel(has_vres),
        grid=(nj, t // tb),
        in_specs=in_specs,
        out_specs=out_specs,
        out_shape=out_shape,
        scratch_shapes=[pltpu.VMEM((RW_CH // RWKV_HEAD_DIM, RWKV_HEAD_DIM, RWKV_HEAD_DIM), F32)],
        compiler_params=_cparams(("arbitrary", "arbitrary")),
        name="rwkv",
    )(*args)
    return outs[0], (v_first if has_vres else outs[1])


def _unit_lower_inverse(a_strict, c):
    row = lax.broadcasted_iota(jnp.int32, (c, c), 0)
    col = lax.broadcasted_iota(jnp.int32, (c, c), 1)
    eye = jnp.where(row == col, 1.0, 0.0)
    bd = lambda x: x.astype(BF16)
    base = SUBLANES
    same_base = row // base == col // base
    d1 = [bd(jnp.where(same_base, a, 0.0)) for a in a_strict]
    d2 = [bd(_dot(d, d)) for d in d1]
    inv = [eye + d.astype(F32) for d in d1]
    inv = [i + _dot(d, bd(i)) for i, d in zip(inv, d2)]
    d4 = [bd(_dot(d, d)) for d in d2]
    inv = [i + _dot(d, bd(i)) for i, d in zip(inv, d4)]
    blk = base
    while blk < c:
        band = (row // (2 * blk) == col // (2 * blk)) & (row // blk != col // blk)
        off = [bd(jnp.where(band, a, 0.0)) for a in a_strict]
        inv_b = [bd(i) for i in inv]
        tmp = [bd(_dot(o, i)) for o, i in zip(off, inv_b)]
        inv = [i + _dot(ib, t) for i, ib, t in zip(inv, inv_b, tmp)]
        blk *= 2
    return inv


def _rwkv_scan_block(r, ld, k, v_all, kn, b, state_ref):
    c = RW_CHUNK
    n = RWKV_HEAD_DIM
    heads = RW_CH // n
    rows = ld.shape[0]
    subs = rows // c

    @pl.when(pl.program_id(1) == 0)
    def _():
        state_ref[...] = jnp.zeros_like(state_ref)

    row = lax.broadcasted_iota(jnp.int32, (c, c), 0)
    col = lax.broadcasted_iota(jnp.int32, (c, c), 1)
    strict = row > col
    incl2 = (lax.broadcasted_iota(jnp.int32, (c, 2 * c), 0)
             >= lax.broadcasted_iota(jnp.int32, (c, 2 * c), 1) % c)
    brow = lax.broadcasted_iota(jnp.int32, (rows, rows), 0)
    bcol = lax.broadcasted_iota(jnp.int32, (rows, rows), 1)
    tri_incl = jnp.where((brow >= bcol) & (brow // c == bcol // c), 1.0, 0.0).astype(BF16)

    ld_hi, ld_lo = _split_bf16(ld)
    cum = _dot(tri_incl, ld_hi) + _dot(tri_incl, ld_lo)
    g_inc = jnp.exp(cum)
    g_inv = jnp.exp(-cum)
    a_t = -kn * jnp.exp(cum - ld)
    b_t = b * g_inv
    k_t = k * g_inv
    r_t = r * g_inc

    bd = lambda x: x.astype(BF16)
    idx = [(s, h) for s in range(subs) for h in range(heads)]
    rs = lambda s: slice(s * c, (s + 1) * c)
    ls = lambda h: slice(h * n, (h + 1) * n)
    v_h = [v_all[rs(s), ls(h)] for s, h in idx]
    a_h = [a_t[rs(s), ls(h)] for s, h in idx]
    r_h = [bd(r_t[rs(s), ls(h)]) for s, h in idx]
    ar = [bd(jnp.concatenate([a_h[i], r_t[rs(s), ls(h)]], axis=0)) for i, (s, h) in enumerate(idx)]
    bk = [bd(jnp.concatenate([b_t[rs(s), ls(h)], k_t[rs(s), ls(h)]], axis=0)) for s, h in idx]
    p1 = [_dot_nt(x, y) for x, y in zip(ar, bk)]
    a_ab = [jnp.where(strict, p[:c, :c], 0.0) for p in p1]
    a_ak = [bd(jnp.where(strict, p[:c, c:], 0.0)) for p in p1]
    a_r = [bd(jnp.where(incl2, p[c:, :], 0.0)) for p in p1]
    akv = [_dot(x, bd(y)) for x, y in zip(a_ak, v_h)]
    inv = _unit_lower_inverse(a_ab, c)
    sol = [_dot(bd(inv[i]), bd(jnp.concatenate([a_h[i], akv[i]], axis=1))) for i in range(len(idx))]

    state = [state_ref[h] for h in range(heads)]
    y_rows = []
    for s in range(subs):
        at = lambda lst, h: lst[s * heads + h]
        sb = [bd(x) for x in state]
        u = [_dot_nt(bd(at(sol, h)[:, :n]), sb[h]) + at(sol, h)[:, n:] for h in range(heads)]
        uv = [bd(jnp.concatenate([u[h], at(v_h, h)], axis=0)) for h in range(heads)]
        ys = [_dot_nt(at(r_h, h), sb[h]) + _dot(at(a_r, h), uv[h]) for h in range(heads)]
        g_last = g_inc[(s + 1) * c - 1:(s + 1) * c, :]
        state = [(state[h] + _dot_tn(uv[h], at(bk, h))) * g_last[:, ls(h)] for h in range(heads)]
        y_rows.append(jnp.concatenate(ys, axis=1))
    for h in range(heads):
        state_ref[h] = state[h]
    return jnp.concatenate(y_rows, axis=0)


def _rwkv_finish(y, g, bonus, ln_g, ln_b):
    n = RWKV_HEAD_DIM
    ones_bd = _head_block_ones(2 * LANES)
    mean = _head_sum(y, ones_bd) * (1.0 / n)
    yc = y - mean
    var = _head_sum(yc * yc, ones_bd) * (1.0 / n)
    yn = yc * lax.rsqrt(var + GN_EPS) * ln_g + ln_b
    return (yn + bonus) * g


def _gelu_tanh(x):
    return 0.5 * x * (1.0 + jnp.tanh(np.sqrt(2.0 / np.pi).astype(np.float32) * (x + 0.044715 * (x * x * x))))


def _gmlp_kernel(u_ref, v_ref, lng_ref, lnb_ref, ws_ref, bs_ref, o_ref, *, chunks):
    ch = GMLP_CHUNK
    gd = GMLP_GROUP_DIM
    u = _gelu_tanh(u_ref[...])
    v = _gelu_tanh(v_ref[...])
    mu = jnp.mean(v, axis=-1, keepdims=True)
    vc = v - mu
    var = jnp.mean(vc * vc, axis=-1, keepdims=True)
    vn = (vc * lax.rsqrt(var + LN_EPS) * lng_ref[...] + lnb_ref[...]).astype(BF16)
    row = lax.broadcasted_iota(jnp.int32, (ch, ch), 0)
    col = lax.broadcasted_iota(jnp.int32, (ch, ch), 1)
    causal = row >= col
    bs = bs_ref[...]
    for g in range(GMLP_GROUPS):
        w = jnp.where(causal, ws_ref[g], 0.0).astype(BF16)
        bias = bs[:, g:g + 1]
        for n in range(chunks):
            f = _dot(w, vn[n * ch:(n + 1) * ch, g * gd:(g + 1) * gd]) + bias
            o_ref[n * ch:(n + 1) * ch, g * gd:(g + 1) * gd] = (
                u[n * ch:(n + 1) * ch, g * gd:(g + 1) * gd] * f).astype(o_ref.dtype)


def _gmlp(proj, ln_g, ln_b, w_s, b_s, tb=256):
    t = proj.shape[0]
    bs_t = jnp.pad(b_s.T, ((0, 0), (0, LANES - GMLP_GROUPS)))
    return pl.pallas_call(
        functools.partial(_gmlp_kernel, chunks=tb // GMLP_CHUNK),
        grid=(t // tb,),
        in_specs=[pl.BlockSpec((tb, GMLP_WIDTH), lambda i: (i, P_GU // GMLP_WIDTH)),
                  pl.BlockSpec((tb, GMLP_WIDTH), lambda i: (i, P_GV // GMLP_WIDTH)),
                  pl.BlockSpec((1, GMLP_WIDTH), lambda i: (0, 0)),
                  pl.BlockSpec((1, GMLP_WIDTH), lambda i: (0, 0)),
                  pl.BlockSpec((GMLP_GROUPS, GMLP_CHUNK, GMLP_CHUNK), lambda i: (0, 0, 0)),
                  pl.BlockSpec((GMLP_CHUNK, LANES), lambda i: (0, 0))],
        out_specs=pl.BlockSpec((tb, GMLP_WIDTH), lambda i: (i, 0)),
        out_shape=jax.ShapeDtypeStruct((t, GMLP_WIDTH), BF16),
        compiler_params=_cparams(("arbitrary",)),
        name="gmlp",
    )(proj, proj, ln_g.reshape(1, -1), ln_b.reshape(1, -1), w_s, bs_t)


def _rope_lanes(t, cc, s1, s2):
    return t * cc + pltpu.roll(t, LANES - MLA_ROPE_DIM // 2, 1) * s1 + pltpu.roll(t, MLA_ROPE_DIM // 2, 1) * s2


def _mla_proj_kernel(cq_ref, ckv_ref, kr_ref, qn_ref, kvn_ref, wq_ref, wk_ref, wvt_ref, cc_ref, s1_ref, s2_ref,
                     q_o, kn_o, vt_o, kr_o):
    cq = cq_ref[...]
    qn = (cq * lax.rsqrt(jnp.mean(cq * cq, axis=-1, keepdims=True) + RMS_EPS) * qn_ref[...]).astype(BF16)
    ckv = ckv_ref[...]
    kvn = (ckv * lax.rsqrt(jnp.mean(ckv * ckv, axis=-1, keepdims=True) + RMS_EPS) * kvn_ref[...]).astype(BF16)
    cc, s1, s2 = cc_ref[...], s1_ref[...], s2_ref[...]
    scale = MLA_QK_DIM ** -0.5 * np.log2(np.e)
    for h in range(MLA_HEADS):
        q = _dot(qn, wq_ref[:, 2 * LANES * h:2 * LANES * (h + 1)]) * scale
        q_o[:, 2 * LANES * h:2 * LANES * h + LANES] = q[:, :LANES].astype(BF16)
        q_o[:, 2 * LANES * h + LANES:2 * LANES * (h + 1)] = _rope_lanes(q[:, LANES:], cc, s1, s2).astype(BF16)
    kn_o[...] = _dot(kvn, wk_ref[...]).astype(BF16)
    vt_o[...] = _dot_nt(wvt_ref[...], kvn).astype(BF16)
    kr_o[...] = _rope_lanes(kr_ref[...], cc, s1, s2).astype(BF16)


def _mla_proj(proj, positions, q_norm, kv_norm, w_uq, w_ukv, tm=512):
    t = proj.shape[0]
    h = MLA_HEADS
    half = MLA_ROPE_DIM // 2
    inv_freq = jnp.power(ROPE_THETA, -jnp.arange(0, MLA_ROPE_DIM, 2, dtype=F32) / MLA_ROPE_DIM)
    ang = positions.reshape(t).astype(F32)[:, None] * inv_freq
    cos, sin = jnp.cos(ang), jnp.sin(ang)
    z = jnp.zeros((t, half), F32)
    cc = jnp.concatenate([cos, cos, z, z], axis=1)
    s1 = jnp.concatenate([-sin, z, z, z], axis=1)
    s2 = jnp.concatenate([z, sin, z, z], axis=1)
    wq = w_uq.reshape(MLA_Q_RANK, h, MLA_QK_DIM)
    wq = jnp.pad(wq, ((0, 0), (0, 0), (0, 2 * LANES - MLA_QK_DIM))).reshape(MLA_Q_RANK, h * 2 * LANES).astype(BF16)
    wkv = w_ukv.reshape(MLA_KV_RANK, h, MLA_NOPE_DIM + MLA_V_DIM)
    wk = wkv[:, :, :MLA_NOPE_DIM].reshape(MLA_KV_RANK, h * MLA_NOPE_DIM).astype(BF16)
    wvt = wkv[:, :, MLA_NOPE_DIM:].reshape(MLA_KV_RANK, h * MLA_V_DIM).T.astype(BF16)
    full = lambda a: pl.BlockSpec(a.shape, lambda i: (0,) * a.ndim)
    tab = pl.BlockSpec((tm, LANES), lambda i: (i, 0))
    qn2, kvn2 = q_norm.reshape(1, -1), kv_norm.reshape(1, -1)
    return pl.pallas_call(
        _mla_proj_kernel,
        grid=(t // tm,),
        in_specs=[pl.BlockSpec((tm, MLA_Q_RANK), lambda i: (i, P_CQ // MLA_Q_RANK)),
                  pl.BlockSpec((tm, MLA_KV_RANK), lambda i: (i, P_CKV // MLA_KV_RANK)),
                  pl.BlockSpec((tm, LANES), lambda i: (i, P_KROPE // LANES)),
                  full(qn2), full(kvn2), full(wq), full(wk), full(wvt), tab, tab, tab],
        out_specs=[pl.BlockSpec((tm, h * 2 * LANES), lambda i: (i, 0)),
                   pl.BlockSpec((tm, h * MLA_NOPE_DIM), lambda i: (i, 0)),
                   pl.BlockSpec((h * MLA_V_DIM, tm), lambda i: (0, i)),
                   pl.BlockSpec((tm, LANES), lambda i: (i, 0))],
        out_shape=[jax.ShapeDtypeStruct((t, h * 2 * LANES), BF16),
                   jax.ShapeDtypeStruct((t, h * MLA_NOPE_DIM), BF16),
                   jax.ShapeDtypeStruct((h * MLA_V_DIM, t), BF16),
                   jax.ShapeDtypeStruct((t, LANES), BF16)],
        compiler_params=_cparams(("arbitrary",)),
        name="mla_proj",
    )(proj, proj, proj, qn2, kvn2, wq, wk, wvt, cc, s1, s2)


def _flash_kernel(qi_ref, kj_ref, q_ref, kn_ref, kr_ref, vt_ref, o_ref, m_sc, l_sc, acc_sc, st_sc, *, tq, tk):
    s = pl.program_id(1)
    qi, kj = qi_ref[s], kj_ref[s]

    @pl.when(kj == 0)
    def _():
        m_sc[...] = jnp.full_like(m_sc, -jnp.inf)
        l_sc[...] = jnp.zeros_like(l_sc)
        acc_sc[...] = jnp.zeros_like(acc_sc)

    qt = 2 * LANES
    kb = 2 * LANES

    def step(masked):
        k = jnp.concatenate([kn_ref[...], kr_ref[...]], axis=1)
        m_all = m_sc[...]
        m_news, alphas = [], []
        for c0 in range(0, tq, qt):
            st = _dot_nt(k, q_ref[c0:c0 + qt, :])
            if masked:
                key = kj * tk + lax.broadcasted_iota(jnp.int32, (tk, qt), 0)
                qry = qi * tq + c0 + lax.broadcasted_iota(jnp.int32, (tk, qt), 1)
                st = jnp.where(key <= qry, st, -jnp.inf)
            st_sc[:, c0:c0 + qt] = st
            m_prev = m_all[:, c0:c0 + qt]
            m_new = jnp.maximum(m_prev, jnp.max(st, axis=0, keepdims=True))
            m_news.append(m_new)
            alphas.append(jnp.exp2(m_prev - m_new))
        pvs, sums = [], []
        for t_i, c0 in enumerate(range(0, tq, qt)):
            pv, ps = None, None
            for r0 in range(0, tk, kb):
                p = jnp.exp2(st_sc[r0:r0 + kb, c0:c0 + qt] - m_news[t_i])
                part = jnp.sum(p, axis=0, keepdims=True)
                prod = _dot(vt_ref[:, r0:r0 + kb], p.astype(BF16))
                ps = part if ps is None else ps + part
                pv = prod if pv is None else pv + prod
            pvs.append(pv)
            sums.append(ps)
        alpha = jnp.concatenate(alphas, axis=1)
        m_sc[...] = jnp.concatenate(m_news, axis=1)
        l_sc[...] = alpha * l_sc[...] + jnp.concatenate(sums, axis=1)
        acc_sc[...] = alpha * acc_sc[...] + jnp.concatenate(pvs, axis=1)

    last_key_of_block = kj * tk + tk - 1
    on_diag = last_key_of_block > qi * tq

    @pl.when(jnp.logical_not(on_diag))
    def _():
        step(False)

    @pl.when(on_diag)
    def _():
        step(True)

    @pl.when(last_key_of_block >= qi * tq + tq - 1)
    def _():
        o_ref[...] = (acc_sc[...] / l_sc[...]).T.astype(o_ref.dtype)


def _flash(q, kn, vt, kr, tq=512, tk=512):
    t = q.shape[0]
    assert tq % tk == 0
    pairs = [(i, j) for i in range(t // tq) for j in range((i + 1) * tq // tk)]
    qi = jnp.asarray([pr[0] for pr in pairs], jnp.int32)
    kj = jnp.asarray([pr[1] for pr in pairs], jnp.int32)
    grid_spec = pltpu.PrefetchScalarGridSpec(
        num_scalar_prefetch=2,
        grid=(MLA_HEADS, len(pairs)),
        in_specs=[pl.BlockSpec((tq, 2 * LANES), lambda h, s, qi, kj: (qi[s], h)),
                  pl.BlockSpec((tk, MLA_NOPE_DIM), lambda h, s, qi, kj: (kj[s], h)),
                  pl.BlockSpec((tk, LANES), lambda h, s, qi, kj: (kj[s], 0)),
                  pl.BlockSpec((MLA_V_DIM, tk), lambda h, s, qi, kj: (h, kj[s]))],
        out_specs=pl.BlockSpec((tq, MLA_V_DIM), lambda h, s, qi, kj: (qi[s], h)),
        scratch_shapes=[pltpu.VMEM((1, tq), F32), pltpu.VMEM((1, tq), F32), pltpu.VMEM((MLA_V_DIM, tq), F32),
                        pltpu.VMEM((tk, tq), F32)],
    )
    return pl.pallas_call(
        functools.partial(_flash_kernel, tq=tq, tk=tk),
        grid_spec=grid_spec,
        out_shape=jax.ShapeDtypeStruct((t, MLA_HEADS * MLA_V_DIM), BF16),
        compiler_params=_cparams(("arbitrary", "arbitrary")),
        name="flash",
    )(qi, kj, q, kn, kr, vt)


def _router_kernel(h_ref, rwt_ref, bias_ref, exp_o, pos_o, gate_o, cnt_o, carry_sc):
    e = N_EXPERTS
    per = e // N_EXPERT_GROUPS
    h_hi, h_lo = _split_bf16(h_ref[...])
    w_hi, w_lo = _split_bf16(rwt_ref[...])
    logits = _dot_nt(w_hi, h_hi) + _dot_nt(w_hi, h_lo) + _dot_nt(w_lo, h_hi)
    scores = _sigmoid(logits)
    biased = scores + bias_ref[...][:, 0:1]
    tb = biased.shape[1]
    neg = -jnp.inf
    sub = lax.broadcasted_iota(jnp.int32, (per, tb), 0)
    grp_rows = []
    for g in range(N_EXPERT_GROUPS):
        blk = biased[g * per:(g + 1) * per, :]
        m1 = jnp.max(blk, axis=0, keepdims=True)
        first = jnp.min(jnp.where(blk == m1, sub, per), axis=0, keepdims=True)
        m2 = jnp.max(jnp.where(sub == first, neg, blk), axis=0, keepdims=True)
        grp_rows.append(m1 + m2)
    grp = jnp.concatenate(grp_rows, axis=0)
    gidx = lax.broadcasted_iota(jnp.int32, grp.shape, 0)
    grank = jnp.zeros(grp.shape, jnp.int32)
    for g in range(N_EXPERT_GROUPS):
        other = grp[g:g + 1, :]
        ahead = (other > grp) | ((other == grp) & (g < gidx))
        grank = grank + jnp.where(ahead, 1, 0)
    gsel = grank < TOPK_GROUPS
    masked = jnp.concatenate(
        [jnp.where(gsel[g:g + 1, :], biased[g * per:(g + 1) * per, :], neg) for g in range(N_EXPERT_GROUPS)], axis=0)
    eidx = lax.broadcasted_iota(jnp.int32, masked.shape, 0)
    rank = jnp.zeros(masked.shape, jnp.int32)
    for j in range(e):
        other = masked[j:j + 1, :]
        ahead = (other > masked) | ((other == masked) & (j < eidx))
        rank = rank + jnp.where(ahead, 1, 0)
    chosen = rank < TOP_K
    sel = jnp.where(chosen, scores, 0.0)
    gate = sel / jnp.sum(sel, axis=0, keepdims=True) * ROUTED_SCALE

    @pl.when(pl.program_id(0) == 0)
    def _():
        carry_sc[...] = jnp.zeros_like(carry_sc)

    chosen_f = jnp.where(chosen, 1.0, 0.0)
    earlier = (lax.broadcasted_iota(jnp.int32, (tb, tb), 0) < lax.broadcasted_iota(jnp.int32, (tb, tb), 1))
    carry = carry_sc[...]
    pos = _dot(chosen_f.astype(BF16), jnp.where(earlier, 1.0, 0.0).astype(BF16)) + carry[:, 0:1]
    carry_sc[...] = carry + jnp.sum(chosen_f, axis=1, keepdims=True)
    cnt_o[...] = carry_sc[...].astype(jnp.int32)

    eidx_f = eidx.astype(F32)
    pick = lambda hit, val: jnp.sum(jnp.where(hit, val, 0.0), axis=0, keepdims=True)
    hits = [rank == k for k in range(TOP_K)]
    exp_o[...] = jnp.concatenate([pick(hit, eidx_f) for hit in hits], axis=0).astype(jnp.int32)
    pos_o[...] = jnp.concatenate([pick(hit, pos) for hit in hits], axis=0).astype(jnp.int32)
    gate_o[...] = jnp.concatenate([pick(hit, gate) for hit in hits], axis=0)


def _router(h, router_w, router_bias, tb=512):
    t, d = h.shape
    bias = jnp.broadcast_to(router_bias.astype(F32)[:, None], (N_EXPERTS, LANES))
    per_tok = pl.BlockSpec((TOP_K, tb), lambda i: (0, i))
    return pl.pallas_call(
        _router_kernel,
        grid=(t // tb,),
        in_specs=[pl.BlockSpec((tb, d), lambda i: (i, 0)),
                  pl.BlockSpec((N_EXPERTS, d), lambda i: (0, 0)),
                  pl.BlockSpec((N_EXPERTS, LANES), lambda i: (0, 0))],
        out_specs=[per_tok, per_tok, per_tok, pl.BlockSpec((N_EXPERTS, LANES), lambda i: (0, 0))],
        out_shape=[jax.ShapeDtypeStruct((TOP_K, t), jnp.int32), jax.ShapeDtypeStruct((TOP_K, t), jnp.int32),
                   jax.ShapeDtypeStruct((TOP_K, t), F32), jax.ShapeDtypeStruct((N_EXPERTS, LANES), jnp.int32)],
        scratch_shapes=[pltpu.VMEM((N_EXPERTS, LANES), F32)],
        compiler_params=_cparams(("arbitrary",)),
        name="router",
    )(h, router_w.T, bias)


HALF_MASK = 0xFFFF0000


def _pack_bf16_pairs(x):
    n = x.shape[1] // 2
    lo = pltpu.bitcast(x[:, :n].astype(BF16).astype(F32), jnp.uint32) >> 16
    hi = pltpu.bitcast(x[:, n:].astype(BF16).astype(F32), jnp.uint32) & jnp.uint32(HALF_MASK)
    return lo | hi


def _unpack_bf16_pairs(w):
    return pltpu.bitcast(w << 16, F32), pltpu.bitcast(w & jnp.uint32(HALF_MASK), F32)


def _dispatch_plan(exp_r, pos_r, cnt, bm, nb_max):
    ids = jnp.arange(N_EXPERTS, dtype=jnp.int32)
    nb = (cnt + bm - 1) // bm
    bend = jnp.sum(jnp.where(ids[None, :] <= ids[:, None], nb[None, :], 0), axis=1)
    total = bend[-1]
    slot_start = (bend - nb) * bm
    start_of = jnp.sum(jnp.where(exp_r[:, :, None] == ids, slot_start, 0), axis=-1)
    slot = (start_of + pos_r).T.reshape(-1)
    bidx = jnp.arange(nb_max, dtype=jnp.int32)
    blocks = jnp.minimum(bidx, total - 1)
    blk_exp = jnp.sum(jnp.where(bend[None, :] <= blocks[:, None], 1, 0), axis=1)
    first = jnp.where((bidx == 0) | (blk_exp != jnp.roll(blk_exp, 1)), 1, 0)
    turn = jnp.sum(jnp.where(bidx[None, :] <= bidx[:, None], first[None, :], 0), axis=1) - 1
    later_first = jnp.where((bidx[None, :] > bidx[:, None]) & (first[None, :] == 1), bidx[None, :], nb_max)
    nxt_blk = jnp.min(later_first, axis=1)
    nxt_exp = jnp.sum(jnp.where(bidx[None, :] == nxt_blk[:, None], blk_exp[None, :] + 1, 0), axis=1) - 1
    meta = jnp.concatenate([blk_exp, first, turn % 2, nxt_exp]).astype(jnp.int32)
    lo = jnp.concatenate([slot_start + cnt, (total * bm).reshape(1)])
    hi = jnp.concatenate([slot_start + nb * bm, jnp.full((1,), nb_max * bm, jnp.int32)])
    gaps = jnp.stack([lo, hi], axis=1).reshape(-1).astype(jnp.int32)
    return slot.astype(jnp.int32), meta, total.reshape(1).astype(jnp.int32), gaps


ZERO_ROWS = 256


def _dispatch_kernel(slot_ref, gaps_ref, h_ref, xs_hbm, zero_sc, sem, zsem, *, tb, steps, n_gaps):
    i = pl.program_id(0)

    def fill_gaps(act):
        def gap(r, carry):
            lo, hi = gaps_ref[2 * r], gaps_ref[2 * r + 1]
            lo_tile = jnp.minimum((lo + SUBLANES - 1) // SUBLANES * SUBLANES, hi)

            def single(row, carry2):
                act(pltpu.make_async_copy(zero_sc.at[pl.ds(0, 1), :], xs_hbm.at[pl.ds(row, 1), :], zsem))
                return carry2

            lax.fori_loop(lo, lo_tile, single, 0)
            whole = (hi - lo_tile) // ZERO_ROWS

            def chunk(c, carry2):
                start = pl.multiple_of(lo_tile + c * ZERO_ROWS, SUBLANES)
                act(pltpu.make_async_copy(zero_sc, xs_hbm.at[pl.ds(start, ZERO_ROWS), :], zsem))
                return carry2

            lax.fori_loop(0, whole, chunk, 0)
            off = lo_tile + whole * ZERO_ROWS
            rem = hi - off
            size = ZERO_ROWS // 2
            while size >= SUBLANES:
                take = (rem & size) != 0

                @pl.when(take)
                def _(off=off, size=size):
                    start = pl.multiple_of(off, SUBLANES)
                    act(pltpu.make_async_copy(zero_sc.at[pl.ds(0, size), :], xs_hbm.at[pl.ds(start, size), :], zsem))

                off = off + jnp.where(take, size, 0)
                size //= 2
            return carry

        lax.fori_loop(0, n_gaps, gap, 0)

    @pl.when(i == 0)
    def _():
        zero_sc[...] = jnp.zeros_like(zero_sc)
        fill_gaps(lambda c: c.start())

    def body(t, carry):
        for k in range(TOP_K):
            s = slot_ref[t * TOP_K + k]
            pltpu.make_async_copy(h_ref.at[pl.ds(t, 1), :], xs_hbm.at[pl.ds(s, 1), :], sem).start()
        return carry

    lax.fori_loop(0, tb, body, 0)
    rows = tb * TOP_K
    pltpu.make_async_copy(xs_hbm.at[pl.ds(0, rows), :], xs_hbm.at[pl.ds(0, rows), :], sem).wait()

    @pl.when(i == steps - 1)
    def _():
        fill_gaps(lambda c: c.wait())


def _dispatch(h_pk, slot, gaps, n_slots, tb):
    t, w = h_pk.shape
    steps = t // tb
    return pl.pallas_call(
        functools.partial(_dispatch_kernel, tb=tb, steps=steps, n_gaps=gaps.shape[0] // 2),
        grid=(steps,),
        in_specs=[pl.BlockSpec((tb * TOP_K,), lambda i: (i,), memory_space=pltpu.SMEM),
                  pl.BlockSpec(memory_space=pltpu.SMEM),
                  pl.BlockSpec((tb, w), lambda i: (i, 0))],
        out_specs=pl.BlockSpec(memory_space=pl.ANY),
        out_shape=jax.ShapeDtypeStruct((n_slots, w), jnp.uint32),
        scratch_shapes=[pltpu.VMEM((ZERO_ROWS, w), jnp.uint32), pltpu.SemaphoreType.DMA(()),
                        pltpu.SemaphoreType.DMA(())],
        compiler_params=pltpu.CompilerParams(dimension_semantics=("arbitrary",), vmem_limit_bytes=VMEM_LIMIT,
                                             has_side_effects=True, disable_bounds_checks=True),
        name="moe_dispatch",
    )(slot, gaps, h_pk)


def _expert_meta(meta_ref, nb):
    b = pl.program_id(0)
    return meta_ref[b], meta_ref[nb + b] == 1, meta_ref[2 * nb + b], meta_ref[3 * nb + b]


def _expert_weight_turn(meta_ref, nb, layer, hbm_refs, stage_ref, sem, cast_to):
    e, first, par, nxt = _expert_meta(meta_ref, nb)

    def copies(expert, half):
        return [pltpu.make_async_copy(w.at[layer, expert], stage_ref.at[half, n], sem.at[half, n])
                for n, w in enumerate(hbm_refs)]

    @pl.when(pl.program_id(0) == 0)
    def _():
        for c in copies(e, par):
            c.start()

    @pl.when(first)
    def _():
        for c in copies(e, par):
            c.wait()
        cast_to(stage_ref.at[par])

        @pl.when(nxt >= 0)
        def _():
            for c in copies(nxt, 1 - par):
                c.start()


def _expert_up_kernel(meta_ref, tot_ref, xs_ref, w1_hbm, w3_hbm, act_ref, stage_sc, w13_sc, sem, *, nb, layer):
    def cast_to(staged):
        w13_sc[:, :D_EXPERT] = staged[0].astype(BF16)
        w13_sc[:, D_EXPERT:] = staged[1].astype(BF16)

    _expert_weight_turn(meta_ref, nb, layer, [w1_hbm, w3_hbm], stage_sc, sem, cast_to)

    @pl.when(pl.program_id(0) < tot_ref[0])
    def _():
        lo, hi = _unpack_bf16_pairs(xs_ref[...])
        x = jnp.concatenate([lo.astype(BF16), hi.astype(BF16)], axis=1)
        hgu = _dot(x, w13_sc[...])
        hg, hu = hgu[:, :D_EXPERT], hgu[:, D_EXPERT:]
        act_ref[...] = (hg * _sigmoid(hg) * hu).astype(BF16)

    @pl.when(pl.program_id(0) >= tot_ref[0])
    def _():
        act_ref[...] = jnp.zeros_like(act_ref)


def _expert_down_kernel(meta_ref, tot_ref, act_ref, w2_hbm, ys_ref, stage_sc, w2_sc, sem, *, nb, layer):
    def cast_to(staged):
        w2_sc[...] = staged[0].astype(BF16)

    _expert_weight_turn(meta_ref, nb, layer, [w2_hbm], stage_sc, sem, cast_to)

    @pl.when(pl.program_id(0) < tot_ref[0])
    def _():
        ys_ref[...] = _pack_bf16_pairs(_dot(act_ref[...], w2_sc[...]))

    @pl.when(pl.program_id(0) >= tot_ref[0])
    def _():
        ys_ref[...] = jnp.zeros_like(ys_ref)


def _experts(xs, meta, total, w1, w3, w2, layer, bm):
    n_slots, w = xs.shape
    d = 2 * w
    nb = n_slots // bm
    used = lambda b, meta, tot: (jnp.minimum(b, tot[0] - 1), 0)
    every = lambda b, meta, tot: (b, 0)
    hbm = pl.BlockSpec(memory_space=pl.ANY)
    act = pl.pallas_call(
        functools.partial(_expert_up_kernel, nb=nb, layer=layer),
        grid_spec=pltpu.PrefetchScalarGridSpec(
            num_scalar_prefetch=2,
            grid=(nb,),
            in_specs=[pl.BlockSpec((bm, w), used), hbm, hbm],
            out_specs=pl.BlockSpec((bm, D_EXPERT), every),
            scratch_shapes=[pltpu.VMEM((2, 2, d, D_EXPERT), F32), pltpu.VMEM((d, 2 * D_EXPERT), BF16),
                            pltpu.SemaphoreType.DMA((2, 2))]),
        out_shape=jax.ShapeDtypeStruct((n_slots, D_EXPERT), BF16),
        compiler_params=_cparams(("arbitrary",)),
        name="moe_up",
    )(meta, total, xs, w1, w3)
    return pl.pallas_call(
        functools.partial(_expert_down_kernel, nb=nb, layer=layer),
        grid_spec=pltpu.PrefetchScalarGridSpec(
            num_scalar_prefetch=2,
            grid=(nb,),
            in_specs=[pl.BlockSpec((bm, D_EXPERT), used), hbm],
            out_specs=pl.BlockSpec((bm, w), every),
            scratch_shapes=[pltpu.VMEM((2, 1, D_EXPERT, d), F32), pltpu.VMEM((D_EXPERT, d), BF16),
                            pltpu.SemaphoreType.DMA((2, 1))]),
        out_shape=jax.ShapeDtypeStruct((n_slots, w), jnp.uint32),
        compiler_params=_cparams(("arbitrary",)),
        name="moe_down",
    )(meta, total, act, w2)


def _combine_kernel(slot_ref, slot_next_ref, gate_ref, h_ref, sw13_ref, sw2_ref, g_ref, b_ref, ys_hbm,
                    of_ref, ob_ref, rows_a, rows_b, shared_sc, sem, *, tb, steps):
    i = pl.program_id(0)
    grp = 2 * SUBLANES

    def issue(table_ref, buf, buf_sem, t0, first=0, count=grp):
        for j in range(first, first + count):
            for k in range(TOP_K):
                s = table_ref[(t0 + j) * TOP_K + k]
                pltpu.make_async_copy(ys_hbm.at[pl.ds(s, 1), :], buf.at[k, pl.ds(t0 + j, 1), :], buf_sem).start()

    def wait_block(buf, buf_sem):
        pltpu.make_async_copy(buf, buf, buf_sem).wait()

    @pl.when(i == 0)
    def _():
        def first(g, carry):
            issue(slot_ref, rows_a, sem.at[0], g * grp)
            return carry

        lax.fori_loop(0, tb // grp, first, 0)

    hgu = _dot(h_ref[...].astype(BF16), sw13_ref[...])
    hg, hu = hgu[:, :D_EXPERT], hgu[:, D_EXPERT:]
    shared_sc[...] = _dot((hg * _sigmoid(hg) * hu).astype(BF16), sw2_ref[...])

    def run(cur, cur_sem, nxt, nxt_sem):
        wait_block(cur, cur_sem)

        def group(g, carry):
            r0 = pl.multiple_of(g * grp, grp)
            rows = pl.ds(r0, grp)
            gate = gate_ref[rows, :]
            acc_lo = jnp.zeros((grp, cur.shape[-1]), F32)
            acc_hi = jnp.zeros((grp, cur.shape[-1]), F32)
            per_k = grp // TOP_K
            for k in range(TOP_K):
                issue(slot_next_ref, nxt, nxt_sem, r0, k * per_k, per_k)
                lo, hi = _unpack_bf16_pairs(cur[k, rows, :])
                gk = gate[:, k:k + 1]
                acc_lo = acc_lo + gk * lo
                acc_hi = acc_hi + gk * hi
            t = ALPHA * h_ref[rows, :] + shared_sc[rows, :] + jnp.concatenate([acc_lo, acc_hi], axis=1)
            mu = jnp.mean(t, axis=-1, keepdims=True)
            c = t - mu
            var = jnp.mean(c * c, axis=-1, keepdims=True)
            out = c * lax.rsqrt(var + LN_EPS) * g_ref[...] + b_ref[...]
            of_ref[rows, :] = out
            ob_ref[rows, :] = out.astype(BF16)
            return carry

        lax.fori_loop(0, tb // grp, group, 0)

        @pl.when(i == steps - 1)
        def _():
            wait_block(nxt, nxt_sem)

    even = lax.rem(i, 2) == 0

    @pl.when(even)
    def _():
        run(rows_a, sem.at[0], rows_b, sem.at[1])

    @pl.when(jnp.logical_not(even))
    def _():
        run(rows_b, sem.at[1], rows_a, sem.at[0])


def _combine(ys, slot, gate_tk, hf, sw1, sw3, sw2, g, b, tb):
    t, d = hf.shape
    steps = t // tb
    sw13 = jnp.concatenate([sw1, sw3], axis=1).astype(BF16)
    row = pl.BlockSpec((tb, d), lambda i: (i, 0))
    full = lambda a: pl.BlockSpec(a.shape, lambda i: (0,) * a.ndim)
    g2, b2, sw2b = g.reshape(1, d), b.reshape(1, d), sw2.astype(BF16)
    return pl.pallas_call(
        functools.partial(_combine_kernel, tb=tb, steps=steps),
        grid=(steps,),
        in_specs=[pl.BlockSpec((tb * TOP_K,), lambda i: (i,), memory_space=pltpu.SMEM),
                  pl.BlockSpec((tb * TOP_K,), lambda i: (jnp.minimum(i + 1, steps - 1),), memory_space=pltpu.SMEM),
                  pl.BlockSpec((tb, TOP_K), lambda i: (i, 0)),
                  row, full(sw13), full(sw2b), full(g2), full(b2),
                  pl.BlockSpec(memory_space=pl.ANY)],
        out_specs=[row, row],
        out_shape=[jax.ShapeDtypeStruct((t, d), F32), jax.ShapeDtypeStruct((t, d), BF16)],
        scratch_shapes=[pltpu.VMEM((TOP_K, tb, ys.shape[1]), jnp.uint32),
                        pltpu.VMEM((TOP_K, tb, ys.shape[1]), jnp.uint32),
                        pltpu.VMEM((tb, d), F32), pltpu.SemaphoreType.DMA((2,))],
        compiler_params=pltpu.CompilerParams(dimension_semantics=("arbitrary",), vmem_limit_bytes=VMEM_LIMIT,
                                             disable_bounds_checks=True),
        name="moe_combine",
    )(slot, slot, gate_tk, hf, sw13, sw2b, g2, b2, ys)


def _moe_ffn(hf, h_pk, router_w, router_bias, w1, w3, w2, layer, sw1, sw3, sw2, ln_g, ln_b, bm):
    t = hf.shape[0]
    exp_r, pos_r, gate_r, cnt = _router(hf, router_w, router_bias, tb=min(512, t))
    nb_max = t * TOP_K // bm + N_EXPERTS
    slot, meta, total, gaps = _dispatch_plan(exp_r, pos_r, cnt[:, 0], bm, nb_max)
    xs = _dispatch(h_pk, slot, gaps, nb_max * bm, tb=min(256, t // TOP_K))
    ys = _experts(xs, meta, total, w1, w3, w2, layer, bm)
    return _combine(ys, slot, gate_r.T, hf, sw1, sw3, sw2, ln_g, ln_b, tb=min(128, t))


def _w_in_layout_kernel(w_ref, o_ref, *, segments):
    o_ref[...] = jnp.zeros_like(o_ref)
    for src, dst, width in segments:
        o_ref[:, dst:dst + width] = w_ref[0, :, src:src + width].astype(BF16)


def _prep_w_in(w_stack, index, has_vres, tr=256):
    _, d, cols = w_stack.shape
    sizes = [RWKV_WIDTH, RWKV_WIDTH, RWKV_WIDTH, DECAY_LORA, AAA_LORA, GATE_LORA,
             GMLP_WIDTH, GMLP_WIDTH, MLA_Q_RANK, MLA_KV_RANK, MLA_ROPE_DIM]
    dsts = [P_R, P_K, P_V, P_WLO, P_ALO, P_GLO, P_GU, P_GV, P_CQ, P_CKV, P_KROPE]
    if has_vres:
        sizes.append(MV_LORA)
        dsts.append(P_VLO)
    srcs = np.concatenate([[0], np.cumsum(sizes)])[:-1]
    segments = tuple((int(s), int(t), int(n)) for s, t, n in zip(srcs, dsts, sizes))
    return pl.pallas_call(
        functools.partial(_w_in_layout_kernel, segments=segments),
        grid=(d // tr,),
        in_specs=[pl.BlockSpec((1, tr, cols), lambda i: (index, i, 0))],
        out_specs=pl.BlockSpec((tr, P_COLS), lambda i: (i, 0)),
        out_shape=jax.ShapeDtypeStruct((d, P_COLS), BF16),
        compiler_params=_cparams(("arbitrary",)),
        name="w_in_layout",
    )(w_stack)


def _prep_mu(mu, mu_vres):
    out = jnp.zeros((1, P_COLS), F32)
    offs = np.concatenate([[0], np.cumsum([RWKV_WIDTH] * 3 + [DECAY_LORA, AAA_LORA, GATE_LORA])])
    for dst, i in zip([P_R, P_K, P_V, P_WLO, P_ALO, P_GLO], range(6)):
        out = lax.dynamic_update_slice(out, mu[offs[i]:offs[i + 1]].reshape(1, -1), (0, dst))
    if mu_vres is not None:
        out = lax.dynamic_update_slice(out, mu_vres.reshape(1, -1), (0, P_VLO))
    return out


def kernel(x, positions, w_in_first, w_in_rest, rwkv_mu, rwkv_mu_vres, rwkv_w0, rwkv_w2, rwkv_a0, rwkv_a2, rwkv_v0, rwkv_v2, rwkv_g2, rwkv_k_k, rwkv_k_a, rwkv_r_k, rwkv_ln_g, rwkv_ln_b, gmlp_ln_g, gmlp_ln_b, gmlp_w_s, gmlp_b_s, mla_q_norm, mla_kv_norm, mla_w_uq, mla_w_ukv, w_out, ln1_g, ln1_b, router_w, router_bias, exp_w1, exp_w3, exp_w2, shared_w1, shared_w3, shared_w2, ln2_g, ln2_b):
    b, s, d = x.shape
    t = b * s
    p = dict(rwkv_w0=rwkv_w0, rwkv_w2=rwkv_w2, rwkv_a0=rwkv_a0, rwkv_a2=rwkv_a2, rwkv_v0=rwkv_v0, rwkv_v2=rwkv_v2,
             rwkv_g2=rwkv_g2, rwkv_k_k=rwkv_k_k, rwkv_k_a=rwkv_k_a, rwkv_r_k=rwkv_r_k)
    xf = x.reshape(t, d)
    xb = xf.astype(BF16)
    v_first = None
    for l in range(DEPTH):
        has_vres = l > 0
        w_in = _prep_w_in(w_in_first[None], 0, False) if l == 0 else _prep_w_in(w_in_rest, l - 1, True)
        mu_p = _prep_mu(rwkv_mu[l], rwkv_mu_vres[l - 1] if has_vres else None)
        proj = _matmul(xb, w_in, F32, tm=min(1024, t), tn=1280)
        y_a, v_first = _rwkv_group(proj, mu_p, p, l, v_first, rwkv_ln_g[l], rwkv_ln_b[l])
        y_b = _gmlp(proj, gmlp_ln_g[l], gmlp_ln_b[l], gmlp_w_s[l], gmlp_b_s[l], tb=min(256, t))
        q, kn, vt, kr = _mla_proj(proj, positions, mla_q_norm[l], mla_kv_norm[l], mla_w_uq[l], mla_w_ukv[l],
                                  tm=min(512, t))
        y_c = _flash(q, kn, vt, kr, tq=min(1024, t), tk=min(1024, t))
        mix = _out_proj(y_a, y_b, y_c, w_out, l, tm=min(512, t), tn=1024)
        hf, h_pk = _res_ln(xf, mix, ln1_g[l], ln1_b[l], tm=min(256, t))
        xf, xb = _moe_ffn(hf, h_pk, router_w[l], router_bias[l],
                          exp_w1, exp_w3, exp_w2, l,
                          shared_w1[l], shared_w3[l], shared_w2[l], ln2_g[l], ln2_b[l], bm=MOE_ROW_BLOCK)
    return xf.reshape(b, s, d)
```
